```python
import jax
import jax.numpy as jnp
from jax import lax
import numpy as np


D_MODEL = 1024
BATCH = 4
SEQ = 8192
DEPTH = 1

D_MIX = D_MODEL
D_ATTN = D_MIX // 2
HEAD_DIM = 64
N_ATTN_HEADS = D_ATTN // HEAD_DIM
D_POOL = D_MIX - D_ATTN
POOL_WINDOWS = (2, 4, 8, 16)
N_POOL_GROUPS = len(POOL_WINDOWS)
POOL_GROUP_DIM = D_POOL // N_POOL_GROUPS
D_IN_PROJ = 3 * D_ATTN + D_POOL
SB_BLOCK = 128

N_EXPERTS = 64
TOP_K = 6
N_EXPERT_GROUPS = 8
TOPK_GROUPS = 4
D_EXPERT = 256
D_SHARED = 256
ROUTED_SCALE = 2.5
MOE_BLOCK = 128

N_MOD = 6
RMS_EPS = 1e-6

kernel_name = 'hybrid_stickbreak_pool_moe_adaln'


def rms_norm(x, gain):
    x32 = x.astype(jnp.float32)
    y = x32 * lax.rsqrt(jnp.mean(jnp.square(x32), axis=-1, keepdims=True) + RMS_EPS)
    return (y * gain.astype(jnp.float32)).astype(x.dtype)


def stick_breaking_attention(q, k, v):
    B, H, S, dh = q.shape
    nb = S // SB_BLOCK
    scale = dh ** -0.5
    q_blocks = q.reshape(B, H, nb, SB_BLOCK, dh).transpose(2, 0, 1, 3, 4)
    offs = jnp.arange(SB_BLOCK, dtype=jnp.int32)

    def one_query_block(args):
        qi, q_blk = args
        t_pos = qi * SB_BLOCK + offs[:, None]

        def body(it, carry):
            acc, log_surv = carry
            kb = qi - it
            k_blk = lax.dynamic_slice_in_dim(k, kb * SB_BLOCK, SB_BLOCK, axis=2)
            v_blk = lax.dynamic_slice_in_dim(v, kb * SB_BLOCK, SB_BLOCK, axis=2)
            z = jnp.einsum('bhqd,bhkd->bhqk', q_blk, k_blk).astype(jnp.float32) * scale
            valid = (kb * SB_BLOCK + offs[None, :]) < t_pos
            log_beta = jnp.where(valid, jax.nn.log_sigmoid(z), -jnp.inf)
            log_fail = jnp.where(valid, jax.nn.log_sigmoid(-z), 0.0)
            later = lax.cumsum(log_fail, axis=3, reverse=True) - log_fail
            a = jnp.exp(log_beta + later + log_surv[..., None])
            acc = acc + jnp.einsum('bhqk,bhkd->bhqd', a, v_blk.astype(jnp.float32))
            return acc, log_surv + jnp.sum(log_fail, axis=-1)

        init = (jnp.zeros((B, H, SB_BLOCK, dh), jnp.float32),
                jnp.zeros((B, H, SB_BLOCK), jnp.float32))
        acc, _ = lax.fori_loop(0, qi + 1, body, init)
        return acc.astype(v.dtype)

    out = lax.map(one_query_block, (jnp.arange(nb, dtype=jnp.int32), q_blocks))
    return out.transpose(1, 2, 0, 3, 4).reshape(B, H, S, dh)


def multiscale_pool(u, w_pool, pool_scale):
    B, S, _ = u.shape
    ug = u.reshape(B, S, N_POOL_GROUPS, POOL_GROUP_DIM).astype(jnp.float32)
    cs0 = jnp.pad(jnp.cumsum(ug, axis=1), ((0, 0), (1, 0), (0, 0), (0, 0)))
    outs = []
    for g, w in enumerate(POOL_WINDOWS):
        c_g = cs0[:, :, g]
        lower = jnp.pad(c_g[:, :S + 1 - w], ((0, 0), (w - 1, 0), (0, 0)))
        count = jnp.minimum(jnp.arange(1, S + 1), w).astype(jnp.float32)[None, :, None]
        outs.append((c_g[:, 1:] - lower) / count)
    pooled = jnp.stack(outs, axis=2)
    mixed = jnp.einsum('bsgc,gcd->bsgd', pooled - ug, w_pool.astype(jnp.float32))
    return (mixed.reshape(B, S, D_POOL) * pool_scale.astype(jnp.float32)).astype(u.dtype)


def route(h, w_router, router_bias):
    T = h.shape[0]
    scores = jax.nn.sigmoid((h @ w_router).astype(jnp.float32))
    biased = scores + router_bias.astype(jnp.float32)
    grp = biased.reshape(T, N_EXPERT_GROUPS, N_EXPERTS // N_EXPERT_GROUPS)
    grp_score = jnp.sum(lax.top_k(grp, 2)[0], axis=-1)
    _, top_groups = lax.top_k(grp_score, TOPK_GROUPS)
    group_mask = jnp.any(jax.nn.one_hot(top_groups, N_EXPERT_GROUPS, dtype=jnp.bool_), axis=1)
    expert_mask = jnp.repeat(group_mask, N_EXPERTS // N_EXPERT_GROUPS, axis=1)
    _, idx = lax.top_k(jnp.where(expert_mask, biased, -jnp.inf), TOP_K)
    wts = jnp.take_along_axis(scores, idx, axis=1)
    wts = wts / jnp.sum(wts, axis=-1, keepdims=True) * ROUTED_SCALE
    return idx, wts


def swiglu(h, w_g, w_u, w_d):
    return (jax.nn.silu(h @ w_g) * (h @ w_u)) @ w_d


def routed_experts(h, idx, wts, w_gate, w_up, w_down):
    T, D = h.shape
    M = T * TOP_K
    n_blocks = -(-M // MOE_BLOCK) + N_EXPERTS
    P = n_blocks * MOE_BLOCK
    flat_e = idx.reshape(-1).astype(jnp.int32)
    flat_tok = jnp.repeat(jnp.arange(T, dtype=jnp.int32), TOP_K)
    flat_w = wts.reshape(-1)
    order = jnp.argsort(flat_e)
    sorted_e = flat_e[order]
    counts = jnp.bincount(flat_e, length=N_EXPERTS).astype(jnp.int32)
    padded = ((counts + MOE_BLOCK - 1) // MOE_BLOCK) * MOE_BLOCK
    start = jnp.cumsum(counts) - counts
    padded_start = jnp.cumsum(padded) - padded
    dest = padded_start[sorted_e] + (jnp.arange(M, dtype=jnp.int32) - start[sorted_e])
    row_tok = jnp.zeros((P,), jnp.int32).at[dest].set(flat_tok[order])
    row_w = jnp.zeros((P,), jnp.float32).at[dest].set(flat_w[order])
    block_ends = jnp.cumsum(padded // MOE_BLOCK)
    block_expert = jnp.clip(jnp.searchsorted(block_ends, jnp.arange(n_blocks), side='right'),
                            0, N_EXPERTS - 1).astype(jnp.int32)

    def expert_block(args):
        tok, wt, e = args
        y = swiglu(h[tok], w_gate[e], w_up[e], w_down[e])
        return y * wt[:, None].astype(y.dtype)

    y = lax.map(expert_block, (row_tok.reshape(n_blocks, MOE_BLOCK),
                               row_w.reshape(n_blocks, MOE_BLOCK), block_expert))
    return jax.ops.segment_sum(y.reshape(P, D), row_tok, num_segments=T)


def setup_inputs(seed: int = 0) -> dict:
    key = jax.random.key(seed)
    ks = jax.random.split(key, 20)
    f32 = jnp.float32
    L, D = DEPTH, D_MODEL
    nrm = lambda k, shape, s: jax.random.normal(k, shape, f32) * s
    return {
        'x': nrm(ks[0], (BATCH, SEQ, D), 1.0),
        'c': nrm(ks[1], (BATCH, D), 1.0),
        'w_ada': nrm(ks[2], (L, D, N_MOD * D), 0.5 * D ** -0.5),
        'b_ada': nrm(ks[3], (L, N_MOD * D), 0.02),
        'norm1_g': 1.0 + nrm(ks[4], (L, D), 0.05),
        'norm2_g': 1.0 + nrm(ks[5], (L, D), 0.05),
        'w_in': nrm(ks[6], (L, D, D_IN_PROJ), D ** -0.5),
        'q_norm_g': 1.0 + nrm(ks[7], (L, HEAD_DIM), 0.05),
        'k_norm_g': 1.0 + nrm(ks[8], (L, HEAD_DIM), 0.05),
        'w_pool': nrm(ks[9], (L, N_POOL_GROUPS, POOL_GROUP_DIM, POOL_GROUP_DIM), POOL_GROUP_DIM ** -0.5),
        'pool_scale': 1.0 + nrm(ks[10], (L, D_POOL), 0.1),
        'w_out': nrm(ks[11], (L, D_MIX, D), D_MIX ** -0.5),
        'w_router': nrm(ks[12], (L, D, N_EXPERTS), D ** -0.5),
        'router_bias': nrm(ks[13], (L, N_EXPERTS), 0.01),
        'w_gate': nrm(ks[14], (L, N_EXPERTS, D, D_EXPERT), D ** -0.5),
        'w_up': nrm(ks[15], (L, N_EXPERTS, D, D_EXPERT), D ** -0.5),
        'w_down': nrm(ks[16], (L, N_EXPERTS, D_EXPERT, D), D_EXPERT ** -0.5),
        'ws_gate': nrm(ks[17], (L, D, D_SHARED), D ** -0.5),
        'ws_up': nrm(ks[18], (L, D, D_SHARED), D ** -0.5),
        'ws_down': nrm(ks[19], (L, D_SHARED, D), D_SHARED ** -0.5),
    }


def reference(x, c, w_ada, b_ada, norm1_g, norm2_g, w_in, q_norm_g, k_norm_g, w_pool, pool_scale,
              w_out, w_router, router_bias, w_gate, w_up, w_down, ws_gate, ws_up, ws_down):
    B, S, D = x.shape
    c_act = jax.nn.silu(c)

    def to_heads(t):
        return t.reshape(B, S, N_ATTN_HEADS, HEAD_DIM).transpose(0, 2, 1, 3)

    for l in range(DEPTH):
        mod = c_act @ w_ada[l] + b_ada[l]
        shift1, scale1, gate1, shift2, scale2, gate2 = jnp.split(mod[:, None, :], N_MOD, axis=-1)

        h = rms_norm(x, norm1_g[l]) * (1 + scale1) + shift1
        proj = h @ w_in[l]
        hq, hk, hv, hp = jnp.split(proj, [D_ATTN, 2 * D_ATTN, 3 * D_ATTN], axis=-1)
        q = rms_norm(to_heads(hq), q_norm_g[l])
        k = rms_norm(to_heads(hk), k_norm_g[l])
        v = to_heads(hv)
        attn = stick_breaking_attention(q, k, v).transpose(0, 2, 1, 3).reshape(B, S, D_ATTN)
        pool = multiscale_pool(hp, w_pool[l], pool_scale[l])
        mix = jnp.concatenate([attn, pool], axis=-1)
        x = x + gate1 * (mix @ w_out[l])

        h2 = (rms_norm(x, norm2_g[l]) * (1 + scale2) + shift2).reshape(B * S, D)
        idx, wts = route(h2, w_router[l], router_bias[l])
        y = routed_experts(h2, idx, wts, w_gate[l], w_up[l], w_down[l]) \
            + swiglu(h2, ws_gate[l], ws_up[l], ws_down[l])
        x = x + gate2 * y.reshape(B, S, D)
    return x
```

```python
import functools

import jax
import jax.numpy as jnp
from jax import lax
from jax.experimental import pallas as pl
from jax.experimental.pallas import tpu as pltpu

F32 = jnp.float32
BF16 = jnp.bfloat16

HEAD_DIM = 64
N_HEADS = 8
D_ATTN = N_HEADS * HEAD_DIM
POOL_WINDOWS = (2, 4, 8, 16)
POOL_GROUP_DIM = 128
D_POOL = len(POOL_WINDOWS) * POOL_GROUP_DIM
MAX_WINDOW = max(POOL_WINDOWS)
N_EXPERTS = 64
TOP_K = 6
N_GROUPS = 8
GROUP_SIZE = N_EXPERTS // N_GROUPS
TOPK_GROUPS = 4
ROUTED_SCALE = 2.5
RMS_EPS = 1e-6
N_MOD = 6

LANES = 128
SUBLANES = 8
VMEM_LIMIT = 56 * 1024 * 1024

TS_PROJ = 512
TQ = 256
TK = 256
TS_MOE = 1024


def _split_bf16(a):
    hi = a.astype(BF16)
    lo = (a - hi.astype(F32)).astype(BF16)
    return hi, lo


def _dot(a, b):
    return jnp.dot(a, b, preferred_element_type=F32)


def _dot_nt(a, b):
    return lax.dot_general(a, b, (((1,), (1,)), ((), ())), preferred_element_type=F32)


def _dot3(a, b):
    ah, al = _split_bf16(a)
    bh, bl = _split_bf16(b)
    return _dot(ah, bh) + _dot(ah, bl) + _dot(al, bh)


def _dot3_nt(a, b):
    ah, al = _split_bf16(a)
    bh, bl = _split_bf16(b)
    return _dot_nt(ah, bh) + _dot_nt(ah, bl) + _dot_nt(al, bh)


def _silu(x):
    return x * (1.0 / (1.0 + jnp.exp(-x)))


def _rms_mod(x, gain, scale, shift):
    ms = jnp.mean(x * x, axis=-1, keepdims=True)
    y = x * lax.rsqrt(ms + RMS_EPS) * gain
    return y * (1.0 + scale) + shift


def _adaln_kernel(c_ref, w_ref, b_ref, o_ref):
    c = c_ref[...]
    o_ref[...] = _dot3(_silu(c), w_ref[...]) + b_ref[...]


def _adaln(c, w_ada, b_ada):
    nb, D = c.shape
    B = -(-nb // SUBLANES) * SUBLANES
    c = jnp.pad(c, ((0, B - nb), (0, 0)))
    N = w_ada.shape[1]
    tn = 1024
    out = pl.pallas_call(
        _adaln_kernel,
        grid=(N // tn,),
        in_specs=[pl.BlockSpec((B, D), lambda j: (0, 0)),
                  pl.BlockSpec((D, tn), lambda j: (0, j)),
                  pl.BlockSpec((1, tn), lambda j: (0, j))],
        out_specs=pl.BlockSpec((B, tn), lambda j: (0, j)),
        out_shape=jax.ShapeDtypeStruct((B, N), F32),
        compiler_params=pltpu.CompilerParams(dimension_semantics=("arbitrary",),
                                             vmem_limit_bytes=VMEM_LIMIT),
        name="adaln",
    )(c, w_ada, b_ada.reshape(1, N))
    return out[:nb]


def _inproj_kernel(x_ref, mod_ref, g1_ref, win_ref, qg_ref, kg_ref, hm_ref, wp_ref, ps_ref,
                   q_ref, k_ref, v_ref, p_ref, ext_ref):
    si = pl.program_id(1)
    ts = x_ref.shape[1]
    x = x_ref[0]
    shift1 = mod_ref[0, 0:1, :]
    scale1 = mod_ref[0, 1:2, :]
    h = _rms_mod(x, g1_ref[...], scale1, shift1)
    proj = _dot(h.astype(BF16), win_ref[...])

    hm = hm_ref[...]

    def head_norm(t, gain):
        hi, lo = _split_bf16(t * t)
        ms = _dot(hi, hm) + _dot(lo, hm)
        return t * lax.rsqrt(ms + RMS_EPS) * gain

    hq = proj[:, 0:D_ATTN]
    hk = proj[:, D_ATTN:2 * D_ATTN]
    q_ref[0] = (head_norm(hq, qg_ref[...]) * (HEAD_DIM ** -0.5)).astype(BF16)
    k_ref[0] = head_norm(hk, kg_ref[...]).astype(BF16)
    v_ref[0] = proj[:, 2 * D_ATTN:3 * D_ATTN].astype(BF16)

    hp = proj[:, 3 * D_ATTN:]

    @pl.when(si == 0)
    def _():
        ext_ref[0:MAX_WINDOW, :] = jnp.zeros((MAX_WINDOW, D_POOL), F32)

    ext_ref[MAX_WINDOW:, :] = hp
    pos = si * ts + lax.broadcasted_iota(jnp.int32, (ts, 1), 0)
    for g, w in enumerate(POOL_WINDOWS):
        lo_l, hi_l = g * POOL_GROUP_DIM, (g + 1) * POOL_GROUP_DIM
        u = hp[:, lo_l:hi_l]
        acc = u
        for i in range(1, w):
            acc = acc + ext_ref[pl.ds(MAX_WINDOW - i, ts), lo_l:hi_l]
        count = jnp.minimum(pos + 1, w).astype(F32)
        d = acc / count - u
        mixed = _dot(d.astype(BF16), wp_ref[g])
        p_ref[0, :, lo_l:hi_l] = (mixed * ps_ref[:, lo_l:hi_l]).astype(BF16)
    ext_ref[0:MAX_WINDOW, :] = hp[ts - MAX_WINDOW:, :]


def _inproj(x, mod3, g1, win_bf, qg_t, kg_t, head_mean, wpool_bf, pool_scale):
    B, S, D = x.shape
    ts = TS_PROJ
    out_sd = jax.ShapeDtypeStruct((B, S, D_ATTN), BF16)
    blk = pl.BlockSpec((1, ts, D_ATTN), lambda b, s: (b, s, 0))
    const2 = lambda b, s: (0, 0)
    return pl.pallas_call(
        _inproj_kernel,
        grid=(B, S // ts),
        in_specs=[pl.BlockSpec((1, ts, D), lambda b, s: (b, s, 0)),
                  pl.BlockSpec((1, N_MOD, D), lambda b, s: (b, 0, 0)),
                  pl.BlockSpec((1, D), const2),
                  pl.BlockSpec(win_bf.shape, const2),
                  pl.BlockSpec((1, D_ATTN), const2),
                  pl.BlockSpec((1, D_ATTN), const2),
                  pl.BlockSpec((D_ATTN, D_ATTN), const2),
                  pl.BlockSpec(wpool_bf.shape, lambda b, s: (0, 0, 0)),
                  pl.BlockSpec((1, D_POOL), const2)],
        out_specs=[blk, blk, blk, blk],
        out_shape=[out_sd, out_sd, out_sd, out_sd],
        scratch_shapes=[pltpu.VMEM((MAX_WINDOW + ts, D_POOL), F32)],
        compiler_params=pltpu.CompilerParams(dimension_semantics=("arbitrary", "arbitrary"),
                                             vmem_limit_bytes=VMEM_LIMIT),
        name="inproj",
    )(x, mod3, g1, win_bf, qg_t, kg_t, head_mean, wpool_bf, pool_scale)


def _attn_kernel(q_ref, k_ref, v_ref, u_ref, o_ref):
    qi = pl.program_id(2)
    q = q_ref[0]
    lane = lax.broadcasted_iota(jnp.int32, (TQ, LANES), 1)
    first = lane < HEAD_DIM
    zero = jnp.zeros_like(q)
    qh = (jnp.where(first, q, zero), jnp.where(first, zero, q))
    u = u_ref[...]

    row = lax.broadcasted_iota(jnp.int32, (TQ, TK), 0)
    col = lax.broadcasted_iota(jnp.int32, (TQ, TK), 1)
    valid = col < row

    def block(kb, carry, masked):
        start = pl.multiple_of(kb * TK, TK)
        k = k_ref[0, pl.ds(start, TK), :]
        v = v_ref[0, pl.ds(start, TK), :]
        out = []
        for h in range(2):
            acc, surv = carry[2 * h], carry[2 * h + 1]
            z = _dot_nt(qh[h], k)
            sp = jnp.maximum(z, 0.0) + jnp.log(1.0 + jnp.exp(-jnp.abs(z)))
            if masked:
                sp = jnp.where(valid, sp, 0.0)
            r = _dot(sp.astype(BF16), u)
            arg = z + r + surv
            if masked:
                arg = jnp.where(valid, arg, -jnp.inf)
            a = jnp.exp(arg)
            acc = acc + _dot(a.astype(BF16), v)
            surv = surv + r[:, 0:1]
            out += [acc, surv]
        return tuple(out)

    init = (jnp.zeros((TQ, LANES), F32), jnp.zeros((TQ, 1), F32),
            jnp.zeros((TQ, LANES), F32), jnp.zeros((TQ, 1), F32))
    carry = block(qi, init, True)
    carry = lax.fori_loop(0, qi, lambda i, c: block(qi - 1 - i, c, False), carry)
    o_ref[0] = jnp.where(first, carry[0], carry[2]).astype(BF16)


def _attention(q, k, v, umat):
    B, S, _ = q.shape
    n_pairs = D_ATTN // LANES
    kv_spec = pl.BlockSpec((1, S, LANES), lambda b, p, i: (b, 0, p))
    q_spec = pl.BlockSpec((1, TQ, LANES), lambda b, p, i: (b, i, p))
    return pl.pallas_call(
        _attn_kernel,
        grid=(B, n_pairs, S // TQ),
        in_specs=[q_spec, kv_spec, kv_spec, pl.BlockSpec((TK, TK), lambda b, p, i: (0, 0))],
        out_specs=q_spec,
        out_shape=jax.ShapeDtypeStruct((B, S, D_ATTN), BF16),
        compiler_params=pltpu.CompilerParams(
            dimension_semantics=("arbitrary", "arbitrary", "arbitrary"),
            vmem_limit_bytes=VMEM_LIMIT),
        name="stickbreak_attn",
    )(q, k, v, umat)


def _route_t(scores, biased):
    ts = scores.shape[-1]
    neg = -jnp.inf
    b3 = biased.reshape(N_GROUPS, GROUP_SIZE, ts)
    e_in_g = lax.broadcasted_iota(jnp.int32, b3.shape, 1)
    m1 = jnp.max(b3, axis=1, keepdims=True)
    i1 = jnp.min(jnp.where(b3 == m1, e_in_g, GROUP_SIZE), axis=1, keepdims=True)
    m2 = jnp.max(jnp.where(e_in_g == i1, neg, b3), axis=1, keepdims=True)
    gs = (m1 + m2)[:, 0, :]
    g_iota = lax.broadcasted_iota(jnp.int32, gs.shape, 0)
    g_sel = jnp.zeros(gs.shape, jnp.bool_)
    for _ in range(TOPK_GROUPS):
        gm = jnp.max(gs, axis=0, keepdims=True)
        gi = jnp.min(jnp.where(gs == gm, g_iota, N_GROUPS), axis=0, keepdims=True)
        pick = g_iota == gi
        g_sel = jnp.logical_or(g_sel, pick)
        gs = jnp.where(pick, neg, gs)
    masked = jnp.where(g_sel[:, None, :], b3, neg)
    flat = lax.broadcasted_iota(jnp.int32, b3.shape, 0) * GROUP_SIZE + e_in_g
    sel = jnp.zeros(b3.shape, jnp.bool_)
    for _ in range(TOP_K):
        m = jnp.max(jnp.max(masked, axis=1, keepdims=True), axis=0, keepdims=True)
        cand = jnp.where(masked == m, flat, N_EXPERTS)
        idx = jnp.min(jnp.min(cand, axis=1, keepdims=True), axis=0, keepdims=True)
        pick = flat == idx
        sel = jnp.logical_or(sel, pick)
        masked = jnp.where(pick, neg, masked)
    s3 = scores.reshape(N_GROUPS, GROUP_SIZE, ts)
    w = jnp.where(sel, s3, 0.0)
    tot = jnp.sum(jnp.sum(w, axis=1, keepdims=True), axis=0, keepdims=True)
    return (w / tot * ROUTED_SCALE).reshape(N_EXPERTS, ts)


def _outproj_kernel(x_ref, a_ref, p_ref, mod_ref, wo_ref, g2_ref, wrt_ref, rb_ref, x1_ref, wt_ref):
    x = x_ref[0]
    gate1 = mod_ref[0, 2:3, :]
    shift2 = mod_ref[0, 3:4, :]
    scale2 = mod_ref[0, 4:5, :]
    mixp = _dot(a_ref[0], wo_ref[0:D_ATTN, :]) + _dot(p_ref[0], wo_ref[D_ATTN:, :])
    x1 = x + gate1 * mixp
    x1_ref[0] = x1
    h2 = _rms_mod(x1, g2_ref[...], scale2, shift2)
    logits_t = _dot3_nt(wrt_ref[...], h2)
    scores = 1.0 / (1.0 + jnp.exp(-logits_t))
    wd_t = _route_t(scores, scores + rb_ref[...])
    ts = wd_t.shape[-1]
    wd_pad = jnp.concatenate([wd_t, jnp.zeros((LANES - N_EXPERTS, ts), F32)], axis=0)
    wt_ref[0] = wd_pad.T


def _outproj(x, attn, pool, mod3, wo_bf, g2, wr_t, rbias):
    B, S, D = x.shape
    ts = TS_PROJ
    const2 = lambda b, s: (0, 0)
    tok = lambda w: pl.BlockSpec((1, ts, w), lambda b, s: (b, s, 0))
    return pl.pallas_call(
        _outproj_kernel,
        grid=(B, S // ts),
        in_specs=[tok(D), tok(D_ATTN), tok(D_POOL),
                  pl.BlockSpec((1, N_MOD, D), lambda b, s: (b, 0, 0)),
                  pl.BlockSpec(wo_bf.shape, const2),
                  pl.BlockSpec((1, D), const2),
                  pl.BlockSpec(wr_t.shape, const2),
                  pl.BlockSpec((N_EXPERTS, 1), const2)],
        out_specs=[tok(D), tok(LANES)],
        out_shape=[jax.ShapeDtypeStruct((B, S, D), F32), jax.ShapeDtypeStruct((B, S, LANES), F32)],
        compiler_params=pltpu.CompilerParams(dimension_semantics=("arbitrary", "arbitrary"),
                                             vmem_limit_bytes=VMEM_LIMIT),
        name="outproj_router",
    )(x, attn, pool, mod3, wo_bf, g2, wr_t, rbias)


def _swiglu_hidden(h_bf, wgu):
    gu = _dot(h_bf, wgu)
    dh = wgu.shape[-1] // 2
    return _silu(gu[:, :dh]) * gu[:, dh:]


def _moe_kernel(x1_ref, wt_ref, mod_ref, g2_ref, wgu_ref, wd_ref, sgu_ref, sd_ref, o_ref, h_ref, acc_ref):
    e = pl.program_id(2)

    @pl.when(e == 0)
    def _():
        shift2 = mod_ref[0, 3:4, :]
        scale2 = mod_ref[0, 4:5, :]
        h2 = _rms_mod(x1_ref[0], g2_ref[...], scale2, shift2).astype(BF16)
        h_ref[...] = h2
        acc_ref[...] = _dot(_swiglu_hidden(h2, sgu_ref[...]).astype(BF16), sd_ref[...])

    wt = wt_ref[0]
    lane = lax.broadcasted_iota(jnp.int32, wt.shape, 1)
    wcol = jnp.sum(jnp.where(lane == e, wt, 0.0), axis=1, keepdims=True)
    act = _swiglu_hidden(h_ref[...], wgu_ref[0])
    acc_ref[...] += wcol * _dot(act.astype(BF16), wd_ref[0])

    @pl.when(e == N_EXPERTS - 1)
    def _():
        gate2 = mod_ref[0, 5:6, :]
        o_ref[0] = x1_ref[0] + gate2 * acc_ref[...]


def _moe(x1, wtok, mod3, g2, wgu_bf, wd_bf, sgu_bf, sd_bf):
    B, S, D = x1.shape
    ts = TS_MOE
    tok = lambda w: pl.BlockSpec((1, ts, w), lambda b, s, e: (b, s, 0))
    const2 = lambda b, s, e: (0, 0)
    return pl.pallas_call(
        _moe_kernel,
        grid=(B, S // ts, N_EXPERTS),
        in_specs=[tok(D), tok(LANES),
                  pl.BlockSpec((1, N_MOD, D), lambda b, s, e: (b, 0, 0)),
                  pl.BlockSpec((1, D), const2),
                  pl.BlockSpec((1,) + wgu_bf.shape[1:], lambda b, s, e: (e, 0, 0)),
                  pl.BlockSpec((1,) + wd_bf.shape[1:], lambda b, s, e: (e, 0, 0)),
                  pl.BlockSpec(sgu_bf.shape, const2),
                  pl.BlockSpec(sd_bf.shape, const2)],
        out_specs=tok(D),
        out_shape=jax.ShapeDtypeStruct((B, S, D), F32),
        scratch_shapes=[pltpu.VMEM((ts, D), BF16), pltpu.VMEM((ts, D), F32)],
        compiler_params=pltpu.CompilerParams(
            dimension_semantics=("arbitrary", "arbitrary", "arbitrary"),
            vmem_limit_bytes=VMEM_LIMIT),
        name="moe",
    )(x1, wtok, mod3, g2, wgu_bf, wd_bf, sgu_bf, sd_bf)


def _layer(x, c_act_mod, norm1_g, norm2_g, w_in, q_norm_g, k_norm_g, w_pool, pool_scale, w_out,
           w_router, router_bias, w_gate, w_up, w_down, ws_gate, ws_up, ws_down):
    B, S, D = x.shape
    mod3 = c_act_mod.reshape(B, N_MOD, D)
    head_of = jnp.arange(D_ATTN, dtype=jnp.int32) // HEAD_DIM
    head_mean = jnp.where(head_of[:, None] == head_of[None, :], 1.0 / HEAD_DIM, 0.0).astype(BF16)
    j = jnp.arange(TK, dtype=jnp.int32)
    umat = jnp.where(j[:, None] >= j[None, :], -1.0, 0.0).astype(BF16)

    q, k, v, pool = _inproj(
        x, mod3, norm1_g.reshape(1, D), w_in.astype(BF16),
        jnp.tile(q_norm_g, N_HEADS).reshape(1, D_ATTN), jnp.tile(k_norm_g, N_HEADS).reshape(1, D_ATTN),
        head_mean, w_pool.astype(BF16), pool_scale.reshape(1, D_POOL))
    attn = _attention(q, k, v, umat)
    x1, wtok = _outproj(x, attn, pool, mod3, w_out.astype(BF16), norm2_g.reshape(1, D),
                        w_router.T, router_bias.reshape(N_EXPERTS, 1))
    wgu = jnp.concatenate([w_gate, w_up], axis=-1).astype(BF16)
    sgu = jnp.concatenate([ws_gate, ws_up], axis=-1).astype(BF16)
    return _moe(x1, wtok, mod3, norm2_g.reshape(1, D), wgu, w_down.astype(BF16), sgu, ws_down.astype(BF16))


def kernel(x, c, w_ada, b_ada, norm1_g, norm2_g, w_in, q_norm_g, k_norm_g, w_pool, pool_scale, w_out,
           w_router, router_bias, w_gate, w_up, w_down, ws_gate, ws_up, ws_down):
    depth = w_ada.shape[0]
    for l in range(depth):
        mod = _adaln(c, w_ada[l], b_ada[l])
        x = _layer(x, mod, norm1_g[l], norm2_g[l], w_in[l], q_norm_g[l], k_norm_g[l], w_pool[l],
                   pool_scale[l], w_out[l], w_router[l], router_bias[l], w_gate[l], w_up[l], w_down[l],
                   ws_gate[l], ws_up[l], ws_down[l])
    return x
```

```python
import functools

import jax
import jax.numpy as jnp
from jax import lax
from jax.experimental import pallas as pl
from jax.experimental.pallas import tpu as pltpu

F32 = jnp.float32
BF16 = jnp.bfloat16

HEAD_DIM = 64
N_HEADS = 8
D_ATTN = N_HEADS * HEAD_DIM
POOL_WINDOWS = (2, 4, 8, 16)
POOL_GROUP_DIM = 128
D_POOL = len(POOL_WINDOWS) * POOL_GROUP_DIM
MAX_WINDOW = max(POOL_WINDOWS)
N_EXPERTS = 64
TOP_K = 6
N_GROUPS = 8
GROUP_SIZE = N_EXPERTS // N_GROUPS
TOPK_GROUPS = 4
ROUTED_SCALE = 2.5
RMS_EPS = 1e-6
N_MOD = 6

LANES = 128
SUBLANES = 8
VMEM_LIMIT = 56 * 1024 * 1024

TS_PROJ = 512
TQ = 512
TK = 256
TS_MOE = 1024


def _split_bf16(a):
    hi = a.astype(BF16)
    lo = (a - hi.astype(F32)).astype(BF16)
    return hi, lo


def _dot(a, b):
    return jnp.dot(a, b, preferred_element_type=F32)


def _dot_nt(a, b):
    return lax.dot_general(a, b, (((1,), (1,)), ((), ())), preferred_element_type=F32)


def _dot3(a, b):
    ah, al = _split_bf16(a)
    bh, bl = _split_bf16(b)
    return _dot(ah, bh) + _dot(ah, bl) + _dot(al, bh)


def _dot3_nt(a, b):
    ah, al = _split_bf16(a)
    bh, bl = _split_bf16(b)
    return _dot_nt(ah, bh) + _dot_nt(ah, bl) + _dot_nt(al, bh)


def _silu(x):
    return x * (1.0 / (1.0 + jnp.exp(-x)))


def _rms_mod(x, gain, scale, shift):
    ms = jnp.mean(x * x, axis=-1, keepdims=True)
    y = x * lax.rsqrt(ms + RMS_EPS) * gain
    return y * (1.0 + scale) + shift


def _adaln_kernel(c_ref, w_ref, b_ref, o_ref):
    c = c_ref[...]
    o_ref[...] = _dot3(_silu(c), w_ref[...]) + b_ref[...]


def _adaln(c, w_ada, b_ada):
    nb, D = c.shape
    B = -(-nb // SUBLANES) * SUBLANES
    c = jnp.pad(c, ((0, B - nb), (0, 0)))
    N = w_ada.shape[1]
    tn = 1024
    out = pl.pallas_call(
        _adaln_kernel,
        grid=(N // tn,),
        in_specs=[pl.BlockSpec((B, D), lambda j: (0, 0)),
                  pl.BlockSpec((D, tn), lambda j: (0, j)),
                  pl.BlockSpec((1, tn), lambda j: (0, j))],
        out_specs=pl.BlockSpec((B, tn), lambda j: (0, j)),
        out_shape=jax.ShapeDtypeStruct((B, N), F32),
        compiler_params=pltpu.CompilerParams(dimension_semantics=("arbitrary",),
                                             vmem_limit_bytes=VMEM_LIMIT),
        name="adaln",
    )(c, w_ada, b_ada.reshape(1, N))
    return out[:nb]


def _inproj_kernel(x_ref, mod_ref, g1_ref, win_ref, qg_ref, kg_ref, hm_ref, wp_ref, ps_ref,
                   q_ref, k_ref, v_ref, p_ref, ext_ref):
    si = pl.program_id(1)
    ts = x_ref.shape[1]
    x = x_ref[0]
    shift1 = mod_ref[0, 0:1, :]
    scale1 = mod_ref[0, 1:2, :]
    h = _rms_mod(x, g1_ref[...], scale1, shift1)
    proj = _dot(h.astype(BF16), win_ref[...])

    hm = hm_ref[...]

    def head_norm(t, gain):
        hi, lo = _split_bf16(t * t)
        ms = _dot(hi, hm) + _dot(lo, hm)
        return t * lax.rsqrt(ms + RMS_EPS) * gain

    hq = proj[:, 0:D_ATTN]
    hk = proj[:, D_ATTN:2 * D_ATTN]
    q_ref[0] = (head_norm(hq, qg_ref[...]) * (HEAD_DIM ** -0.5)).astype(BF16)
    k_ref[0] = head_norm(hk, kg_ref[...]).astype(BF16)
    v_ref[0] = proj[:, 2 * D_ATTN:3 * D_ATTN].astype(BF16)

    hp = proj[:, 3 * D_ATTN:]

    @pl.when(si == 0)
    def _():
        ext_ref[0:MAX_WINDOW, :] = jnp.zeros((MAX_WINDOW, D_POOL), F32)

    ext_ref[MAX_WINDOW:, :] = hp
    pos = si * ts + lax.broadcasted_iota(jnp.int32, (ts, 1), 0)
    for g, w in enumerate(POOL_WINDOWS):
        lo_l, hi_l = g * POOL_GROUP_DIM, (g + 1) * POOL_GROUP_DIM
        u = hp[:, lo_l:hi_l]
        acc = u
        for i in range(1, w):
            acc = acc + ext_ref[pl.ds(MAX_WINDOW - i, ts), lo_l:hi_l]
        count = jnp.minimum(pos + 1, w).astype(F32)
        d = acc / count - u
        mixed = _dot(d.astype(BF16), wp_ref[g])
        p_ref[0, :, lo_l:hi_l] = (mixed * ps_ref[:, lo_l:hi_l]).astype(BF16)
    ext_ref[0:MAX_WINDOW, :] = hp[ts - MAX_WINDOW:, :]


def _inproj(x, mod3, g1, win_bf, qg_t, kg_t, head_mean, wpool_bf, pool_scale):
    B, S, D = x.shape
    ts = TS_PROJ
    out_sd = jax.ShapeDtypeStruct((B, S, D_ATTN), BF16)
    blk = pl.BlockSpec((1, ts, D_ATTN), lambda b, s: (b, s, 0))
    const2 = lambda b, s: (0, 0)
    return pl.pallas_call(
        _inproj_kernel,
        grid=(B, S // ts),
        in_specs=[pl.BlockSpec((1, ts, D), lambda b, s: (b, s, 0)),
                  pl.BlockSpec((1, N_MOD, D), lambda b, s: (b, 0, 0)),
                  pl.BlockSpec((1, D), const2),
                  pl.BlockSpec(win_bf.shape, const2),
                  pl.BlockSpec((1, D_ATTN), const2),
                  pl.BlockSpec((1, D_ATTN), const2),
                  pl.BlockSpec((D_ATTN, D_ATTN), const2),
                  pl.BlockSpec(wpool_bf.shape, lambda b, s: (0, 0, 0)),
                  pl.BlockSpec((1, D_POOL), const2)],
        out_specs=[blk, blk, blk, blk],
        out_shape=[out_sd, out_sd, out_sd, out_sd],
        scratch_shapes=[pltpu.VMEM((MAX_WINDOW + ts, D_POOL), F32)],
        compiler_params=pltpu.CompilerParams(dimension_semantics=("arbitrary", "arbitrary"),
                                             vmem_limit_bytes=VMEM_LIMIT),
        name="inproj",
    )(x, mod3, g1, win_bf, qg_t, kg_t, head_mean, wpool_bf, pool_scale)


def _attn_kernel(q_ref, k_ref, v_ref, u_ref, o_ref, acc_ref):
    qi = pl.program_id(2)
    q = q_ref[0]
    lane = lax.broadcasted_iota(jnp.int32, (TQ, LANES), 1)
    first = lane < HEAD_DIM
    zero = jnp.zeros_like(q)
    qh = (jnp.where(first, q, zero), jnp.where(first, zero, q))
    u = u_ref[...]
    row = lax.broadcasted_iota(jnp.int32, (TQ, TK), 0)
    col = lax.broadcasted_iota(jnp.int32, (TQ, TK), 1)
    acc_ref[...] = jnp.zeros(acc_ref.shape, F32)

    def block(kb, survs, diag):
        start = pl.multiple_of(kb * TK, TK)
        k = k_ref[0, pl.ds(start, TK), :]
        v = v_ref[0, pl.ds(start, TK), :]
        if diag is not None:
            valid = col + diag * TK < row
        out = []
        for h in range(2):
            z = _dot_nt(qh[h], k)
            neg_abs = pltpu.bitcast(pltpu.bitcast(z, jnp.uint32) | jnp.uint32(0x80000000), F32)
            sp = jnp.maximum(z, 0.0) + jnp.log(1.0 + jnp.exp(neg_abs))
            if diag is not None:
                sp = jnp.where(valid, sp, 0.0)
            r = _dot(sp.astype(BF16), u)
            arg = z + r + survs[h]
            if diag is not None:
                arg = jnp.where(valid, arg, -jnp.inf)
            acc_ref[h] += _dot(jnp.exp(arg).astype(BF16), v)
            out.append(survs[h] + r[:, 0:1])
        return tuple(out)

    survs = (jnp.zeros((TQ, 1), F32), jnp.zeros((TQ, 1), F32))
    n_diag = TQ // TK
    for d in reversed(range(n_diag)):
        survs = block(qi * n_diag + d, survs, d)
    def full_blocks(i, c):
        for d in range(n_diag):
            c = block((qi - i) * n_diag - 1 - d, c, None)
        return c

    lax.fori_loop(0, qi, full_blocks, survs)
    o_ref[0] = jnp.where(first, acc_ref[0], acc_ref[1]).astype(BF16)


def _attention(q, k, v, umat):
    B, S, _ = q.shape
    n_pairs = D_ATTN // LANES
    kv_spec = pl.BlockSpec((1, S, LANES), lambda b, p, i: (b, 0, p))
    q_spec = pl.BlockSpec((1, TQ, LANES), lambda b, p, i: (b, i, p))
    return pl.pallas_call(
        _attn_kernel,
        grid=(B, n_pairs, S // TQ),
        in_specs=[q_spec, kv_spec, kv_spec, pl.BlockSpec((TK, TK), lambda b, p, i: (0, 0))],
        out_specs=q_spec,
        out_shape=jax.ShapeDtypeStruct((B, S, D_ATTN), BF16),
        scratch_shapes=[pltpu.VMEM((2, TQ, LANES), F32)],
        compiler_params=pltpu.CompilerParams(
            dimension_semantics=("arbitrary", "arbitrary", "arbitrary"),
            vmem_limit_bytes=VMEM_LIMIT),
        name="stickbreak_attn",
    )(q, k, v, umat)


def _route_t(scores, biased):
    ts = scores.shape[-1]
    neg = -jnp.inf
    b3 = biased.reshape(N_GROUPS, GROUP_SIZE, ts)
    e_in_g = lax.broadcasted_iota(jnp.int32, b3.shape, 1)
    m1 = jnp.max(b3, axis=1, keepdims=True)
    i1 = jnp.min(jnp.where(b3 == m1, e_in_g, GROUP_SIZE), axis=1, keepdims=True)
    m2 = jnp.max(jnp.where(e_in_g == i1, neg, b3), axis=1, keepdims=True)
    gs = (m1 + m2)[:, 0, :]
    g_iota = lax.broadcasted_iota(jnp.int32, gs.shape, 0)
    g_sel = jnp.zeros(gs.shape, jnp.bool_)
    for _ in range(TOPK_GROUPS):
        gm = jnp.max(gs, axis=0, keepdims=True)
        gi = jnp.min(jnp.where(gs == gm, g_iota, N_GROUPS), axis=0, keepdims=True)
        pick = g_iota == gi
        g_sel = jnp.logical_or(g_sel, pick)
        gs = jnp.where(pick, neg, gs)
    masked = jnp.where(g_sel[:, None, :], b3, neg)
    flat = lax.broadcasted_iota(jnp.int32, b3.shape, 0) * GROUP_SIZE + e_in_g
    sel = jnp.zeros(b3.shape, jnp.bool_)
    for _ in range(TOP_K):
        m = jnp.max(jnp.max(masked, axis=1, keepdims=True), axis=0, keepdims=True)
        cand = jnp.where(masked == m, flat, N_EXPERTS)
        idx = jnp.min(jnp.min(cand, axis=1, keepdims=True), axis=0, keepdims=True)
        pick = flat == idx
        sel = jnp.logical_or(sel, pick)
        masked = jnp.where(pick, neg, masked)
    s3 = scores.reshape(N_GROUPS, GROUP_SIZE, ts)
    w = jnp.where(sel, s3, 0.0)
    tot = jnp.sum(jnp.sum(w, axis=1, keepdims=True), axis=0, keepdims=True)
    return (w / tot * ROUTED_SCALE).reshape(N_EXPERTS, ts)


def _outproj_kernel(x_ref, a_ref, p_ref, mod_ref, wo_ref, g2_ref, wrt_ref, rb_ref, x1_ref, wt_ref):
    x = x_ref[0]
    gate1 = mod_ref[0, 2:3, :]
    shift2 = mod_ref[0, 3:4, :]
    scale2 = mod_ref[0, 4:5, :]
    mixp = _dot(a_ref[0], wo_ref[0:D_ATTN, :]) + _dot(p_ref[0], wo_ref[D_ATTN:, :])
    x1 = x + gate1 * mixp
    x1_ref[0] = x1
    h2 = _rms_mod(x1, g2_ref[...], scale2, shift2)
    logits_t = _dot3_nt(wrt_ref[...], h2)
    scores = 1.0 / (1.0 + jnp.exp(-logits_t))
    wd_t = _route_t(scores, scores + rb_ref[...])
    ts = wd_t.shape[-1]
    wd_pad = jnp.concatenate([wd_t, jnp.zeros((LANES - N_EXPERTS, ts), F32)], axis=0)
    wt_ref[0] = wd_pad.T


def _outproj(x, attn, pool, mod3, wo_bf, g2, wr_t, rbias):
    B, S, D = x.shape
    ts = TS_PROJ
    const2 = lambda b, s: (0, 0)
    tok = lambda w: pl.BlockSpec((1, ts, w), lambda b, s: (b, s, 0))
    return pl.pallas_call(
        _outproj_kernel,
        grid=(B, S // ts),
        in_specs=[tok(D), tok(D_ATTN), tok(D_POOL),
                  pl.BlockSpec((1, N_MOD, D), lambda b, s: (b, 0, 0)),
                  pl.BlockSpec(wo_bf.shape, const2),
                  pl.BlockSpec((1, D), const2),
                  pl.BlockSpec(wr_t.shape, const2),
                  pl.BlockSpec((N_EXPERTS, 1), const2)],
        out_specs=[tok(D), tok(LANES)],
        out_shape=[jax.ShapeDtypeStruct((B, S, D), F32), jax.ShapeDtypeStruct((B, S, LANES), F32)],
        compiler_params=pltpu.CompilerParams(dimension_semantics=("arbitrary", "arbitrary"),
                                             vmem_limit_bytes=VMEM_LIMIT),
        name="outproj_router",
    )(x, attn, pool, mod3, wo_bf, g2, wr_t, rbias)


def _swiglu_hidden(h_bf, wgu):
    gu = _dot(h_bf, wgu)
    dh = wgu.shape[-1] // 2
    return _silu(gu[:, :dh]) * gu[:, dh:]


def _moe_kernel(x1_ref, wt_ref, mod_ref, g2_ref, wgu_ref, wd_ref, sgu_ref, sd_ref, o_ref, h_ref, acc_ref):
    e = pl.program_id(2)

    @pl.when(e == 0)
    def _():
        shift2 = mod_ref[0, 3:4, :]
        scale2 = mod_ref[0, 4:5, :]
        h2 = _rms_mod(x1_ref[0], g2_ref[...], scale2, shift2).astype(BF16)
        h_ref[...] = h2
        acc_ref[...] = _dot(_swiglu_hidden(h2, sgu_ref[...]).astype(BF16), sd_ref[...])

    wt = wt_ref[0]
    lane = lax.broadcasted_iota(jnp.int32, wt.shape, 1)
    wcol = jnp.sum(jnp.where(lane == e, wt, 0.0), axis=1, keepdims=True)
    act = _swiglu_hidden(h_ref[...], wgu_ref[0])
    acc_ref[...] += wcol * _dot(act.astype(BF16), wd_ref[0])

    @pl.when(e == N_EXPERTS - 1)
    def _():
        gate2 = mod_ref[0, 5:6, :]
        o_ref[0] = x1_ref[0] + gate2 * acc_ref[...]


def _moe(x1, wtok, mod3, g2, wgu_bf, wd_bf, sgu_bf, sd_bf):
    B, S, D = x1.shape
    ts = TS_MOE
    tok = lambda w: pl.BlockSpec((1, ts, w), lambda b, s, e: (b, s, 0))
    const2 = lambda b, s, e: (0, 0)
    return pl.pallas_call(
        _moe_kernel,
        grid=(B, S // ts, N_EXPERTS),
        in_specs=[tok(D), tok(LANES),
                  pl.BlockSpec((1, N_MOD, D), lambda b, s, e: (b, 0, 0)),
                  pl.BlockSpec((1, D), const2),
                  pl.BlockSpec((1,) + wgu_bf.shape[1:], lambda b, s, e: (e, 0, 0)),
                  pl.BlockSpec((1,) + wd_bf.shape[1:], lambda b, s, e: (e, 0, 0)),
                  pl.BlockSpec(sgu_bf.shape, const2),
                  pl.BlockSpec(sd_bf.shape, const2)],
        out_specs=tok(D),
        out_shape=jax.ShapeDtypeStruct((B, S, D), F32),
        scratch_shapes=[pltpu.VMEM((ts, D), BF16), pltpu.VMEM((ts, D), F32)],
        compiler_params=pltpu.CompilerParams(
            dimension_semantics=("arbitrary", "arbitrary", "arbitrary"),
            vmem_limit_bytes=VMEM_LIMIT),
        name="moe",
    )(x1, wtok, mod3, g2, wgu_bf, wd_bf, sgu_bf, sd_bf)


def _layer(x, c_act_mod, norm1_g, norm2_g, w_in, q_norm_g, k_norm_g, w_pool, pool_scale, w_out,
           w_router, router_bias, w_gate, w_up, w_down, ws_gate, ws_up, ws_down):
    B, S, D = x.shape
    mod3 = c_act_mod.reshape(B, N_MOD, D)
    head_of = jnp.arange(D_ATTN, dtype=jnp.int32) // HEAD_DIM
    head_mean = jnp.where(head_of[:, None] == head_of[None, :], 1.0 / HEAD_DIM, 0.0).astype(BF16)
    j = jnp.arange(TK, dtype=jnp.int32)
    umat = jnp.where(j[:, None] >= j[None, :], -1.0, 0.0).astype(BF16)

    q, k, v, pool = _inproj(
        x, mod3, norm1_g.reshape(1, D), w_in.astype(BF16),
        jnp.tile(q_norm_g, N_HEADS).reshape(1, D_ATTN), jnp.tile(k_norm_g, N_HEADS).reshape(1, D_ATTN),
        head_mean, w_pool.astype(BF16), pool_scale.reshape(1, D_POOL))
    attn = _attention(q, k, v, umat)
    x1, wtok = _outproj(x, attn, pool, mod3, w_out.astype(BF16), norm2_g.reshape(1, D),
                        w_router.T, router_bias.reshape(N_EXPERTS, 1))
    wgu = jnp.concatenate([w_gate, w_up], axis=-1).astype(BF16)
    sgu = jnp.concatenate([ws_gate, ws_up], axis=-1).astype(BF16)
    return _moe(x1, wtok, mod3, norm2_g.reshape(1, D), wgu, w_down.astype(BF16), sgu, ws_down.astype(BF16))


def kernel(x, c, w_ada, b_ada, norm1_g, norm2_g, w_in, q_norm_g, k_norm_g, w_pool, pool_scale, w_out,
           w_router, router_bias, w_gate, w_up, w_down, ws_gate, ws_up, ws_down):
    depth = w_ada.shape[0]
    for l in range(depth):
        mod = _adaln(c, w_ada[l], b_ada[l])
        x = _layer(x, mod, norm1_g[l], norm2_g[l], w_in[l], q_norm_g[l], k_norm_g[l], w_pool[l],
                   pool_scale[l], w_out[l], w_router[l], router_bias[l], w_gate[l], w_up[l], w_down[l],
                   ws_gate[l], ws_up[l], ws_down[l])
    return x
```

```python
import functools

import jax
import jax.numpy as jnp
from jax import lax
from jax.experimental import pallas as pl
from jax.experimental.pallas import tpu as pltpu
from jax.experimental.pallas import tpu_sc as plsc

F32 = jnp.float32
BF16 = jnp.bfloat16

HEAD_DIM = 64
N_HEADS = 8
D_ATTN = N_HEADS * HEAD_DIM
POOL_WINDOWS = (2, 4, 8, 16)
POOL_GROUP_DIM = 128
D_POOL = len(POOL_WINDOWS) * POOL_GROUP_DIM
MAX_WINDOW = max(POOL_WINDOWS)
N_EXPERTS = 64
TOP_K = 6
N_GROUPS = 8
GROUP_SIZE = N_EXPERTS // N_GROUPS
TOPK_GROUPS = 4
ROUTED_SCALE = 2.5
RMS_EPS = 1e-6
N_MOD = 6

LANES = 128
SUBLANES = 8
VMEM_LIMIT = 56 * 1024 * 1024

TS_PROJ = 512
TQ = 512
TK = 256
TS_SLOT = 2048
ROW_BLOCK_LOG2 = 8
ROW_BLOCK = 1 << ROW_BLOCK_LOG2
SC_CORES = 2
SC_SUBCORES = 16
SC_CHUNK = 64


def _split_bf16(a):
    hi = a.astype(BF16)
    lo = (a - hi.astype(F32)).astype(BF16)
    return hi, lo


def _dot(a, b):
    return jnp.dot(a, b, preferred_element_type=F32)


def _dot_nt(a, b):
    return lax.dot_general(a, b, (((1,), (1,)), ((), ())), preferred_element_type=F32)


def _dot3(a, b):
    ah, al = _split_bf16(a)
    bh, bl = _split_bf16(b)
    return _dot(ah, bh) + _dot(ah, bl) + _dot(al, bh)


def _dot3_nt(a, b):
    ah, al = _split_bf16(a)
    bh, bl = _split_bf16(b)
    return _dot_nt(ah, bh) + _dot_nt(ah, bl) + _dot_nt(al, bh)


def _silu(x):
    return x * (1.0 / (1.0 + jnp.exp(-x)))


def _rms_mod(x, gain, scale, shift):
    ms = jnp.mean(x * x, axis=-1, keepdims=True)
    y = x * lax.rsqrt(ms + RMS_EPS) * gain
    return y * (1.0 + scale) + shift


def _adaln_kernel(c_ref, w_ref, b_ref, o_ref):
    c = c_ref[...]
    o_ref[...] = _dot3(_silu(c), w_ref[...]) + b_ref[...]


def _adaln(c, w_ada, b_ada):
    nb, D = c.shape
    B = -(-nb // SUBLANES) * SUBLANES
    c = jnp.pad(c, ((0, B - nb), (0, 0)))
    N = w_ada.shape[1]
    tn = 1024
    out = pl.pallas_call(
        _adaln_kernel,
        grid=(N // tn,),
        in_specs=[pl.BlockSpec((B, D), lambda j: (0, 0)),
                  pl.BlockSpec((D, tn), lambda j: (0, j)),
                  pl.BlockSpec((1, tn), lambda j: (0, j))],
        out_specs=pl.BlockSpec((B, tn), lambda j: (0, j)),
        out_shape=jax.ShapeDtypeStruct((B, N), F32),
        compiler_params=pltpu.CompilerParams(dimension_semantics=("arbitrary",),
                                             vmem_limit_bytes=VMEM_LIMIT),
        name="adaln",
    )(c, w_ada, b_ada.reshape(1, N))
    return out[:nb]


def _inproj_kernel(x_ref, mod_ref, g1_ref, win_ref, qg_ref, kg_ref, hm_ref, wp_ref, ps_ref,
                   q_ref, k_ref, v_ref, p_ref, ext_ref):
    si = pl.program_id(1)
    ts = x_ref.shape[1]
    x = x_ref[0]
    shift1 = mod_ref[0, 0:1, :]
    scale1 = mod_ref[0, 1:2, :]
    h = _rms_mod(x, g1_ref[...], scale1, shift1)
    proj = _dot(h.astype(BF16), win_ref[...])

    hm = hm_ref[...]

    def head_norm(t, gain):
        hi, lo = _split_bf16(t * t)
        ms = _dot(hi, hm) + _dot(lo, hm)
        return t * lax.rsqrt(ms + RMS_EPS) * gain

    hq = proj[:, 0:D_ATTN]
    hk = proj[:, D_ATTN:2 * D_ATTN]
    q_ref[0] = (head_norm(hq, qg_ref[...]) * (HEAD_DIM ** -0.5)).astype(BF16)
    k_ref[0] = head_norm(hk, kg_ref[...]).astype(BF16)
    v_ref[0] = proj[:, 2 * D_ATTN:3 * D_ATTN].astype(BF16)

    hp = proj[:, 3 * D_ATTN:]

    @pl.when(si == 0)
    def _():
        ext_ref[0:MAX_WINDOW, :] = jnp.zeros((MAX_WINDOW, D_POOL), F32)

    ext_ref[MAX_WINDOW:, :] = hp
    pos = si * ts + lax.broadcasted_iota(jnp.int32, (ts, 1), 0)
    for g, w in enumerate(POOL_WINDOWS):
        lo_l, hi_l = g * POOL_GROUP_DIM, (g + 1) * POOL_GROUP_DIM
        u = hp[:, lo_l:hi_l]
        acc = u
        for i in range(1, w):
            acc = acc + ext_ref[pl.ds(MAX_WINDOW - i, ts), lo_l:hi_l]
        count = jnp.minimum(pos + 1, w).astype(F32)
        d = acc / count - u
        mixed = _dot(d.astype(BF16), wp_ref[g])
        p_ref[0, :, lo_l:hi_l] = (mixed * ps_ref[:, lo_l:hi_l]).astype(BF16)
    ext_ref[0:MAX_WINDOW, :] = hp[ts - MAX_WINDOW:, :]


def _inproj(x, mod3, g1, win_bf, qg_t, kg_t, head_mean, wpool_bf, pool_scale):
    B, S, D = x.shape
    ts = TS_PROJ
    out_sd = jax.ShapeDtypeStruct((B, S, D_ATTN), BF16)
    blk = pl.BlockSpec((1, ts, D_ATTN), lambda b, s: (b, s, 0))
    const2 = lambda b, s: (0, 0)
    return pl.pallas_call(
        _inproj_kernel,
        grid=(B, S // ts),
        in_specs=[pl.BlockSpec((1, ts, D), lambda b, s: (b, s, 0)),
                  pl.BlockSpec((1, N_MOD, D), lambda b, s: (b, 0, 0)),
                  pl.BlockSpec((1, D), const2),
                  pl.BlockSpec(win_bf.shape, const2),
                  pl.BlockSpec((1, D_ATTN), const2),
                  pl.BlockSpec((1, D_ATTN), const2),
                  pl.BlockSpec((D_ATTN, D_ATTN), const2),
                  pl.BlockSpec(wpool_bf.shape, lambda b, s: (0, 0, 0)),
                  pl.BlockSpec((1, D_POOL), const2)],
        out_specs=[blk, blk, blk, blk],
        out_shape=[out_sd, out_sd, out_sd, out_sd],
        scratch_shapes=[pltpu.VMEM((MAX_WINDOW + ts, D_POOL), F32)],
        compiler_params=pltpu.CompilerParams(dimension_semantics=("arbitrary", "arbitrary"),
                                             vmem_limit_bytes=VMEM_LIMIT),
        name="inproj",
    )(x, mod3, g1, win_bf, qg_t, kg_t, head_mean, wpool_bf, pool_scale)


def _attn_kernel(q_ref, k_ref, v_ref, u_ref, o_ref, acc_ref):
    qi = pl.program_id(2)
    q = q_ref[0]
    lane = lax.broadcasted_iota(jnp.int32, (TQ, LANES), 1)
    first = lane < HEAD_DIM
    zero = jnp.zeros_like(q)
    qh = (jnp.where(first, q, zero), jnp.where(first, zero, q))
    u = u_ref[...]
    row = lax.broadcasted_iota(jnp.int32, (TQ, TK), 0)
    col = lax.broadcasted_iota(jnp.int32, (TQ, TK), 1)
    acc_ref[...] = jnp.zeros(acc_ref.shape, F32)

    def block(kb, survs, diag):
        start = pl.multiple_of(kb * TK, TK)
        k = k_ref[0, pl.ds(start, TK), :]
        v = v_ref[0, pl.ds(start, TK), :]
        if diag is not None:
            valid = col + diag * TK < row
        out = []
        for h in range(2):
            z = _dot_nt(qh[h], k)
            neg_abs = pltpu.bitcast(pltpu.bitcast(z, jnp.uint32) | jnp.uint32(0x80000000), F32)
            sp = jnp.maximum(z, 0.0) + jnp.log(1.0 + jnp.exp(neg_abs))
            if diag is not None:
                sp = jnp.where(valid, sp, 0.0)
            r = _dot(sp.astype(BF16), u)
            arg = z + r + survs[h]
            if diag is not None:
                arg = jnp.where(valid, arg, -jnp.inf)
            acc_ref[h] += _dot(jnp.exp(arg).astype(BF16), v)
            out.append(survs[h] + r[:, 0:1])
        return tuple(out)

    survs = (jnp.zeros((TQ, 1), F32), jnp.zeros((TQ, 1), F32))
    n_diag = TQ // TK
    for d in reversed(range(n_diag)):
        survs = block(qi * n_diag + d, survs, d)
    def full_blocks(i, c):
        for d in range(n_diag):
            c = block((qi - i) * n_diag - 1 - d, c, None)
        return c

    lax.fori_loop(0, qi, full_blocks, survs)
    o_ref[0] = jnp.where(first, acc_ref[0], acc_ref[1]).astype(BF16)


def _attention(q, k, v, umat):
    B, S, _ = q.shape
    n_pairs = D_ATTN // LANES
    kv_spec = pl.BlockSpec((1, S, LANES), lambda b, p, i: (b, 0, p))
    q_spec = pl.BlockSpec((1, TQ, LANES), lambda b, p, i: (b, i, p))
    return pl.pallas_call(
        _attn_kernel,
        grid=(B, n_pairs, S // TQ),
        in_specs=[q_spec, kv_spec, kv_spec, pl.BlockSpec((TK, TK), lambda b, p, i: (0, 0))],
        out_specs=q_spec,
        out_shape=jax.ShapeDtypeStruct((B, S, D_ATTN), BF16),
        scratch_shapes=[pltpu.VMEM((2, TQ, LANES), F32)],
        compiler_params=pltpu.CompilerParams(
            dimension_semantics=("arbitrary", "arbitrary", "arbitrary"),
            vmem_limit_bytes=VMEM_LIMIT),
        name="stickbreak_attn",
    )(q, k, v, umat)


def _route_t(scores, biased):
    ts = scores.shape[-1]
    neg = -jnp.inf
    b3 = biased.reshape(N_GROUPS, GROUP_SIZE, ts)
    e_in_g = lax.broadcasted_iota(jnp.int32, b3.shape, 1)
    m1 = jnp.max(b3, axis=1, keepdims=True)
    i1 = jnp.min(jnp.where(b3 == m1, e_in_g, GROUP_SIZE), axis=1, keepdims=True)
    m2 = jnp.max(jnp.where(e_in_g == i1, neg, b3), axis=1, keepdims=True)
    gs = (m1 + m2)[:, 0, :]
    g_iota = lax.broadcasted_iota(jnp.int32, gs.shape, 0)
    g_sel = jnp.zeros(gs.shape, jnp.bool_)
    for _ in range(TOPK_GROUPS):
        gm = jnp.max(gs, axis=0, keepdims=True)
        gi = jnp.min(jnp.where(gs == gm, g_iota, N_GROUPS), axis=0, keepdims=True)
        pick = g_iota == gi
        g_sel = jnp.logical_or(g_sel, pick)
        gs = jnp.where(pick, neg, gs)
    masked = jnp.where(g_sel[:, None, :], b3, neg)
    flat = lax.broadcasted_iota(jnp.int32, b3.shape, 0) * GROUP_SIZE + e_in_g
    sel = jnp.zeros(b3.shape, jnp.bool_)
    for _ in range(TOP_K):
        m = jnp.max(jnp.max(masked, axis=1, keepdims=True), axis=0, keepdims=True)
        cand = jnp.where(masked == m, flat, N_EXPERTS)
        idx = jnp.min(jnp.min(cand, axis=1, keepdims=True), axis=0, keepdims=True)
        pick = flat == idx
        sel = jnp.logical_or(sel, pick)
        masked = jnp.where(pick, neg, masked)
    s3 = scores.reshape(N_GROUPS, GROUP_SIZE, ts)
    w = jnp.where(sel, s3, 0.0)
    tot = jnp.sum(jnp.sum(w, axis=1, keepdims=True), axis=0, keepdims=True)
    return (w / tot * ROUTED_SCALE).reshape(N_EXPERTS, ts), sel.reshape(N_EXPERTS, ts)


def _pack_bf16_pair(lo, hi):
    lo_bits = pltpu.bitcast(lo.astype(BF16).astype(F32), jnp.uint32) >> 16
    hi_bits = pltpu.bitcast(hi.astype(BF16).astype(F32), jnp.uint32) & jnp.uint32(0xFFFF0000)
    return lo_bits | hi_bits


def _unpack_bf16_pair(p):
    lo = pltpu.bitcast(p << 16, F32).astype(BF16)
    hi = pltpu.bitcast(p & jnp.uint32(0xFFFF0000), F32).astype(BF16)
    return lo, hi


def _outproj_kernel(x_ref, a_ref, p_ref, mod_ref, wo_ref, g2_ref, wrt_ref, rb_ref, ut_ref,
                    x1_ref, hp_ref, rank_ref, wd_ref, cnt_ref, run_ref):
    first_step = jnp.logical_and(pl.program_id(0) == 0, pl.program_id(1) == 0)

    @pl.when(first_step)
    def _():
        run_ref[...] = jnp.zeros(run_ref.shape, jnp.int32)

    x = x_ref[0]
    gate1 = mod_ref[0, 2:3, :]
    shift2 = mod_ref[0, 3:4, :]
    scale2 = mod_ref[0, 4:5, :]
    mixp = _dot(a_ref[0], wo_ref[0:D_ATTN, :]) + _dot(p_ref[0], wo_ref[D_ATTN:, :])
    x1 = x + gate1 * mixp
    x1_ref[0] = x1
    h2 = _rms_mod(x1, g2_ref[...], scale2, shift2)
    half = h2.shape[-1] // 2
    hp_ref[0] = _pack_bf16_pair(h2[:, :half], h2[:, half:])
    logits_t = _dot3_nt(wrt_ref[...], h2)
    scores = 1.0 / (1.0 + jnp.exp(-logits_t))
    wd_t, sel = _route_t(scores, scores + rb_ref[...])
    wd_ref[...] = wd_t
    self_f = jnp.where(sel, 1.0, 0.0)
    before = _dot(self_f.astype(BF16), ut_ref[...]).astype(jnp.int32)
    run = run_ref[:, 0:1]
    rank_ref[...] = jnp.where(sel, run + before, -1)
    run_ref[...] = run_ref[...] + jnp.sum(self_f, axis=1, keepdims=True).astype(jnp.int32)
    cnt_ref[...] = run_ref[...]


def _outproj(x, attn, pool, mod3, wo_bf, g2, wr_t, rbias, ut):
    B, S, D = x.shape
    ts = TS_PROJ
    n_s = S // ts
    const2 = lambda b, s: (0, 0)
    tok = lambda w: pl.BlockSpec((1, ts, w), lambda b, s: (b, s, 0))
    tok_t = pl.BlockSpec((N_EXPERTS, ts), lambda b, s: (0, b * n_s + s))
    return pl.pallas_call(
        _outproj_kernel,
        grid=(B, n_s),
        in_specs=[tok(D), tok(D_ATTN), tok(D_POOL),
                  pl.BlockSpec((1, N_MOD, D), lambda b, s: (b, 0, 0)),
                  pl.BlockSpec(wo_bf.shape, const2),
                  pl.BlockSpec((1, D), const2),
                  pl.BlockSpec(wr_t.shape, const2),
                  pl.BlockSpec((N_EXPERTS, 1), const2),
                  pl.BlockSpec((ts, ts), const2)],
        out_specs=[tok(D), tok(D // 2), tok_t, tok_t, pl.BlockSpec((N_EXPERTS, LANES), const2)],
        out_shape=[jax.ShapeDtypeStruct((B, S, D), F32),
                   jax.ShapeDtypeStruct((B, S, D // 2), jnp.uint32),
                   jax.ShapeDtypeStruct((N_EXPERTS, B * S), jnp.int32),
                   jax.ShapeDtypeStruct((N_EXPERTS, B * S), F32),
                   jax.ShapeDtypeStruct((N_EXPERTS, LANES), jnp.int32)],
        scratch_shapes=[pltpu.VMEM((N_EXPERTS, LANES), jnp.int32)],
        compiler_params=pltpu.CompilerParams(dimension_semantics=("arbitrary", "arbitrary"),
                                             vmem_limit_bytes=VMEM_LIMIT),
        name="outproj_router",
    )(x, attn, pool, mod3, wo_bf, g2, wr_t, rbias, ut)


def _n_row_blocks(n_tokens):
    return -(-(n_tokens * TOP_K + N_EXPERTS * (ROW_BLOCK - 1)) // ROW_BLOCK)


def _slots_kernel(rank_ref, wd_ref, cnt_ref, lt_ref, slot_ref, wk_ref, be_ref, nu_ref):
    lt = lt_ref[...]
    nblk = lax.shift_right_logical(cnt_ref[...] + (ROW_BLOCK - 1), ROW_BLOCK_LOG2).astype(F32)
    nb_hi, nb_lo = _split_bf16(nblk)
    blk_start = _dot(lt, nb_hi) + _dot(lt, nb_lo)

    @pl.when(pl.program_id(0) == 0)
    def _():
        blk_end = (blk_start + nblk)[:, 0:1]
        b_iota = lax.broadcasted_iota(jnp.int32, (N_EXPERTS, be_ref.shape[-1]), 1).astype(F32)
        owner = jnp.sum(jnp.where(blk_end <= b_iota, 1, 0), axis=0, keepdims=True)
        be_ref[...] = jnp.minimum(owner, N_EXPERTS - 1)
        nu_ref[...] = jnp.broadcast_to(blk_end[N_EXPERTS - 1:, :].astype(jnp.int32), nu_ref.shape)

    rank = rank_ref[...]
    sel = rank >= 0
    row_start = (blk_start[:, 0:1] * ROW_BLOCK).astype(jnp.int32)
    slot_d = row_start + rank
    wd = wd_ref[...]
    choice = _dot(lt, jnp.where(sel, 1.0, 0.0).astype(BF16)).astype(jnp.int32)
    ts = rank.shape[-1]
    slots, wks = [], []
    for k in range(TOP_K):
        m = jnp.logical_and(sel, choice == k)
        slots.append(jnp.sum(jnp.where(m, slot_d, 0), axis=0, keepdims=True))
        wks.append(jnp.sum(jnp.where(m, wd, 0.0), axis=0, keepdims=True))
    slot_ref[...] = jnp.concatenate(slots + [jnp.zeros((SUBLANES - TOP_K, ts), jnp.int32)], axis=0)
    wk_pad = jnp.concatenate(wks + [jnp.zeros((LANES - TOP_K, ts), F32)], axis=0)
    wk_ref[...] = wk_pad.T


def _slots(rank_t, wd_t, cnt, lt):
    T = rank_t.shape[1]
    ts = TS_SLOT
    nb_pad = -(-_n_row_blocks(T) // LANES) * LANES
    const = lambda i: (0, 0)
    tok_t = pl.BlockSpec((N_EXPERTS, ts), lambda i: (0, i))
    return pl.pallas_call(
        _slots_kernel,
        grid=(T // ts,),
        in_specs=[tok_t, tok_t, pl.BlockSpec((N_EXPERTS, LANES), const), pl.BlockSpec((N_EXPERTS, N_EXPERTS), const)],
        out_specs=[pl.BlockSpec((SUBLANES, ts), lambda i: (0, i)),
                   pl.BlockSpec((ts, LANES), lambda i: (i, 0)),
                   pl.BlockSpec((1, nb_pad), const),
                   pl.BlockSpec((1, LANES), const)],
        out_shape=[jax.ShapeDtypeStruct((SUBLANES, T), jnp.int32),
                   jax.ShapeDtypeStruct((T, LANES), F32),
                   jax.ShapeDtypeStruct((1, nb_pad), jnp.int32),
                   jax.ShapeDtypeStruct((1, LANES), jnp.int32)],
        compiler_params=pltpu.CompilerParams(dimension_semantics=("arbitrary",), vmem_limit_bytes=VMEM_LIMIT),
        name="slots",
    )(rank_t, wd_t, cnt, lt)


def _sc_mesh():
    return plsc.VectorSubcoreMesh(core_axis_name="c", subcore_axis_name="s",
                                  num_cores=SC_CORES, num_subcores=SC_SUBCORES)


def _sc_dispatch(rows, slot_flat, n_out):
    T, width = rows.shape
    n_workers = SC_CORES * SC_SUBCORES
    per_worker = T // n_workers
    steps = per_worker // SC_CHUNK

    @functools.partial(
        pl.kernel, mesh=_sc_mesh(),
        out_type=jax.ShapeDtypeStruct((n_out, width), rows.dtype),
        scratch_types=[pltpu.VMEM((SC_CHUNK, width), rows.dtype)]
        + [pltpu.VMEM((SC_CHUNK,), jnp.int32)] * TOP_K + [pltpu.SemaphoreType.DMA],
        name="dispatch",
    )
    def run(rows_hbm, slot_hbm, out_hbm, rows_v, *rest):
        idx_v, sem = rest[:TOP_K], rest[TOP_K]
        base = (lax.axis_index("s") * SC_CORES + lax.axis_index("c")) * per_worker

        @pl.loop(0, steps)
        def _(i):
            off = base + i * SC_CHUNK
            pltpu.sync_copy(rows_hbm.at[pl.ds(off, SC_CHUNK)], rows_v)
            for k in range(TOP_K):
                pltpu.sync_copy(slot_hbm.at[pl.ds(k * T + off, SC_CHUNK)], idx_v[k])
            copies = [pltpu.async_copy(rows_v, out_hbm.at[idx_v[k]], sem) for k in range(TOP_K)]
            for cp in copies:
                cp.wait()

    return run(rows, slot_flat)


def _sc_gather(rows, idx):
    n = idx.shape[0]
    width = rows.shape[1]
    n_workers = SC_CORES * SC_SUBCORES
    per_worker = n // n_workers
    steps = per_worker // SC_CHUNK

    @functools.partial(
        pl.kernel, mesh=_sc_mesh(),
        out_type=jax.ShapeDtypeStruct((n, width), rows.dtype),
        scratch_types=[pltpu.VMEM((SC_CHUNK,), jnp.int32), pltpu.VMEM((SC_CHUNK, width), rows.dtype),
                       pltpu.SemaphoreType.DMA],
        name="combine",
    )
    def run(rows_hbm, idx_hbm, out_hbm, idx_v, rows_v, sem):
        base = (lax.axis_index("s") * SC_CORES + lax.axis_index("c")) * per_worker

        @pl.loop(0, steps)
        def _(i):
            off = base + i * SC_CHUNK
            pltpu.sync_copy(idx_hbm.at[pl.ds(off, SC_CHUNK)], idx_v)
            pltpu.async_copy(rows_hbm.at[idx_v], rows_v, sem).wait()
            pltpu.sync_copy(rows_v, out_hbm.at[pl.ds(off, SC_CHUNK)])

    return run(rows, idx)


def _swiglu_packed(xp, wgu, wd):
    lo, hi = _unpack_bf16_pair(xp)
    half = xp.shape[-1]
    gu = _dot(lo, wgu[:half, :]) + _dot(hi, wgu[half:, :])
    dh = wgu.shape[-1] // 2
    act = _silu(gu[:, :dh]) * gu[:, dh:]
    return _dot(act.astype(BF16), wd)


def _experts_kernel(be_ref, nu_ref, x_ref, wgu_ref, wd_ref, y_ref):
    @pl.when(pl.program_id(0) < nu_ref[0])
    def _():
        y = _swiglu_packed(x_ref[...], wgu_ref[0], wd_ref[0])
        half = y.shape[-1] // 2
        y_ref[...] = _pack_bf16_pair(y[:, :half], y[:, half:])


def _experts(xs, block_expert, n_used, wgu_bf, wd_bf):
    P, half = xs.shape
    grid_spec = pltpu.PrefetchScalarGridSpec(
        num_scalar_prefetch=2,
        grid=(P // ROW_BLOCK,),
        in_specs=[pl.BlockSpec((ROW_BLOCK, half), lambda b, be, nu: (b, 0)),
                  pl.BlockSpec((1,) + wgu_bf.shape[1:], lambda b, be, nu: (be[b], 0, 0)),
                  pl.BlockSpec((1,) + wd_bf.shape[1:], lambda b, be, nu: (be[b], 0, 0))],
        out_specs=pl.BlockSpec((ROW_BLOCK, half), lambda b, be, nu: (b, 0)),
    )
    return pl.pallas_call(
        _experts_kernel,
        grid_spec=grid_spec,
        out_shape=jax.ShapeDtypeStruct((P, half), jnp.uint32),
        compiler_params=pltpu.CompilerParams(dimension_semantics=("arbitrary",), vmem_limit_bytes=VMEM_LIMIT),
        name="experts",
    )(block_expert, n_used, xs, wgu_bf, wd_bf)


def _final_kernel(x1_ref, hp_ref, g_ref, wk_ref, mod_ref, sgu_ref, sd_ref, o_ref):
    acc = _swiglu_packed(hp_ref[0], sgu_ref[...], sd_ref[...])
    wk = wk_ref[0]
    for k in range(TOP_K):
        lo, hi = _unpack_bf16_pair(g_ref[k, 0])
        y = jnp.concatenate([lo.astype(F32), hi.astype(F32)], axis=-1)
        acc = acc + wk[:, k:k + 1] * y
    gate2 = mod_ref[0, 5:6, :]
    o_ref[0] = x1_ref[0] + gate2 * acc


def _final(x1, hp, g, wk_tok, mod3, sgu_bf, sd_bf):
    B, S, D = x1.shape
    ts = TS_PROJ
    tok = lambda w: pl.BlockSpec((1, ts, w), lambda b, s: (b, s, 0))
    const2 = lambda b, s: (0, 0)
    return pl.pallas_call(
        _final_kernel,
        grid=(B, S // ts),
        in_specs=[tok(D), tok(D // 2),
                  pl.BlockSpec((TOP_K, 1, ts, D // 2), lambda b, s: (0, b, s, 0)),
                  tok(LANES),
                  pl.BlockSpec((1, N_MOD, D), lambda b, s: (b, 0, 0)),
                  pl.BlockSpec(sgu_bf.shape, const2),
                  pl.BlockSpec(sd_bf.shape, const2)],
        out_specs=tok(D),
        out_shape=jax.ShapeDtypeStruct((B, S, D), F32),
        compiler_params=pltpu.CompilerParams(dimension_semantics=("arbitrary", "arbitrary"),
                                             vmem_limit_bytes=VMEM_LIMIT),
        name="final",
    )(x1, hp, g, wk_tok, mod3, sgu_bf, sd_bf)


def _layer(x, c_act_mod, norm1_g, norm2_g, w_in, q_norm_g, k_norm_g, w_pool, pool_scale, w_out,
           w_router, router_bias, w_gate, w_up, w_down, ws_gate, ws_up, ws_down):
    B, S, D = x.shape
    mod3 = c_act_mod.reshape(B, N_MOD, D)
    head_of = jnp.arange(D_ATTN, dtype=jnp.int32) // HEAD_DIM
    head_mean = jnp.where(head_of[:, None] == head_of[None, :], 1.0 / HEAD_DIM, 0.0).astype(BF16)
    j = jnp.arange(TK, dtype=jnp.int32)
    umat = jnp.where(j[:, None] >= j[None, :], -1.0, 0.0).astype(BF16)

    q, k, v, pool = _inproj(
        x, mod3, norm1_g.reshape(1, D), w_in.astype(BF16),
        jnp.tile(q_norm_g, N_HEADS).reshape(1, D_ATTN), jnp.tile(k_norm_g, N_HEADS).reshape(1, D_ATTN),
        head_mean, w_pool.astype(BF16), pool_scale.reshape(1, D_POOL))
    attn = _attention(q, k, v, umat)
    t = jnp.arange(TS_PROJ, dtype=jnp.int32)
    ut = (t[:, None] < t[None, :]).astype(BF16)
    e = jnp.arange(N_EXPERTS, dtype=jnp.int32)
    lt = (e[None, :] < e[:, None]).astype(BF16)
    x1, hp, rank_t, wd_t, cnt = _outproj(x, attn, pool, mod3, w_out.astype(BF16), norm2_g.reshape(1, D),
                                         w_router.T, router_bias.reshape(N_EXPERTS, 1), ut)
    T = B * S
    slots, wk_tok, block_expert, n_used = _slots(rank_t, wd_t, cnt, lt)
    n_blocks = _n_row_blocks(T)
    slot_flat = slots[:TOP_K].reshape(TOP_K * T)
    xs = _sc_dispatch(hp.reshape(T, D // 2), slot_flat, n_blocks * ROW_BLOCK)
    wgu = jnp.concatenate([w_gate, w_up], axis=-1).astype(BF16)
    ys = _experts(xs, block_expert[0, :n_blocks], n_used[0, :1], wgu, w_down.astype(BF16))
    g = _sc_gather(ys, slot_flat).reshape(TOP_K, B, S, D // 2)
    sgu = jnp.concatenate([ws_gate, ws_up], axis=-1).astype(BF16)
    return _final(x1, hp, g, wk_tok.reshape(B, S, LANES), mod3, sgu, ws_down.astype(BF16))


def kernel(x, c, w_ada, b_ada, norm1_g, norm2_g, w_in, q_norm_g, k_norm_g, w_pool, pool_scale, w_out,
           w_router, router_bias, w_gate, w_up, w_down, ws_gate, ws_up, ws_down):
    depth = w_ada.shape[0]
    for l in range(depth):
        mod = _adaln(c, w_ada[l], b_ada[l])
        x = _layer(x, mod, norm1_g[l], norm2_g[l], w_in[l], q_norm_g[l], k_norm_g[l], w_pool[l],
                   pool_scale[l], w_out[l], w_router[l], router_bias[l], w_gate[l], w_up[l], w_down[l],
                   ws_gate[l], ws_up[l], ws_down[l])
    return x
```

```python
import functools

import jax
import jax.numpy as jnp
from jax import lax
from jax.experimental import pallas as pl
from jax.experimental.pallas import tpu as pltpu
from jax.experimental.pallas import tpu_sc as plsc

F32 = jnp.float32
BF16 = jnp.bfloat16

HEAD_DIM = 64
N_HEADS = 8
D_ATTN = N_HEADS * HEAD_DIM
POOL_WINDOWS = (2, 4, 8, 16)
POOL_GROUP_DIM = 128
D_POOL = len(POOL_WINDOWS) * POOL_GROUP_DIM
MAX_WINDOW = max(POOL_WINDOWS)
N_EXPERTS = 64
TOP_K = 6
N_GROUPS = 8
GROUP_SIZE = N_EXPERTS // N_GROUPS
TOPK_GROUPS = 4
ROUTED_SCALE = 2.5
RMS_EPS = 1e-6
N_MOD = 6

LANES = 128
SUBLANES = 8
VMEM_LIMIT = 56 * 1024 * 1024

TS_PROJ = 512
TQ = 1024
TK = 256
TS_SLOT = 2048
ROW_BLOCK_LOG2 = 9
ROW_BLOCK = 1 << ROW_BLOCK_LOG2
SC_CORES = 2
SC_SUBCORES = 16
SC_CHUNK = 64


def _split_bf16(a):
    hi = a.astype(BF16)
    lo = (a - hi.astype(F32)).astype(BF16)
    return hi, lo


def _dot(a, b):
    return jnp.dot(a, b, preferred_element_type=F32)


def _dot_nt(a, b):
    return lax.dot_general(a, b, (((1,), (1,)), ((), ())), preferred_element_type=F32)


def _dot3(a, b):
    ah, al = _split_bf16(a)
    bh, bl = _split_bf16(b)
    return _dot(ah, bh) + _dot(ah, bl) + _dot(al, bh)


def _dot3_nt(a, b):
    ah, al = _split_bf16(a)
    bh, bl = _split_bf16(b)
    return _dot_nt(ah, bh) + _dot_nt(ah, bl) + _dot_nt(al, bh)


def _silu(x):
    return x * (1.0 / (1.0 + jnp.exp(-x)))


def _rms_mod(x, gain, scale, shift):
    ms = jnp.mean(x * x, axis=-1, keepdims=True)
    y = x * lax.rsqrt(ms + RMS_EPS) * gain
    return y * (1.0 + scale) + shift


def _adaln_kernel(c_ref, w_ref, b_ref, o_ref):
    c = c_ref[...]
    o_ref[...] = _dot3(_silu(c), w_ref[...]) + b_ref[...]


def _adaln(c, w_ada, b_ada):
    nb, D = c.shape
    B = -(-nb // SUBLANES) * SUBLANES
    c = jnp.pad(c, ((0, B - nb), (0, 0)))
    N = w_ada.shape[1]
    tn = 1024
    out = pl.pallas_call(
        _adaln_kernel,
        grid=(N // tn,),
        in_specs=[pl.BlockSpec((B, D), lambda j: (0, 0)),
                  pl.BlockSpec((D, tn), lambda j: (0, j)),
                  pl.BlockSpec((1, tn), lambda j: (0, j))],
        out_specs=pl.BlockSpec((B, tn), lambda j: (0, j)),
        out_shape=jax.ShapeDtypeStruct((B, N), F32),
        compiler_params=pltpu.CompilerParams(dimension_semantics=("arbitrary",),
                                             vmem_limit_bytes=VMEM_LIMIT),
        name="adaln",
    )(c, w_ada, b_ada.reshape(1, N))
    return out[:nb]


def _first_grid_step():
    return jnp.logical_and(pl.program_id(0) == 0, pl.program_id(1) == 0)


def _inproj_kernel(x_ref, mod_ref, g1_ref, win32_ref, qg_ref, kg_ref, hm_ref, wp32_ref, ps_ref,
                   q_ref, k_ref, v_ref, p_ref, ext_ref, win_ref, wp_ref):
    @pl.when(_first_grid_step())
    def _():
        win_ref[...] = win32_ref[...].astype(BF16)
        wp_ref[...] = wp32_ref[...].astype(BF16)

    si = pl.program_id(1)
    ts = x_ref.shape[1]
    x = x_ref[0]
    shift1 = mod_ref[0, 0:1, :]
    scale1 = mod_ref[0, 1:2, :]
    h = _rms_mod(x, g1_ref[...], scale1, shift1)
    proj = _dot(h.astype(BF16), win_ref[...])

    hm = hm_ref[...]

    def head_norm(t, gain):
        hi, lo = _split_bf16(t * t)
        ms = _dot(hi, hm) + _dot(lo, hm)
        return t * lax.rsqrt(ms + RMS_EPS) * gain

    hq = proj[:, 0:D_ATTN]
    hk = proj[:, D_ATTN:2 * D_ATTN]
    q_ref[0] = (head_norm(hq, qg_ref[...]) * (HEAD_DIM ** -0.5)).astype(BF16)
    k_ref[0] = head_norm(hk, kg_ref[...]).astype(BF16)
    v_ref[0] = proj[:, 2 * D_ATTN:3 * D_ATTN].astype(BF16)

    hp = proj[:, 3 * D_ATTN:]

    @pl.when(si == 0)
    def _():
        ext_ref[0:MAX_WINDOW, :] = jnp.zeros((MAX_WINDOW, D_POOL), F32)

    ext_ref[MAX_WINDOW:, :] = hp
    pos = si * ts + lax.broadcasted_iota(jnp.int32, (ts, 1), 0)
    for g, w in enumerate(POOL_WINDOWS):
        lo_l, hi_l = g * POOL_GROUP_DIM, (g + 1) * POOL_GROUP_DIM
        u = hp[:, lo_l:hi_l]
        acc = u
        for i in range(1, w):
            acc = acc + ext_ref[pl.ds(MAX_WINDOW - i, ts), lo_l:hi_l]
        count = jnp.minimum(pos + 1, w).astype(F32)
        d = acc / count - u
        mixed = _dot(d.astype(BF16), wp_ref[g])
        p_ref[0, :, lo_l:hi_l] = (mixed * ps_ref[:, lo_l:hi_l]).astype(BF16)
    ext_ref[0:MAX_WINDOW, :] = hp[ts - MAX_WINDOW:, :]


def _inproj(x, mod3, g1, w_in, qg_t, kg_t, head_mean, w_pool, pool_scale):
    B, S, D = x.shape
    ts = TS_PROJ
    out_sd = jax.ShapeDtypeStruct((B, S, D_ATTN), BF16)
    blk = pl.BlockSpec((1, ts, D_ATTN), lambda b, s: (b, s, 0))
    const2 = lambda b, s: (0, 0)
    return pl.pallas_call(
        _inproj_kernel,
        grid=(B, S // ts),
        in_specs=[pl.BlockSpec((1, ts, D), lambda b, s: (b, s, 0)),
                  pl.BlockSpec((1, N_MOD, D), lambda b, s: (b, 0, 0)),
                  pl.BlockSpec((1, D), const2),
                  pl.BlockSpec(w_in.shape, const2),
                  pl.BlockSpec((1, D_ATTN), const2),
                  pl.BlockSpec((1, D_ATTN), const2),
                  pl.BlockSpec((D_ATTN, D_ATTN), const2),
                  pl.BlockSpec(w_pool.shape, lambda b, s: (0, 0, 0)),
                  pl.BlockSpec((1, D_POOL), const2)],
        out_specs=[blk, blk, blk, blk],
        out_shape=[out_sd, out_sd, out_sd, out_sd],
        scratch_shapes=[pltpu.VMEM((MAX_WINDOW + ts, D_POOL), F32),
                        pltpu.VMEM(w_in.shape, BF16), pltpu.VMEM(w_pool.shape, BF16)],
        compiler_params=pltpu.CompilerParams(dimension_semantics=("arbitrary", "arbitrary"),
                                             vmem_limit_bytes=VMEM_LIMIT),
        name="inproj",
    )(x, mod3, g1, w_in, qg_t, kg_t, head_mean, w_pool, pool_scale)


def _attn_kernel(q_ref, k_ref, v_ref, u_ref, o_ref, acc_ref):
    qi = pl.program_id(2)
    q = q_ref[0]
    lane = lax.broadcasted_iota(jnp.int32, (TQ, LANES), 1)
    first = lane < HEAD_DIM
    zero = jnp.zeros_like(q)
    qh = (jnp.where(first, q, zero), jnp.where(first, zero, q))
    u = u_ref[...]
    row = lax.broadcasted_iota(jnp.int32, (TQ, TK), 0)
    col = lax.broadcasted_iota(jnp.int32, (TQ, TK), 1)
    acc_ref[...] = jnp.zeros(acc_ref.shape, F32)

    def block(kb, survs, diag):
        start = pl.multiple_of(kb * TK, TK)
        k = k_ref[0, pl.ds(start, TK), :]
        v = v_ref[0, pl.ds(start, TK), :]
        if diag is not None:
            valid = col + diag * TK < row
        out = []
        for h in range(2):
            z = _dot_nt(qh[h], k)
            neg_abs = pltpu.bitcast(pltpu.bitcast(z, jnp.uint32) | jnp.uint32(0x80000000), F32)
            sp = jnp.maximum(z, 0.0) + jnp.log(1.0 + jnp.exp(neg_abs))
            if diag is not None:
                sp = jnp.where(valid, sp, 0.0)
            r = _dot(sp.astype(BF16), u)
            arg = z + r + survs[h]
            if diag is not None:
                arg = jnp.where(valid, arg, -jnp.inf)
            acc_ref[h] += _dot(jnp.exp(arg).astype(BF16), v)
            out.append(survs[h] + r[:, 0:1])
        return tuple(out)

    survs = (jnp.zeros((TQ, 1), F32), jnp.zeros((TQ, 1), F32))
    n_diag = TQ // TK
    for d in reversed(range(n_diag)):
        survs = block(qi * n_diag + d, survs, d)
    def full_blocks(i, c):
        for d in range(n_diag):
            c = block((qi - i) * n_diag - 1 - d, c, None)
        return c

    lax.fori_loop(0, qi, full_blocks, survs)
    o_ref[0] = jnp.where(first, acc_ref[0], acc_ref[1]).astype(BF16)


def _attention(q, k, v, umat):
    B, S, _ = q.shape
    n_pairs = D_ATTN // LANES
    kv_spec = pl.BlockSpec((1, S, LANES), lambda b, p, i: (b, 0, p))
    q_spec = pl.BlockSpec((1, TQ, LANES), lambda b, p, i: (b, i, p))
    return pl.pallas_call(
        _attn_kernel,
        grid=(B, n_pairs, S // TQ),
        in_specs=[q_spec, kv_spec, kv_spec, pl.BlockSpec((TK, TK), lambda b, p, i: (0, 0))],
        out_specs=q_spec,
        out_shape=jax.ShapeDtypeStruct((B, S, D_ATTN), BF16),
        scratch_shapes=[pltpu.VMEM((2, TQ, LANES), F32)],
        compiler_params=pltpu.CompilerParams(
            dimension_semantics=("arbitrary", "arbitrary", "arbitrary"),
            vmem_limit_bytes=VMEM_LIMIT),
        name="stickbreak_attn",
    )(q, k, v, umat)


def _route_t(scores, biased):
    ts = scores.shape[-1]
    neg = -jnp.inf
    b3 = biased.reshape(N_GROUPS, GROUP_SIZE, ts)
    e_in_g = lax.broadcasted_iota(jnp.int32, b3.shape, 1)
    m1 = jnp.max(b3, axis=1, keepdims=True)
    i1 = jnp.min(jnp.where(b3 == m1, e_in_g, GROUP_SIZE), axis=1, keepdims=True)
    m2 = jnp.max(jnp.where(e_in_g == i1, neg, b3), axis=1, keepdims=True)
    gs = (m1 + m2)[:, 0, :]
    g_iota = lax.broadcasted_iota(jnp.int32, gs.shape, 0)
    g_sel = jnp.zeros(gs.shape, jnp.bool_)
    for _ in range(TOPK_GROUPS):
        gm = jnp.max(gs, axis=0, keepdims=True)
        gi = jnp.min(jnp.where(gs == gm, g_iota, N_GROUPS), axis=0, keepdims=True)
        pick = g_iota == gi
        g_sel = jnp.logical_or(g_sel, pick)
        gs = jnp.where(pick, neg, gs)
    masked = jnp.where(g_sel[:, None, :], b3, neg)
    flat = lax.broadcasted_iota(jnp.int32, b3.shape, 0) * GROUP_SIZE + e_in_g
    sel = jnp.zeros(b3.shape, jnp.bool_)
    for _ in range(TOP_K):
        m = jnp.max(jnp.max(masked, axis=1, keepdims=True), axis=0, keepdims=True)
        cand = jnp.where(masked == m, flat, N_EXPERTS)
        idx = jnp.min(jnp.min(cand, axis=1, keepdims=True), axis=0, keepdims=True)
        pick = flat == idx
        sel = jnp.logical_or(sel, pick)
        masked = jnp.where(pick, neg, masked)
    s3 = scores.reshape(N_GROUPS, GROUP_SIZE, ts)
    w = jnp.where(sel, s3, 0.0)
    tot = jnp.sum(jnp.sum(w, axis=1, keepdims=True), axis=0, keepdims=True)
    return (w / tot * ROUTED_SCALE).reshape(N_EXPERTS, ts), sel.reshape(N_EXPERTS, ts)


def _pack_bf16_pair(lo, hi):
    lo_bits = pltpu.bitcast(lo.astype(BF16).astype(F32), jnp.uint32) >> 16
    hi_bits = pltpu.bitcast(hi.astype(BF16).astype(F32), jnp.uint32) & jnp.uint32(0xFFFF0000)
    return lo_bits | hi_bits


def _unpack_bf16_pair(p):
    lo = pltpu.bitcast(p << 16, F32).astype(BF16)
    hi = pltpu.bitcast(p & jnp.uint32(0xFFFF0000), F32).astype(BF16)
    return lo, hi


def _outproj_kernel(x_ref, a_ref, p_ref, mod_ref, wo32_ref, g2_ref, wrt_ref, rb_ref, ut_ref,
                    x1_ref, hp_ref, rank_ref, wd_ref, cnt_ref, run_ref, wo_ref):
    @pl.when(_first_grid_step())
    def _():
        run_ref[...] = jnp.zeros(run_ref.shape, jnp.int32)
        wo_ref[...] = wo32_ref[...].astype(BF16)

    x = x_ref[0]
    gate1 = mod_ref[0, 2:3, :]
    shift2 = mod_ref[0, 3:4, :]
    scale2 = mod_ref[0, 4:5, :]
    mixp = _dot(a_ref[0], wo_ref[0:D_ATTN, :]) + _dot(p_ref[0], wo_ref[D_ATTN:, :])
    x1 = x + gate1 * mixp
    x1_ref[0] = x1
    h2 = _rms_mod(x1, g2_ref[...], scale2, shift2)
    half = h2.shape[-1] // 2
    hp_ref[0] = _pack_bf16_pair(h2[:, :half], h2[:, half:])
    logits_t = _dot3_nt(wrt_ref[...], h2)
    scores = 1.0 / (1.0 + jnp.exp(-logits_t))
    wd_t, sel = _route_t(scores, scores + rb_ref[...])
    wd_ref[...] = wd_t
    self_f = jnp.where(sel, 1.0, 0.0)
    before = _dot(self_f.astype(BF16), ut_ref[...]).astype(jnp.int32)
    run = run_ref[:, 0:1]
    rank_ref[...] = jnp.where(sel, run + before, -1)
    run_ref[...] = run_ref[...] + jnp.sum(self_f, axis=1, keepdims=True).astype(jnp.int32)
    cnt_ref[...] = run_ref[...]


def _outproj(x, attn, pool, mod3, w_out, g2, wr_t, rbias, ut):
    B, S, D = x.shape
    ts = TS_PROJ
    n_s = S // ts
    const2 = lambda b, s: (0, 0)
    tok = lambda w: pl.BlockSpec((1, ts, w), lambda b, s: (b, s, 0))
    tok_t = pl.BlockSpec((N_EXPERTS, ts), lambda b, s: (0, b * n_s + s))
    return pl.pallas_call(
        _outproj_kernel,
        grid=(B, n_s),
        in_specs=[tok(D), tok(D_ATTN), tok(D_POOL),
                  pl.BlockSpec((1, N_MOD, D), lambda b, s: (b, 0, 0)),
                  pl.BlockSpec(w_out.shape, const2),
                  pl.BlockSpec((1, D), const2),
                  pl.BlockSpec(wr_t.shape, const2),
                  pl.BlockSpec((N_EXPERTS, 1), const2),
                  pl.BlockSpec((ts, ts), const2)],
        out_specs=[tok(D), tok(D // 2), tok_t, tok_t, pl.BlockSpec((N_EXPERTS, LANES), const2)],
        out_shape=[jax.ShapeDtypeStruct((B, S, D), F32),
                   jax.ShapeDtypeStruct((B, S, D // 2), jnp.uint32),
                   jax.ShapeDtypeStruct((N_EXPERTS, B * S), jnp.int32),
                   jax.ShapeDtypeStruct((N_EXPERTS, B * S), F32),
                   jax.ShapeDtypeStruct((N_EXPERTS, LANES), jnp.int32)],
        scratch_shapes=[pltpu.VMEM((N_EXPERTS, LANES), jnp.int32), pltpu.VMEM(w_out.shape, BF16)],
        compiler_params=pltpu.CompilerParams(dimension_semantics=("arbitrary", "arbitrary"),
                                             vmem_limit_bytes=VMEM_LIMIT),
        name="outproj_router",
    )(x, attn, pool, mod3, w_out, g2, wr_t, rbias, ut)


def _n_row_blocks(n_tokens):
    return -(-(n_tokens * TOP_K + N_EXPERTS * (ROW_BLOCK - 1)) // ROW_BLOCK)


def _slots_kernel(rank_ref, wd_ref, cnt_ref, lt_ref, slot_ref, wk_ref, be_ref, nu_ref):
    lt = lt_ref[...]
    nblk = lax.shift_right_logical(cnt_ref[...] + (ROW_BLOCK - 1), ROW_BLOCK_LOG2).astype(F32)
    nb_hi, nb_lo = _split_bf16(nblk)
    blk_start = _dot(lt, nb_hi) + _dot(lt, nb_lo)

    @pl.when(pl.program_id(0) == 0)
    def _():
        blk_end = (blk_start + nblk)[:, 0:1]
        b_iota = lax.broadcasted_iota(jnp.int32, (N_EXPERTS, be_ref.shape[-1]), 1).astype(F32)
        owner = jnp.sum(jnp.where(blk_end <= b_iota, 1, 0), axis=0, keepdims=True)
        be_ref[...] = jnp.minimum(owner, N_EXPERTS - 1)
        nu_ref[...] = jnp.broadcast_to(blk_end[N_EXPERTS - 1:, :].astype(jnp.int32), nu_ref.shape)

    rank = rank_ref[...]
    sel = rank >= 0
    row_start = (blk_start[:, 0:1] * ROW_BLOCK).astype(jnp.int32)
    slot_d = row_start + rank
    wd = wd_ref[...]
    choice = _dot(lt, jnp.where(sel, 1.0, 0.0).astype(BF16)).astype(jnp.int32)
    ts = rank.shape[-1]
    slots, wks = [], []
    for k in range(TOP_K):
        m = jnp.logical_and(sel, choice == k)
        slots.append(jnp.sum(jnp.where(m, slot_d, 0), axis=0, keepdims=True))
        wks.append(jnp.sum(jnp.where(m, wd, 0.0), axis=0, keepdims=True))
    slot_ref[...] = jnp.concatenate(slots + [jnp.zeros((SUBLANES - TOP_K, ts), jnp.int32)], axis=0)
    wk_pad = jnp.concatenate(wks + [jnp.zeros((LANES - TOP_K, ts), F32)], axis=0)
    wk_ref[...] = wk_pad.T


def _slots(rank_t, wd_t, cnt, lt):
    T = rank_t.shape[1]
    ts = TS_SLOT
    nb_pad = -(-_n_row_blocks(T) // LANES) * LANES
    const = lambda i: (0, 0)
    tok_t = pl.BlockSpec((N_EXPERTS, ts), lambda i: (0, i))
    return pl.pallas_call(
        _slots_kernel,
        grid=(T // ts,),
        in_specs=[tok_t, tok_t, pl.BlockSpec((N_EXPERTS, LANES), const), pl.BlockSpec((N_EXPERTS, N_EXPERTS), const)],
        out_specs=[pl.BlockSpec((SUBLANES, ts), lambda i: (0, i)),
                   pl.BlockSpec((ts, LANES), lambda i: (i, 0)),
                   pl.BlockSpec((1, nb_pad), const),
                   pl.BlockSpec((1, LANES), const)],
        out_shape=[jax.ShapeDtypeStruct((SUBLANES, T), jnp.int32),
                   jax.ShapeDtypeStruct((T, LANES), F32),
                   jax.ShapeDtypeStruct((1, nb_pad), jnp.int32),
                   jax.ShapeDtypeStruct((1, LANES), jnp.int32)],
        compiler_params=pltpu.CompilerParams(dimension_semantics=("arbitrary",), vmem_limit_bytes=VMEM_LIMIT),
        name="slots",
    )(rank_t, wd_t, cnt, lt)


def _sc_mesh():
    return plsc.VectorSubcoreMesh(core_axis_name="c", subcore_axis_name="s",
                                  num_cores=SC_CORES, num_subcores=SC_SUBCORES)


def _sc_dispatch(rows, slot_flat, n_out):
    T, width = rows.shape
    n_workers = SC_CORES * SC_SUBCORES
    per_worker = T // n_workers
    steps = per_worker // SC_CHUNK

    @functools.partial(
        pl.kernel, mesh=_sc_mesh(),
        out_type=jax.ShapeDtypeStruct((n_out, width), rows.dtype),
        scratch_types=[pltpu.VMEM((SC_CHUNK, width), rows.dtype)]
        + [pltpu.VMEM((SC_CHUNK,), jnp.int32)] * TOP_K + [pltpu.SemaphoreType.DMA],
        name="dispatch",
    )
    def run(rows_hbm, slot_hbm, out_hbm, rows_v, *rest):
        idx_v, sem = rest[:TOP_K], rest[TOP_K]
        base = (lax.axis_index("s") * SC_CORES + lax.axis_index("c")) * per_worker

        @pl.loop(0, steps)
        def _(i):
            off = base + i * SC_CHUNK
            pltpu.sync_copy(rows_hbm.at[pl.ds(off, SC_CHUNK)], rows_v)
            for k in range(TOP_K):
                pltpu.sync_copy(slot_hbm.at[pl.ds(k * T + off, SC_CHUNK)], idx_v[k])
            copies = [pltpu.async_copy(rows_v, out_hbm.at[idx_v[k]], sem) for k in range(TOP_K)]
            for cp in copies:
                cp.wait()

    return run(rows, slot_flat)


def _sc_gather(rows, idx):
    n = idx.shape[0]
    width = rows.shape[1]
    n_workers = SC_CORES * SC_SUBCORES
    per_worker = n // n_workers
    steps = per_worker // SC_CHUNK

    assert steps % 2 == 0
    slot_types = [pltpu.VMEM((SC_CHUNK,), jnp.int32), pltpu.VMEM((SC_CHUNK, width), rows.dtype),
                  pltpu.SemaphoreType.DMA]

    @functools.partial(
        pl.kernel, mesh=_sc_mesh(),
        out_type=jax.ShapeDtypeStruct((n, width), rows.dtype),
        scratch_types=slot_types * 2,
        name="combine",
    )
    def run(rows_hbm, idx_hbm, out_hbm, *scratch):
        base = (lax.axis_index("s") * SC_CORES + lax.axis_index("c")) * per_worker
        slots = (scratch[0:3], scratch[3:6])

        def gather(slot):
            idx_v, rows_v, sem = slots[slot]
            return pltpu.make_async_copy(rows_hbm.at[idx_v], rows_v, sem)

        def start(chunk, slot):
            pltpu.sync_copy(idx_hbm.at[pl.ds(base + chunk * SC_CHUNK, SC_CHUNK)], slots[slot][0])
            gather(slot).start()

        def finish(chunk, slot):
            gather(slot).wait()
            pltpu.sync_copy(slots[slot][1], out_hbm.at[pl.ds(base + chunk * SC_CHUNK, SC_CHUNK)])

        start(0, 0)

        @pl.loop(0, steps, step=2)
        def _(chunk):
            start(chunk + 1, 1)
            finish(chunk, 0)

            @pl.when(chunk + 2 < steps)
            def _():
                start(chunk + 2, 0)

            finish(chunk + 1, 1)

    return run(rows, idx)


def _swiglu_packed(xp, wgu, wd):
    lo, hi = _unpack_bf16_pair(xp)
    half = xp.shape[-1]
    gu = _dot(lo, wgu[:half, :]) + _dot(hi, wgu[half:, :])
    dh = wgu.shape[-1] // 2
    act = _silu(gu[:, :dh]) * gu[:, dh:]
    return _dot(act.astype(BF16), wd)


def _cast_swiglu_weights(wg32_ref, wu32_ref, wd32_ref, wgu_ref, wd_ref):
    dh = wg32_ref.shape[-1]
    wgu_ref[:, :dh] = wg32_ref[...].reshape(wg32_ref.shape[-2:]).astype(BF16)
    wgu_ref[:, dh:] = wu32_ref[...].reshape(wu32_ref.shape[-2:]).astype(BF16)
    wd_ref[...] = wd32_ref[...].reshape(wd32_ref.shape[-2:]).astype(BF16)


def _experts_kernel(be_ref, nu_ref, x_ref, wg32_ref, wu32_ref, wd32_ref, y_ref, wgu_ref, wd_ref):
    b = pl.program_id(0)
    new_expert = jnp.logical_or(b == 0, be_ref[b] != be_ref[jnp.maximum(b - 1, 0)])

    @pl.when(new_expert)
    def _():
        _cast_swiglu_weights(wg32_ref, wu32_ref, wd32_ref, wgu_ref, wd_ref)

    @pl.when(b < nu_ref[0])
    def _():
        y = _swiglu_packed(x_ref[...], wgu_ref[...], wd_ref[...])
        half = y.shape[-1] // 2
        y_ref[...] = _pack_bf16_pair(y[:, :half], y[:, half:])


def _experts(xs, block_expert, n_used, w_gate, w_up, w_down):
    P, half = xs.shape
    _, D, dh = w_gate.shape
    by_expert = lambda shape: pl.BlockSpec((1,) + shape, lambda b, be, nu: (be[b], 0, 0))
    grid_spec = pltpu.PrefetchScalarGridSpec(
        num_scalar_prefetch=2,
        grid=(P // ROW_BLOCK,),
        in_specs=[pl.BlockSpec((ROW_BLOCK, half), lambda b, be, nu: (b, 0)),
                  by_expert((D, dh)), by_expert((D, dh)), by_expert((dh, D))],
        out_specs=pl.BlockSpec((ROW_BLOCK, half), lambda b, be, nu: (b, 0)),
        scratch_shapes=[pltpu.VMEM((D, 2 * dh), BF16), pltpu.VMEM((dh, D), BF16)],
    )
    return pl.pallas_call(
        _experts_kernel,
        grid_spec=grid_spec,
        out_shape=jax.ShapeDtypeStruct((P, half), jnp.uint32),
        compiler_params=pltpu.CompilerParams(dimension_semantics=("arbitrary",), vmem_limit_bytes=VMEM_LIMIT),
        name="experts",
    )(block_expert, n_used, xs, w_gate, w_up, w_down)


def _final_kernel(x1_ref, hp_ref, g_ref, wk_ref, mod_ref, sg32_ref, su32_ref, sd32_ref, o_ref, sgu_ref, sd_ref):
    @pl.when(_first_grid_step())
    def _():
        _cast_swiglu_weights(sg32_ref, su32_ref, sd32_ref, sgu_ref, sd_ref)

    acc = _swiglu_packed(hp_ref[0], sgu_ref[...], sd_ref[...])
    wk = wk_ref[0]
    for k in range(TOP_K):
        lo, hi = _unpack_bf16_pair(g_ref[k, 0])
        y = jnp.concatenate([lo.astype(F32), hi.astype(F32)], axis=-1)
        acc = acc + wk[:, k:k + 1] * y
    gate2 = mod_ref[0, 5:6, :]
    o_ref[0] = x1_ref[0] + gate2 * acc


def _final(x1, hp, g, wk_tok, mod3, ws_gate, ws_up, ws_down):
    B, S, D = x1.shape
    dh = ws_gate.shape[-1]
    ts = TS_PROJ
    tok = lambda w: pl.BlockSpec((1, ts, w), lambda b, s: (b, s, 0))
    const2 = lambda b, s: (0, 0)
    return pl.pallas_call(
        _final_kernel,
        grid=(B, S // ts),
        in_specs=[tok(D), tok(D // 2),
                  pl.BlockSpec((TOP_K, 1, ts, D // 2), lambda b, s: (0, b, s, 0)),
                  tok(LANES),
                  pl.BlockSpec((1, N_MOD, D), lambda b, s: (b, 0, 0)),
                  pl.BlockSpec(ws_gate.shape, const2),
                  pl.BlockSpec(ws_up.shape, const2),
                  pl.BlockSpec(ws_down.shape, const2)],
        out_specs=tok(D),
        out_shape=jax.ShapeDtypeStruct((B, S, D), F32),
        scratch_shapes=[pltpu.VMEM((D, 2 * dh), BF16), pltpu.VMEM((dh, D), BF16)],
        compiler_params=pltpu.CompilerParams(dimension_semantics=("arbitrary", "arbitrary"),
                                             vmem_limit_bytes=VMEM_LIMIT),
        name="final",
    )(x1, hp, g, wk_tok, mod3, ws_gate, ws_up, ws_down)


def _layer(x, c_act_mod, norm1_g, norm2_g, w_in, q_norm_g, k_norm_g, w_pool, pool_scale, w_out,
           w_router, router_bias, w_gate, w_up, w_down, ws_gate, ws_up, ws_down):
    B, S, D = x.shape
    mod3 = c_act_mod.reshape(B, N_MOD, D)
    head_of = jnp.arange(D_ATTN, dtype=jnp.int32) // HEAD_DIM
    head_mean = jnp.where(head_of[:, None] == head_of[None, :], 1.0 / HEAD_DIM, 0.0).astype(BF16)
    j = jnp.arange(TK, dtype=jnp.int32)
    umat = jnp.where(j[:, None] >= j[None, :], -1.0, 0.0).astype(BF16)

    q, k, v, pool = _inproj(
        x, mod3, norm1_g.reshape(1, D), w_in,
        jnp.tile(q_norm_g, N_HEADS).reshape(1, D_ATTN), jnp.tile(k_norm_g, N_HEADS).reshape(1, D_ATTN),
        head_mean, w_pool, pool_scale.reshape(1, D_POOL))
    attn = _attention(q, k, v, umat)
    t = jnp.arange(TS_PROJ, dtype=jnp.int32)
    ut = (t[:, None] < t[None, :]).astype(BF16)
    e = jnp.arange(N_EXPERTS, dtype=jnp.int32)
    lt = (e[None, :] < e[:, None]).astype(BF16)
    x1, hp, rank_t, wd_t, cnt = _outproj(x, attn, pool, mod3, w_out, norm2_g.reshape(1, D),
                                         w_router.T, router_bias.reshape(N_EXPERTS, 1), ut)
    T = B * S
    slots, wk_tok, block_expert, n_used = _slots(rank_t, wd_t, cnt, lt)
    n_blocks = _n_row_blocks(T)
    slot_flat = slots[:TOP_K].reshape(TOP_K * T)
    xs = _sc_dispatch(hp.reshape(T, D // 2), slot_flat, n_blocks * ROW_BLOCK)
    ys = _experts(xs, block_expert[0, :n_blocks], n_used[0, :1], w_gate, w_up, w_down)
    g = _sc_gather(ys, slot_flat).reshape(TOP_K, B, S, D // 2)
    return _final(x1, hp, g, wk_tok.reshape(B, S, LANES), mod3, ws_gate, ws_up, ws_down)


def kernel(x, c, w_ada, b_ada, norm1_g, norm2_g, w_in, q_norm_g, k_norm_g, w_pool, pool_scale, w_out,
           w_router, router_bias, w_gate, w_up, w_down, ws_gate, ws_up, ws_down):
    depth = w_ada.shape[0]
    for l in range(depth):
        mod = _adaln(c, w_ada[l], b_ada[l])
        x = _layer(x, mod, norm1_g[l], norm2_g[l], w_in[l], q_norm_g[l], k_norm_g[l], w_pool[l],
                   pool_scale[l], w_out[l], w_router[l], router_bias[l], w_gate[l], w_up[l], w_down[l],
                   ws_gate[l], ws_up[l], ws_down[l])
    return x
```

```python
import functools

import jax
import jax.numpy as jnp
from jax import lax
from jax.experimental import pallas as pl
from jax.experimental.pallas import tpu as pltpu
from jax.experimental.pallas import tpu_sc as plsc

F32 = jnp.float32
BF16 = jnp.bfloat16

HEAD_DIM = 64
N_HEADS = 8
D_ATTN = N_HEADS * HEAD_DIM
POOL_WINDOWS = (2, 4, 8, 16)
POOL_GROUP_DIM = 128
D_POOL = len(POOL_WINDOWS) * POOL_GROUP_DIM
MAX_WINDOW = max(POOL_WINDOWS)
N_EXPERTS = 64
TOP_K = 6
N_GROUPS = 8
GROUP_SIZE = N_EXPERTS // N_GROUPS
TOPK_GROUPS = 4
ROUTED_SCALE = 2.5
RMS_EPS = 1e-6
N_MOD = 6

LANES = 128
SUBLANES = 8
VMEM_LIMIT = 56 * 1024 * 1024

TS_PROJ = 512
TQ = 1024
TK = 256
TS_SLOT = 2048
ROW_BLOCK_LOG2 = 9
ROW_BLOCK = 1 << ROW_BLOCK_LOG2
SC_CORES = 2
SC_SUBCORES = 16
SC_CHUNK = 64


def _split_bf16(a):
    hi = a.astype(BF16)
    lo = (a - hi.astype(F32)).astype(BF16)
    return hi, lo


def _dot(a, b):
    return jnp.dot(a, b, preferred_element_type=F32)


def _dot_nt(a, b):
    return lax.dot_general(a, b, (((1,), (1,)), ((), ())), preferred_element_type=F32)


def _dot3(a, b):
    ah, al = _split_bf16(a)
    bh, bl = _split_bf16(b)
    return _dot(ah, bh) + _dot(ah, bl) + _dot(al, bh)


def _dot3_nt(a, b):
    ah, al = _split_bf16(a)
    bh, bl = _split_bf16(b)
    return _dot_nt(ah, bh) + _dot_nt(ah, bl) + _dot_nt(al, bh)


def _silu(x):
    return x * (1.0 / (1.0 + jnp.exp(-x)))


def _rms_mod(x, gain, scale, shift):
    ms = jnp.mean(x * x, axis=-1, keepdims=True)
    y = x * lax.rsqrt(ms + RMS_EPS) * gain
    return y * (1.0 + scale) + shift


def _adaln_kernel(c_ref, w_ref, b_ref, o_ref):
    c = c_ref[...]
    o_ref[...] = _dot3(_silu(c), w_ref[...]) + b_ref[...]


def _adaln(c, w_ada, b_ada):
    nb, D = c.shape
    B = -(-nb // SUBLANES) * SUBLANES
    c = jnp.pad(c, ((0, B - nb), (0, 0)))
    N = w_ada.shape[1]
    tn = 1024
    out = pl.pallas_call(
        _adaln_kernel,
        grid=(N // tn,),
        in_specs=[pl.BlockSpec((B, D), lambda j: (0, 0)),
                  pl.BlockSpec((D, tn), lambda j: (0, j)),
                  pl.BlockSpec((1, tn), lambda j: (0, j))],
        out_specs=pl.BlockSpec((B, tn), lambda j: (0, j)),
        out_shape=jax.ShapeDtypeStruct((B, N), F32),
        compiler_params=pltpu.CompilerParams(dimension_semantics=("arbitrary",),
                                             vmem_limit_bytes=VMEM_LIMIT),
        name="adaln",
    )(c, w_ada, b_ada.reshape(1, N))
    return out[:nb]


def _first_grid_step():
    return jnp.logical_and(pl.program_id(0) == 0, pl.program_id(1) == 0)


def _inproj_kernel(x_ref, mod_ref, g1_ref, win32_ref, qg_ref, kg_ref, hm_ref, wp32_ref, ps_ref,
                   q_ref, k_ref, v_ref, p_ref, ext_ref, win_ref, wp_ref):
    @pl.when(_first_grid_step())
    def _():
        win_ref[...] = win32_ref[...].astype(BF16)
        wp_ref[...] = wp32_ref[...].astype(BF16)

    si = pl.program_id(1)
    ts = x_ref.shape[1]
    x = x_ref[0]
    shift1 = mod_ref[0, 0:1, :]
    scale1 = mod_ref[0, 1:2, :]
    h = _rms_mod(x, g1_ref[...], scale1, shift1)
    proj = _dot(h.astype(BF16), win_ref[...])

    hm = hm_ref[...]

    def head_norm(t, gain):
        hi, lo = _split_bf16(t * t)
        ms = _dot(hi, hm) + _dot(lo, hm)
        return t * lax.rsqrt(ms + RMS_EPS) * gain

    hq = proj[:, 0:D_ATTN]
    hk = proj[:, D_ATTN:2 * D_ATTN]
    q_ref[0] = (head_norm(hq, qg_ref[...]) * (HEAD_DIM ** -0.5)).astype(BF16)
    k_ref[0] = head_norm(hk, kg_ref[...]).astype(BF16)
    v_ref[0] = proj[:, 2 * D_ATTN:3 * D_ATTN].astype(BF16)

    hp = proj[:, 3 * D_ATTN:]

    @pl.when(si == 0)
    def _():
        ext_ref[0:MAX_WINDOW, :] = jnp.zeros((MAX_WINDOW, D_POOL), F32)

    ext_ref[MAX_WINDOW:, :] = hp
    pos = si * ts + lax.broadcasted_iota(jnp.int32, (ts, 1), 0)
    for g, w in enumerate(POOL_WINDOWS):
        lo_l, hi_l = g * POOL_GROUP_DIM, (g + 1) * POOL_GROUP_DIM
        u = hp[:, lo_l:hi_l]
        acc = u
        for i in range(1, w):
            acc = acc + ext_ref[pl.ds(MAX_WINDOW - i, ts), lo_l:hi_l]
        count = jnp.minimum(pos + 1, w).astype(F32)
        d = acc / count - u
        mixed = _dot(d.astype(BF16), wp_ref[g])
        p_ref[0, :, lo_l:hi_l] = (mixed * ps_ref[:, lo_l:hi_l]).astype(BF16)
    ext_ref[0:MAX_WINDOW, :] = hp[ts - MAX_WINDOW:, :]


def _inproj(x, mod3, g1, w_in, qg_t, kg_t, head_mean, w_pool, pool_scale):
    B, S, D = x.shape
    ts = TS_PROJ
    out_sd = jax.ShapeDtypeStruct((B, S, D_ATTN), BF16)
    blk = pl.BlockSpec((1, ts, D_ATTN), lambda b, s: (b, s, 0))
    const2 = lambda b, s: (0, 0)
    return pl.pallas_call(
        _inproj_kernel,
        grid=(B, S // ts),
        in_specs=[pl.BlockSpec((1, ts, D), lambda b, s: (b, s, 0)),
                  pl.BlockSpec((1, N_MOD, D), lambda b, s: (b, 0, 0)),
                  pl.BlockSpec((1, D), const2),
                  pl.BlockSpec(w_in.shape, const2),
                  pl.BlockSpec((1, D_ATTN), const2),
                  pl.BlockSpec((1, D_ATTN), const2),
                  pl.BlockSpec((D_ATTN, D_ATTN), const2),
                  pl.BlockSpec(w_pool.shape, lambda b, s: (0, 0, 0)),
                  pl.BlockSpec((1, D_POOL), const2)],
        out_specs=[blk, blk, blk, blk],
        out_shape=[out_sd, out_sd, out_sd, out_sd],
        scratch_shapes=[pltpu.VMEM((MAX_WINDOW + ts, D_POOL), F32),
                        pltpu.VMEM(w_in.shape, BF16), pltpu.VMEM(w_pool.shape, BF16)],
        compiler_params=pltpu.CompilerParams(dimension_semantics=("arbitrary", "arbitrary"),
                                             vmem_limit_bytes=VMEM_LIMIT),
        name="inproj",
    )(x, mod3, g1, w_in, qg_t, kg_t, head_mean, w_pool, pool_scale)


def _attn_kernel(q_ref, k_ref, v_ref, u_ref, o_ref, acc_ref):
    qi = pl.program_id(2)
    q = q_ref[0]
    lane = lax.broadcasted_iota(jnp.int32, (TQ, LANES), 1)
    first = lane < HEAD_DIM
    zero = jnp.zeros_like(q)
    qh = (jnp.where(first, q, zero), jnp.where(first, zero, q))
    u = u_ref[...]
    row = lax.broadcasted_iota(jnp.int32, (TQ, TK), 0)
    col = lax.broadcasted_iota(jnp.int32, (TQ, TK), 1)
    acc_ref[...] = jnp.zeros(acc_ref.shape, F32)

    def block(kb, survs, diag):
        start = pl.multiple_of(kb * TK, TK)
        k = k_ref[0, pl.ds(start, TK), :]
        v = v_ref[0, pl.ds(start, TK), :]
        if diag is not None:
            valid = col + diag * TK < row
        out = []
        for h in range(2):
            z = _dot_nt(qh[h], k)
            zb = z.astype(BF16)
            sp = jnp.maximum(zb, 0) + jnp.log(1 + jnp.exp(-jnp.abs(zb)))
            if diag is not None:
                sp = jnp.where(valid, sp, jnp.zeros_like(sp))
            r = _dot(sp, u)
            arg = z + r + survs[h]
            if diag is not None:
                arg = jnp.where(valid, arg, -jnp.inf)
            acc_ref[h] += _dot(jnp.exp(arg).astype(BF16), v)
            out.append(survs[h] + r[:, 0:1])
        return tuple(out)

    survs = (jnp.zeros((TQ, 1), F32), jnp.zeros((TQ, 1), F32))
    n_diag = TQ // TK
    for d in reversed(range(n_diag)):
        survs = block(qi * n_diag + d, survs, d)
    def full_blocks(i, c):
        for d in range(n_diag):
            c = block((qi - i) * n_diag - 1 - d, c, None)
        return c

    lax.fori_loop(0, qi, full_blocks, survs)
    o_ref[0] = jnp.where(first, acc_ref[0], acc_ref[1]).astype(BF16)


def _attention(q, k, v, umat):
    B, S, _ = q.shape
    n_pairs = D_ATTN // LANES
    kv_spec = pl.BlockSpec((1, S, LANES), lambda b, p, i: (b, 0, p))
    q_spec = pl.BlockSpec((1, TQ, LANES), lambda b, p, i: (b, i, p))
    return pl.pallas_call(
        _attn_kernel,
        grid=(B, n_pairs, S // TQ),
        in_specs=[q_spec, kv_spec, kv_spec, pl.BlockSpec((TK, TK), lambda b, p, i: (0, 0))],
        out_specs=q_spec,
        out_shape=jax.ShapeDtypeStruct((B, S, D_ATTN), BF16),
        scratch_shapes=[pltpu.VMEM((2, TQ, LANES), F32)],
        compiler_params=pltpu.CompilerParams(
            dimension_semantics=("arbitrary", "arbitrary", "arbitrary"),
            vmem_limit_bytes=VMEM_LIMIT),
        name="stickbreak_attn",
    )(q, k, v, umat)


def _route_t(scores, biased):
    ts = scores.shape[-1]
    neg = -jnp.inf
    b3 = biased.reshape(N_GROUPS, GROUP_SIZE, ts)
    e_in_g = lax.broadcasted_iota(jnp.int32, b3.shape, 1)
    m1 = jnp.max(b3, axis=1, keepdims=True)
    i1 = jnp.min(jnp.where(b3 == m1, e_in_g, GROUP_SIZE), axis=1, keepdims=True)
    m2 = jnp.max(jnp.where(e_in_g == i1, neg, b3), axis=1, keepdims=True)
    gs = (m1 + m2)[:, 0, :]
    g_iota = lax.broadcasted_iota(jnp.int32, gs.shape, 0)
    g_sel = jnp.zeros(gs.shape, jnp.bool_)
    for _ in range(TOPK_GROUPS):
        gm = jnp.max(gs, axis=0, keepdims=True)
        gi = jnp.min(jnp.where(gs == gm, g_iota, N_GROUPS), axis=0, keepdims=True)
        pick = g_iota == gi
        g_sel = jnp.logical_or(g_sel, pick)
        gs = jnp.where(pick, neg, gs)
    masked = jnp.where(g_sel[:, None, :], b3, neg)
    flat = lax.broadcasted_iota(jnp.int32, b3.shape, 0) * GROUP_SIZE + e_in_g
    sel = jnp.zeros(b3.shape, jnp.bool_)
    for _ in range(TOP_K):
        m = jnp.max(jnp.max(masked, axis=1, keepdims=True), axis=0, keepdims=True)
        cand = jnp.where(masked == m, flat, N_EXPERTS)
        idx = jnp.min(jnp.min(cand, axis=1, keepdims=True), axis=0, keepdims=True)
        pick = flat == idx
        sel = jnp.logical_or(sel, pick)
        masked = jnp.where(pick, neg, masked)
    s3 = scores.reshape(N_GROUPS, GROUP_SIZE, ts)
    w = jnp.where(sel, s3, 0.0)
    tot = jnp.sum(jnp.sum(w, axis=1, keepdims=True), axis=0, keepdims=True)
    return (w / tot * ROUTED_SCALE).reshape(N_EXPERTS, ts), sel.reshape(N_EXPERTS, ts)


def _pack_bf16_pair(lo, hi):
    lo_bits = pltpu.bitcast(lo.astype(BF16).astype(F32), jnp.uint32) >> 16
    hi_bits = pltpu.bitcast(hi.astype(BF16).astype(F32), jnp.uint32) & jnp.uint32(0xFFFF0000)
    return lo_bits | hi_bits


def _unpack_bf16_pair(p):
    lo = pltpu.bitcast(p << 16, F32).astype(BF16)
    hi = pltpu.bitcast(p & jnp.uint32(0xFFFF0000), F32).astype(BF16)
    return lo, hi


def _outproj_kernel(x_ref, a_ref, p_ref, mod_ref, wo32_ref, g2_ref, wrt_ref, rb_ref, ut_ref,
                    x1_ref, hp_ref, rank_ref, wd_ref, cnt_ref, run_ref, wo_ref):
    @pl.when(_first_grid_step())
    def _():
        run_ref[...] = jnp.zeros(run_ref.shape, jnp.int32)
        wo_ref[...] = wo32_ref[...].astype(BF16)

    x = x_ref[0]
    gate1 = mod_ref[0, 2:3, :]
    shift2 = mod_ref[0, 3:4, :]
    scale2 = mod_ref[0, 4:5, :]
    mixp = _dot(a_ref[0], wo_ref[0:D_ATTN, :]) + _dot(p_ref[0], wo_ref[D_ATTN:, :])
    x1 = x + gate1 * mixp
    x1_ref[0] = x1
    h2 = _rms_mod(x1, g2_ref[...], scale2, shift2)
    half = h2.shape[-1] // 2
    hp_ref[0] = _pack_bf16_pair(h2[:, :half], h2[:, half:])
    logits_t = _dot3_nt(wrt_ref[...], h2)
    scores = 1.0 / (1.0 + jnp.exp(-logits_t))
    wd_t, sel = _route_t(scores, scores + rb_ref[...])
    wd_ref[...] = wd_t
    self_f = jnp.where(sel, 1.0, 0.0)
    before = _dot(self_f.astype(BF16), ut_ref[...]).astype(jnp.int32)
    run = run_ref[:, 0:1]
    rank_ref[...] = jnp.where(sel, run + before, -1)
    run_ref[...] = run_ref[...] + jnp.sum(self_f, axis=1, keepdims=True).astype(jnp.int32)
    cnt_ref[...] = run_ref[...]


def _outproj(x, attn, pool, mod3, w_out, g2, wr_t, rbias, ut):
    B, S, D = x.shape
    ts = TS_PROJ
    n_s = S // ts
    const2 = lambda b, s: (0, 0)
    tok = lambda w: pl.BlockSpec((1, ts, w), lambda b, s: (b, s, 0))
    tok_t = pl.BlockSpec((N_EXPERTS, ts), lambda b, s: (0, b * n_s + s))
    return pl.pallas_call(
        _outproj_kernel,
        grid=(B, n_s),
        in_specs=[tok(D), tok(D_ATTN), tok(D_POOL),
                  pl.BlockSpec((1, N_MOD, D), lambda b, s: (b, 0, 0)),
                  pl.BlockSpec(w_out.shape, const2),
                  pl.BlockSpec((1, D), const2),
                  pl.BlockSpec(wr_t.shape, const2),
                  pl.BlockSpec((N_EXPERTS, 1), const2),
                  pl.BlockSpec((ts, ts), const2)],
        out_specs=[tok(D), tok(D // 2), tok_t, tok_t, pl.BlockSpec((N_EXPERTS, LANES), const2)],
        out_shape=[jax.ShapeDtypeStruct((B, S, D), F32),
                   jax.ShapeDtypeStruct((B, S, D // 2), jnp.uint32),
                   jax.ShapeDtypeStruct((N_EXPERTS, B * S), jnp.int32),
                   jax.ShapeDtypeStruct((N_EXPERTS, B * S), F32),
                   jax.ShapeDtypeStruct((N_EXPERTS, LANES), jnp.int32)],
        scratch_shapes=[pltpu.VMEM((N_EXPERTS, LANES), jnp.int32), pltpu.VMEM(w_out.shape, BF16)],
        compiler_params=pltpu.CompilerParams(dimension_semantics=("arbitrary", "arbitrary"),
                                             vmem_limit_bytes=VMEM_LIMIT),
        name="outproj_router",
    )(x, attn, pool, mod3, w_out, g2, wr_t, rbias, ut)


def _n_row_blocks(n_tokens):
    return -(-(n_tokens * TOP_K + N_EXPERTS * (ROW_BLOCK - 1)) // ROW_BLOCK)


def _slots_kernel(rank_ref, wd_ref, cnt_ref, lt_ref, slot_ref, wk_ref, be_ref, nu_ref):
    lt = lt_ref[...]
    nblk = lax.shift_right_logical(cnt_ref[...] + (ROW_BLOCK - 1), ROW_BLOCK_LOG2).astype(F32)
    nb_hi, nb_lo = _split_bf16(nblk)
    blk_start = _dot(lt, nb_hi) + _dot(lt, nb_lo)

    @pl.when(pl.program_id(0) == 0)
    def _():
        blk_end = (blk_start + nblk)[:, 0:1]
        b_iota = lax.broadcasted_iota(jnp.int32, (N_EXPERTS, be_ref.shape[-1]), 1).astype(F32)
        owner = jnp.sum(jnp.where(blk_end <= b_iota, 1, 0), axis=0, keepdims=True)
        be_ref[...] = jnp.minimum(owner, N_EXPERTS - 1)
        nu_ref[...] = jnp.broadcast_to(blk_end[N_EXPERTS - 1:, :].astype(jnp.int32), nu_ref.shape)

    rank = rank_ref[...]
    sel = rank >= 0
    row_start = (blk_start[:, 0:1] * ROW_BLOCK).astype(jnp.int32)
    slot_d = row_start + rank
    wd = wd_ref[...]
    choice = _dot(lt, jnp.where(sel, 1.0, 0.0).astype(BF16)).astype(jnp.int32)
    ts = rank.shape[-1]
    slots, wks = [], []
    for k in range(TOP_K):
        m = jnp.logical_and(sel, choice == k)
        slots.append(jnp.sum(jnp.where(m, slot_d, 0), axis=0, keepdims=True))
        wks.append(jnp.sum(jnp.where(m, wd, 0.0), axis=0, keepdims=True))
    slot_ref[...] = jnp.concatenate(slots + [jnp.zeros((SUBLANES - TOP_K, ts), jnp.int32)], axis=0)
    wk_pad = jnp.concatenate(wks + [jnp.zeros((LANES - TOP_K, ts), F32)], axis=0)
    wk_ref[...] = wk_pad.T


def _slots(rank_t, wd_t, cnt, lt):
    T = rank_t.shape[1]
    ts = TS_SLOT
    nb_pad = -(-_n_row_blocks(T) // LANES) * LANES
    const = lambda i: (0, 0)
    tok_t = pl.BlockSpec((N_EXPERTS, ts), lambda i: (0, i))
    return pl.pallas_call(
        _slots_kernel,
        grid=(T // ts,),
        in_specs=[tok_t, tok_t, pl.BlockSpec((N_EXPERTS, LANES), const), pl.BlockSpec((N_EXPERTS, N_EXPERTS), const)],
        out_specs=[pl.BlockSpec((SUBLANES, ts), lambda i: (0, i)),
                   pl.BlockSpec((ts, LANES), lambda i: (i, 0)),
                   pl.BlockSpec((1, nb_pad), const),
                   pl.BlockSpec((1, LANES), const)],
        out_shape=[jax.ShapeDtypeStruct((SUBLANES, T), jnp.int32),
                   jax.ShapeDtypeStruct((T, LANES), F32),
                   jax.ShapeDtypeStruct((1, nb_pad), jnp.int32),
                   jax.ShapeDtypeStruct((1, LANES), jnp.int32)],
        compiler_params=pltpu.CompilerParams(dimension_semantics=("arbitrary",), vmem_limit_bytes=VMEM_LIMIT),
        name="slots",
    )(rank_t, wd_t, cnt, lt)


def _sc_mesh():
    return plsc.VectorSubcoreMesh(core_axis_name="c", subcore_axis_name="s",
                                  num_cores=SC_CORES, num_subcores=SC_SUBCORES)


def _sc_dispatch(rows, slot_flat, n_out):
    T, width = rows.shape
    n_workers = SC_CORES * SC_SUBCORES
    per_worker = T // n_workers
    steps = per_worker // SC_CHUNK

    @functools.partial(
        pl.kernel, mesh=_sc_mesh(),
        out_type=jax.ShapeDtypeStruct((n_out, width), rows.dtype),
        scratch_types=[pltpu.VMEM((SC_CHUNK, width), rows.dtype)]
        + [pltpu.VMEM((SC_CHUNK,), jnp.int32)] * TOP_K + [pltpu.SemaphoreType.DMA],
        name="dispatch",
    )
    def run(rows_hbm, slot_hbm, out_hbm, rows_v, *rest):
        idx_v, sem = rest[:TOP_K], rest[TOP_K]
        base = (lax.axis_index("s") * SC_CORES + lax.axis_index("c")) * per_worker

        @pl.loop(0, steps)
        def _(i):
            off = base + i * SC_CHUNK
            pltpu.sync_copy(rows_hbm.at[pl.ds(off, SC_CHUNK)], rows_v)
            for k in range(TOP_K):
                pltpu.sync_copy(slot_hbm.at[pl.ds(k * T + off, SC_CHUNK)], idx_v[k])
            copies = [pltpu.async_copy(rows_v, out_hbm.at[idx_v[k]], sem) for k in range(TOP_K)]
            for cp in copies:
                cp.wait()

    return run(rows, slot_flat)


def _sc_gather(rows, idx):
    n = idx.shape[0]
    width = rows.shape[1]
    n_workers = SC_CORES * SC_SUBCORES
    per_worker = n // n_workers
    steps = per_worker // SC_CHUNK

    assert steps % 2 == 0
    slot_types = [pltpu.VMEM((SC_CHUNK,), jnp.int32), pltpu.VMEM((SC_CHUNK, width), rows.dtype),
                  pltpu.SemaphoreType.DMA]

    @functools.partial(
        pl.kernel, mesh=_sc_mesh(),
        out_type=jax.ShapeDtypeStruct((n, width), rows.dtype),
        scratch_types=slot_types * 2,
        name="combine",
    )
    def run(rows_hbm, idx_hbm, out_hbm, *scratch):
        base = (lax.axis_index("s") * SC_CORES + lax.axis_index("c")) * per_worker
        slots = (scratch[0:3], scratch[3:6])

        def gather(slot):
            idx_v, rows_v, sem = slots[slot]
            return pltpu.make_async_copy(rows_hbm.at[idx_v], rows_v, sem)

        def start(chunk, slot):
            pltpu.sync_copy(idx_hbm.at[pl.ds(base + chunk * SC_CHUNK, SC_CHUNK)], slots[slot][0])
            gather(slot).start()

        def finish(chunk, slot):
            gather(slot).wait()
            pltpu.sync_copy(slots[slot][1], out_hbm.at[pl.ds(base + chunk * SC_CHUNK, SC_CHUNK)])

        start(0, 0)

        @pl.loop(0, steps, step=2)
        def _(chunk):
            start(chunk + 1, 1)
            finish(chunk, 0)

            @pl.when(chunk + 2 < steps)
            def _():
                start(chunk + 2, 0)

            finish(chunk + 1, 1)

    return run(rows, idx)


def _swiglu_packed(xp, wgu, wd):
    lo, hi = _unpack_bf16_pair(xp)
    half = xp.shape[-1]
    gu = _dot(lo, wgu[:half, :]) + _dot(hi, wgu[half:, :])
    dh = wgu.shape[-1] // 2
    act = _silu(gu[:, :dh]) * gu[:, dh:]
    return _dot(act.astype(BF16), wd)


def _cast_swiglu_weights(wg32_ref, wu32_ref, wd32_ref, wgu_ref, wd_ref):
    dh = wg32_ref.shape[-1]
    wgu_ref[:, :dh] = wg32_ref[...].reshape(wg32_ref.shape[-2:]).astype(BF16)
    wgu_ref[:, dh:] = wu32_ref[...].reshape(wu32_ref.shape[-2:]).astype(BF16)
    wd_ref[...] = wd32_ref[...].reshape(wd32_ref.shape[-2:]).astype(BF16)


def _experts_kernel(be_ref, nu_ref, x_ref, wg32_ref, wu32_ref, wd32_ref, y_ref, wgu_ref, wd_ref):
    b = pl.program_id(0)
    new_expert = jnp.logical_or(b == 0, be_ref[b] != be_ref[jnp.maximum(b - 1, 0)])

    @pl.when(new_expert)
    def _():
        _cast_swiglu_weights(wg32_ref, wu32_ref, wd32_ref, wgu_ref, wd_ref)

    @pl.when(b < nu_ref[0])
    def _():
        y = _swiglu_packed(x_ref[...], wgu_ref[...], wd_ref[...])
        half = y.shape[-1] // 2
        y_ref[...] = _pack_bf16_pair(y[:, :half], y[:, half:])


def _experts(xs, block_expert, n_used, w_gate, w_up, w_down):
    P, half = xs.shape
    _, D, dh = w_gate.shape
    by_expert = lambda shape: pl.BlockSpec((1,) + shape, lambda b, be, nu: (be[b], 0, 0))
    grid_spec = pltpu.PrefetchScalarGridSpec(
        num_scalar_prefetch=2,
        grid=(P // ROW_BLOCK,),
        in_specs=[pl.BlockSpec((ROW_BLOCK, half), lambda b, be, nu: (b, 0)),
                  by_expert((D, dh)), by_expert((D, dh)), by_expert((dh, D))],
        out_specs=pl.BlockSpec((ROW_BLOCK, half), lambda b, be, nu: (b, 0)),
        scratch_shapes=[pltpu.VMEM((D, 2 * dh), BF16), pltpu.VMEM((dh, D), BF16)],
    )
    return pl.pallas_call(
        _experts_kernel,
        grid_spec=grid_spec,
        out_shape=jax.ShapeDtypeStruct((P, half), jnp.uint32),
        compiler_params=pltpu.CompilerParams(dimension_semantics=("arbitrary",), vmem_limit_bytes=VMEM_LIMIT),
        name="experts",
    )(block_expert, n_used, xs, w_gate, w_up, w_down)


def _final_kernel(x1_ref, hp_ref, g_ref, wk_ref, mod_ref, sg32_ref, su32_ref, sd32_ref, o_ref, sgu_ref, sd_ref):
    @pl.when(_first_grid_step())
    def _():
        _cast_swiglu_weights(sg32_ref, su32_ref, sd32_ref, sgu_ref, sd_ref)

    acc = _swiglu_packed(hp_ref[0], sgu_ref[...], sd_ref[...])
    wk = wk_ref[0]
    for k in range(TOP_K):
        lo, hi = _unpack_bf16_pair(g_ref[k, 0])
        y = jnp.concatenate([lo.astype(F32), hi.astype(F32)], axis=-1)
        acc = acc + wk[:, k:k + 1] * y
    gate2 = mod_ref[0, 5:6, :]
    o_ref[0] = x1_ref[0] + gate2 * acc


def _final(x1, hp, g, wk_tok, mod3, ws_gate, ws_up, ws_down):
    B, S, D = x1.shape
    dh = ws_gate.shape[-1]
    ts = TS_PROJ
    tok = lambda w: pl.BlockSpec((1, ts, w), lambda b, s: (b, s, 0))
    const2 = lambda b, s: (0, 0)
    return pl.pallas_call(
        _final_kernel,
        grid=(B, S // ts),
        in_specs=[tok(D), tok(D // 2),
                  pl.BlockSpec((TOP_K, 1, ts, D // 2), lambda b, s: (0, b, s, 0)),
                  tok(LANES),
                  pl.BlockSpec((1, N_MOD, D), lambda b, s: (b, 0, 0)),
                  pl.BlockSpec(ws_gate.shape, const2),
                  pl.BlockSpec(ws_up.shape, const2),
                  pl.BlockSpec(ws_down.shape, const2)],
        out_specs=tok(D),
        out_shape=jax.ShapeDtypeStruct((B, S, D), F32),
        scratch_shapes=[pltpu.VMEM((D, 2 * dh), BF16), pltpu.VMEM((dh, D), BF16)],
        compiler_params=pltpu.CompilerParams(dimension_semantics=("arbitrary", "arbitrary"),
                                             vmem_limit_bytes=VMEM_LIMIT),
        name="final",
    )(x1, hp, g, wk_tok, mod3, ws_gate, ws_up, ws_down)


def _layer(x, c_act_mod, norm1_g, norm2_g, w_in, q_norm_g, k_norm_g, w_pool, pool_scale, w_out,
           w_router, router_bias, w_gate, w_up, w_down, ws_gate, ws_up, ws_down):
    B, S, D = x.shape
    mod3 = c_act_mod.reshape(B, N_MOD, D)
    head_of = jnp.arange(D_ATTN, dtype=jnp.int32) // HEAD_DIM
    head_mean = jnp.where(head_of[:, None] == head_of[None, :], 1.0 / HEAD_DIM, 0.0).astype(BF16)
    j = jnp.arange(TK, dtype=jnp.int32)
    umat = jnp.where(j[:, None] >= j[None, :], -1.0, 0.0).astype(BF16)

    q, k, v, pool = _inproj(
        x, mod3, norm1_g.reshape(1, D), w_in,
        jnp.tile(q_norm_g, N_HEADS).reshape(1, D_ATTN), jnp.tile(k_norm_g, N_HEADS).reshape(1, D_ATTN),
        head_mean, w_pool, pool_scale.reshape(1, D_POOL))
    attn = _attention(q, k, v, umat)
    t = jnp.arange(TS_PROJ, dtype=jnp.int32)
    ut = (t[:, None] < t[None, :]).astype(BF16)
    e = jnp.arange(N_EXPERTS, dtype=jnp.int32)
    lt = (e[None, :] < e[:, None]).astype(BF16)
    x1, hp, rank_t, wd_t, cnt = _outproj(x, attn, pool, mod3, w_out, norm2_g.reshape(1, D),
                                         w_router.T, router_bias.reshape(N_EXPERTS, 1), ut)
    T = B * S
    slots, wk_tok, block_expert, n_used = _slots(rank_t, wd_t, cnt, lt)
    n_blocks = _n_row_blocks(T)
    slot_flat = slots[:TOP_K].reshape(TOP_K * T)
    xs = _sc_dispatch(hp.reshape(T, D // 2), slot_flat, n_blocks * ROW_BLOCK)
    ys = _experts(xs, block_expert[0, :n_blocks], n_used[0, :1], w_gate, w_up, w_down)
    g = _sc_gather(ys, slot_flat).reshape(TOP_K, B, S, D // 2)
    return _final(x1, hp, g, wk_tok.reshape(B, S, LANES), mod3, ws_gate, ws_up, ws_down)


def kernel(x, c, w_ada, b_ada, norm1_g, norm2_g, w_in, q_norm_g, k_norm_g, w_pool, pool_scale, w_out,
           w_router, router_bias, w_gate, w_up, w_down, ws_gate, ws_up, ws_down):
    depth = w_ada.shape[0]
    for l in range(depth):
        mod = _adaln(c, w_ada[l], b_ada[l])
        x = _layer(x, mod, norm1_g[l], norm2_g[l], w_in[l], q_norm_g[l], k_norm_g[l], w_pool[l],
                   pool_scale[l], w_out[l], w_router[l], router_bias[l], w_gate[l], w_up[l], w_down[l],
                   ws_gate[l], ws_up[l], ws_down[l])
    return x
```

```python
import functools

import jax
import jax.numpy as jnp
from jax import lax
from jax.experimental import pallas as pl
from jax.experimental.pallas import tpu as pltpu
from jax.experimental.pallas import tpu_sc as plsc

F32 = jnp.float32
BF16 = jnp.bfloat16

HEAD_DIM = 64
N_HEADS = 8
D_ATTN = N_HEADS * HEAD_DIM
POOL_WINDOWS = (2, 4, 8, 16)
POOL_GROUP_DIM = 128
D_POOL = len(POOL_WINDOWS) * POOL_GROUP_DIM
MAX_WINDOW = max(POOL_WINDOWS)
N_EXPERTS = 64
TOP_K = 6
N_GROUPS = 8
GROUP_SIZE = N_EXPERTS // N_GROUPS
TOPK_GROUPS = 4
ROUTED_SCALE = 2.5
RMS_EPS = 1e-6
N_MOD = 6

LANES = 128
SUBLANES = 8
VMEM_LIMIT = 56 * 1024 * 1024

TS_PROJ = 512
TQ = 1024
TK = 256
TS_SLOT = 2048
ROW_BLOCK_LOG2 = 9
ROW_BLOCK = 1 << ROW_BLOCK_LOG2
MOE_PARTS = 2
SC_CORES = 2
SC_SUBCORES = 16
SC_CHUNK = 64


def _split_bf16(a):
    hi = a.astype(BF16)
    lo = (a - hi.astype(F32)).astype(BF16)
    return hi, lo


def _dot(a, b):
    return jnp.dot(a, b, preferred_element_type=F32)


def _dot_nt(a, b):
    return lax.dot_general(a, b, (((1,), (1,)), ((), ())), preferred_element_type=F32)


def _dot3(a, b):
    ah, al = _split_bf16(a)
    bh, bl = _split_bf16(b)
    return _dot(ah, bh) + _dot(ah, bl) + _dot(al, bh)


def _dot3_nt(a, b):
    ah, al = _split_bf16(a)
    bh, bl = _split_bf16(b)
    return _dot_nt(ah, bh) + _dot_nt(ah, bl) + _dot_nt(al, bh)


def _silu(x):
    return x * (1.0 / (1.0 + jnp.exp(-x)))


def _rms_mod(x, gain, scale, shift):
    ms = jnp.mean(x * x, axis=-1, keepdims=True)
    y = x * lax.rsqrt(ms + RMS_EPS) * gain
    return y * (1.0 + scale) + shift


def _adaln_kernel(c_ref, w_ref, b_ref, o_ref):
    c = c_ref[...]
    o_ref[...] = _dot3(_silu(c), w_ref[...]) + b_ref[...]


def _adaln(c, w_ada, b_ada):
    nb, D = c.shape
    B = -(-nb // SUBLANES) * SUBLANES
    c = jnp.pad(c, ((0, B - nb), (0, 0)))
    N = w_ada.shape[1]
    tn = 1024
    out = pl.pallas_call(
        _adaln_kernel,
        grid=(N // tn,),
        in_specs=[pl.BlockSpec((B, D), lambda j: (0, 0)),
                  pl.BlockSpec((D, tn), lambda j: (0, j)),
                  pl.BlockSpec((1, tn), lambda j: (0, j))],
        out_specs=pl.BlockSpec((B, tn), lambda j: (0, j)),
        out_shape=jax.ShapeDtypeStruct((B, N), F32),
        compiler_params=pltpu.CompilerParams(dimension_semantics=("arbitrary",),
                                             vmem_limit_bytes=VMEM_LIMIT),
        name="adaln",
    )(c, w_ada, b_ada.reshape(1, N))
    return out[:nb]


def _first_grid_step():
    return jnp.logical_and(pl.program_id(0) == 0, pl.program_id(1) == 0)


def _inproj_kernel(x_ref, mod_ref, g1_ref, win32_ref, qg_ref, kg_ref, hm_ref, wp32_ref, ps_ref,
                   q_ref, k_ref, v_ref, p_ref, ext_ref, win_ref, wp_ref):
    @pl.when(_first_grid_step())
    def _():
        win_ref[...] = win32_ref[...].astype(BF16)
        wp_ref[...] = wp32_ref[...].astype(BF16)

    si = pl.program_id(1)
    ts = x_ref.shape[1]
    x = x_ref[0]
    shift1 = mod_ref[0, 0:1, :]
    scale1 = mod_ref[0, 1:2, :]
    h = _rms_mod(x, g1_ref[...], scale1, shift1)
    proj = _dot(h.astype(BF16), win_ref[...])

    hm = hm_ref[...]

    def head_norm(t, gain):
        hi, lo = _split_bf16(t * t)
        ms = _dot(hi, hm) + _dot(lo, hm)
        return t * lax.rsqrt(ms + RMS_EPS) * gain

    hq = proj[:, 0:D_ATTN]
    hk = proj[:, D_ATTN:2 * D_ATTN]
    q_ref[0] = (head_norm(hq, qg_ref[...]) * (HEAD_DIM ** -0.5)).astype(BF16)
    k_ref[0] = head_norm(hk, kg_ref[...]).astype(BF16)
    v_ref[0] = proj[:, 2 * D_ATTN:3 * D_ATTN].astype(BF16)

    hp = proj[:, 3 * D_ATTN:]

    @pl.when(si == 0)
    def _():
        ext_ref[0:MAX_WINDOW, :] = jnp.zeros((MAX_WINDOW, D_POOL), F32)

    ext_ref[MAX_WINDOW:, :] = hp
    pos = si * ts + lax.broadcasted_iota(jnp.int32, (ts, 1), 0)
    for g, w in enumerate(POOL_WINDOWS):
        lo_l, hi_l = g * POOL_GROUP_DIM, (g + 1) * POOL_GROUP_DIM
        u = hp[:, lo_l:hi_l]
        acc = u
        for i in range(1, w):
            acc = acc + ext_ref[pl.ds(MAX_WINDOW - i, ts), lo_l:hi_l]
        count = jnp.minimum(pos + 1, w).astype(F32)
        d = acc / count - u
        mixed = _dot(d.astype(BF16), wp_ref[g])
        p_ref[0, :, lo_l:hi_l] = (mixed * ps_ref[:, lo_l:hi_l]).astype(BF16)
    ext_ref[0:MAX_WINDOW, :] = hp[ts - MAX_WINDOW:, :]


def _inproj(x, mod3, g1, w_in, qg_t, kg_t, head_mean, w_pool, pool_scale):
    B, S, D = x.shape
    ts = TS_PROJ
    out_sd = jax.ShapeDtypeStruct((B, S, D_ATTN), BF16)
    blk = pl.BlockSpec((1, ts, D_ATTN), lambda b, s: (b, s, 0))
    const2 = lambda b, s: (0, 0)
    return pl.pallas_call(
        _inproj_kernel,
        grid=(B, S // ts),
        in_specs=[pl.BlockSpec((1, ts, D), lambda b, s: (b, s, 0)),
                  pl.BlockSpec((1, N_MOD, D), lambda b, s: (b, 0, 0)),
                  pl.BlockSpec((1, D), const2),
                  pl.BlockSpec(w_in.shape, const2),
                  pl.BlockSpec((1, D_ATTN), const2),
                  pl.BlockSpec((1, D_ATTN), const2),
                  pl.BlockSpec((D_ATTN, D_ATTN), const2),
                  pl.BlockSpec(w_pool.shape, lambda b, s: (0, 0, 0)),
                  pl.BlockSpec((1, D_POOL), const2)],
        out_specs=[blk, blk, blk, blk],
        out_shape=[out_sd, out_sd, out_sd, out_sd],
        scratch_shapes=[pltpu.VMEM((MAX_WINDOW + ts, D_POOL), F32),
                        pltpu.VMEM(w_in.shape, BF16), pltpu.VMEM(w_pool.shape, BF16)],
        compiler_params=pltpu.CompilerParams(dimension_semantics=("arbitrary", "arbitrary"),
                                             vmem_limit_bytes=VMEM_LIMIT),
        name="inproj",
    )(x, mod3, g1, w_in, qg_t, kg_t, head_mean, w_pool, pool_scale)


def _attn_kernel(q_ref, k_ref, v_ref, u_ref, o_ref, acc_ref):
    qi = pl.program_id(2)
    q = q_ref[0]
    lane = lax.broadcasted_iota(jnp.int32, (TQ, LANES), 1)
    first = lane < HEAD_DIM
    zero = jnp.zeros_like(q)
    qh = (jnp.where(first, q, zero), jnp.where(first, zero, q))
    u = u_ref[...]
    row = lax.broadcasted_iota(jnp.int32, (TQ, TK), 0)
    col = lax.broadcasted_iota(jnp.int32, (TQ, TK), 1)
    acc_ref[...] = jnp.zeros(acc_ref.shape, F32)

    def block(kb, survs, diag):
        start = pl.multiple_of(kb * TK, TK)
        k = k_ref[0, pl.ds(start, TK), :]
        v = v_ref[0, pl.ds(start, TK), :]
        r0 = 0 if diag is None else diag * TK
        if diag is not None:
            valid = (col + r0 < row)[r0:]
        out = []
        for h in range(2):
            z = _dot_nt(qh[h][r0:], k)
            zb = z.astype(BF16)
            sp = jnp.maximum(zb, 0) + jnp.log(1 + jnp.exp(-jnp.abs(zb)))
            if diag is not None:
                sp = jnp.where(valid, sp, jnp.zeros_like(sp))
            r = _dot(sp, u)
            arg = z + r + survs[h][r0:]
            if diag is not None:
                arg = jnp.where(valid, arg, -jnp.inf)
            acc_ref[h, r0:, :] += _dot(jnp.exp(arg).astype(BF16), v)
            surv = survs[h][r0:] + r[:, 0:1]
            out.append(surv if r0 == 0 else jnp.concatenate([survs[h][:r0], surv], axis=0))
        return tuple(out)

    survs = (jnp.zeros((TQ, 1), F32), jnp.zeros((TQ, 1), F32))
    n_diag = TQ // TK
    for d in reversed(range(n_diag)):
        survs = block(qi * n_diag + d, survs, d)
    def full_blocks(i, c):
        for d in range(n_diag):
            c = block((qi - i) * n_diag - 1 - d, c, None)
        return c

    lax.fori_loop(0, qi, full_blocks, survs)
    o_ref[0] = jnp.where(first, acc_ref[0], acc_ref[1]).astype(BF16)


def _attention(q, k, v, umat):
    B, S, _ = q.shape
    n_pairs = D_ATTN // LANES
    kv_spec = pl.BlockSpec((1, S, LANES), lambda b, p, i: (b, 0, p))
    q_spec = pl.BlockSpec((1, TQ, LANES), lambda b, p, i: (b, i, p))
    return pl.pallas_call(
        _attn_kernel,
        grid=(B, n_pairs, S // TQ),
        in_specs=[q_spec, kv_spec, kv_spec, pl.BlockSpec((TK, TK), lambda b, p, i: (0, 0))],
        out_specs=q_spec,
        out_shape=jax.ShapeDtypeStruct((B, S, D_ATTN), BF16),
        scratch_shapes=[pltpu.VMEM((2, TQ, LANES), F32)],
        compiler_params=pltpu.CompilerParams(
            dimension_semantics=("arbitrary", "arbitrary", "arbitrary"),
            vmem_limit_bytes=VMEM_LIMIT),
        name="stickbreak_attn",
    )(q, k, v, umat)


def _route_t(scores, biased):
    ts = scores.shape[-1]
    neg = -jnp.inf
    b3 = biased.reshape(N_GROUPS, GROUP_SIZE, ts)
    e_in_g = lax.broadcasted_iota(jnp.int32, b3.shape, 1)
    m1 = jnp.max(b3, axis=1, keepdims=True)
    i1 = jnp.min(jnp.where(b3 == m1, e_in_g, GROUP_SIZE), axis=1, keepdims=True)
    m2 = jnp.max(jnp.where(e_in_g == i1, neg, b3), axis=1, keepdims=True)
    gs = (m1 + m2)[:, 0, :]
    g_iota = lax.broadcasted_iota(jnp.int32, gs.shape, 0)
    g_sel = jnp.zeros(gs.shape, jnp.bool_)
    for _ in range(TOPK_GROUPS):
        gm = jnp.max(gs, axis=0, keepdims=True)
        gi = jnp.min(jnp.where(gs == gm, g_iota, N_GROUPS), axis=0, keepdims=True)
        pick = g_iota == gi
        g_sel = jnp.logical_or(g_sel, pick)
        gs = jnp.where(pick, neg, gs)
    masked = jnp.where(g_sel[:, None, :], b3, neg)
    flat = lax.broadcasted_iota(jnp.int32, b3.shape, 0) * GROUP_SIZE + e_in_g
    sel = jnp.zeros(b3.shape, jnp.bool_)
    for _ in range(TOP_K):
        m = jnp.max(jnp.max(masked, axis=1, keepdims=True), axis=0, keepdims=True)
        cand = jnp.where(masked == m, flat, N_EXPERTS)
        idx = jnp.min(jnp.min(cand, axis=1, keepdims=True), axis=0, keepdims=True)
        pick = flat == idx
        sel = jnp.logical_or(sel, pick)
        masked = jnp.where(pick, neg, masked)
    s3 = scores.reshape(N_GROUPS, GROUP_SIZE, ts)
    w = jnp.where(sel, s3, 0.0)
    tot = jnp.sum(jnp.sum(w, axis=1, keepdims=True), axis=0, keepdims=True)
    return (w / tot * ROUTED_SCALE).reshape(N_EXPERTS, ts), sel.reshape(N_EXPERTS, ts)


def _pack_bf16_pair(lo, hi):
    lo_bits = pltpu.bitcast(lo.astype(BF16).astype(F32), jnp.uint32) >> 16
    hi_bits = pltpu.bitcast(hi.astype(BF16).astype(F32), jnp.uint32) & jnp.uint32(0xFFFF0000)
    return lo_bits | hi_bits


def _unpack_bf16_pair(p):
    lo = pltpu.bitcast(p << 16, F32).astype(BF16)
    hi = pltpu.bitcast(p & jnp.uint32(0xFFFF0000), F32).astype(BF16)
    return lo, hi


def _outproj_kernel(x_ref, a_ref, p_ref, mod_ref, wo32_ref, g2_ref, wrt_ref, rb_ref, ut_ref,
                    x1_ref, hp_ref, rank_ref, wd_ref, cnt_ref, run_ref, wo_ref):
    @pl.when(_first_grid_step())
    def _():
        run_ref[...] = jnp.zeros(run_ref.shape, jnp.int32)
        wo_ref[...] = wo32_ref[...].astype(BF16)

    x = x_ref[0]
    gate1 = mod_ref[0, 2:3, :]
    shift2 = mod_ref[0, 3:4, :]
    scale2 = mod_ref[0, 4:5, :]
    mixp = _dot(a_ref[0], wo_ref[0:D_ATTN, :]) + _dot(p_ref[0], wo_ref[D_ATTN:, :])
    x1 = x + gate1 * mixp
    x1_ref[0] = x1
    h2 = _rms_mod(x1, g2_ref[...], scale2, shift2)
    half = h2.shape[-1] // 2
    hp_ref[0] = _pack_bf16_pair(h2[:, :half], h2[:, half:])
    logits_t = _dot3_nt(wrt_ref[...], h2)
    scores = 1.0 / (1.0 + jnp.exp(-logits_t))
    wd_t, sel = _route_t(scores, scores + rb_ref[...])
    wd_ref[...] = wd_t
    self_f = jnp.where(sel, 1.0, 0.0)
    before = _dot(self_f.astype(BF16), ut_ref[...]).astype(jnp.int32)
    run = run_ref[:, 0:1]
    rank_ref[...] = jnp.where(sel, run + before, -1)
    run_ref[...] = run_ref[...] + jnp.sum(self_f, axis=1, keepdims=True).astype(jnp.int32)
    cnt_ref[...] = run_ref[...]


def _outproj(x, attn, pool, mod3, w_out, g2, wr_t, rbias, ut, b0, B):
    _, S, D = x.shape
    ts = TS_PROJ
    n_s = S // ts
    const2 = lambda b, s: (0, 0)
    tok_in = lambda w: pl.BlockSpec((1, ts, w), lambda b, s: (b + b0, s, 0))
    tok = lambda w: pl.BlockSpec((1, ts, w), lambda b, s: (b, s, 0))
    tok_t = pl.BlockSpec((N_EXPERTS, ts), lambda b, s: (0, b * n_s + s))
    return pl.pallas_call(
        _outproj_kernel,
        grid=(B, n_s),
        in_specs=[tok_in(D), tok_in(D_ATTN), tok_in(D_POOL),
                  pl.BlockSpec((1, N_MOD, D), lambda b, s: (b + b0, 0, 0)),
                  pl.BlockSpec(w_out.shape, const2),
                  pl.BlockSpec((1, D), const2),
                  pl.BlockSpec(wr_t.shape, const2),
                  pl.BlockSpec((N_EXPERTS, 1), const2),
                  pl.BlockSpec((ts, ts), const2)],
        out_specs=[tok(D), tok(D // 2), tok_t, tok_t, pl.BlockSpec((N_EXPERTS, LANES), const2)],
        out_shape=[jax.ShapeDtypeStruct((B, S, D), F32),
                   jax.ShapeDtypeStruct((B, S, D // 2), jnp.uint32),
                   jax.ShapeDtypeStruct((N_EXPERTS, B * S), jnp.int32),
                   jax.ShapeDtypeStruct((N_EXPERTS, B * S), F32),
                   jax.ShapeDtypeStruct((N_EXPERTS, LANES), jnp.int32)],
        scratch_shapes=[pltpu.VMEM((N_EXPERTS, LANES), jnp.int32), pltpu.VMEM(w_out.shape, BF16)],
        compiler_params=pltpu.CompilerParams(dimension_semantics=("arbitrary", "arbitrary"),
                                             vmem_limit_bytes=VMEM_LIMIT),
        name="outproj_router",
    )(x, attn, pool, mod3, w_out, g2, wr_t, rbias, ut)


def _n_row_blocks(n_tokens):
    return -(-(n_tokens * TOP_K + N_EXPERTS * (ROW_BLOCK - 1)) // ROW_BLOCK)


def _slots_kernel(rank_ref, wd_ref, cnt_ref, lt_ref, slot_ref, wk_ref, be_ref, nu_ref):
    lt = lt_ref[...]
    nblk = lax.shift_right_logical(cnt_ref[...] + (ROW_BLOCK - 1), ROW_BLOCK_LOG2).astype(F32)
    nb_hi, nb_lo = _split_bf16(nblk)
    blk_start = _dot(lt, nb_hi) + _dot(lt, nb_lo)

    @pl.when(pl.program_id(0) == 0)
    def _():
        blk_end = (blk_start + nblk)[:, 0:1]
        b_iota = lax.broadcasted_iota(jnp.int32, (N_EXPERTS, be_ref.shape[-1]), 1).astype(F32)
        owner = jnp.sum(jnp.where(blk_end <= b_iota, 1, 0), axis=0, keepdims=True)
        be_ref[...] = jnp.minimum(owner, N_EXPERTS - 1)
        nu_ref[...] = jnp.broadcast_to(blk_end[N_EXPERTS - 1:, :].astype(jnp.int32), nu_ref.shape)

    rank = rank_ref[...]
    sel = rank >= 0
    row_start = (blk_start[:, 0:1] * ROW_BLOCK).astype(jnp.int32)
    slot_d = row_start + rank
    wd = wd_ref[...]
    choice = _dot(lt, jnp.where(sel, 1.0, 0.0).astype(BF16)).astype(jnp.int32)
    ts = rank.shape[-1]
    slots, wks = [], []
    for k in range(TOP_K):
        m = jnp.logical_and(sel, choice == k)
        slots.append(jnp.sum(jnp.where(m, slot_d, 0), axis=0, keepdims=True))
        wks.append(jnp.sum(jnp.where(m, wd, 0.0), axis=0, keepdims=True))
    slot_ref[...] = jnp.concatenate(slots + [jnp.zeros((SUBLANES - TOP_K, ts), jnp.int32)], axis=0)
    wk_pad = jnp.concatenate(wks + [jnp.zeros((LANES - TOP_K, ts), F32)], axis=0)
    wk_ref[...] = wk_pad.T


def _slots(rank_t, wd_t, cnt, lt):
    T = rank_t.shape[1]
    ts = TS_SLOT
    nb_pad = -(-_n_row_blocks(T) // LANES) * LANES
    const = lambda i: (0, 0)
    tok_t = pl.BlockSpec((N_EXPERTS, ts), lambda i: (0, i))
    return pl.pallas_call(
        _slots_kernel,
        grid=(T // ts,),
        in_specs=[tok_t, tok_t, pl.BlockSpec((N_EXPERTS, LANES), const), pl.BlockSpec((N_EXPERTS, N_EXPERTS), const)],
        out_specs=[pl.BlockSpec((SUBLANES, ts), lambda i: (0, i)),
                   pl.BlockSpec((ts, LANES), lambda i: (i, 0)),
                   pl.BlockSpec((1, nb_pad), const),
                   pl.BlockSpec((1, LANES), const)],
        out_shape=[jax.ShapeDtypeStruct((SUBLANES, T), jnp.int32),
                   jax.ShapeDtypeStruct((T, LANES), F32),
                   jax.ShapeDtypeStruct((1, nb_pad), jnp.int32),
                   jax.ShapeDtypeStruct((1, LANES), jnp.int32)],
        compiler_params=pltpu.CompilerParams(dimension_semantics=("arbitrary",), vmem_limit_bytes=VMEM_LIMIT),
        name="slots",
    )(rank_t, wd_t, cnt, lt)


def _sc_mesh():
    return plsc.VectorSubcoreMesh(core_axis_name="c", subcore_axis_name="s",
                                  num_cores=SC_CORES, num_subcores=SC_SUBCORES)


def _sc_dispatch(rows, slot_flat, n_out):
    T, width = rows.shape
    n_workers = SC_CORES * SC_SUBCORES
    per_worker = T // n_workers
    steps = per_worker // SC_CHUNK

    @functools.partial(
        pl.kernel, mesh=_sc_mesh(),
        out_type=jax.ShapeDtypeStruct((n_out, width), rows.dtype),
        scratch_types=[pltpu.VMEM((SC_CHUNK, width), rows.dtype)]
        + [pltpu.VMEM((SC_CHUNK,), jnp.int32)] * TOP_K + [pltpu.SemaphoreType.DMA],
        name="dispatch",
    )
    def run(rows_hbm, slot_hbm, out_hbm, rows_v, *rest):
        idx_v, sem = rest[:TOP_K], rest[TOP_K]
        base = (lax.axis_index("s") * SC_CORES + lax.axis_index("c")) * per_worker

        @pl.loop(0, steps)
        def _(i):
            off = base + i * SC_CHUNK
            pltpu.sync_copy(rows_hbm.at[pl.ds(off, SC_CHUNK)], rows_v)
            for k in range(TOP_K):
                pltpu.sync_copy(slot_hbm.at[pl.ds(k * T + off, SC_CHUNK)], idx_v[k])
            copies = [pltpu.async_copy(rows_v, out_hbm.at[idx_v[k]], sem) for k in range(TOP_K)]
            for cp in copies:
                cp.wait()

    return run(rows, slot_flat)


def _sc_gather(rows, idx):
    n = idx.shape[0]
    width = rows.shape[1]
    n_workers = SC_CORES * SC_SUBCORES
    per_worker = n // n_workers
    steps = per_worker // SC_CHUNK

    assert steps % 2 == 0
    slot_types = [pltpu.VMEM((SC_CHUNK,), jnp.int32), pltpu.VMEM((SC_CHUNK, width), rows.dtype),
                  pltpu.SemaphoreType.DMA]

    @functools.partial(
        pl.kernel, mesh=_sc_mesh(),
        out_type=jax.ShapeDtypeStruct((n, width), rows.dtype),
        scratch_types=slot_types * 2,
        name="combine",
    )
    def run(rows_hbm, idx_hbm, out_hbm, *scratch):
        base = (lax.axis_index("s") * SC_CORES + lax.axis_index("c")) * per_worker
        slots = (scratch[0:3], scratch[3:6])

        def gather(slot):
            idx_v, rows_v, sem = slots[slot]
            return pltpu.make_async_copy(rows_hbm.at[idx_v], rows_v, sem)

        def start(chunk, slot):
            pltpu.sync_copy(idx_hbm.at[pl.ds(base + chunk * SC_CHUNK, SC_CHUNK)], slots[slot][0])
            gather(slot).start()

        def finish(chunk, slot):
            gather(slot).wait()
            pltpu.sync_copy(slots[slot][1], out_hbm.at[pl.ds(base + chunk * SC_CHUNK, SC_CHUNK)])

        start(0, 0)

        @pl.loop(0, steps, step=2)
        def _(chunk):
            start(chunk + 1, 1)
            finish(chunk, 0)

            @pl.when(chunk + 2 < steps)
            def _():
                start(chunk + 2, 0)

            finish(chunk + 1, 1)

    return run(rows, idx)


def _swiglu_packed(xp, wgu, wd):
    lo, hi = _unpack_bf16_pair(xp)
    half = xp.shape[-1]
    gu = _dot(lo, wgu[:half, :]) + _dot(hi, wgu[half:, :])
    dh = wgu.shape[-1] // 2
    act = _silu(gu[:, :dh]) * gu[:, dh:]
    return _dot(act.astype(BF16), wd)


def _cast_swiglu_weights(wg32_ref, wu32_ref, wd32_ref, wgu_ref, wd_ref):
    dh = wg32_ref.shape[-1]
    wgu_ref[:, :dh] = wg32_ref[...].reshape(wg32_ref.shape[-2:]).astype(BF16)
    wgu_ref[:, dh:] = wu32_ref[...].reshape(wu32_ref.shape[-2:]).astype(BF16)
    wd_ref[...] = wd32_ref[...].reshape(wd32_ref.shape[-2:]).astype(BF16)


def _experts_kernel(be_ref, nu_ref, x_ref, wg32_ref, wu32_ref, wd32_ref, y_ref, wgu_ref, wd_ref):
    b = pl.program_id(0)
    new_expert = jnp.logical_or(b == 0, be_ref[b] != be_ref[jnp.maximum(b - 1, 0)])

    @pl.when(new_expert)
    def _():
        _cast_swiglu_weights(wg32_ref, wu32_ref, wd32_ref, wgu_ref, wd_ref)

    @pl.when(b < nu_ref[0])
    def _():
        y = _swiglu_packed(x_ref[...], wgu_ref[...], wd_ref[...])
        half = y.shape[-1] // 2
        y_ref[...] = _pack_bf16_pair(y[:, :half], y[:, half:])


def _experts(xs, block_expert, n_used, w_gate, w_up, w_down):
    P, half = xs.shape
    _, D, dh = w_gate.shape
    by_expert = lambda shape: pl.BlockSpec((1,) + shape, lambda b, be, nu: (be[b], 0, 0))
    grid_spec = pltpu.PrefetchScalarGridSpec(
        num_scalar_prefetch=2,
        grid=(P // ROW_BLOCK,),
        in_specs=[pl.BlockSpec((ROW_BLOCK, half), lambda b, be, nu: (b, 0)),
                  by_expert((D, dh)), by_expert((D, dh)), by_expert((dh, D))],
        out_specs=pl.BlockSpec((ROW_BLOCK, half), lambda b, be, nu: (b, 0)),
        scratch_shapes=[pltpu.VMEM((D, 2 * dh), BF16), pltpu.VMEM((dh, D), BF16)],
    )
    return pl.pallas_call(
        _experts_kernel,
        grid_spec=grid_spec,
        out_shape=jax.ShapeDtypeStruct((P, half), jnp.uint32),
        compiler_params=pltpu.CompilerParams(dimension_semantics=("arbitrary",), vmem_limit_bytes=VMEM_LIMIT),
        name="experts",
    )(block_expert, n_used, xs, w_gate, w_up, w_down)


def _final_kernel(x1_ref, hp_ref, g_ref, wk_ref, mod_ref, sg32_ref, su32_ref, sd32_ref, *rest):
    o_ref, sgu_ref, sd_ref = rest[-3:]

    @pl.when(_first_grid_step())
    def _():
        _cast_swiglu_weights(sg32_ref, su32_ref, sd32_ref, sgu_ref, sd_ref)

    acc = _swiglu_packed(hp_ref[0], sgu_ref[...], sd_ref[...])
    wk = wk_ref[0]
    for k in range(TOP_K):
        lo, hi = _unpack_bf16_pair(g_ref[k, 0])
        y = jnp.concatenate([lo.astype(F32), hi.astype(F32)], axis=-1)
        acc = acc + wk[:, k:k + 1] * y
    gate2 = mod_ref[0, 5:6, :]
    o_ref[0] = x1_ref[0] + gate2 * acc


def _final(x1, hp, g, wk_tok, mod3, ws_gate, ws_up, ws_down, prev_out, b0, b_total):
    B, S, D = x1.shape
    dh = ws_gate.shape[-1]
    ts = TS_PROJ
    tok = lambda w: pl.BlockSpec((1, ts, w), lambda b, s: (b, s, 0))
    const2 = lambda b, s: (0, 0)
    in_specs = [tok(D), tok(D // 2),
                pl.BlockSpec((TOP_K, 1, ts, D // 2), lambda b, s: (0, b, s, 0)),
                tok(LANES),
                pl.BlockSpec((1, N_MOD, D), lambda b, s: (b + b0, 0, 0)),
                pl.BlockSpec(ws_gate.shape, const2),
                pl.BlockSpec(ws_up.shape, const2),
                pl.BlockSpec(ws_down.shape, const2)]
    args = [x1, hp, g, wk_tok, mod3, ws_gate, ws_up, ws_down]
    aliases = {}
    if prev_out is not None:
        in_specs.append(pl.BlockSpec(memory_space=pl.ANY))
        args.append(prev_out)
        aliases = {len(args) - 1: 0}
    return pl.pallas_call(
        _final_kernel,
        grid=(B, S // ts),
        in_specs=in_specs,
        out_specs=pl.BlockSpec((1, ts, D), lambda b, s: (b + b0, s, 0)),
        out_shape=jax.ShapeDtypeStruct((b_total, S, D), F32),
        scratch_shapes=[pltpu.VMEM((D, 2 * dh), BF16), pltpu.VMEM((dh, D), BF16)],
        input_output_aliases=aliases,
        compiler_params=pltpu.CompilerParams(dimension_semantics=("arbitrary", "arbitrary"),
                                             vmem_limit_bytes=VMEM_LIMIT),
        name="final",
    )(*args)


def _layer(x, c_act_mod, norm1_g, norm2_g, w_in, q_norm_g, k_norm_g, w_pool, pool_scale, w_out,
           w_router, router_bias, w_gate, w_up, w_down, ws_gate, ws_up, ws_down):
    B, S, D = x.shape
    mod3 = c_act_mod.reshape(B, N_MOD, D)
    head_of = jnp.arange(D_ATTN, dtype=jnp.int32) // HEAD_DIM
    head_mean = jnp.where(head_of[:, None] == head_of[None, :], 1.0 / HEAD_DIM, 0.0).astype(BF16)
    j = jnp.arange(TK, dtype=jnp.int32)
    umat = jnp.where(j[:, None] >= j[None, :], -1.0, 0.0).astype(BF16)

    q, k, v, pool = _inproj(
        x, mod3, norm1_g.reshape(1, D), w_in,
        jnp.tile(q_norm_g, N_HEADS).reshape(1, D_ATTN), jnp.tile(k_norm_g, N_HEADS).reshape(1, D_ATTN),
        head_mean, w_pool, pool_scale.reshape(1, D_POOL))
    attn = _attention(q, k, v, umat)
    t = jnp.arange(TS_PROJ, dtype=jnp.int32)
    ut = (t[:, None] < t[None, :]).astype(BF16)
    e = jnp.arange(N_EXPERTS, dtype=jnp.int32)
    lt = (e[None, :] < e[:, None]).astype(BF16)
    bp = B // MOE_PARTS
    T = bp * S
    n_blocks = _n_row_blocks(T)
    out = None
    for part in range(MOE_PARTS):
        b0 = part * bp
        x1, hp, rank_t, wd_t, cnt = _outproj(x, attn, pool, mod3, w_out, norm2_g.reshape(1, D),
                                             w_router.T, router_bias.reshape(N_EXPERTS, 1), ut, b0, bp)
        slots, wk_tok, block_expert, n_used = _slots(rank_t, wd_t, cnt, lt)
        slot_flat = slots[:TOP_K].reshape(TOP_K * T)
        xs = _sc_dispatch(hp.reshape(T, D // 2), slot_flat, n_blocks * ROW_BLOCK)
        ys = _experts(xs, block_expert[0, :n_blocks], n_used[0, :1], w_gate, w_up, w_down)
        g = _sc_gather(ys, slot_flat).reshape(TOP_K, bp, S, D // 2)
        out = _final(x1, hp, g, wk_tok.reshape(bp, S, LANES), mod3, ws_gate, ws_up, ws_down, out, b0, B)
    return out


def kernel(x, c, w_ada, b_ada, norm1_g, norm2_g, w_in, q_norm_g, k_norm_g, w_pool, pool_scale, w_out,
           w_router, router_bias, w_gate, w_up, w_down, ws_gate, ws_up, ws_down):
    depth = w_ada.shape[0]
    for l in range(depth):
        mod = _adaln(c, w_ada[l], b_ada[l])
        x = _layer(x, mod, norm1_g[l], norm2_g[l], w_in[l], q_norm_g[l], k_norm_g[l], w_pool[l],
                   pool_scale[l], w_out[l], w_router[l], router_bias[l], w_gate[l], w_up[l], w_down[l],
                   ws_gate[l], ws_up[l], ws_down[l])
    return x
```

```python
import functools

import jax
import jax.numpy as jnp
from jax import lax
from jax.experimental import pallas as pl
from jax.experimental.pallas import tpu as pltpu
from jax.experimental.pallas import tpu_sc as plsc

F32 = jnp.float32
BF16 = jnp.bfloat16

HEAD_DIM = 64
N_HEADS = 8
D_ATTN = N_HEADS * HEAD_DIM
POOL_WINDOWS = (2, 4, 8, 16)
POOL_GROUP_DIM = 128
D_POOL = len(POOL_WINDOWS) * POOL_GROUP_DIM
MAX_WINDOW = max(POOL_WINDOWS)
N_EXPERTS = 64
TOP_K = 6
N_GROUPS = 8
GROUP_SIZE = N_EXPERTS // N_GROUPS
TOPK_GROUPS = 4
ROUTED_SCALE = 2.5
RMS_EPS = 1e-6
N_MOD = 6

LANES = 128
SUBLANES = 8
VMEM_LIMIT = 56 * 1024 * 1024

TS_PROJ = 512
TQ = 1024
TK = 256
TS_SLOT = 2048
ROW_BLOCK_LOG2 = 9
ROW_BLOCK = 1 << ROW_BLOCK_LOG2
BLOCKS_PER_STEP = 2
MOE_PARTS = 2
SC_CORES = 2
SC_SUBCORES = 16
SC_CHUNK = 64


def _split_bf16(a):
    hi = a.astype(BF16)
    lo = (a - hi.astype(F32)).astype(BF16)
    return hi, lo


def _dot(a, b):
    return jnp.dot(a, b, preferred_element_type=F32)


def _dot_nt(a, b):
    return lax.dot_general(a, b, (((1,), (1,)), ((), ())), preferred_element_type=F32)


def _dot3(a, b):
    ah, al = _split_bf16(a)
    bh, bl = _split_bf16(b)
    return _dot(ah, bh) + _dot(ah, bl) + _dot(al, bh)


def _dot3_nt(a, b):
    ah, al = _split_bf16(a)
    bh, bl = _split_bf16(b)
    return _dot_nt(ah, bh) + _dot_nt(ah, bl) + _dot_nt(al, bh)


def _silu(x):
    return x * (1.0 / (1.0 + jnp.exp(-x)))


def _rms_mod(x, gain, scale, shift):
    ms = jnp.mean(x * x, axis=-1, keepdims=True)
    y = x * lax.rsqrt(ms + RMS_EPS) * gain
    return y * (1.0 + scale) + shift


def _adaln_kernel(c_ref, w_ref, b_ref, o_ref):
    c = c_ref[...]
    o_ref[...] = _dot3(_silu(c), w_ref[...]) + b_ref[...]


def _adaln(c, w_ada, b_ada):
    nb, D = c.shape
    B = -(-nb // SUBLANES) * SUBLANES
    c = jnp.pad(c, ((0, B - nb), (0, 0)))
    N = w_ada.shape[1]
    tn = 1024
    out = pl.pallas_call(
        _adaln_kernel,
        grid=(N // tn,),
        in_specs=[pl.BlockSpec((B, D), lambda j: (0, 0)),
                  pl.BlockSpec((D, tn), lambda j: (0, j)),
                  pl.BlockSpec((1, tn), lambda j: (0, j))],
        out_specs=pl.BlockSpec((B, tn), lambda j: (0, j)),
        out_shape=jax.ShapeDtypeStruct((B, N), F32),
        compiler_params=pltpu.CompilerParams(dimension_semantics=("arbitrary",),
                                             vmem_limit_bytes=VMEM_LIMIT),
        name="adaln",
    )(c, w_ada, b_ada.reshape(1, N))
    return out[:nb]


def _first_grid_step():
    return jnp.logical_and(pl.program_id(0) == 0, pl.program_id(1) == 0)


def _inproj_kernel(x_ref, mod_ref, g1_ref, win32_ref, qg_ref, kg_ref, hm_ref, wp32_ref, ps_ref,
                   q_ref, k_ref, v_ref, p_ref, ext_ref, win_ref, wp_ref):
    @pl.when(_first_grid_step())
    def _():
        win_ref[...] = win32_ref[...].astype(BF16)
        wp_ref[...] = wp32_ref[...].astype(BF16)

    si = pl.program_id(1)
    ts = x_ref.shape[1]
    x = x_ref[0]
    shift1 = mod_ref[0, 0:1, :]
    scale1 = mod_ref[0, 1:2, :]
    h = _rms_mod(x, g1_ref[...], scale1, shift1)
    proj = _dot(h.astype(BF16), win_ref[...])

    hm = hm_ref[...]

    def head_norm(t, gain):
        ms = _dot((t * t).astype(BF16), hm)
        return t * lax.rsqrt(ms + RMS_EPS) * gain

    hq = proj[:, 0:D_ATTN]
    hk = proj[:, D_ATTN:2 * D_ATTN]
    q_ref[0] = (head_norm(hq, qg_ref[...]) * (HEAD_DIM ** -0.5)).astype(BF16)
    k_ref[0] = head_norm(hk, kg_ref[...]).astype(BF16)
    v_ref[0] = proj[:, 2 * D_ATTN:3 * D_ATTN].astype(BF16)

    hp = proj[:, 3 * D_ATTN:]

    @pl.when(si == 0)
    def _():
        ext_ref[0:MAX_WINDOW, :] = jnp.zeros((MAX_WINDOW, D_POOL), F32)

    ext_ref[MAX_WINDOW:, :] = hp
    pos = si * ts + lax.broadcasted_iota(jnp.int32, (ts, 1), 0)
    for g, w in enumerate(POOL_WINDOWS):
        lo_l, hi_l = g * POOL_GROUP_DIM, (g + 1) * POOL_GROUP_DIM
        u = hp[:, lo_l:hi_l]
        acc = u
        for i in range(1, w):
            acc = acc + ext_ref[pl.ds(MAX_WINDOW - i, ts), lo_l:hi_l]
        count = jnp.minimum(pos + 1, w).astype(F32)
        d = acc / count - u
        mixed = _dot(d.astype(BF16), wp_ref[g])
        p_ref[0, :, lo_l:hi_l] = (mixed * ps_ref[:, lo_l:hi_l]).astype(BF16)
    ext_ref[0:MAX_WINDOW, :] = hp[ts - MAX_WINDOW:, :]


def _inproj(x, mod3, g1, w_in, qg_t, kg_t, head_mean, w_pool, pool_scale):
    B, S, D = x.shape
    ts = TS_PROJ
    out_sd = jax.ShapeDtypeStruct((B, S, D_ATTN), BF16)
    blk = pl.BlockSpec((1, ts, D_ATTN), lambda b, s: (b, s, 0))
    const2 = lambda b, s: (0, 0)
    return pl.pallas_call(
        _inproj_kernel,
        grid=(B, S // ts),
        in_specs=[pl.BlockSpec((1, ts, D), lambda b, s: (b, s, 0)),
                  pl.BlockSpec((1, N_MOD, D), lambda b, s: (b, 0, 0)),
                  pl.BlockSpec((1, D), const2),
                  pl.BlockSpec(w_in.shape, const2),
                  pl.BlockSpec((1, D_ATTN), const2),
                  pl.BlockSpec((1, D_ATTN), const2),
                  pl.BlockSpec((D_ATTN, D_ATTN), const2),
                  pl.BlockSpec(w_pool.shape, lambda b, s: (0, 0, 0)),
                  pl.BlockSpec((1, D_POOL), const2)],
        out_specs=[blk, blk, blk, blk],
        out_shape=[out_sd, out_sd, out_sd, out_sd],
        scratch_shapes=[pltpu.VMEM((MAX_WINDOW + ts, D_POOL), F32),
                        pltpu.VMEM(w_in.shape, BF16), pltpu.VMEM(w_pool.shape, BF16)],
        compiler_params=pltpu.CompilerParams(dimension_semantics=("arbitrary", "arbitrary"),
                                             vmem_limit_bytes=VMEM_LIMIT),
        name="inproj",
    )(x, mod3, g1, w_in, qg_t, kg_t, head_mean, w_pool, pool_scale)


def _attn_kernel(q_ref, k_ref, v_ref, u_ref, o_ref, acc_ref):
    qi = pl.program_id(2)
    q = q_ref[0]
    lane = lax.broadcasted_iota(jnp.int32, (TQ, LANES), 1)
    first = lane < HEAD_DIM
    zero = jnp.zeros_like(q)
    qh = (jnp.where(first, q, zero), jnp.where(first, zero, q))
    u = u_ref[...]
    row = lax.broadcasted_iota(jnp.int32, (TQ, TK), 0)
    col = lax.broadcasted_iota(jnp.int32, (TQ, TK), 1)
    acc_ref[...] = jnp.zeros(acc_ref.shape, F32)

    def block(kb, survs, diag):
        start = pl.multiple_of(kb * TK, TK)
        k = k_ref[0, pl.ds(start, TK), :]
        v = v_ref[0, pl.ds(start, TK), :]
        r0 = 0 if diag is None else diag * TK
        if diag is not None:
            valid = (col + r0 < row)[r0:]
        out = []
        for h in range(2):
            z = _dot_nt(qh[h][r0:], k)
            zb = z.astype(BF16)
            sp = jnp.maximum(zb, 0) + jnp.log(1 + jnp.exp(-jnp.abs(zb)))
            if diag is not None:
                sp = jnp.where(valid, sp, jnp.zeros_like(sp))
            r = _dot(sp, u)
            arg = z + r + survs[h][r0:]
            if diag is not None:
                arg = jnp.where(valid, arg, -jnp.inf)
            acc_ref[h, r0:, :] += _dot(jnp.exp(arg).astype(BF16), v)
            surv = survs[h][r0:] + r[:, 0:1]
            out.append(surv if r0 == 0 else jnp.concatenate([survs[h][:r0], surv], axis=0))
        return tuple(out)

    survs = (jnp.zeros((TQ, 1), F32), jnp.zeros((TQ, 1), F32))
    n_diag = TQ // TK
    for d in reversed(range(n_diag)):
        survs = block(qi * n_diag + d, survs, d)
    def full_blocks(i, c):
        for d in range(n_diag):
            c = block((qi - i) * n_diag - 1 - d, c, None)
        return c

    lax.fori_loop(0, qi, full_blocks, survs)
    o_ref[0] = jnp.where(first, acc_ref[0], acc_ref[1]).astype(BF16)


def _attention(q, k, v, umat):
    B, S, _ = q.shape
    n_pairs = D_ATTN // LANES
    kv_spec = pl.BlockSpec((1, S, LANES), lambda b, p, i: (b, 0, p))
    q_spec = pl.BlockSpec((1, TQ, LANES), lambda b, p, i: (b, i, p))
    return pl.pallas_call(
        _attn_kernel,
        grid=(B, n_pairs, S // TQ),
        in_specs=[q_spec, kv_spec, kv_spec, pl.BlockSpec((TK, TK), lambda b, p, i: (0, 0))],
        out_specs=q_spec,
        out_shape=jax.ShapeDtypeStruct((B, S, D_ATTN), BF16),
        scratch_shapes=[pltpu.VMEM((2, TQ, LANES), F32)],
        compiler_params=pltpu.CompilerParams(
            dimension_semantics=("arbitrary", "arbitrary", "arbitrary"),
            vmem_limit_bytes=VMEM_LIMIT),
        name="stickbreak_attn",
    )(q, k, v, umat)


def _route_t(scores, biased):
    ts = scores.shape[-1]
    neg = -jnp.inf
    b3 = biased.reshape(N_GROUPS, GROUP_SIZE, ts)
    e_in_g = lax.broadcasted_iota(jnp.int32, b3.shape, 1)
    m1 = jnp.max(b3, axis=1, keepdims=True)
    i1 = jnp.min(jnp.where(b3 == m1, e_in_g, GROUP_SIZE), axis=1, keepdims=True)
    m2 = jnp.max(jnp.where(e_in_g == i1, neg, b3), axis=1, keepdims=True)
    gs = (m1 + m2)[:, 0, :]
    g_iota = lax.broadcasted_iota(jnp.int32, gs.shape, 0)
    g_sel = jnp.zeros(gs.shape, jnp.bool_)
    for _ in range(TOPK_GROUPS):
        gm = jnp.max(gs, axis=0, keepdims=True)
        gi = jnp.min(jnp.where(gs == gm, g_iota, N_GROUPS), axis=0, keepdims=True)
        pick = g_iota == gi
        g_sel = jnp.logical_or(g_sel, pick)
        gs = jnp.where(pick, neg, gs)
    masked = jnp.where(g_sel[:, None, :], b3, neg)
    flat = lax.broadcasted_iota(jnp.int32, b3.shape, 0) * GROUP_SIZE + e_in_g
    sel = jnp.zeros(b3.shape, jnp.bool_)
    for _ in range(TOP_K):
        m = jnp.max(jnp.max(masked, axis=1, keepdims=True), axis=0, keepdims=True)
        cand = jnp.where(masked == m, flat, N_EXPERTS)
        idx = jnp.min(jnp.min(cand, axis=1, keepdims=True), axis=0, keepdims=True)
        pick = flat == idx
        sel = jnp.logical_or(sel, pick)
        masked = jnp.where(pick, neg, masked)
    s3 = scores.reshape(N_GROUPS, GROUP_SIZE, ts)
    w = jnp.where(sel, s3, 0.0)
    tot = jnp.sum(jnp.sum(w, axis=1, keepdims=True), axis=0, keepdims=True)
    return (w / tot * ROUTED_SCALE).reshape(N_EXPERTS, ts), sel.reshape(N_EXPERTS, ts)


def _pack_bf16_pair(lo, hi):
    lo_bits = pltpu.bitcast(lo.astype(BF16).astype(F32), jnp.uint32) >> 16
    hi_bits = pltpu.bitcast(hi.astype(BF16).astype(F32), jnp.uint32) & jnp.uint32(0xFFFF0000)
    return lo_bits | hi_bits


def _unpack_bf16_pair(p):
    lo = pltpu.bitcast(p << 16, F32).astype(BF16)
    hi = pltpu.bitcast(p & jnp.uint32(0xFFFF0000), F32).astype(BF16)
    return lo, hi


def _outproj_kernel(x_ref, a_ref, p_ref, mod_ref, wo32_ref, g2_ref, wrt_ref, rb_ref, ut_ref,
                    x1_ref, hp_ref, rank_ref, wd_ref, cnt_ref, run_ref, wo_ref):
    @pl.when(_first_grid_step())
    def _():
        run_ref[...] = jnp.zeros(run_ref.shape, jnp.int32)
        wo_ref[...] = wo32_ref[...].astype(BF16)

    x = x_ref[0]
    gate1 = mod_ref[0, 2:3, :]
    shift2 = mod_ref[0, 3:4, :]
    scale2 = mod_ref[0, 4:5, :]
    mixp = _dot(a_ref[0], wo_ref[0:D_ATTN, :]) + _dot(p_ref[0], wo_ref[D_ATTN:, :])
    x1 = x + gate1 * mixp
    x1_ref[0] = x1
    h2 = _rms_mod(x1, g2_ref[...], scale2, shift2)
    half = h2.shape[-1] // 2
    hp_ref[0] = _pack_bf16_pair(h2[:, :half], h2[:, half:])
    logits_t = _dot3_nt(wrt_ref[...], h2)
    scores = 1.0 / (1.0 + jnp.exp(-logits_t))
    wd_t, sel = _route_t(scores, scores + rb_ref[...])
    wd_ref[...] = wd_t
    self_f = jnp.where(sel, 1.0, 0.0)
    before = _dot(self_f.astype(BF16), ut_ref[...]).astype(jnp.int32)
    run = run_ref[:, 0:1]
    rank_ref[...] = jnp.where(sel, run + before, -1)
    run_ref[...] = run_ref[...] + jnp.sum(self_f, axis=1, keepdims=True).astype(jnp.int32)
    cnt_ref[...] = run_ref[...]


def _outproj(x, attn, pool, mod3, w_out, g2, wr_t, rbias, ut, b0, B):
    _, S, D = x.shape
    ts = TS_PROJ
    n_s = S // ts
    const2 = lambda b, s: (0, 0)
    tok_in = lambda w: pl.BlockSpec((1, ts, w), lambda b, s: (b + b0, s, 0))
    tok = lambda w: pl.BlockSpec((1, ts, w), lambda b, s: (b, s, 0))
    tok_t = pl.BlockSpec((N_EXPERTS, ts), lambda b, s: (0, b * n_s + s))
    return pl.pallas_call(
        _outproj_kernel,
        grid=(B, n_s),
        in_specs=[tok_in(D), tok_in(D_ATTN), tok_in(D_POOL),
                  pl.BlockSpec((1, N_MOD, D), lambda b, s: (b + b0, 0, 0)),
                  pl.BlockSpec(w_out.shape, const2),
                  pl.BlockSpec((1, D), const2),
                  pl.BlockSpec(wr_t.shape, const2),
                  pl.BlockSpec((N_EXPERTS, 1), const2),
                  pl.BlockSpec((ts, ts), const2)],
        out_specs=[tok(D), tok(D // 2), tok_t, tok_t, pl.BlockSpec((N_EXPERTS, LANES), const2)],
        out_shape=[jax.ShapeDtypeStruct((B, S, D), F32),
                   jax.ShapeDtypeStruct((B, S, D // 2), jnp.uint32),
                   jax.ShapeDtypeStruct((N_EXPERTS, B * S), jnp.int32),
                   jax.ShapeDtypeStruct((N_EXPERTS, B * S), F32),
                   jax.ShapeDtypeStruct((N_EXPERTS, LANES), jnp.int32)],
        scratch_shapes=[pltpu.VMEM((N_EXPERTS, LANES), jnp.int32), pltpu.VMEM(w_out.shape, BF16)],
        compiler_params=pltpu.CompilerParams(dimension_semantics=("arbitrary", "arbitrary"),
                                             vmem_limit_bytes=VMEM_LIMIT),
        name="outproj_router",
    )(x, attn, pool, mod3, w_out, g2, wr_t, rbias, ut)


def _n_row_blocks(n_tokens):
    return -(-(n_tokens * TOP_K + N_EXPERTS * (ROW_BLOCK - 1)) // ROW_BLOCK)


def _slots_kernel(rank_ref, wd_ref, cnt_ref, lt_ref, slot_ref, wk_ref, be_ref, nu_ref):
    lt = lt_ref[...]
    nblk = lax.shift_right_logical(cnt_ref[...] + (ROW_BLOCK - 1), ROW_BLOCK_LOG2).astype(F32)
    nb_hi, nb_lo = _split_bf16(nblk)
    blk_start = _dot(lt, nb_hi) + _dot(lt, nb_lo)

    @pl.when(pl.program_id(0) == 0)
    def _():
        blk_end = (blk_start + nblk)[:, 0:1]
        b_iota = lax.broadcasted_iota(jnp.int32, (N_EXPERTS, be_ref.shape[-1]), 1).astype(F32)
        owner = jnp.sum(jnp.where(blk_end <= b_iota, 1, 0), axis=0, keepdims=True)
        be_ref[...] = jnp.minimum(owner, N_EXPERTS - 1)
        nu_ref[...] = jnp.broadcast_to(blk_end[N_EXPERTS - 1:, :].astype(jnp.int32), nu_ref.shape)

    rank = rank_ref[...]
    sel = rank >= 0
    row_start = (blk_start[:, 0:1] * ROW_BLOCK).astype(jnp.int32)
    slot_d = row_start + rank
    wd = wd_ref[...]
    choice = _dot(lt, jnp.where(sel, 1.0, 0.0).astype(BF16)).astype(jnp.int32)
    ts = rank.shape[-1]
    slots, wks = [], []
    for k in range(TOP_K):
        m = jnp.logical_and(sel, choice == k)
        slots.append(jnp.sum(jnp.where(m, slot_d, 0), axis=0, keepdims=True))
        wks.append(jnp.sum(jnp.where(m, wd, 0.0), axis=0, keepdims=True))
    slot_ref[...] = jnp.concatenate(slots + [jnp.zeros((SUBLANES - TOP_K, ts), jnp.int32)], axis=0)
    wk_pad = jnp.concatenate(wks + [jnp.zeros((LANES - TOP_K, ts), F32)], axis=0)
    wk_ref[...] = wk_pad.T


def _slots(rank_t, wd_t, cnt, lt):
    T = rank_t.shape[1]
    ts = TS_SLOT
    nb_pad = -(-_n_row_blocks(T) // LANES) * LANES
    const = lambda i: (0, 0)
    tok_t = pl.BlockSpec((N_EXPERTS, ts), lambda i: (0, i))
    return pl.pallas_call(
        _slots_kernel,
        grid=(T // ts,),
        in_specs=[tok_t, tok_t, pl.BlockSpec((N_EXPERTS, LANES), const), pl.BlockSpec((N_EXPERTS, N_EXPERTS), const)],
        out_specs=[pl.BlockSpec((SUBLANES, ts), lambda i: (0, i)),
                   pl.BlockSpec((ts, LANES), lambda i: (i, 0)),
                   pl.BlockSpec((1, nb_pad), const),
                   pl.BlockSpec((1, LANES), const)],
        out_shape=[jax.ShapeDtypeStruct((SUBLANES, T), jnp.int32),
                   jax.ShapeDtypeStruct((T, LANES), F32),
                   jax.ShapeDtypeStruct((1, nb_pad), jnp.int32),
                   jax.ShapeDtypeStruct((1, LANES), jnp.int32)],
        compiler_params=pltpu.CompilerParams(dimension_semantics=("arbitrary",), vmem_limit_bytes=VMEM_LIMIT),
        name="slots",
    )(rank_t, wd_t, cnt, lt)


def _sc_mesh():
    return plsc.VectorSubcoreMesh(core_axis_name="c", subcore_axis_name="s",
                                  num_cores=SC_CORES, num_subcores=SC_SUBCORES)


def _sc_dispatch(rows, slot_flat, n_out):
    T, width = rows.shape
    n_workers = SC_CORES * SC_SUBCORES
    per_worker = T // n_workers
    steps = per_worker // SC_CHUNK

    @functools.partial(
        pl.kernel, mesh=_sc_mesh(),
        out_type=jax.ShapeDtypeStruct((n_out, width), rows.dtype),
        scratch_types=[pltpu.VMEM((SC_CHUNK, width), rows.dtype)]
        + [pltpu.VMEM((SC_CHUNK,), jnp.int32)] * TOP_K + [pltpu.SemaphoreType.DMA],
        name="dispatch",
    )
    def run(rows_hbm, slot_hbm, out_hbm, rows_v, *rest):
        idx_v, sem = rest[:TOP_K], rest[TOP_K]
        base = (lax.axis_index("s") * SC_CORES + lax.axis_index("c")) * per_worker

        @pl.loop(0, steps)
        def _(i):
            off = base + i * SC_CHUNK
            pltpu.sync_copy(rows_hbm.at[pl.ds(off, SC_CHUNK)], rows_v)
            for k in range(TOP_K):
                pltpu.sync_copy(slot_hbm.at[pl.ds(k * T + off, SC_CHUNK)], idx_v[k])
            copies = [pltpu.async_copy(rows_v, out_hbm.at[idx_v[k]], sem) for k in range(TOP_K)]
            for cp in copies:
                cp.wait()

    return run(rows, slot_flat)


def _sc_gather(rows, idx):
    n = idx.shape[0]
    width = rows.shape[1]
    n_workers = SC_CORES * SC_SUBCORES
    per_worker = n // n_workers
    steps = per_worker // SC_CHUNK

    assert steps % 2 == 0
    slot_types = [pltpu.VMEM((SC_CHUNK,), jnp.int32), pltpu.VMEM((SC_CHUNK, width), rows.dtype),
                  pltpu.SemaphoreType.DMA]

    @functools.partial(
        pl.kernel, mesh=_sc_mesh(),
        out_type=jax.ShapeDtypeStruct((n, width), rows.dtype),
        scratch_types=slot_types * 2,
        name="combine",
    )
    def run(rows_hbm, idx_hbm, out_hbm, *scratch):
        base = (lax.axis_index("s") * SC_CORES + lax.axis_index("c")) * per_worker
        slots = (scratch[0:3], scratch[3:6])

        def gather(slot):
            idx_v, rows_v, sem = slots[slot]
            return pltpu.make_async_copy(rows_hbm.at[idx_v], rows_v, sem)

        def start(chunk, slot):
            pltpu.sync_copy(idx_hbm.at[pl.ds(base + chunk * SC_CHUNK, SC_CHUNK)], slots[slot][0])
            gather(slot).start()

        def finish(chunk, slot):
            gather(slot).wait()
            pltpu.sync_copy(slots[slot][1], out_hbm.at[pl.ds(base + chunk * SC_CHUNK, SC_CHUNK)])

        start(0, 0)

        @pl.loop(0, steps, step=2)
        def _(chunk):
            start(chunk + 1, 1)
            finish(chunk, 0)

            @pl.when(chunk + 2 < steps)
            def _():
                start(chunk + 2, 0)

            finish(chunk + 1, 1)

    return run(rows, idx)


def _swiglu_packed(xp, wgu, wd):
    lo, hi = _unpack_bf16_pair(xp)
    half = xp.shape[-1]
    gu = _dot(lo, wgu[:half, :]) + _dot(hi, wgu[half:, :])
    dh = wgu.shape[-1] // 2
    act = _silu(gu[:, :dh]) * gu[:, dh:]
    return _dot(act.astype(BF16), wd)


def _cast_swiglu_weights(wg32_ref, wu32_ref, wd32_ref, wgu_ref, wd_ref):
    dh = wg32_ref.shape[-1]
    wgu_ref[:, :dh] = wg32_ref[...].reshape(wg32_ref.shape[-2:]).astype(BF16)
    wgu_ref[:, dh:] = wu32_ref[...].reshape(wu32_ref.shape[-2:]).astype(BF16)
    wd_ref[...] = wd32_ref[...].reshape(wd32_ref.shape[-2:]).astype(BF16)


def _experts_kernel(be_ref, nu_ref, x_ref, *refs):
    n = BLOCKS_PER_STEP
    w32 = [refs[3 * j:3 * j + 3] for j in range(n)]
    y_ref = refs[3 * n]
    wbf = [refs[3 * n + 1 + 2 * j:3 * n + 3 + 2 * j] for j in range(n)]
    step = pl.program_id(0)
    for j in range(n):
        blk = step * n + j
        new_expert = jnp.logical_or(step == 0, be_ref[blk] != be_ref[jnp.maximum(blk - n, 0)])

        @pl.when(new_expert)
        def _():
            _cast_swiglu_weights(*w32[j], *wbf[j])

        @pl.when(blk < nu_ref[0])
        def _():
            rows = pl.ds(j * ROW_BLOCK, ROW_BLOCK)
            y = _swiglu_packed(x_ref[rows, :], wbf[j][0][...], wbf[j][1][...])
            half = y.shape[-1] // 2
            y_ref[rows, :] = _pack_bf16_pair(y[:, :half], y[:, half:])


def _experts(xs, block_expert, n_used, w_gate, w_up, w_down):
    P, half = xs.shape
    _, D, dh = w_gate.shape
    n = BLOCKS_PER_STEP
    rows = n * ROW_BLOCK
    assert P % rows == 0

    def by_expert(shape, j):
        return pl.BlockSpec((1,) + shape, lambda s, be, nu: (be[s * n + j], 0, 0))

    w_specs, w_args = [], []
    for j in range(n):
        w_specs += [by_expert((D, dh), j), by_expert((D, dh), j), by_expert((dh, D), j)]
        w_args += [w_gate, w_up, w_down]
    grid_spec = pltpu.PrefetchScalarGridSpec(
        num_scalar_prefetch=2,
        grid=(P // rows,),
        in_specs=[pl.BlockSpec((rows, half), lambda s, be, nu: (s, 0))] + w_specs,
        out_specs=pl.BlockSpec((rows, half), lambda s, be, nu: (s, 0)),
        scratch_shapes=[pltpu.VMEM((D, 2 * dh), BF16), pltpu.VMEM((dh, D), BF16)] * n,
    )
    return pl.pallas_call(
        _experts_kernel,
        grid_spec=grid_spec,
        out_shape=jax.ShapeDtypeStruct((P, half), jnp.uint32),
        compiler_params=pltpu.CompilerParams(dimension_semantics=("arbitrary",), vmem_limit_bytes=VMEM_LIMIT),
        name="experts",
    )(block_expert, n_used, xs, *w_args)


def _final_kernel(x1_ref, hp_ref, g_ref, wk_ref, mod_ref, sg32_ref, su32_ref, sd32_ref, *rest):
    o_ref, sgu_ref, sd_ref = rest[-3:]

    @pl.when(_first_grid_step())
    def _():
        _cast_swiglu_weights(sg32_ref, su32_ref, sd32_ref, sgu_ref, sd_ref)

    acc = _swiglu_packed(hp_ref[0], sgu_ref[...], sd_ref[...])
    wk = wk_ref[0]
    for k in range(TOP_K):
        lo, hi = _unpack_bf16_pair(g_ref[k, 0])
        y = jnp.concatenate([lo.astype(F32), hi.astype(F32)], axis=-1)
        acc = acc + wk[:, k:k + 1] * y
    gate2 = mod_ref[0, 5:6, :]
    o_ref[0] = x1_ref[0] + gate2 * acc


def _final(x1, hp, g, wk_tok, mod3, ws_gate, ws_up, ws_down, prev_out, b0, b_total):
    B, S, D = x1.shape
    dh = ws_gate.shape[-1]
    ts = TS_PROJ
    tok = lambda w: pl.BlockSpec((1, ts, w), lambda b, s: (b, s, 0))
    const2 = lambda b, s: (0, 0)
    in_specs = [tok(D), tok(D // 2),
                pl.BlockSpec((TOP_K, 1, ts, D // 2), lambda b, s: (0, b, s, 0)),
                tok(LANES),
                pl.BlockSpec((1, N_MOD, D), lambda b, s: (b + b0, 0, 0)),
                pl.BlockSpec(ws_gate.shape, const2),
                pl.BlockSpec(ws_up.shape, const2),
                pl.BlockSpec(ws_down.shape, const2)]
    args = [x1, hp, g, wk_tok, mod3, ws_gate, ws_up, ws_down]
    aliases = {}
    if prev_out is not None:
        in_specs.append(pl.BlockSpec(memory_space=pl.ANY))
        args.append(prev_out)
        aliases = {len(args) - 1: 0}
    return pl.pallas_call(
        _final_kernel,
        grid=(B, S // ts),
        in_specs=in_specs,
        out_specs=pl.BlockSpec((1, ts, D), lambda b, s: (b + b0, s, 0)),
        out_shape=jax.ShapeDtypeStruct((b_total, S, D), F32),
        scratch_shapes=[pltpu.VMEM((D, 2 * dh), BF16), pltpu.VMEM((dh, D), BF16)],
        input_output_aliases=aliases,
        compiler_params=pltpu.CompilerParams(dimension_semantics=("arbitrary", "arbitrary"),
                                             vmem_limit_bytes=VMEM_LIMIT),
        name="final",
    )(*args)


def _layer(x, c_act_mod, norm1_g, norm2_g, w_in, q_norm_g, k_norm_g, w_pool, pool_scale, w_out,
           w_router, router_bias, w_gate, w_up, w_down, ws_gate, ws_up, ws_down):
    B, S, D = x.shape
    mod3 = c_act_mod.reshape(B, N_MOD, D)
    head_of = jnp.arange(D_ATTN, dtype=jnp.int32) // HEAD_DIM
    head_mean = jnp.where(head_of[:, None] == head_of[None, :], 1.0 / HEAD_DIM, 0.0).astype(BF16)
    j = jnp.arange(TK, dtype=jnp.int32)
    umat = jnp.where(j[:, None] >= j[None, :], -1.0, 0.0).astype(BF16)

    q, k, v, pool = _inproj(
        x, mod3, norm1_g.reshape(1, D), w_in,
        jnp.tile(q_norm_g, N_HEADS).reshape(1, D_ATTN), jnp.tile(k_norm_g, N_HEADS).reshape(1, D_ATTN),
        head_mean, w_pool, pool_scale.reshape(1, D_POOL))
    attn = _attention(q, k, v, umat)
    t = jnp.arange(TS_PROJ, dtype=jnp.int32)
    ut = (t[:, None] < t[None, :]).astype(BF16)
    e = jnp.arange(N_EXPERTS, dtype=jnp.int32)
    lt = (e[None, :] < e[:, None]).astype(BF16)
    bp = B // MOE_PARTS
    T = bp * S
    n_blocks = _n_row_blocks(T)
    out = None
    for part in range(MOE_PARTS):
        b0 = part * bp
        x1, hp, rank_t, wd_t, cnt = _outproj(x, attn, pool, mod3, w_out, norm2_g.reshape(1, D),
                                             w_router.T, router_bias.reshape(N_EXPERTS, 1), ut, b0, bp)
        slots, wk_tok, block_expert, n_used = _slots(rank_t, wd_t, cnt, lt)
        slot_flat = slots[:TOP_K].reshape(TOP_K * T)
        xs = _sc_dispatch(hp.reshape(T, D // 2), slot_flat, n_blocks * ROW_BLOCK)
        ys = _experts(xs, block_expert[0, :n_blocks], n_used[0, :1], w_gate, w_up, w_down)
        g = _sc_gather(ys, slot_flat).reshape(TOP_K, bp, S, D // 2)
        out = _final(x1, hp, g, wk_tok.reshape(bp, S, LANES), mod3, ws_gate, ws_up, ws_down, out, b0, B)
    return out


def kernel(x, c, w_ada, b_ada, norm1_g, norm2_g, w_in, q_norm_g, k_norm_g, w_pool, pool_scale, w_out,
           w_router, router_bias, w_gate, w_up, w_down, ws_gate, ws_up, ws_down):
    depth = w_ada.shape[0]
    for l in range(depth):
        mod = _adaln(c, w_ada[l], b_ada[l])
        x = _layer(x, mod, norm1_g[l], norm2_g[l], w_in[l], q_norm_g[l], k_norm_g[l], w_pool[l],
                   pool_scale[l], w_out[l], w_router[l], router_bias[l], w_gate[l], w_up[l], w_down[l],
                   ws_gate[l], ws_up[l], ws_down[l])
    return x
```

```python
import functools

import jax
import jax.numpy as jnp
from jax import lax
from jax.experimental import pallas as pl
from jax.experimental.pallas import tpu as pltpu
from jax.experimental.pallas import tpu_sc as plsc

F32 = jnp.float32
BF16 = jnp.bfloat16

HEAD_DIM = 64
N_HEADS = 8
D_ATTN = N_HEADS * HEAD_DIM
POOL_WINDOWS = (2, 4, 8, 16)
POOL_GROUP_DIM = 128
D_POOL = len(POOL_WINDOWS) * POOL_GROUP_DIM
MAX_WINDOW = max(POOL_WINDOWS)
N_EXPERTS = 64
TOP_K = 6
N_GROUPS = 8
GROUP_SIZE = N_EXPERTS // N_GROUPS
TOPK_GROUPS = 4
ROUTED_SCALE = 2.5
RMS_EPS = 1e-6
N_MOD = 6

LANES = 128
SUBLANES = 8
VMEM_LIMIT = 56 * 1024 * 1024

TS_PROJ = 1024
TQ = 1024
TK = 256
TS_SLOT = 2048
ROW_BLOCK_LOG2 = 9
ROW_BLOCK = 1 << ROW_BLOCK_LOG2
BLOCKS_PER_STEP = 2
MOE_PARTS = 2
SC_CORES = 2
SC_SUBCORES = 16
SC_CHUNK = 64


def _split_bf16(a):
    hi = a.astype(BF16)
    lo = (a - hi.astype(F32)).astype(BF16)
    return hi, lo


def _dot(a, b):
    return jnp.dot(a, b, preferred_element_type=F32)


def _dot_nt(a, b):
    return lax.dot_general(a, b, (((1,), (1,)), ((), ())), preferred_element_type=F32)


def _dot3(a, b):
    ah, al = _split_bf16(a)
    bh, bl = _split_bf16(b)
    return _dot(ah, bh) + _dot(ah, bl) + _dot(al, bh)


def _dot3_nt(a, b):
    ah, al = _split_bf16(a)
    bh, bl = _split_bf16(b)
    return _dot_nt(ah, bh) + _dot_nt(ah, bl) + _dot_nt(al, bh)


def _silu(x):
    return x * (1.0 / (1.0 + jnp.exp(-x)))


def _rms_mod(x, gain, scale, shift):
    ms = jnp.mean(x * x, axis=-1, keepdims=True)
    y = x * lax.rsqrt(ms + RMS_EPS) * gain
    return y * (1.0 + scale) + shift


def _adaln_kernel(c_ref, w_ref, b_ref, o_ref):
    c = c_ref[...]
    o_ref[...] = _dot3(_silu(c), w_ref[...]) + b_ref[...]


def _adaln(c, w_ada, b_ada):
    nb, D = c.shape
    B = -(-nb // SUBLANES) * SUBLANES
    c = jnp.pad(c, ((0, B - nb), (0, 0)))
    N = w_ada.shape[1]
    tn = 1024
    out = pl.pallas_call(
        _adaln_kernel,
        grid=(N // tn,),
        in_specs=[pl.BlockSpec((B, D), lambda j: (0, 0)),
                  pl.BlockSpec((D, tn), lambda j: (0, j)),
                  pl.BlockSpec((1, tn), lambda j: (0, j))],
        out_specs=pl.BlockSpec((B, tn), lambda j: (0, j)),
        out_shape=jax.ShapeDtypeStruct((B, N), F32),
        compiler_params=pltpu.CompilerParams(dimension_semantics=("arbitrary",),
                                             vmem_limit_bytes=VMEM_LIMIT),
        name="adaln",
    )(c, w_ada, b_ada.reshape(1, N))
    return out[:nb]


def _first_grid_step():
    return jnp.logical_and(pl.program_id(0) == 0, pl.program_id(1) == 0)


def _inproj_kernel(x_ref, mod_ref, g1_ref, win32_ref, qg_ref, kg_ref, hm_ref, wp32_ref, ps_ref,
                   q_ref, k_ref, v_ref, p_ref, ext_ref, win_ref, wp_ref):
    @pl.when(_first_grid_step())
    def _():
        win_ref[...] = win32_ref[...].astype(BF16)
        wp_ref[...] = wp32_ref[...].astype(BF16)

    si = pl.program_id(1)
    ts = x_ref.shape[1]
    x = x_ref[0]
    shift1 = mod_ref[0, 0:1, :]
    scale1 = mod_ref[0, 1:2, :]
    h = _rms_mod(x, g1_ref[...], scale1, shift1)
    proj = _dot(h.astype(BF16), win_ref[...])

    hm = hm_ref[...]

    def head_norm(t, gain):
        ms = _dot((t * t).astype(BF16), hm)
        return t * lax.rsqrt(ms + RMS_EPS) * gain

    hq = proj[:, 0:D_ATTN]
    hk = proj[:, D_ATTN:2 * D_ATTN]
    q_ref[0] = (head_norm(hq, qg_ref[...]) * (HEAD_DIM ** -0.5)).astype(BF16)
    k_ref[0] = head_norm(hk, kg_ref[...]).astype(BF16)
    v_ref[0] = proj[:, 2 * D_ATTN:3 * D_ATTN].astype(BF16)

    hp = proj[:, 3 * D_ATTN:]

    @pl.when(si == 0)
    def _():
        ext_ref[0:MAX_WINDOW, :] = jnp.zeros((MAX_WINDOW, D_POOL), F32)

    ext_ref[MAX_WINDOW:, :] = hp
    pos = si * ts + lax.broadcasted_iota(jnp.int32, (ts, 1), 0)
    for g, w in enumerate(POOL_WINDOWS):
        lo_l, hi_l = g * POOL_GROUP_DIM, (g + 1) * POOL_GROUP_DIM
        u = hp[:, lo_l:hi_l]
        acc = u
        for i in range(1, w):
            acc = acc + ext_ref[pl.ds(MAX_WINDOW - i, ts), lo_l:hi_l]
        count = jnp.minimum(pos + 1, w).astype(F32)
        d = acc / count - u
        mixed = _dot(d.astype(BF16), wp_ref[g])
        p_ref[0, :, lo_l:hi_l] = (mixed * ps_ref[:, lo_l:hi_l]).astype(BF16)
    ext_ref[0:MAX_WINDOW, :] = hp[ts - MAX_WINDOW:, :]


def _inproj(x, mod3, g1, w_in, qg_t, kg_t, head_mean, w_pool, pool_scale):
    B, S, D = x.shape
    ts = TS_PROJ
    out_sd = jax.ShapeDtypeStruct((B, S, D_ATTN), BF16)
    blk = pl.BlockSpec((1, ts, D_ATTN), lambda b, s: (b, s, 0))
    const2 = lambda b, s: (0, 0)
    return pl.pallas_call(
        _inproj_kernel,
        grid=(B, S // ts),
        in_specs=[pl.BlockSpec((1, ts, D), lambda b, s: (b, s, 0)),
                  pl.BlockSpec((1, N_MOD, D), lambda b, s: (b, 0, 0)),
                  pl.BlockSpec((1, D), const2),
                  pl.BlockSpec(w_in.shape, const2),
                  pl.BlockSpec((1, D_ATTN), const2),
                  pl.BlockSpec((1, D_ATTN), const2),
                  pl.BlockSpec((D_ATTN, D_ATTN), const2),
                  pl.BlockSpec(w_pool.shape, lambda b, s: (0, 0, 0)),
                  pl.BlockSpec((1, D_POOL), const2)],
        out_specs=[blk, blk, blk, blk],
        out_shape=[out_sd, out_sd, out_sd, out_sd],
        scratch_shapes=[pltpu.VMEM((MAX_WINDOW + ts, D_POOL), F32),
                        pltpu.VMEM(w_in.shape, BF16), pltpu.VMEM(w_pool.shape, BF16)],
        compiler_params=pltpu.CompilerParams(dimension_semantics=("arbitrary", "arbitrary"),
                                             vmem_limit_bytes=VMEM_LIMIT),
        name="inproj",
    )(x, mod3, g1, w_in, qg_t, kg_t, head_mean, w_pool, pool_scale)


def _attn_kernel(q_ref, k_ref, v_ref, u_ref, o_ref, acc_ref):
    qi = pl.program_id(2)
    q = q_ref[0]
    lane = lax.broadcasted_iota(jnp.int32, (TQ, LANES), 1)
    first = lane < HEAD_DIM
    zero = jnp.zeros_like(q)
    qh = (jnp.where(first, q, zero), jnp.where(first, zero, q))
    u = u_ref[...]
    row = lax.broadcasted_iota(jnp.int32, (TQ, TK), 0)
    col = lax.broadcasted_iota(jnp.int32, (TQ, TK), 1)
    acc_ref[...] = jnp.zeros(acc_ref.shape, F32)

    def block(kb, survs, diag):
        start = pl.multiple_of(kb * TK, TK)
        k = k_ref[0, pl.ds(start, TK), :]
        v = v_ref[0, pl.ds(start, TK), :]
        r0 = 0 if diag is None else diag * TK
        if diag is not None:
            valid = (col + r0 < row)[r0:]
        out = []
        for h in range(2):
            z = _dot_nt(qh[h][r0:], k)
            zb = z.astype(BF16)
            sp = jnp.maximum(zb, 0) + jnp.log(1 + jnp.exp(-jnp.abs(zb)))
            if diag is not None:
                sp = jnp.where(valid, sp, jnp.zeros_like(sp))
            r = _dot(sp, u)
            arg = z + r + survs[h][r0:]
            if diag is not None:
                arg = jnp.where(valid, arg, -jnp.inf)
            acc_ref[h, r0:, :] += _dot(jnp.exp(arg).astype(BF16), v)
            surv = survs[h][r0:] + r[:, 0:1]
            out.append(surv if r0 == 0 else jnp.concatenate([survs[h][:r0], surv], axis=0))
        return tuple(out)

    survs = (jnp.zeros((TQ, 1), F32), jnp.zeros((TQ, 1), F32))
    n_diag = TQ // TK
    for d in reversed(range(n_diag)):
        survs = block(qi * n_diag + d, survs, d)
    def full_blocks(i, c):
        for d in range(n_diag):
            c = block((qi - i) * n_diag - 1 - d, c, None)
        return c

    lax.fori_loop(0, qi, full_blocks, survs)
    o_ref[0] = jnp.where(first, acc_ref[0], acc_ref[1]).astype(BF16)


def _attention(q, k, v, umat):
    B, S, _ = q.shape
    n_pairs = D_ATTN // LANES
    kv_spec = pl.BlockSpec((1, S, LANES), lambda b, p, i: (b, 0, p))
    q_spec = pl.BlockSpec((1, TQ, LANES), lambda b, p, i: (b, i, p))
    return pl.pallas_call(
        _attn_kernel,
        grid=(B, n_pairs, S // TQ),
        in_specs=[q_spec, kv_spec, kv_spec, pl.BlockSpec((TK, TK), lambda b, p, i: (0, 0))],
        out_specs=q_spec,
        out_shape=jax.ShapeDtypeStruct((B, S, D_ATTN), BF16),
        scratch_shapes=[pltpu.VMEM((2, TQ, LANES), F32)],
        compiler_params=pltpu.CompilerParams(
            dimension_semantics=("arbitrary", "arbitrary", "arbitrary"),
            vmem_limit_bytes=VMEM_LIMIT),
        name="stickbreak_attn",
    )(q, k, v, umat)


def _route_t(scores, biased):
    ts = scores.shape[-1]
    neg = -jnp.inf
    b3 = biased.reshape(N_GROUPS, GROUP_SIZE, ts)
    e_in_g = lax.broadcasted_iota(jnp.int32, b3.shape, 1)
    m1 = jnp.max(b3, axis=1, keepdims=True)
    i1 = jnp.min(jnp.where(b3 == m1, e_in_g, GROUP_SIZE), axis=1, keepdims=True)
    m2 = jnp.max(jnp.where(e_in_g == i1, neg, b3), axis=1, keepdims=True)
    gs = (m1 + m2)[:, 0, :]
    g_iota = lax.broadcasted_iota(jnp.int32, gs.shape, 0)
    g_sel = jnp.zeros(gs.shape, jnp.bool_)
    for _ in range(TOPK_GROUPS):
        gm = jnp.max(gs, axis=0, keepdims=True)
        gi = jnp.min(jnp.where(gs == gm, g_iota, N_GROUPS), axis=0, keepdims=True)
        pick = g_iota == gi
        g_sel = jnp.logical_or(g_sel, pick)
        gs = jnp.where(pick, neg, gs)
    masked = jnp.where(g_sel[:, None, :], b3, neg)
    flat = lax.broadcasted_iota(jnp.int32, b3.shape, 0) * GROUP_SIZE + e_in_g
    sel = jnp.zeros(b3.shape, jnp.bool_)
    for _ in range(TOP_K):
        m = jnp.max(jnp.max(masked, axis=1, keepdims=True), axis=0, keepdims=True)
        cand = jnp.where(masked == m, flat, N_EXPERTS)
        idx = jnp.min(jnp.min(cand, axis=1, keepdims=True), axis=0, keepdims=True)
        pick = flat == idx
        sel = jnp.logical_or(sel, pick)
        masked = jnp.where(pick, neg, masked)
    s3 = scores.reshape(N_GROUPS, GROUP_SIZE, ts)
    w = jnp.where(sel, s3, 0.0)
    tot = jnp.sum(jnp.sum(w, axis=1, keepdims=True), axis=0, keepdims=True)
    return (w / tot * ROUTED_SCALE).reshape(N_EXPERTS, ts), sel.reshape(N_EXPERTS, ts)


def _pack_bf16_pair(lo, hi):
    lo_bits = pltpu.bitcast(lo.astype(BF16).astype(F32), jnp.uint32) >> 16
    hi_bits = pltpu.bitcast(hi.astype(BF16).astype(F32), jnp.uint32) & jnp.uint32(0xFFFF0000)
    return lo_bits | hi_bits


def _unpack_bf16_pair(p):
    lo = pltpu.bitcast(p << 16, F32).astype(BF16)
    hi = pltpu.bitcast(p & jnp.uint32(0xFFFF0000), F32).astype(BF16)
    return lo, hi


def _outproj_kernel(x_ref, a_ref, p_ref, mod_ref, wo32_ref, g2_ref, wrt_ref, rb_ref, ut_ref,
                    x1_ref, hp_ref, rank_ref, wd_ref, cnt_ref, run_ref, wo_ref):
    @pl.when(_first_grid_step())
    def _():
        run_ref[...] = jnp.zeros(run_ref.shape, jnp.int32)
        wo_ref[...] = wo32_ref[...].astype(BF16)

    gate1 = mod_ref[0, 2:3, :]
    shift2 = mod_ref[0, 3:4, :]
    scale2 = mod_ref[0, 4:5, :]
    mixp = _dot(a_ref[0], wo_ref[0:D_ATTN, :]) + _dot(p_ref[0], wo_ref[D_ATTN:, :])
    x1 = x_ref[0] + gate1 * mixp
    x1_ref[0] = x1
    h2 = _rms_mod(x1, g2_ref[...], scale2, shift2)
    half = h2.shape[-1] // 2
    hp_ref[0] = _pack_bf16_pair(h2[:, :half], h2[:, half:])
    logits_t = _dot3_nt(wrt_ref[...], h2)
    scores = 1.0 / (1.0 + jnp.exp(-logits_t))
    wd_t, sel = _route_t(scores, scores + rb_ref[...])
    wd_ref[...] = wd_t
    self_f = jnp.where(sel, 1.0, 0.0)
    before = _dot(self_f.astype(BF16), ut_ref[...]).astype(jnp.int32)
    run = run_ref[:, 0:1]
    rank_ref[...] = jnp.where(sel, run + before, -1)
    run_ref[...] = run_ref[...] + jnp.sum(self_f, axis=1, keepdims=True).astype(jnp.int32)
    cnt_ref[...] = run_ref[...]


def _outproj(x, attn, pool, mod3, w_out, g2, wr_t, rbias, ut, b0, B):
    _, S, D = x.shape
    ts = TS_PROJ
    n_s = S // ts
    const2 = lambda b, s: (0, 0)
    tok_in = lambda w: pl.BlockSpec((1, ts, w), lambda b, s: (b + b0, s, 0))
    tok = lambda w: pl.BlockSpec((1, ts, w), lambda b, s: (b, s, 0))
    tok_t = pl.BlockSpec((N_EXPERTS, ts), lambda b, s: (0, b * n_s + s))
    return pl.pallas_call(
        _outproj_kernel,
        grid=(B, n_s),
        in_specs=[tok_in(D), tok_in(D_ATTN), tok_in(D_POOL),
                  pl.BlockSpec((1, N_MOD, D), lambda b, s: (b + b0, 0, 0)),
                  pl.BlockSpec(w_out.shape, const2),
                  pl.BlockSpec((1, D), const2),
                  pl.BlockSpec(wr_t.shape, const2),
                  pl.BlockSpec((N_EXPERTS, 1), const2),
                  pl.BlockSpec((ts, ts), const2)],
        out_specs=[tok(D), tok(D // 2), tok_t, tok_t, pl.BlockSpec((N_EXPERTS, LANES), const2)],
        out_shape=[jax.ShapeDtypeStruct((B, S, D), F32),
                   jax.ShapeDtypeStruct((B, S, D // 2), jnp.uint32),
                   jax.ShapeDtypeStruct((N_EXPERTS, B * S), jnp.int32),
                   jax.ShapeDtypeStruct((N_EXPERTS, B * S), F32),
                   jax.ShapeDtypeStruct((N_EXPERTS, LANES), jnp.int32)],
        scratch_shapes=[pltpu.VMEM((N_EXPERTS, LANES), jnp.int32), pltpu.VMEM(w_out.shape, BF16)],
        compiler_params=pltpu.CompilerParams(dimension_semantics=("arbitrary", "arbitrary"),
                                             vmem_limit_bytes=VMEM_LIMIT),
        name="outproj_router",
    )(x, attn, pool, mod3, w_out, g2, wr_t, rbias, ut)


def _n_row_blocks(n_tokens):
    n_blocks = -(-(n_tokens * TOP_K + N_EXPERTS * (ROW_BLOCK - 1)) // ROW_BLOCK)
    return -(-n_blocks // BLOCKS_PER_STEP) * BLOCKS_PER_STEP


def _slots_kernel(rank_ref, wd_ref, cnt_ref, lt_ref, slot_ref, wk_ref, be_ref, nu_ref):
    lt = lt_ref[...]
    nblk = lax.shift_right_logical(cnt_ref[...] + (ROW_BLOCK - 1), ROW_BLOCK_LOG2).astype(F32)
    nb_hi, nb_lo = _split_bf16(nblk)
    blk_start = _dot(lt, nb_hi) + _dot(lt, nb_lo)

    @pl.when(pl.program_id(0) == 0)
    def _():
        blk_end = (blk_start + nblk)[:, 0:1]
        b_iota = lax.broadcasted_iota(jnp.int32, (N_EXPERTS, be_ref.shape[-1]), 1).astype(F32)
        owner = jnp.sum(jnp.where(blk_end <= b_iota, 1, 0), axis=0, keepdims=True)
        be_ref[...] = jnp.minimum(owner, N_EXPERTS - 1)
        nu_ref[...] = jnp.broadcast_to(blk_end[N_EXPERTS - 1:, :].astype(jnp.int32), nu_ref.shape)

    rank = rank_ref[...]
    sel = rank >= 0
    row_start = (blk_start[:, 0:1] * ROW_BLOCK).astype(jnp.int32)
    slot_d = row_start + rank
    wd = wd_ref[...]
    choice = _dot(lt, jnp.where(sel, 1.0, 0.0).astype(BF16)).astype(jnp.int32)
    ts = rank.shape[-1]
    slots, wks = [], []
    for k in range(TOP_K):
        m = jnp.logical_and(sel, choice == k)
        slots.append(jnp.sum(jnp.where(m, slot_d, 0), axis=0, keepdims=True))
        wks.append(jnp.sum(jnp.where(m, wd, 0.0), axis=0, keepdims=True))
    slot_ref[...] = jnp.concatenate(slots + [jnp.zeros((SUBLANES - TOP_K, ts), jnp.int32)], axis=0)
    wk_pad = jnp.concatenate(wks + [jnp.zeros((LANES - TOP_K, ts), F32)], axis=0)
    wk_ref[...] = wk_pad.T


def _slots(rank_t, wd_t, cnt, lt):
    T = rank_t.shape[1]
    ts = TS_SLOT
    nb_pad = -(-_n_row_blocks(T) // LANES) * LANES
    const = lambda i: (0, 0)
    tok_t = pl.BlockSpec((N_EXPERTS, ts), lambda i: (0, i))
    return pl.pallas_call(
        _slots_kernel,
        grid=(T // ts,),
        in_specs=[tok_t, tok_t, pl.BlockSpec((N_EXPERTS, LANES), const), pl.BlockSpec((N_EXPERTS, N_EXPERTS), const)],
        out_specs=[pl.BlockSpec((SUBLANES, ts), lambda i: (0, i)),
                   pl.BlockSpec((ts, LANES), lambda i: (i, 0)),
                   pl.BlockSpec((1, nb_pad), const),
                   pl.BlockSpec((1, LANES), const)],
        out_shape=[jax.ShapeDtypeStruct((SUBLANES, T), jnp.int32),
                   jax.ShapeDtypeStruct((T, LANES), F32),
                   jax.ShapeDtypeStruct((1, nb_pad), jnp.int32),
                   jax.ShapeDtypeStruct((1, LANES), jnp.int32)],
        compiler_params=pltpu.CompilerParams(dimension_semantics=("arbitrary",), vmem_limit_bytes=VMEM_LIMIT),
        name="slots",
    )(rank_t, wd_t, cnt, lt)


def _sc_mesh():
    return plsc.VectorSubcoreMesh(core_axis_name="c", subcore_axis_name="s",
                                  num_cores=SC_CORES, num_subcores=SC_SUBCORES)


def _sc_dispatch(rows, slot_flat, n_out):
    T, width = rows.shape
    n_workers = SC_CORES * SC_SUBCORES
    per_worker = T // n_workers
    steps = per_worker // SC_CHUNK

    @functools.partial(
        pl.kernel, mesh=_sc_mesh(),
        out_type=jax.ShapeDtypeStruct((n_out, width), rows.dtype),
        scratch_types=[pltpu.VMEM((SC_CHUNK, width), rows.dtype)]
        + [pltpu.VMEM((SC_CHUNK,), jnp.int32)] * TOP_K + [pltpu.SemaphoreType.DMA],
        name="dispatch",
    )
    def run(rows_hbm, slot_hbm, out_hbm, rows_v, *rest):
        idx_v, sem = rest[:TOP_K], rest[TOP_K]
        base = (lax.axis_index("s") * SC_CORES + lax.axis_index("c")) * per_worker

        @pl.loop(0, steps)
        def _(i):
            off = base + i * SC_CHUNK
            pltpu.sync_copy(rows_hbm.at[pl.ds(off, SC_CHUNK)], rows_v)
            for k in range(TOP_K):
                pltpu.sync_copy(slot_hbm.at[pl.ds(k * T + off, SC_CHUNK)], idx_v[k])
            copies = [pltpu.async_copy(rows_v, out_hbm.at[idx_v[k]], sem) for k in range(TOP_K)]
            for cp in copies:
                cp.wait()

    return run(rows, slot_flat)


def _sc_gather(rows, idx):
    n = idx.shape[0]
    width = rows.shape[1]
    n_workers = SC_CORES * SC_SUBCORES
    per_worker = n // n_workers
    steps = per_worker // SC_CHUNK

    assert steps % 2 == 0
    slot_types = [pltpu.VMEM((SC_CHUNK,), jnp.int32), pltpu.VMEM((SC_CHUNK, width), rows.dtype),
                  pltpu.SemaphoreType.DMA]

    @functools.partial(
        pl.kernel, mesh=_sc_mesh(),
        out_type=jax.ShapeDtypeStruct((n, width), rows.dtype),
        scratch_types=slot_types * 2,
        name="combine",
    )
    def run(rows_hbm, idx_hbm, out_hbm, *scratch):
        base = (lax.axis_index("s") * SC_CORES + lax.axis_index("c")) * per_worker
        slots = (scratch[0:3], scratch[3:6])

        def gather(slot):
            idx_v, rows_v, sem = slots[slot]
            return pltpu.make_async_copy(rows_hbm.at[idx_v], rows_v, sem)

        def start(chunk, slot):
            pltpu.sync_copy(idx_hbm.at[pl.ds(base + chunk * SC_CHUNK, SC_CHUNK)], slots[slot][0])
            gather(slot).start()

        def finish(chunk, slot):
            gather(slot).wait()
            pltpu.sync_copy(slots[slot][1], out_hbm.at[pl.ds(base + chunk * SC_CHUNK, SC_CHUNK)])

        start(0, 0)

        @pl.loop(0, steps, step=2)
        def _(chunk):
            start(chunk + 1, 1)
            finish(chunk, 0)

            @pl.when(chunk + 2 < steps)
            def _():
                start(chunk + 2, 0)

            finish(chunk + 1, 1)

    return run(rows, idx)


def _swiglu_packed(xp, wgu, wd):
    lo, hi = _unpack_bf16_pair(xp)
    half = xp.shape[-1]
    gu = _dot(lo, wgu[:half, :]) + _dot(hi, wgu[half:, :])
    dh = wgu.shape[-1] // 2
    act = _silu(gu[:, :dh]) * gu[:, dh:]
    return _dot(act.astype(BF16), wd)


def _cast_swiglu_weights(wg32_ref, wu32_ref, wd32_ref, wgu_ref, wd_ref):
    dh = wg32_ref.shape[-1]
    wgu_ref[:, :dh] = wg32_ref[...].reshape(wg32_ref.shape[-2:]).astype(BF16)
    wgu_ref[:, dh:] = wu32_ref[...].reshape(wu32_ref.shape[-2:]).astype(BF16)
    wd_ref[...] = wd32_ref[...].reshape(wd32_ref.shape[-2:]).astype(BF16)


def _experts_kernel(be_ref, nu_ref, x_ref, *refs):
    n = BLOCKS_PER_STEP
    w32 = [refs[3 * j:3 * j + 3] for j in range(n)]
    y_ref = refs[3 * n]
    wbf = [refs[3 * n + 1 + 2 * j:3 * n + 3 + 2 * j] for j in range(n)]
    step = pl.program_id(0)
    for j in range(n):
        blk = step * n + j
        new_expert = jnp.logical_or(step == 0, be_ref[blk] != be_ref[jnp.maximum(blk - n, 0)])

        @pl.when(new_expert)
        def _():
            _cast_swiglu_weights(*w32[j], *wbf[j])

        @pl.when(blk < nu_ref[0])
        def _():
            rows = pl.ds(j * ROW_BLOCK, ROW_BLOCK)
            y = _swiglu_packed(x_ref[rows, :], wbf[j][0][...], wbf[j][1][...])
            half = y.shape[-1] // 2
            y_ref[rows, :] = _pack_bf16_pair(y[:, :half], y[:, half:])


def _experts(xs, block_expert, n_used, w_gate, w_up, w_down):
    P, half = xs.shape
    _, D, dh = w_gate.shape
    n = BLOCKS_PER_STEP
    rows = n * ROW_BLOCK
    assert P % rows == 0

    def by_expert(shape, j):
        return pl.BlockSpec((1,) + shape, lambda s, be, nu: (be[s * n + j], 0, 0))

    w_specs, w_args = [], []
    for j in range(n):
        w_specs += [by_expert((D, dh), j), by_expert((D, dh), j), by_expert((dh, D), j)]
        w_args += [w_gate, w_up, w_down]
    def row_step(s, be, nu):
        return jnp.minimum(s, lax.div(jnp.maximum(nu[0], 1) - 1, n)), 0

    grid_spec = pltpu.PrefetchScalarGridSpec(
        num_scalar_prefetch=2,
        grid=(P // rows,),
        in_specs=[pl.BlockSpec((rows, half), row_step)] + w_specs,
        out_specs=pl.BlockSpec((rows, half), row_step),
        scratch_shapes=[pltpu.VMEM((D, 2 * dh), BF16), pltpu.VMEM((dh, D), BF16)] * n,
    )
    return pl.pallas_call(
        _experts_kernel,
        grid_spec=grid_spec,
        out_shape=jax.ShapeDtypeStruct((P, half), jnp.uint32),
        compiler_params=pltpu.CompilerParams(dimension_semantics=("arbitrary",), vmem_limit_bytes=VMEM_LIMIT),
        name="experts",
    )(block_expert, n_used, xs, *w_args)


def _final_kernel(x1_ref, hp_ref, g_ref, wk_ref, mod_ref, sg32_ref, su32_ref, sd32_ref, *rest):
    o_ref, sgu_ref, sd_ref = rest[-3:]

    @pl.when(_first_grid_step())
    def _():
        _cast_swiglu_weights(sg32_ref, su32_ref, sd32_ref, sgu_ref, sd_ref)

    acc = _swiglu_packed(hp_ref[0], sgu_ref[...], sd_ref[...])
    wk = wk_ref[0]
    for k in range(TOP_K):
        lo, hi = _unpack_bf16_pair(g_ref[k, 0])
        y = jnp.concatenate([lo.astype(F32), hi.astype(F32)], axis=-1)
        acc = acc + wk[:, k:k + 1] * y
    gate2 = mod_ref[0, 5:6, :]
    o_ref[0] = x1_ref[0] + gate2 * acc


def _final(x1, hp, g, wk_tok, mod3, ws_gate, ws_up, ws_down, prev_out, b0, b_total):
    B, S, D = x1.shape
    dh = ws_gate.shape[-1]
    ts = TS_PROJ
    tok = lambda w: pl.BlockSpec((1, ts, w), lambda b, s: (b, s, 0))
    const2 = lambda b, s: (0, 0)
    in_specs = [tok(D), tok(D // 2),
                pl.BlockSpec((TOP_K, 1, ts, D // 2), lambda b, s: (0, b, s, 0)),
                tok(LANES),
                pl.BlockSpec((1, N_MOD, D), lambda b, s: (b + b0, 0, 0)),
                pl.BlockSpec(ws_gate.shape, const2),
                pl.BlockSpec(ws_up.shape, const2),
                pl.BlockSpec(ws_down.shape, const2)]
    args = [x1, hp, g, wk_tok, mod3, ws_gate, ws_up, ws_down]
    aliases = {}
    if prev_out is not None:
        in_specs.append(pl.BlockSpec(memory_space=pl.ANY))
        args.append(prev_out)
        aliases = {len(args) - 1: 0}
    return pl.pallas_call(
        _final_kernel,
        grid=(B, S // ts),
        in_specs=in_specs,
        out_specs=pl.BlockSpec((1, ts, D), lambda b, s: (b + b0, s, 0)),
        out_shape=jax.ShapeDtypeStruct((b_total, S, D), F32),
        scratch_shapes=[pltpu.VMEM((D, 2 * dh), BF16), pltpu.VMEM((dh, D), BF16)],
        input_output_aliases=aliases,
        compiler_params=pltpu.CompilerParams(dimension_semantics=("arbitrary", "arbitrary"),
                                             vmem_limit_bytes=VMEM_LIMIT),
        name="final",
    )(*args)


def _layer(x, c_act_mod, norm1_g, norm2_g, w_in, q_norm_g, k_norm_g, w_pool, pool_scale, w_out,
           w_router, router_bias, w_gate, w_up, w_down, ws_gate, ws_up, ws_down):
    B, S, D = x.shape
    mod3 = c_act_mod.reshape(B, N_MOD, D)
    head_of = jnp.arange(D_ATTN, dtype=jnp.int32) // HEAD_DIM
    head_mean = jnp.where(head_of[:, None] == head_of[None, :], 1.0 / HEAD_DIM, 0.0).astype(BF16)
    j = jnp.arange(TK, dtype=jnp.int32)
    umat = jnp.where(j[:, None] >= j[None, :], -1.0, 0.0).astype(BF16)

    q, k, v, pool = _inproj(
        x, mod3, norm1_g.reshape(1, D), w_in,
        jnp.tile(q_norm_g, N_HEADS).reshape(1, D_ATTN), jnp.tile(k_norm_g, N_HEADS).reshape(1, D_ATTN),
        head_mean, w_pool, pool_scale.reshape(1, D_POOL))
    attn = _attention(q, k, v, umat)
    t = jnp.arange(TS_PROJ, dtype=jnp.int32)
    ut = (t[:, None] < t[None, :]).astype(BF16)
    e = jnp.arange(N_EXPERTS, dtype=jnp.int32)
    lt = (e[None, :] < e[:, None]).astype(BF16)
    bp = B // MOE_PARTS
    T = bp * S
    n_blocks = _n_row_blocks(T)
    out = None
    for part in range(MOE_PARTS):
        b0 = part * bp
        x1, hp, rank_t, wd_t, cnt = _outproj(x, attn, pool, mod3, w_out, norm2_g.reshape(1, D),
                                             w_router.T, router_bias.reshape(N_EXPERTS, 1), ut, b0, bp)
        slots, wk_tok, block_expert, n_used = _slots(rank_t, wd_t, cnt, lt)
        slot_flat = slots[:TOP_K].reshape(TOP_K * T)
        xs = _sc_dispatch(hp.reshape(T, D // 2), slot_flat, n_blocks * ROW_BLOCK)
        ys = _experts(xs, block_expert[0, :n_blocks], n_used[0, :1], w_gate, w_up, w_down)
        g = _sc_gather(ys, slot_flat).reshape(TOP_K, bp, S, D // 2)
        out = _final(x1, hp, g, wk_tok.reshape(bp, S, LANES), mod3, ws_gate, ws_up, ws_down, out, b0, B)
    return out


def kernel(x, c, w_ada, b_ada, norm1_g, norm2_g, w_in, q_norm_g, k_norm_g, w_pool, pool_scale, w_out,
           w_router, router_bias, w_gate, w_up, w_down, ws_gate, ws_up, ws_down):
    depth = w_ada.shape[0]
    for l in range(depth):
        mod = _adaln(c, w_ada[l], b_ada[l])
        x = _layer(x, mod, norm1_g[l], norm2_g[l], w_in[l], q_norm_g[l], k_norm_g[l], w_pool[l],
                   pool_scale[l], w_out[l], w_router[l], router_bias[l], w_gate[l], w_up[l], w_down[l],
                   ws_gate[l], ws_up[l], ws_down[l])
    return x
```

```python
import functools

import jax
import jax.numpy as jnp
from jax import lax
from jax.experimental import pallas as pl
from jax.experimental.pallas import tpu as pltpu
from jax.experimental.pallas import tpu_sc as plsc

F32 = jnp.float32
BF16 = jnp.bfloat16

HEAD_DIM = 64
N_HEADS = 8
D_ATTN = N_HEADS * HEAD_DIM
POOL_WINDOWS = (2, 4, 8, 16)
POOL_GROUP_DIM = 128
D_POOL = len(POOL_WINDOWS) * POOL_GROUP_DIM
MAX_WINDOW = max(POOL_WINDOWS)
N_EXPERTS = 64
TOP_K = 6
N_GROUPS = 8
GROUP_SIZE = N_EXPERTS // N_GROUPS
TOPK_GROUPS = 4
ROUTED_SCALE = 2.5
RMS_EPS = 1e-6
N_MOD = 6

LANES = 128
SUBLANES = 8
VMEM_LIMIT = 56 * 1024 * 1024

TS_PROJ = 1024
TQ = 1024
TK = 256
TS_SLOT = 2048
ROW_BLOCK_LOG2 = 9
ROW_BLOCK = 1 << ROW_BLOCK_LOG2
BLOCKS_PER_STEP = 2
MOE_PARTS = 2
SC_CORES = 2
SC_SUBCORES = 16
SC_CHUNK = 64


def _split_bf16(a):
    hi = a.astype(BF16)
    lo = (a - hi.astype(F32)).astype(BF16)
    return hi, lo


def _dot(a, b):
    return jnp.dot(a, b, preferred_element_type=F32)


def _dot_nt(a, b):
    return lax.dot_general(a, b, (((1,), (1,)), ((), ())), preferred_element_type=F32)


def _dot3(a, b):
    ah, al = _split_bf16(a)
    bh, bl = _split_bf16(b)
    return _dot(ah, bh) + _dot(ah, bl) + _dot(al, bh)


def _dot3_nt(a, b):
    ah, al = _split_bf16(a)
    bh, bl = _split_bf16(b)
    return _dot_nt(ah, bh) + _dot_nt(ah, bl) + _dot_nt(al, bh)


def _silu(x):
    return x * (1.0 / (1.0 + jnp.exp(-x)))


def _rms_mod(x, gain, scale, shift):
    ms = jnp.mean(x * x, axis=-1, keepdims=True)
    y = x * lax.rsqrt(ms + RMS_EPS) * gain
    return y * (1.0 + scale) + shift


def _adaln_kernel(c_ref, w_ref, b_ref, o_ref):
    c = c_ref[...]
    o_ref[...] = _dot3(_silu(c), w_ref[...]) + b_ref[...]


def _adaln(c, w_ada, b_ada):
    nb, D = c.shape
    B = -(-nb // SUBLANES) * SUBLANES
    c = jnp.pad(c, ((0, B - nb), (0, 0)))
    N = w_ada.shape[1]
    tn = 1024
    out = pl.pallas_call(
        _adaln_kernel,
        grid=(N // tn,),
        in_specs=[pl.BlockSpec((B, D), lambda j: (0, 0)),
                  pl.BlockSpec((D, tn), lambda j: (0, j)),
                  pl.BlockSpec((1, tn), lambda j: (0, j))],
        out_specs=pl.BlockSpec((B, tn), lambda j: (0, j)),
        out_shape=jax.ShapeDtypeStruct((B, N), F32),
        compiler_params=pltpu.CompilerParams(dimension_semantics=("arbitrary",),
                                             vmem_limit_bytes=VMEM_LIMIT),
        name="adaln",
    )(c, w_ada, b_ada.reshape(1, N))
    return out[:nb]


def _first_grid_step():
    return jnp.logical_and(pl.program_id(0) == 0, pl.program_id(1) == 0)


def _inproj_kernel(x_ref, mod_ref, g1_ref, win32_ref, qg_ref, kg_ref, hm_ref, wp32_ref, ps_ref,
                   q_ref, k_ref, v_ref, p_ref, ext_ref, win_ref, wp_ref):
    @pl.when(_first_grid_step())
    def _():
        win_ref[...] = win32_ref[...].astype(BF16)
        wp_ref[...] = wp32_ref[...].astype(BF16)

    si = pl.program_id(1)
    ts = x_ref.shape[1]
    x = x_ref[0]
    shift1 = mod_ref[0, 0:1, :]
    scale1 = mod_ref[0, 1:2, :]
    h = _rms_mod(x, g1_ref[...], scale1, shift1)
    proj = _dot(h.astype(BF16), win_ref[...])

    hm = hm_ref[...]

    def head_norm(t, gain):
        ms = _dot((t * t).astype(BF16), hm)
        return t * lax.rsqrt(ms + RMS_EPS) * gain

    hq = proj[:, 0:D_ATTN]
    hk = proj[:, D_ATTN:2 * D_ATTN]
    q_ref[0] = (head_norm(hq, qg_ref[...]) * (HEAD_DIM ** -0.5)).astype(BF16)
    k_ref[0] = head_norm(hk, kg_ref[...]).astype(BF16)
    v_ref[0] = proj[:, 2 * D_ATTN:3 * D_ATTN].astype(BF16)

    hp = proj[:, 3 * D_ATTN:]

    @pl.when(si == 0)
    def _():
        ext_ref[0:MAX_WINDOW, :] = jnp.zeros((MAX_WINDOW, D_POOL), F32)

    ext_ref[MAX_WINDOW:, :] = hp
    pos = si * ts + lax.broadcasted_iota(jnp.int32, (ts, 1), 0)
    for g, w in enumerate(POOL_WINDOWS):
        lo_l, hi_l = g * POOL_GROUP_DIM, (g + 1) * POOL_GROUP_DIM
        u = hp[:, lo_l:hi_l]
        acc = u
        for i in range(1, w):
            acc = acc + ext_ref[pl.ds(MAX_WINDOW - i, ts), lo_l:hi_l]
        count = jnp.minimum(pos + 1, w).astype(F32)
        d = acc / count - u
        mixed = _dot(d.astype(BF16), wp_ref[g])
        p_ref[0, :, lo_l:hi_l] = (mixed * ps_ref[:, lo_l:hi_l]).astype(BF16)
    ext_ref[0:MAX_WINDOW, :] = hp[ts - MAX_WINDOW:, :]


def _inproj(x, mod3, g1, w_in, qg_t, kg_t, head_mean, w_pool, pool_scale):
    B, S, D = x.shape
    ts = TS_PROJ
    out_sd = jax.ShapeDtypeStruct((B, S, D_ATTN), BF16)
    blk = pl.BlockSpec((1, ts, D_ATTN), lambda b, s: (b, s, 0))
    const2 = lambda b, s: (0, 0)
    return pl.pallas_call(
        _inproj_kernel,
        grid=(B, S // ts),
        in_specs=[pl.BlockSpec((1, ts, D), lambda b, s: (b, s, 0)),
                  pl.BlockSpec((1, N_MOD, D), lambda b, s: (b, 0, 0)),
                  pl.BlockSpec((1, D), const2),
                  pl.BlockSpec(w_in.shape, const2),
                  pl.BlockSpec((1, D_ATTN), const2),
                  pl.BlockSpec((1, D_ATTN), const2),
                  pl.BlockSpec((D_ATTN, D_ATTN), const2),
                  pl.BlockSpec(w_pool.shape, lambda b, s: (0, 0, 0)),
                  pl.BlockSpec((1, D_POOL), const2)],
        out_specs=[blk, blk, blk, blk],
        out_shape=[out_sd, out_sd, out_sd, out_sd],
        scratch_shapes=[pltpu.VMEM((MAX_WINDOW + ts, D_POOL), F32),
                        pltpu.VMEM(w_in.shape, BF16), pltpu.VMEM(w_pool.shape, BF16)],
        compiler_params=pltpu.CompilerParams(dimension_semantics=("arbitrary", "arbitrary"),
                                             vmem_limit_bytes=VMEM_LIMIT),
        name="inproj",
    )(x, mod3, g1, w_in, qg_t, kg_t, head_mean, w_pool, pool_scale)


def _attn_kernel(q_ref, k_ref, v_ref, u_ref, o_ref, acc_ref):
    qi = pl.program_id(2)
    q = q_ref[0]
    lane = lax.broadcasted_iota(jnp.int32, (TQ, LANES), 1)
    first = lane < HEAD_DIM
    zero = jnp.zeros_like(q)
    qh = (jnp.where(first, q, zero), jnp.where(first, zero, q))
    u = u_ref[...]
    row = lax.broadcasted_iota(jnp.int32, (TQ, TK), 0)
    col = lax.broadcasted_iota(jnp.int32, (TQ, TK), 1)
    acc_ref[...] = jnp.zeros(acc_ref.shape, F32)

    def block(kb, survs, diag):
        start = pl.multiple_of(kb * TK, TK)
        k = k_ref[0, pl.ds(start, TK), :]
        v = v_ref[0, pl.ds(start, TK), :]
        r0 = 0 if diag is None else diag * TK
        if diag is not None:
            valid = (col + r0 < row)[r0:]
        out = []
        for h in range(2):
            z = _dot_nt(qh[h][r0:], k)
            zb = z.astype(BF16)
            sp = jnp.maximum(zb, 0) + jnp.log(1 + jnp.exp(-jnp.abs(zb)))
            if diag is not None:
                sp = jnp.where(valid, sp, jnp.zeros_like(sp))
            r = _dot(sp, u)
            arg = z + r + survs[h][r0:]
            if diag is not None:
                arg = jnp.where(valid, arg, -jnp.inf)
            acc_ref[h, r0:, :] += _dot(jnp.exp(arg).astype(BF16), v)
            surv = survs[h][r0:] + r[:, 0:1]
            out.append(surv if r0 == 0 else jnp.concatenate([survs[h][:r0], surv], axis=0))
        return tuple(out)

    survs = (jnp.zeros((TQ, 1), F32), jnp.zeros((TQ, 1), F32))
    n_diag = TQ // TK
    for d in reversed(range(n_diag)):
        survs = block(qi * n_diag + d, survs, d)
    def full_blocks(i, c):
        for d in range(n_diag):
            c = block((qi - i) * n_diag - 1 - d, c, None)
        return c

    lax.fori_loop(0, qi, full_blocks, survs)
    o_ref[0] = jnp.where(first, acc_ref[0], acc_ref[1]).astype(BF16)


def _attention(q, k, v, umat):
    B, S, _ = q.shape
    n_pairs = D_ATTN // LANES
    kv_spec = pl.BlockSpec((1, S, LANES), lambda b, p, i: (b, 0, p))
    q_spec = pl.BlockSpec((1, TQ, LANES), lambda b, p, i: (b, i, p))
    return pl.pallas_call(
        _attn_kernel,
        grid=(B, n_pairs, S // TQ),
        in_specs=[q_spec, kv_spec, kv_spec, pl.BlockSpec((TK, TK), lambda b, p, i: (0, 0))],
        out_specs=q_spec,
        out_shape=jax.ShapeDtypeStruct((B, S, D_ATTN), BF16),
        scratch_shapes=[pltpu.VMEM((2, TQ, LANES), F32)],
        compiler_params=pltpu.CompilerParams(
            dimension_semantics=("arbitrary", "arbitrary", "arbitrary"),
            vmem_limit_bytes=VMEM_LIMIT),
        name="stickbreak_attn",
    )(q, k, v, umat)


def _route_t(scores, biased):
    ts = scores.shape[-1]
    neg = -jnp.inf
    b3 = biased.reshape(N_GROUPS, GROUP_SIZE, ts)
    e_in_g = lax.broadcasted_iota(jnp.int32, b3.shape, 1)
    m1 = jnp.max(b3, axis=1, keepdims=True)
    i1 = jnp.min(jnp.where(b3 == m1, e_in_g, GROUP_SIZE), axis=1, keepdims=True)
    m2 = jnp.max(jnp.where(e_in_g == i1, neg, b3), axis=1, keepdims=True)
    gs = (m1 + m2)[:, 0, :]
    g_iota = lax.broadcasted_iota(jnp.int32, gs.shape, 0)
    g_sel = jnp.zeros(gs.shape, jnp.bool_)
    for _ in range(TOPK_GROUPS):
        gm = jnp.max(gs, axis=0, keepdims=True)
        gi = jnp.min(jnp.where(gs == gm, g_iota, N_GROUPS), axis=0, keepdims=True)
        pick = g_iota == gi
        g_sel = jnp.logical_or(g_sel, pick)
        gs = jnp.where(pick, neg, gs)
    masked = jnp.where(g_sel[:, None, :], b3, neg)
    flat = lax.broadcasted_iota(jnp.int32, b3.shape, 0) * GROUP_SIZE + e_in_g
    sel = jnp.zeros(b3.shape, jnp.bool_)
    for _ in range(TOP_K):
        m = jnp.max(jnp.max(masked, axis=1, keepdims=True), axis=0, keepdims=True)
        cand = jnp.where(masked == m, flat, N_EXPERTS)
        idx = jnp.min(jnp.min(cand, axis=1, keepdims=True), axis=0, keepdims=True)
        pick = flat == idx
        sel = jnp.logical_or(sel, pick)
        masked = jnp.where(pick, neg, masked)
    s3 = scores.reshape(N_GROUPS, GROUP_SIZE, ts)
    w = jnp.where(sel, s3, 0.0)
    tot = jnp.sum(jnp.sum(w, axis=1, keepdims=True), axis=0, keepdims=True)
    return (w / tot * ROUTED_SCALE).reshape(N_EXPERTS, ts), sel.reshape(N_EXPERTS, ts)


def _pack_bf16_pair(lo, hi):
    lo_bits = pltpu.bitcast(lo.astype(BF16).astype(F32), jnp.uint32) >> 16
    hi_bits = pltpu.bitcast(hi.astype(BF16).astype(F32), jnp.uint32) & jnp.uint32(0xFFFF0000)
    return lo_bits | hi_bits


def _unpack_bf16_pair(p):
    lo = pltpu.bitcast(p << 16, F32).astype(BF16)
    hi = pltpu.bitcast(p & jnp.uint32(0xFFFF0000), F32).astype(BF16)
    return lo, hi


def _outproj_kernel(x_ref, a_ref, p_ref, mod_ref, wo32_ref, g2_ref, wrt_ref, rb_ref, ut_ref,
                    x1_ref, hp_ref, rank_ref, wd_ref, cnt_ref, run_ref, wo_ref):
    @pl.when(_first_grid_step())
    def _():
        run_ref[...] = jnp.zeros(run_ref.shape, jnp.int32)
        wo_ref[...] = wo32_ref[...].astype(BF16)

    gate1 = mod_ref[0, 2:3, :]
    shift2 = mod_ref[0, 3:4, :]
    scale2 = mod_ref[0, 4:5, :]
    mixp = _dot(a_ref[0], wo_ref[0:D_ATTN, :]) + _dot(p_ref[0], wo_ref[D_ATTN:, :])
    x1 = x_ref[0] + gate1 * mixp
    x1_ref[0] = x1
    h2 = _rms_mod(x1, g2_ref[...], scale2, shift2)
    half = h2.shape[-1] // 2
    hp_ref[0] = _pack_bf16_pair(h2[:, :half], h2[:, half:])
    logits_t = _dot3_nt(wrt_ref[...], h2)
    scores = 1.0 / (1.0 + jnp.exp(-logits_t))
    wd_t, sel = _route_t(scores, scores + rb_ref[...])
    wd_ref[...] = wd_t
    self_f = jnp.where(sel, 1.0, 0.0)
    before = _dot(self_f.astype(BF16), ut_ref[...]).astype(jnp.int32)
    run = run_ref[:, 0:1]
    rank_ref[...] = jnp.where(sel, run + before, -1)
    run_ref[...] = run_ref[...] + jnp.sum(self_f, axis=1, keepdims=True).astype(jnp.int32)
    cnt_ref[...] = run_ref[...]


def _outproj(x, attn, pool, mod3, w_out, g2, wr_t, rbias, ut, b0, B):
    _, S, D = x.shape
    ts = TS_PROJ
    n_s = S // ts
    const2 = lambda b, s: (0, 0)
    tok_in = lambda w: pl.BlockSpec((1, ts, w), lambda b, s: (b + b0, s, 0))
    tok = lambda w: pl.BlockSpec((1, ts, w), lambda b, s: (b, s, 0))
    tok_t = pl.BlockSpec((N_EXPERTS, ts), lambda b, s: (0, b * n_s + s))
    return pl.pallas_call(
        _outproj_kernel,
        grid=(B, n_s),
        in_specs=[tok_in(D), tok_in(D_ATTN), tok_in(D_POOL),
                  pl.BlockSpec((1, N_MOD, D), lambda b, s: (b + b0, 0, 0)),
                  pl.BlockSpec(w_out.shape, const2),
                  pl.BlockSpec((1, D), const2),
                  pl.BlockSpec(wr_t.shape, const2),
                  pl.BlockSpec((N_EXPERTS, 1), const2),
                  pl.BlockSpec((ts, ts), const2)],
        out_specs=[tok(D), tok(D // 2), tok_t, tok_t, pl.BlockSpec((N_EXPERTS, LANES), const2)],
        out_shape=[jax.ShapeDtypeStruct((B, S, D), F32),
                   jax.ShapeDtypeStruct((B, S, D // 2), jnp.uint32),
                   jax.ShapeDtypeStruct((N_EXPERTS, B * S), jnp.int32),
                   jax.ShapeDtypeStruct((N_EXPERTS, B * S), F32),
                   jax.ShapeDtypeStruct((N_EXPERTS, LANES), jnp.int32)],
        scratch_shapes=[pltpu.VMEM((N_EXPERTS, LANES), jnp.int32), pltpu.VMEM(w_out.shape, BF16)],
        compiler_params=pltpu.CompilerParams(dimension_semantics=("arbitrary", "arbitrary"),
                                             vmem_limit_bytes=VMEM_LIMIT),
        name="outproj_router",
    )(x, attn, pool, mod3, w_out, g2, wr_t, rbias, ut)


def _n_row_blocks(n_tokens):
    n_blocks = -(-(n_tokens * TOP_K + N_EXPERTS * (ROW_BLOCK - 1)) // ROW_BLOCK)
    return -(-n_blocks // BLOCKS_PER_STEP) * BLOCKS_PER_STEP


def _slots_kernel(rank_ref, wd_ref, cnt_ref, lt_ref, slot_ref, wk_ref, be_ref, nu_ref):
    lt = lt_ref[...]
    nblk = lax.shift_right_logical(cnt_ref[...] + (ROW_BLOCK - 1), ROW_BLOCK_LOG2).astype(F32)
    nb_hi, nb_lo = _split_bf16(nblk)
    blk_start = _dot(lt, nb_hi) + _dot(lt, nb_lo)

    @pl.when(pl.program_id(0) == 0)
    def _():
        blk_end = (blk_start + nblk)[:, 0:1]
        b_iota = lax.broadcasted_iota(jnp.int32, (N_EXPERTS, be_ref.shape[-1]), 1).astype(F32)
        owner = jnp.sum(jnp.where(blk_end <= b_iota, 1, 0), axis=0, keepdims=True)
        be_ref[...] = jnp.minimum(owner, N_EXPERTS - 1)
        nu_ref[...] = jnp.broadcast_to(blk_end[N_EXPERTS - 1:, :].astype(jnp.int32), nu_ref.shape)

    rank = rank_ref[...]
    sel = rank >= 0
    row_start = (blk_start[:, 0:1] * ROW_BLOCK).astype(jnp.int32)
    slot_d = row_start + rank
    wd = wd_ref[...]
    choice = _dot(lt, jnp.where(sel, 1.0, 0.0).astype(BF16)).astype(jnp.int32)
    ts = rank.shape[-1]
    slots, wks = [], []
    for k in range(TOP_K):
        m = jnp.logical_and(sel, choice == k)
        slots.append(jnp.sum(jnp.where(m, slot_d, 0), axis=0, keepdims=True))
        wks.append(jnp.sum(jnp.where(m, wd, 0.0), axis=0, keepdims=True))
    slot_ref[...] = jnp.concatenate(slots + [jnp.zeros((SUBLANES - TOP_K, ts), jnp.int32)], axis=0)
    wk_pad = jnp.concatenate(wks + [jnp.zeros((LANES - TOP_K, ts), F32)], axis=0)
    wk_ref[...] = wk_pad.T


def _slots(rank_t, wd_t, cnt, lt):
    T = rank_t.shape[1]
    ts = TS_SLOT
    nb_pad = -(-_n_row_blocks(T) // LANES) * LANES
    const = lambda i: (0, 0)
    tok_t = pl.BlockSpec((N_EXPERTS, ts), lambda i: (0, i))
    return pl.pallas_call(
        _slots_kernel,
        grid=(T // ts,),
        in_specs=[tok_t, tok_t, pl.BlockSpec((N_EXPERTS, LANES), const), pl.BlockSpec((N_EXPERTS, N_EXPERTS), const)],
        out_specs=[pl.BlockSpec((SUBLANES, ts), lambda i: (0, i)),
                   pl.BlockSpec((ts, LANES), lambda i: (i, 0)),
                   pl.BlockSpec((1, nb_pad), const),
                   pl.BlockSpec((1, LANES), const)],
        out_shape=[jax.ShapeDtypeStruct((SUBLANES, T), jnp.int32),
                   jax.ShapeDtypeStruct((T, LANES), F32),
                   jax.ShapeDtypeStruct((1, nb_pad), jnp.int32),
                   jax.ShapeDtypeStruct((1, LANES), jnp.int32)],
        compiler_params=pltpu.CompilerParams(dimension_semantics=("arbitrary",), vmem_limit_bytes=VMEM_LIMIT),
        name="slots",
    )(rank_t, wd_t, cnt, lt)


def _sc_mesh():
    return plsc.VectorSubcoreMesh(core_axis_name="c", subcore_axis_name="s",
                                  num_cores=SC_CORES, num_subcores=SC_SUBCORES)


def _sc_dispatch(rows, slot_flat, n_out):
    T, width = rows.shape
    n_workers = SC_CORES * SC_SUBCORES
    per_worker = T // n_workers
    steps = per_worker // SC_CHUNK

    @functools.partial(
        pl.kernel, mesh=_sc_mesh(),
        out_type=jax.ShapeDtypeStruct((n_out, width), rows.dtype),
        scratch_types=[pltpu.VMEM((SC_CHUNK, width), rows.dtype)]
        + [pltpu.VMEM((SC_CHUNK,), jnp.int32)] * TOP_K + [pltpu.SemaphoreType.DMA],
        name="dispatch",
    )
    def run(rows_hbm, slot_hbm, out_hbm, rows_v, *rest):
        idx_v, sem = rest[:TOP_K], rest[TOP_K]
        base = (lax.axis_index("s") * SC_CORES + lax.axis_index("c")) * per_worker

        @pl.loop(0, steps)
        def _(i):
            off = base + i * SC_CHUNK
            pltpu.sync_copy(rows_hbm.at[pl.ds(off, SC_CHUNK)], rows_v)
            for k in range(TOP_K):
                pltpu.sync_copy(slot_hbm.at[pl.ds(k * T + off, SC_CHUNK)], idx_v[k])
            copies = [pltpu.async_copy(rows_v, out_hbm.at[idx_v[k]], sem) for k in range(TOP_K)]
            for cp in copies:
                cp.wait()

    return run(rows, slot_flat)


def _sc_gather(rows, idx):
    n = idx.shape[0]
    width = rows.shape[1]
    n_workers = SC_CORES * SC_SUBCORES
    per_worker = n // n_workers
    steps = per_worker // SC_CHUNK

    assert steps % 2 == 0
    slot_types = [pltpu.VMEM((SC_CHUNK,), jnp.int32), pltpu.VMEM((SC_CHUNK, width), rows.dtype),
                  pltpu.SemaphoreType.DMA]

    @functools.partial(
        pl.kernel, mesh=_sc_mesh(),
        out_type=jax.ShapeDtypeStruct((n, width), rows.dtype),
        scratch_types=slot_types * 2,
        name="combine",
    )
    def run(rows_hbm, idx_hbm, out_hbm, *scratch):
        base = (lax.axis_index("s") * SC_CORES + lax.axis_index("c")) * per_worker
        slots = (scratch[0:3], scratch[3:6])

        def gather(slot):
            idx_v, rows_v, sem = slots[slot]
            return pltpu.make_async_copy(rows_hbm.at[idx_v], rows_v, sem)

        def start(chunk, slot):
            pltpu.sync_copy(idx_hbm.at[pl.ds(base + chunk * SC_CHUNK, SC_CHUNK)], slots[slot][0])
            gather(slot).start()

        def finish(chunk, slot):
            gather(slot).wait()
            pltpu.sync_copy(slots[slot][1], out_hbm.at[pl.ds(base + chunk * SC_CHUNK, SC_CHUNK)])

        start(0, 0)

        @pl.loop(0, steps, step=2)
        def _(chunk):
            start(chunk + 1, 1)
            finish(chunk, 0)

            @pl.when(chunk + 2 < steps)
            def _():
                start(chunk + 2, 0)

            finish(chunk + 1, 1)

    return run(rows, idx)


def _swiglu_packed(xp, wgu, wd):
    lo, hi = _unpack_bf16_pair(xp)
    half = xp.shape[-1]
    gu = _dot(lo, wgu[:half, :]) + _dot(hi, wgu[half:, :])
    dh = wgu.shape[-1] // 2
    act = _silu(gu[:, :dh]) * gu[:, dh:]
    return _dot(act.astype(BF16), wd)


def _cast_swiglu_weights(wg32_ref, wu32_ref, wd32_ref, wgu_ref, wd_ref):
    dh = wg32_ref.shape[-1]
    wgu_ref[:, :dh] = wg32_ref[...].reshape(wg32_ref.shape[-2:]).astype(BF16)
    wgu_ref[:, dh:] = wu32_ref[...].reshape(wu32_ref.shape[-2:]).astype(BF16)
    wd_ref[...] = wd32_ref[...].reshape(wd32_ref.shape[-2:]).astype(BF16)


def _experts_kernel(be_ref, nu_ref, x_ref, *refs):
    n = BLOCKS_PER_STEP
    w32 = [refs[3 * j:3 * j + 3] for j in range(n)]
    y_ref = refs[3 * n]
    wbf = [refs[3 * n + 1 + 2 * j:3 * n + 3 + 2 * j] for j in range(n)]
    step = pl.program_id(0)
    for j in range(n):
        blk = step * n + j
        new_expert = jnp.logical_or(step == 0, be_ref[blk] != be_ref[jnp.maximum(blk - n, 0)])

        @pl.when(new_expert)
        def _():
            _cast_swiglu_weights(*w32[j], *wbf[j])

        @pl.when(blk < nu_ref[0])
        def _():
            rows = pl.ds(j * ROW_BLOCK, ROW_BLOCK)
            y = _swiglu_packed(x_ref[rows, :], wbf[j][0][...], wbf[j][1][...])
            half = y.shape[-1] // 2
            y_ref[rows, :] = _pack_bf16_pair(y[:, :half], y[:, half:])


def _experts(xs, block_expert, n_used, w_gate, w_up, w_down):
    P, half = xs.shape
    _, D, dh = w_gate.shape
    n = BLOCKS_PER_STEP
    rows = n * ROW_BLOCK
    assert P % rows == 0

    def by_expert(shape, j):
        return pl.BlockSpec((1,) + shape, lambda s, be, nu: (be[s * n + j], 0, 0))

    w_specs, w_args = [], []
    for j in range(n):
        w_specs += [by_expert((D, dh), j), by_expert((D, dh), j), by_expert((dh, D), j)]
        w_args += [w_gate, w_up, w_down]
    def row_step(s, be, nu):
        return jnp.minimum(s, lax.div(jnp.maximum(nu[0], 1) - 1, n)), 0

    grid_spec = pltpu.PrefetchScalarGridSpec(
        num_scalar_prefetch=2,
        grid=(P // rows,),
        in_specs=[pl.BlockSpec((rows, half), row_step)] + w_specs,
        out_specs=pl.BlockSpec((rows, half), row_step),
        scratch_shapes=[pltpu.VMEM((D, 2 * dh), BF16), pltpu.VMEM((dh, D), BF16)] * n,
    )
    return pl.pallas_call(
        _experts_kernel,
        grid_spec=grid_spec,
        out_shape=jax.ShapeDtypeStruct((P, half), jnp.uint32),
        input_output_aliases={2: 0},
        compiler_params=pltpu.CompilerParams(dimension_semantics=("arbitrary",), vmem_limit_bytes=VMEM_LIMIT),
        name="experts",
    )(block_expert, n_used, xs, *w_args)


def _final_kernel(x1_ref, hp_ref, g_ref, wk_ref, mod_ref, sg32_ref, su32_ref, sd32_ref, *rest):
    o_ref, sgu_ref, sd_ref = rest[-3:]

    @pl.when(_first_grid_step())
    def _():
        _cast_swiglu_weights(sg32_ref, su32_ref, sd32_ref, sgu_ref, sd_ref)

    acc = _swiglu_packed(hp_ref[0], sgu_ref[...], sd_ref[...])
    wk = wk_ref[0]
    for k in range(TOP_K):
        lo, hi = _unpack_bf16_pair(g_ref[k, 0])
        y = jnp.concatenate([lo.astype(F32), hi.astype(F32)], axis=-1)
        acc = acc + wk[:, k:k + 1] * y
    gate2 = mod_ref[0, 5:6, :]
    o_ref[0] = x1_ref[0] + gate2 * acc


def _final(x1, hp, g, wk_tok, mod3, ws_gate, ws_up, ws_down, prev_out, b0, b_total):
    B, S, D = x1.shape
    dh = ws_gate.shape[-1]
    ts = TS_PROJ
    tok = lambda w: pl.BlockSpec((1, ts, w), lambda b, s: (b, s, 0))
    const2 = lambda b, s: (0, 0)
    in_specs = [tok(D), tok(D // 2),
                pl.BlockSpec((TOP_K, 1, ts, D // 2), lambda b, s: (0, b, s, 0)),
                tok(LANES),
                pl.BlockSpec((1, N_MOD, D), lambda b, s: (b + b0, 0, 0)),
                pl.BlockSpec(ws_gate.shape, const2),
                pl.BlockSpec(ws_up.shape, const2),
                pl.BlockSpec(ws_down.shape, const2)]
    args = [x1, hp, g, wk_tok, mod3, ws_gate, ws_up, ws_down]
    aliases = {}
    if prev_out is not None:
        in_specs.append(pl.BlockSpec(memory_space=pl.ANY))
        args.append(prev_out)
        aliases = {len(args) - 1: 0}
    return pl.pallas_call(
        _final_kernel,
        grid=(B, S // ts),
        in_specs=in_specs,
        out_specs=pl.BlockSpec((1, ts, D), lambda b, s: (b + b0, s, 0)),
        out_shape=jax.ShapeDtypeStruct((b_total, S, D), F32),
        scratch_shapes=[pltpu.VMEM((D, 2 * dh), BF16), pltpu.VMEM((dh, D), BF16)],
        input_output_aliases=aliases,
        compiler_params=pltpu.CompilerParams(dimension_semantics=("arbitrary", "arbitrary"),
                                             vmem_limit_bytes=VMEM_LIMIT),
        name="final",
    )(*args)


def _layer(x, c_act_mod, norm1_g, norm2_g, w_in, q_norm_g, k_norm_g, w_pool, pool_scale, w_out,
           w_router, router_bias, w_gate, w_up, w_down, ws_gate, ws_up, ws_down):
    B, S, D = x.shape
    mod3 = c_act_mod.reshape(B, N_MOD, D)
    head_of = jnp.arange(D_ATTN, dtype=jnp.int32) // HEAD_DIM
    head_mean = jnp.where(head_of[:, None] == head_of[None, :], 1.0 / HEAD_DIM, 0.0).astype(BF16)
    j = jnp.arange(TK, dtype=jnp.int32)
    umat = jnp.where(j[:, None] >= j[None, :], -1.0, 0.0).astype(BF16)

    q, k, v, pool = _inproj(
        x, mod3, norm1_g.reshape(1, D), w_in,
        jnp.tile(q_norm_g, N_HEADS).reshape(1, D_ATTN), jnp.tile(k_norm_g, N_HEADS).reshape(1, D_ATTN),
        head_mean, w_pool, pool_scale.reshape(1, D_POOL))
    attn = _attention(q, k, v, umat)
    t = jnp.arange(TS_PROJ, dtype=jnp.int32)
    ut = (t[:, None] < t[None, :]).astype(BF16)
    e = jnp.arange(N_EXPERTS, dtype=jnp.int32)
    lt = (e[None, :] < e[:, None]).astype(BF16)
    bp = B // MOE_PARTS
    T = bp * S
    n_blocks = _n_row_blocks(T)
    out = None
    for part in range(MOE_PARTS):
        b0 = part * bp
        x1, hp, rank_t, wd_t, cnt = _outproj(x, attn, pool, mod3, w_out, norm2_g.reshape(1, D),
                                             w_router.T, router_bias.reshape(N_EXPERTS, 1), ut, b0, bp)
        slots, wk_tok, block_expert, n_used = _slots(rank_t, wd_t, cnt, lt)
        slot_flat = slots[:TOP_K].reshape(TOP_K * T)
        xs = _sc_dispatch(hp.reshape(T, D // 2), slot_flat, n_blocks * ROW_BLOCK)
        ys = _experts(xs, block_expert[0, :n_blocks], n_used[0, :1], w_gate, w_up, w_down)
        g = _sc_gather(ys, slot_flat).reshape(TOP_K, bp, S, D // 2)
        out = _final(x1, hp, g, wk_tok.reshape(bp, S, LANES), mod3, ws_gate, ws_up, ws_down, out, b0, B)
    return out


def kernel(x, c, w_ada, b_ada, norm1_g, norm2_g, w_in, q_norm_g, k_norm_g, w_pool, pool_scale, w_out,
           w_router, router_bias, w_gate, w_up, w_down, ws_gate, ws_up, ws_down):
    depth = w_ada.shape[0]
    for l in range(depth):
        mod = _adaln(c, w_ada[l], b_ada[l])
        x = _layer(x, mod, norm1_g[l], norm2_g[l], w_in[l], q_norm_g[l], k_norm_g[l], w_pool[l],
                   pool_scale[l], w_out[l], w_router[l], router_bias[l], w_gate[l], w_up[l], w_down[l],
                   ws_gate[l], ws_up[l], ws_down[l])
    return x
```

```python
import functools

import jax
import jax.numpy as jnp
from jax import lax
from jax.experimental import pallas as pl
from jax.experimental.pallas import tpu as pltpu
from jax.experimental.pallas import tpu_sc as plsc

F32 = jnp.float32
BF16 = jnp.bfloat16

HEAD_DIM = 64
N_HEADS = 8
D_ATTN = N_HEADS * HEAD_DIM
POOL_WINDOWS = (2, 4, 8, 16)
POOL_GROUP_DIM = 128
D_POOL = len(POOL_WINDOWS) * POOL_GROUP_DIM
MAX_WINDOW = max(POOL_WINDOWS)
N_EXPERTS = 64
TOP_K = 6
N_GROUPS = 8
GROUP_SIZE = N_EXPERTS // N_GROUPS
TOPK_GROUPS = 4
ROUTED_SCALE = 2.5
RMS_EPS = 1e-6
N_MOD = 6

LANES = 128
SUBLANES = 8
VMEM_LIMIT = 56 * 1024 * 1024

TS_PROJ = 1024
TQ = 1024
TK = 256
TS_SLOT = 2048
ROW_BLOCK_LOG2 = 9
ROW_BLOCK = 1 << ROW_BLOCK_LOG2
BLOCKS_PER_STEP = 2
MOE_PARTS = 2
SC_CORES = 2
SC_SUBCORES = 16
SC_CHUNK = 64


def _split_bf16(a):
    hi = a.astype(BF16)
    lo = (a - hi.astype(F32)).astype(BF16)
    return hi, lo


def _dot(a, b):
    return jnp.dot(a, b, preferred_element_type=F32)


def _dot_nt(a, b):
    return lax.dot_general(a, b, (((1,), (1,)), ((), ())), preferred_element_type=F32)


def _dot3(a, b):
    ah, al = _split_bf16(a)
    bh, bl = _split_bf16(b)
    return _dot(ah, bh) + _dot(ah, bl) + _dot(al, bh)


def _dot3_nt(a, b):
    ah, al = _split_bf16(a)
    bh, bl = _split_bf16(b)
    return _dot_nt(ah, bh) + _dot_nt(ah, bl) + _dot_nt(al, bh)


def _silu(x):
    return x * (1.0 / (1.0 + jnp.exp(-x)))


def _rms_mod(x, gain, scale, shift):
    ms = jnp.mean(x * x, axis=-1, keepdims=True)
    y = x * lax.rsqrt(ms + RMS_EPS) * gain
    return y * (1.0 + scale) + shift


def _adaln_kernel(c_ref, w_ref, b_ref, o_ref):
    c = c_ref[...]
    o_ref[...] = _dot3(_silu(c), w_ref[...]) + b_ref[...]


def _adaln(c, w_ada, b_ada):
    nb, D = c.shape
    B = -(-nb // SUBLANES) * SUBLANES
    c = jnp.pad(c, ((0, B - nb), (0, 0)))
    N = w_ada.shape[1]
    tn = 1024
    out = pl.pallas_call(
        _adaln_kernel,
        grid=(N // tn,),
        in_specs=[pl.BlockSpec((B, D), lambda j: (0, 0)),
                  pl.BlockSpec((D, tn), lambda j: (0, j)),
                  pl.BlockSpec((1, tn), lambda j: (0, j))],
        out_specs=pl.BlockSpec((B, tn), lambda j: (0, j)),
        out_shape=jax.ShapeDtypeStruct((B, N), F32),
        compiler_params=pltpu.CompilerParams(dimension_semantics=("arbitrary",),
                                             vmem_limit_bytes=VMEM_LIMIT),
        name="adaln",
    )(c, w_ada, b_ada.reshape(1, N))
    return out[:nb]


def _first_grid_step():
    return jnp.logical_and(pl.program_id(0) == 0, pl.program_id(1) == 0)


def _inproj_kernel(x_ref, mod_ref, g1_ref, win32_ref, qg_ref, kg_ref, hm_ref, wp32_ref, ps_ref,
                   q_ref, k_ref, v_ref, p_ref, ext_ref, win_ref, wp_ref):
    @pl.when(_first_grid_step())
    def _():
        win_ref[...] = win32_ref[...].astype(BF16)
        wp_ref[...] = wp32_ref[...].astype(BF16)

    si = pl.program_id(1)
    ts = x_ref.shape[1]
    x = x_ref[0]
    shift1 = mod_ref[0, 0:1, :]
    scale1 = mod_ref[0, 1:2, :]
    h = _rms_mod(x, g1_ref[...], scale1, shift1)
    proj = _dot(h.astype(BF16), win_ref[...])

    hm = hm_ref[...]

    def head_norm(t, gain):
        ms = _dot((t * t).astype(BF16), hm)
        return t * lax.rsqrt(ms + RMS_EPS) * gain

    hq = proj[:, 0:D_ATTN]
    hk = proj[:, D_ATTN:2 * D_ATTN]
    q_ref[0] = (head_norm(hq, qg_ref[...]) * (HEAD_DIM ** -0.5)).astype(BF16)
    k_ref[0] = head_norm(hk, kg_ref[...]).astype(BF16)
    v_ref[0] = proj[:, 2 * D_ATTN:3 * D_ATTN].astype(BF16)

    hp = proj[:, 3 * D_ATTN:]

    @pl.when(si == 0)
    def _():
        ext_ref[0:MAX_WINDOW, :] = jnp.zeros((MAX_WINDOW, D_POOL), F32)

    ext_ref[MAX_WINDOW:, :] = hp
    pos = si * ts + lax.broadcasted_iota(jnp.int32, (ts, 1), 0)
    for g, w in enumerate(POOL_WINDOWS):
        lo_l, hi_l = g * POOL_GROUP_DIM, (g + 1) * POOL_GROUP_DIM
        u = hp[:, lo_l:hi_l]
        acc = u
        for i in range(1, w):
            acc = acc + ext_ref[pl.ds(MAX_WINDOW - i, ts), lo_l:hi_l]
        count = jnp.minimum(pos + 1, w).astype(F32)
        d = acc / count - u
        mixed = _dot(d.astype(BF16), wp_ref[g])
        p_ref[0, :, lo_l:hi_l] = (mixed * ps_ref[:, lo_l:hi_l]).astype(BF16)
    ext_ref[0:MAX_WINDOW, :] = hp[ts - MAX_WINDOW:, :]


def _inproj(x, mod3, g1, w_in, qg_t, kg_t, head_mean, w_pool, pool_scale):
    B, S, D = x.shape
    ts = TS_PROJ
    out_sd = jax.ShapeDtypeStruct((B, S, D_ATTN), BF16)
    blk = pl.BlockSpec((1, ts, D_ATTN), lambda b, s: (b, s, 0))
    const2 = lambda b, s: (0, 0)
    return pl.pallas_call(
        _inproj_kernel,
        grid=(B, S // ts),
        in_specs=[pl.BlockSpec((1, ts, D), lambda b, s: (b, s, 0)),
                  pl.BlockSpec((1, N_MOD, D), lambda b, s: (b, 0, 0)),
                  pl.BlockSpec((1, D), const2),
                  pl.BlockSpec(w_in.shape, const2),
                  pl.BlockSpec((1, D_ATTN), const2),
                  pl.BlockSpec((1, D_ATTN), const2),
                  pl.BlockSpec((D_ATTN, D_ATTN), const2),
                  pl.BlockSpec(w_pool.shape, lambda b, s: (0, 0, 0)),
                  pl.BlockSpec((1, D_POOL), const2)],
        out_specs=[blk, blk, blk, blk],
        out_shape=[out_sd, out_sd, out_sd, out_sd],
        scratch_shapes=[pltpu.VMEM((MAX_WINDOW + ts, D_POOL), F32),
                        pltpu.VMEM(w_in.shape, BF16), pltpu.VMEM(w_pool.shape, BF16)],
        compiler_params=pltpu.CompilerParams(dimension_semantics=("arbitrary", "arbitrary"),
                                             vmem_limit_bytes=VMEM_LIMIT),
        name="inproj",
    )(x, mod3, g1, w_in, qg_t, kg_t, head_mean, w_pool, pool_scale)


def _attn_kernel(q_ref, k_ref, v_ref, u_ref, o_ref, acc_ref):
    qi = pl.program_id(2)
    q = q_ref[0]
    lane = lax.broadcasted_iota(jnp.int32, (TQ, LANES), 1)
    first = lane < HEAD_DIM
    zero = jnp.zeros_like(q)
    qh = (jnp.where(first, q, zero), jnp.where(first, zero, q))
    u = u_ref[...]
    row = lax.broadcasted_iota(jnp.int32, (TQ, TK), 0)
    col = lax.broadcasted_iota(jnp.int32, (TQ, TK), 1)
    acc_ref[...] = jnp.zeros(acc_ref.shape, F32)

    def block(kb, survs, diag):
        start = pl.multiple_of(kb * TK, TK)
        k = k_ref[0, pl.ds(start, TK), :]
        v = v_ref[0, pl.ds(start, TK), :]
        r0 = 0 if diag is None else diag * TK
        if diag is not None:
            valid = (col + r0 < row)[r0:]
        out = []
        for h in range(2):
            z = _dot_nt(qh[h][r0:], k)
            zb = z.astype(BF16)
            sp = jnp.maximum(zb, 0) + jnp.log(1 + jnp.exp(-jnp.abs(zb)))
            if diag is not None:
                sp = jnp.where(valid, sp, jnp.zeros_like(sp))
            r = _dot(sp, u)
            arg = z + r + survs[h][r0:]
            if diag is not None:
                arg = jnp.where(valid, arg, -jnp.inf)
            acc_ref[h, r0:, :] += _dot(jnp.exp(arg).astype(BF16), v)
            surv = survs[h][r0:] + r[:, 0:1]
            out.append(surv if r0 == 0 else jnp.concatenate([survs[h][:r0], surv], axis=0))
        return tuple(out)

    survs = (jnp.zeros((TQ, 1), F32), jnp.zeros((TQ, 1), F32))
    n_diag = TQ // TK
    for d in reversed(range(n_diag)):
        survs = block(qi * n_diag + d, survs, d)
    def full_blocks(i, c):
        for d in range(n_diag):
            c = block((qi - i) * n_diag - 1 - d, c, None)
        return c

    lax.fori_loop(0, qi, full_blocks, survs)
    o_ref[0] = jnp.where(first, acc_ref[0], acc_ref[1]).astype(BF16)


def _attention(q, k, v, umat):
    B, S, _ = q.shape
    n_pairs = D_ATTN // LANES
    kv_spec = pl.BlockSpec((1, S, LANES), lambda b, p, i: (b, 0, p))
    q_spec = pl.BlockSpec((1, TQ, LANES), lambda b, p, i: (b, i, p))
    return pl.pallas_call(
        _attn_kernel,
        grid=(B, n_pairs, S // TQ),
        in_specs=[q_spec, kv_spec, kv_spec, pl.BlockSpec((TK, TK), lambda b, p, i: (0, 0))],
        out_specs=q_spec,
        out_shape=jax.ShapeDtypeStruct((B, S, D_ATTN), BF16),
        scratch_shapes=[pltpu.VMEM((2, TQ, LANES), F32)],
        compiler_params=pltpu.CompilerParams(
            dimension_semantics=("arbitrary", "arbitrary", "arbitrary"),
            vmem_limit_bytes=VMEM_LIMIT),
        name="stickbreak_attn",
    )(q, k, v, umat)


def _route_t(scores, biased):
    ts = scores.shape[-1]
    neg = -jnp.inf
    b3 = biased.reshape(N_GROUPS, GROUP_SIZE, ts)
    e_in_g = lax.broadcasted_iota(jnp.int32, b3.shape, 1)
    m1 = jnp.max(b3, axis=1, keepdims=True)
    i1 = jnp.min(jnp.where(b3 == m1, e_in_g, GROUP_SIZE), axis=1, keepdims=True)
    m2 = jnp.max(jnp.where(e_in_g == i1, neg, b3), axis=1, keepdims=True)
    gs = (m1 + m2)[:, 0, :]
    g_iota = lax.broadcasted_iota(jnp.int32, gs.shape, 0)
    g_sel = jnp.zeros(gs.shape, jnp.bool_)
    for _ in range(TOPK_GROUPS):
        gm = jnp.max(gs, axis=0, keepdims=True)
        gi = jnp.min(jnp.where(gs == gm, g_iota, N_GROUPS), axis=0, keepdims=True)
        pick = g_iota == gi
        g_sel = jnp.logical_or(g_sel, pick)
        gs = jnp.where(pick, neg, gs)
    masked = jnp.where(g_sel[:, None, :], b3, neg)
    flat = lax.broadcasted_iota(jnp.int32, b3.shape, 0) * GROUP_SIZE + e_in_g
    sel = jnp.zeros(b3.shape, jnp.bool_)
    for _ in range(TOP_K):
        m = jnp.max(jnp.max(masked, axis=1, keepdims=True), axis=0, keepdims=True)
        cand = jnp.where(masked == m, flat, N_EXPERTS)
        idx = jnp.min(jnp.min(cand, axis=1, keepdims=True), axis=0, keepdims=True)
        pick = flat == idx
        sel = jnp.logical_or(sel, pick)
        masked = jnp.where(pick, neg, masked)
    s3 = scores.reshape(N_GROUPS, GROUP_SIZE, ts)
    w = jnp.where(sel, s3, 0.0)
    tot = jnp.sum(jnp.sum(w, axis=1, keepdims=True), axis=0, keepdims=True)
    return (w / tot * ROUTED_SCALE).reshape(N_EXPERTS, ts), sel.reshape(N_EXPERTS, ts)


def _pack_bf16_pair(lo, hi):
    lo_bits = pltpu.bitcast(lo.astype(BF16).astype(F32), jnp.uint32) >> 16
    hi_bits = pltpu.bitcast(hi.astype(BF16).astype(F32), jnp.uint32) & jnp.uint32(0xFFFF0000)
    return lo_bits | hi_bits


def _unpack_bf16_pair(p):
    lo = pltpu.bitcast(p << 16, F32).astype(BF16)
    hi = pltpu.bitcast(p & jnp.uint32(0xFFFF0000), F32).astype(BF16)
    return lo, hi


def _outproj_kernel(x_ref, a_ref, p_ref, mod_ref, wo32_ref, g2_ref, wrt_ref, rb_ref, ut_ref,
                    x1_ref, hp_ref, rank_ref, wd_ref, cnt_ref, run_ref, wo_ref):
    @pl.when(_first_grid_step())
    def _():
        run_ref[...] = jnp.zeros(run_ref.shape, jnp.int32)
        wo_ref[...] = wo32_ref[...].astype(BF16)

    gate1 = mod_ref[0, 2:3, :]
    shift2 = mod_ref[0, 3:4, :]
    scale2 = mod_ref[0, 4:5, :]
    mixp = _dot(a_ref[0], wo_ref[0:D_ATTN, :]) + _dot(p_ref[0], wo_ref[D_ATTN:, :])
    x1 = x_ref[0] + gate1 * mixp
    x1_ref[0] = x1
    h2 = _rms_mod(x1, g2_ref[...], scale2, shift2)
    half = h2.shape[-1] // 2
    hp_ref[0] = _pack_bf16_pair(h2[:, :half], h2[:, half:])
    logits_t = _dot3_nt(wrt_ref[...], h2)
    scores = 1.0 / (1.0 + jnp.exp(-logits_t))
    wd_t, sel = _route_t(scores, scores + rb_ref[...])
    wd_ref[...] = wd_t
    self_f = jnp.where(sel, 1.0, 0.0)
    before = _dot(self_f.astype(BF16), ut_ref[...]).astype(jnp.int32)
    run = run_ref[:, 0:1]
    rank_ref[...] = jnp.where(sel, run + before, -1)
    run_ref[...] = run_ref[...] + jnp.sum(self_f, axis=1, keepdims=True).astype(jnp.int32)
    cnt_ref[...] = run_ref[...]


def _outproj(x, attn, pool, mod3, w_out, g2, wr_t, rbias, ut, b0, B):
    _, S, D = x.shape
    ts = TS_PROJ
    n_s = S // ts
    const2 = lambda b, s: (0, 0)
    tok_in = lambda w: pl.BlockSpec((1, ts, w), lambda b, s: (b + b0, s, 0))
    tok = lambda w: pl.BlockSpec((1, ts, w), lambda b, s: (b, s, 0))
    tok_t = pl.BlockSpec((N_EXPERTS, ts), lambda b, s: (0, b * n_s + s))
    return pl.pallas_call(
        _outproj_kernel,
        grid=(B, n_s),
        in_specs=[tok_in(D), tok_in(D_ATTN), tok_in(D_POOL),
                  pl.BlockSpec((1, N_MOD, D), lambda b, s: (b + b0, 0, 0)),
                  pl.BlockSpec(w_out.shape, const2),
                  pl.BlockSpec((1, D), const2),
                  pl.BlockSpec(wr_t.shape, const2),
                  pl.BlockSpec((N_EXPERTS, 1), const2),
                  pl.BlockSpec((ts, ts), const2)],
        out_specs=[tok(D), tok(D // 2), tok_t, tok_t, pl.BlockSpec((N_EXPERTS, LANES), const2)],
        out_shape=[jax.ShapeDtypeStruct((B, S, D), F32),
                   jax.ShapeDtypeStruct((B, S, D // 2), jnp.uint32),
                   jax.ShapeDtypeStruct((N_EXPERTS, B * S), jnp.int32),
                   jax.ShapeDtypeStruct((N_EXPERTS, B * S), F32),
                   jax.ShapeDtypeStruct((N_EXPERTS, LANES), jnp.int32)],
        scratch_shapes=[pltpu.VMEM((N_EXPERTS, LANES), jnp.int32), pltpu.VMEM(w_out.shape, BF16)],
        compiler_params=pltpu.CompilerParams(dimension_semantics=("arbitrary", "arbitrary"),
                                             vmem_limit_bytes=VMEM_LIMIT),
        name="outproj_router",
    )(x, attn, pool, mod3, w_out, g2, wr_t, rbias, ut)


def _n_row_blocks(n_tokens):
    n_blocks = -(-(n_tokens * TOP_K + N_EXPERTS * (ROW_BLOCK - 1)) // ROW_BLOCK)
    return -(-n_blocks // BLOCKS_PER_STEP) * BLOCKS_PER_STEP


def _slots_kernel(rank_ref, wd_ref, cnt_ref, lt_ref, slot_ref, wk_ref, be_ref, nu_ref):
    lt = lt_ref[...]
    nblk = lax.shift_right_logical(cnt_ref[...] + (ROW_BLOCK - 1), ROW_BLOCK_LOG2).astype(F32)
    nb_hi, nb_lo = _split_bf16(nblk)
    blk_start = _dot(lt, nb_hi) + _dot(lt, nb_lo)

    @pl.when(pl.program_id(0) == 0)
    def _():
        blk_end = (blk_start + nblk)[:, 0:1]
        b_iota = lax.broadcasted_iota(jnp.int32, (N_EXPERTS, be_ref.shape[-1]), 1).astype(F32)
        owner = jnp.sum(jnp.where(blk_end <= b_iota, 1, 0), axis=0, keepdims=True)
        be_ref[...] = jnp.minimum(owner, N_EXPERTS - 1)
        nu_ref[...] = jnp.broadcast_to(blk_end[N_EXPERTS - 1:, :].astype(jnp.int32), nu_ref.shape)

    rank = rank_ref[...]
    sel = rank >= 0
    row_start = (blk_start[:, 0:1] * ROW_BLOCK).astype(jnp.int32)
    slot_d = row_start + rank
    wd = wd_ref[...]
    choice = _dot(lt, jnp.where(sel, 1.0, 0.0).astype(BF16)).astype(jnp.int32)
    ts = rank.shape[-1]
    slots, wks = [], []
    for k in range(TOP_K):
        m = jnp.logical_and(sel, choice == k)
        slots.append(jnp.sum(jnp.where(m, slot_d, 0), axis=0, keepdims=True))
        wks.append(jnp.sum(jnp.where(m, wd, 0.0), axis=0, keepdims=True))
    slot_ref[...] = jnp.concatenate(slots + [jnp.zeros((SUBLANES - TOP_K, ts), jnp.int32)], axis=0)
    wk_pad = jnp.concatenate(wks + [jnp.zeros((LANES - TOP_K, ts), F32)], axis=0)
    wk_ref[...] = wk_pad.T


def _slots(rank_t, wd_t, cnt, lt):
    T = rank_t.shape[1]
    ts = TS_SLOT
    nb_pad = -(-_n_row_blocks(T) // LANES) * LANES
    const = lambda i: (0, 0)
    tok_t = pl.BlockSpec((N_EXPERTS, ts), lambda i: (0, i))
    return pl.pallas_call(
        _slots_kernel,
        grid=(T // ts,),
        in_specs=[tok_t, tok_t, pl.BlockSpec((N_EXPERTS, LANES), const), pl.BlockSpec((N_EXPERTS, N_EXPERTS), const)],
        out_specs=[pl.BlockSpec((SUBLANES, ts), lambda i: (0, i)),
                   pl.BlockSpec((ts, LANES), lambda i: (i, 0)),
                   pl.BlockSpec((1, nb_pad), const),
                   pl.BlockSpec((1, LANES), const)],
        out_shape=[jax.ShapeDtypeStruct((SUBLANES, T), jnp.int32),
                   jax.ShapeDtypeStruct((T, LANES), F32),
                   jax.ShapeDtypeStruct((1, nb_pad), jnp.int32),
                   jax.ShapeDtypeStruct((1, LANES), jnp.int32)],
        compiler_params=pltpu.CompilerParams(dimension_semantics=("arbitrary",), vmem_limit_bytes=VMEM_LIMIT),
        name="slots",
    )(rank_t, wd_t, cnt, lt)


def _sc_mesh():
    return plsc.VectorSubcoreMesh(core_axis_name="c", subcore_axis_name="s",
                                  num_cores=SC_CORES, num_subcores=SC_SUBCORES)


def _sc_dispatch(rows, slot_flat, n_out):
    T, width = rows.shape
    n_workers = SC_CORES * SC_SUBCORES
    per_worker = T // n_workers
    steps = per_worker // SC_CHUNK

    @functools.partial(
        pl.kernel, mesh=_sc_mesh(),
        out_type=jax.ShapeDtypeStruct((n_out, width), rows.dtype),
        scratch_types=[pltpu.VMEM((SC_CHUNK, width), rows.dtype)]
        + [pltpu.VMEM((SC_CHUNK,), jnp.int32)] * TOP_K + [pltpu.SemaphoreType.DMA],
        name="dispatch",
    )
    def run(rows_hbm, slot_hbm, out_hbm, rows_v, *rest):
        idx_v, sem = rest[:TOP_K], rest[TOP_K]
        base = (lax.axis_index("s") * SC_CORES + lax.axis_index("c")) * per_worker

        @pl.loop(0, steps)
        def _(i):
            off = base + i * SC_CHUNK
            pltpu.sync_copy(rows_hbm.at[pl.ds(off, SC_CHUNK)], rows_v)
            for k in range(TOP_K):
                pltpu.sync_copy(slot_hbm.at[pl.ds(k * T + off, SC_CHUNK)], idx_v[k])
            copies = [pltpu.async_copy(rows_v, out_hbm.at[idx_v[k]], sem) for k in range(TOP_K)]
            for cp in copies:
                cp.wait()

    return run(rows, slot_flat)


def _sc_gather(rows, idx):
    n = idx.shape[0]
    width = rows.shape[1]
    n_workers = SC_CORES * SC_SUBCORES
    per_worker = n // n_workers
    steps = per_worker // SC_CHUNK

    assert steps % 2 == 0
    slot_types = [pltpu.VMEM((SC_CHUNK,), jnp.int32), pltpu.VMEM((SC_CHUNK, width), rows.dtype),
                  pltpu.SemaphoreType.DMA]

    @functools.partial(
        pl.kernel, mesh=_sc_mesh(),
        out_type=jax.ShapeDtypeStruct((n, width), rows.dtype),
        scratch_types=slot_types * 2,
        name="combine",
    )
    def run(rows_hbm, idx_hbm, out_hbm, *scratch):
        base = (lax.axis_index("s") * SC_CORES + lax.axis_index("c")) * per_worker
        slots = (scratch[0:3], scratch[3:6])

        def gather(slot):
            idx_v, rows_v, sem = slots[slot]
            return pltpu.make_async_copy(rows_hbm.at[idx_v], rows_v, sem)

        def start(chunk, slot):
            pltpu.sync_copy(idx_hbm.at[pl.ds(base + chunk * SC_CHUNK, SC_CHUNK)], slots[slot][0])
            gather(slot).start()

        def finish(chunk, slot):
            gather(slot).wait()
            pltpu.sync_copy(slots[slot][1], out_hbm.at[pl.ds(base + chunk * SC_CHUNK, SC_CHUNK)])

        start(0, 0)

        @pl.loop(0, steps, step=2)
        def _(chunk):
            start(chunk + 1, 1)
            finish(chunk, 0)

            @pl.when(chunk + 2 < steps)
            def _():
                start(chunk + 2, 0)

            finish(chunk + 1, 1)

    return run(rows, idx)


def _swiglu_packed(xp, wgu, wd):
    lo, hi = _unpack_bf16_pair(xp)
    half = xp.shape[-1]
    gu = _dot(lo, wgu[:half, :]) + _dot(hi, wgu[half:, :])
    dh = wgu.shape[-1] // 2
    act = _silu(gu[:, :dh]) * gu[:, dh:]
    return _dot(act.astype(BF16), wd)


def _cast_swiglu_weights(wg32_ref, wu32_ref, wd32_ref, wgu_ref, wd_ref):
    dh = wg32_ref.shape[-1]
    wgu_ref[:, :dh] = wg32_ref[...].reshape(wg32_ref.shape[-2:]).astype(BF16)
    wgu_ref[:, dh:] = wu32_ref[...].reshape(wu32_ref.shape[-2:]).astype(BF16)
    wd_ref[...] = wd32_ref[...].reshape(wd32_ref.shape[-2:]).astype(BF16)


def _experts_kernel(be_ref, nu_ref, x_ref, *refs):
    n = BLOCKS_PER_STEP
    w32 = [refs[3 * j:3 * j + 3] for j in range(n)]
    y_ref = refs[3 * n]
    wbf = [refs[3 * n + 1 + 2 * j:3 * n + 3 + 2 * j] for j in range(n)]
    step = pl.program_id(0)
    for j in range(n):
        blk = step * n + j
        new_expert = jnp.logical_or(step == 0, be_ref[blk] != be_ref[jnp.maximum(blk - n, 0)])

        @pl.when(new_expert)
        def _():
            _cast_swiglu_weights(*w32[j], *wbf[j])

    for j in range(n):
        rows = pl.ds(j * ROW_BLOCK, ROW_BLOCK)
        y = _swiglu_packed(x_ref[rows, :], wbf[j][0][...], wbf[j][1][...])
        half = y.shape[-1] // 2
        y_ref[rows, :] = _pack_bf16_pair(y[:, :half], y[:, half:])


def _experts(xs, block_expert, n_used, w_gate, w_up, w_down):
    P, half = xs.shape
    _, D, dh = w_gate.shape
    n = BLOCKS_PER_STEP
    rows = n * ROW_BLOCK
    assert P % rows == 0

    def by_expert(shape, j):
        return pl.BlockSpec((1,) + shape, lambda s, be, nu: (be[s * n + j], 0, 0))

    w_specs, w_args = [], []
    for j in range(n):
        w_specs += [by_expert((D, dh), j), by_expert((D, dh), j), by_expert((dh, D), j)]
        w_args += [w_gate, w_up, w_down]
    def row_step(s, be, nu):
        return jnp.minimum(s, lax.div(jnp.maximum(nu[0], 1) - 1, n)), 0

    grid_spec = pltpu.PrefetchScalarGridSpec(
        num_scalar_prefetch=2,
        grid=(P // rows,),
        in_specs=[pl.BlockSpec((rows, half), row_step)] + w_specs,
        out_specs=pl.BlockSpec((rows, half), row_step),
        scratch_shapes=[pltpu.VMEM((D, 2 * dh), BF16), pltpu.VMEM((dh, D), BF16)] * n,
    )
    return pl.pallas_call(
        _experts_kernel,
        grid_spec=grid_spec,
        out_shape=jax.ShapeDtypeStruct((P, half), jnp.uint32),
        compiler_params=pltpu.CompilerParams(dimension_semantics=("arbitrary",), vmem_limit_bytes=VMEM_LIMIT),
        name="experts",
    )(block_expert, n_used, xs, *w_args)


def _final_kernel(x1_ref, hp_ref, g_ref, wk_ref, mod_ref, sg32_ref, su32_ref, sd32_ref, *rest):
    o_ref, sgu_ref, sd_ref = rest[-3:]

    @pl.when(_first_grid_step())
    def _():
        _cast_swiglu_weights(sg32_ref, su32_ref, sd32_ref, sgu_ref, sd_ref)

    acc = _swiglu_packed(hp_ref[0], sgu_ref[...], sd_ref[...])
    wk = wk_ref[0]
    for k in range(TOP_K):
        lo, hi = _unpack_bf16_pair(g_ref[k, 0])
        y = jnp.concatenate([lo.astype(F32), hi.astype(F32)], axis=-1)
        acc = acc + wk[:, k:k + 1] * y
    gate2 = mod_ref[0, 5:6, :]
    o_ref[0] = x1_ref[0] + gate2 * acc


def _final(x1, hp, g, wk_tok, mod3, ws_gate, ws_up, ws_down, prev_out, b0, b_total):
    B, S, D = x1.shape
    dh = ws_gate.shape[-1]
    ts = TS_PROJ
    tok = lambda w: pl.BlockSpec((1, ts, w), lambda b, s: (b, s, 0))
    const2 = lambda b, s: (0, 0)
    in_specs = [tok(D), tok(D // 2),
                pl.BlockSpec((TOP_K, 1, ts, D // 2), lambda b, s: (0, b, s, 0)),
                tok(LANES),
                pl.BlockSpec((1, N_MOD, D), lambda b, s: (b + b0, 0, 0)),
                pl.BlockSpec(ws_gate.shape, const2),
                pl.BlockSpec(ws_up.shape, const2),
                pl.BlockSpec(ws_down.shape, const2)]
    args = [x1, hp, g, wk_tok, mod3, ws_gate, ws_up, ws_down]
    aliases = {}
    if prev_out is not None:
        in_specs.append(pl.BlockSpec(memory_space=pl.ANY))
        args.append(prev_out)
        aliases = {len(args) - 1: 0}
    return pl.pallas_call(
        _final_kernel,
        grid=(B, S // ts),
        in_specs=in_specs,
        out_specs=pl.BlockSpec((1, ts, D), lambda b, s: (b + b0, s, 0)),
        out_shape=jax.ShapeDtypeStruct((b_total, S, D), F32),
        scratch_shapes=[pltpu.VMEM((D, 2 * dh), BF16), pltpu.VMEM((dh, D), BF16)],
        input_output_aliases=aliases,
        compiler_params=pltpu.CompilerParams(dimension_semantics=("arbitrary", "arbitrary"),
                                             vmem_limit_bytes=VMEM_LIMIT),
        name="final",
    )(*args)


def _layer(x, c_act_mod, norm1_g, norm2_g, w_in, q_norm_g, k_norm_g, w_pool, pool_scale, w_out,
           w_router, router_bias, w_gate, w_up, w_down, ws_gate, ws_up, ws_down):
    B, S, D = x.shape
    mod3 = c_act_mod.reshape(B, N_MOD, D)
    head_of = jnp.arange(D_ATTN, dtype=jnp.int32) // HEAD_DIM
    head_mean = jnp.where(head_of[:, None] == head_of[None, :], 1.0 / HEAD_DIM, 0.0).astype(BF16)
    j = jnp.arange(TK, dtype=jnp.int32)
    umat = jnp.where(j[:, None] >= j[None, :], -1.0, 0.0).astype(BF16)

    q, k, v, pool = _inproj(
        x, mod3, norm1_g.reshape(1, D), w_in,
        jnp.tile(q_norm_g, N_HEADS).reshape(1, D_ATTN), jnp.tile(k_norm_g, N_HEADS).reshape(1, D_ATTN),
        head_mean, w_pool, pool_scale.reshape(1, D_POOL))
    attn = _attention(q, k, v, umat)
    t = jnp.arange(TS_PROJ, dtype=jnp.int32)
    ut = (t[:, None] < t[None, :]).astype(BF16)
    e = jnp.arange(N_EXPERTS, dtype=jnp.int32)
    lt = (e[None, :] < e[:, None]).astype(BF16)
    bp = B // MOE_PARTS
    T = bp * S
    n_blocks = _n_row_blocks(T)
    out = None
    for part in range(MOE_PARTS):
        b0 = part * bp
        x1, hp, rank_t, wd_t, cnt = _outproj(x, attn, pool, mod3, w_out, norm2_g.reshape(1, D),
                                             w_router.T, router_bias.reshape(N_EXPERTS, 1), ut, b0, bp)
        slots, wk_tok, block_expert, n_used = _slots(rank_t, wd_t, cnt, lt)
        slot_flat = slots[:TOP_K].reshape(TOP_K * T)
        xs = _sc_dispatch(hp.reshape(T, D // 2), slot_flat, n_blocks * ROW_BLOCK)
        ys = _experts(xs, block_expert[0, :n_blocks], n_used[0, :1], w_gate, w_up, w_down)
        g = _sc_gather(ys, slot_flat).reshape(TOP_K, bp, S, D // 2)
        out = _final(x1, hp, g, wk_tok.reshape(bp, S, LANES), mod3, ws_gate, ws_up, ws_down, out, b0, B)
    return out


def kernel(x, c, w_ada, b_ada, norm1_g, norm2_g, w_in, q_norm_g, k_norm_g, w_pool, pool_scale, w_out,
           w_router, router_bias, w_gate, w_up, w_down, ws_gate, ws_up, ws_down):
    depth = w_ada.shape[0]
    for l in range(depth):
        mod = _adaln(c, w_ada[l], b_ada[l])
        x = _layer(x, mod, norm1_g[l], norm2_g[l], w_in[l], q_norm_g[l], k_norm_g[l], w_pool[l],
                   pool_scale[l], w_out[l], w_router[l], router_bias[l], w_gate[l], w_up[l], w_down[l],
                   ws_gate[l], ws_up[l], ws_down[l])
    return x
```

```python
import functools

import jax
import jax.numpy as jnp
from jax import lax
from jax.experimental import pallas as pl
from jax.experimental.pallas import tpu as pltpu
from jax.experimental.pallas import tpu_sc as plsc

F32 = jnp.float32
BF16 = jnp.bfloat16

HEAD_DIM = 64
N_HEADS = 8
D_ATTN = N_HEADS * HEAD_DIM
POOL_WINDOWS = (2, 4, 8, 16)
POOL_GROUP_DIM = 128
D_POOL = len(POOL_WINDOWS) * POOL_GROUP_DIM
MAX_WINDOW = max(POOL_WINDOWS)
N_EXPERTS = 64
TOP_K = 6
N_GROUPS = 8
GROUP_SIZE = N_EXPERTS // N_GROUPS
TOPK_GROUPS = 4
ROUTED_SCALE = 2.5
RMS_EPS = 1e-6
N_MOD = 6

LANES = 128
SUBLANES = 8
VMEM_LIMIT = 56 * 1024 * 1024

TS_PROJ = 1024
TQ = 1024
TK = 256
TS_SLOT = 2048
ROW_BLOCK_LOG2 = 9
ROW_BLOCK = 1 << ROW_BLOCK_LOG2
BLOCKS_PER_STEP = 2
MOE_PARTS = 2
SC_CORES = 2
SC_SUBCORES = 16
SC_CHUNK = 64


def _split_bf16(a):
    hi = a.astype(BF16)
    lo = (a - hi.astype(F32)).astype(BF16)
    return hi, lo


def _dot(a, b):
    return jnp.dot(a, b, preferred_element_type=F32)


def _dot_nt(a, b):
    return lax.dot_general(a, b, (((1,), (1,)), ((), ())), preferred_element_type=F32)


def _dot3(a, b):
    ah, al = _split_bf16(a)
    bh, bl = _split_bf16(b)
    return _dot(ah, bh) + _dot(ah, bl) + _dot(al, bh)


def _dot3_nt(a, b):
    ah, al = _split_bf16(a)
    bh, bl = _split_bf16(b)
    return _dot_nt(ah, bh) + _dot_nt(ah, bl) + _dot_nt(al, bh)


def _silu(x):
    return x * (1.0 / (1.0 + jnp.exp(-x)))


def _rms_mod(x, gain, scale, shift):
    ms = jnp.mean(x * x, axis=-1, keepdims=True)
    y = x * lax.rsqrt(ms + RMS_EPS) * gain
    return y * (1.0 + scale) + shift


def _adaln_kernel(c_ref, w_ref, b_ref, o_ref):
    c = c_ref[...]
    o_ref[...] = _dot3(_silu(c), w_ref[...]) + b_ref[...]


def _adaln(c, w_ada, b_ada):
    nb, D = c.shape
    B = -(-nb // SUBLANES) * SUBLANES
    c = jnp.pad(c, ((0, B - nb), (0, 0)))
    N = w_ada.shape[1]
    tn = 1024
    out = pl.pallas_call(
        _adaln_kernel,
        grid=(N // tn,),
        in_specs=[pl.BlockSpec((B, D), lambda j: (0, 0)),
                  pl.BlockSpec((D, tn), lambda j: (0, j)),
                  pl.BlockSpec((1, tn), lambda j: (0, j))],
        out_specs=pl.BlockSpec((B, tn), lambda j: (0, j)),
        out_shape=jax.ShapeDtypeStruct((B, N), F32),
        compiler_params=pltpu.CompilerParams(dimension_semantics=("arbitrary",),
                                             vmem_limit_bytes=VMEM_LIMIT),
        name="adaln",
    )(c, w_ada, b_ada.reshape(1, N))
    return out[:nb]


def _first_grid_step():
    return jnp.logical_and(pl.program_id(0) == 0, pl.program_id(1) == 0)


def _inproj_kernel(x_ref, mod_ref, g1_ref, win32_ref, qg_ref, kg_ref, hm_ref, wp32_ref, ps_ref,
                   q_ref, k_ref, v_ref, p_ref, ext_ref, win_ref, wp_ref):
    @pl.when(_first_grid_step())
    def _():
        win_ref[...] = win32_ref[...].astype(BF16)
        wp_ref[...] = wp32_ref[...].astype(BF16)

    si = pl.program_id(1)
    ts = x_ref.shape[1]
    x = x_ref[0]
    shift1 = mod_ref[0, 0:1, :]
    scale1 = mod_ref[0, 1:2, :]
    h = _rms_mod(x, g1_ref[...], scale1, shift1)
    proj = _dot(h.astype(BF16), win_ref[...])

    hm = hm_ref[...]

    def head_norm(t, gain):
        ms = _dot((t * t).astype(BF16), hm)
        return t * lax.rsqrt(ms + RMS_EPS) * gain

    hq = proj[:, 0:D_ATTN]
    hk = proj[:, D_ATTN:2 * D_ATTN]
    q_ref[0] = (head_norm(hq, qg_ref[...]) * (HEAD_DIM ** -0.5)).astype(BF16)
    k_ref[0] = head_norm(hk, kg_ref[...]).astype(BF16)
    v_ref[0] = proj[:, 2 * D_ATTN:3 * D_ATTN].astype(BF16)

    hp = proj[:, 3 * D_ATTN:]

    @pl.when(si == 0)
    def _():
        ext_ref[0:MAX_WINDOW, :] = jnp.zeros((MAX_WINDOW, D_POOL), F32)

    ext_ref[MAX_WINDOW:, :] = hp
    pos = si * ts + lax.broadcasted_iota(jnp.int32, (ts, 1), 0)
    for g, w in enumerate(POOL_WINDOWS):
        lo_l, hi_l = g * POOL_GROUP_DIM, (g + 1) * POOL_GROUP_DIM
        u = hp[:, lo_l:hi_l]
        acc = u
        for i in range(1, w):
            acc = acc + ext_ref[pl.ds(MAX_WINDOW - i, ts), lo_l:hi_l]
        count = jnp.minimum(pos + 1, w).astype(F32)
        d = acc / count - u
        mixed = _dot(d.astype(BF16), wp_ref[g])
        p_ref[0, :, lo_l:hi_l] = (mixed * ps_ref[:, lo_l:hi_l]).astype(BF16)
    ext_ref[0:MAX_WINDOW, :] = hp[ts - MAX_WINDOW:, :]


def _inproj(x, mod3, g1, w_in, qg_t, kg_t, head_mean, w_pool, pool_scale):
    B, S, D = x.shape
    ts = TS_PROJ
    out_sd = jax.ShapeDtypeStruct((B, S, D_ATTN), BF16)
    blk = pl.BlockSpec((1, ts, D_ATTN), lambda b, s: (b, s, 0))
    const2 = lambda b, s: (0, 0)
    return pl.pallas_call(
        _inproj_kernel,
        grid=(B, S // ts),
        in_specs=[pl.BlockSpec((1, ts, D), lambda b, s: (b, s, 0)),
                  pl.BlockSpec((1, N_MOD, D), lambda b, s: (b, 0, 0)),
                  pl.BlockSpec((1, D), const2),
                  pl.BlockSpec(w_in.shape, const2),
                  pl.BlockSpec((1, D_ATTN), const2),
                  pl.BlockSpec((1, D_ATTN), const2),
                  pl.BlockSpec((D_ATTN, D_ATTN), const2),
                  pl.BlockSpec(w_pool.shape, lambda b, s: (0, 0, 0)),
                  pl.BlockSpec((1, D_POOL), const2)],
        out_specs=[blk, blk, blk, blk],
        out_shape=[out_sd, out_sd, out_sd, out_sd],
        scratch_shapes=[pltpu.VMEM((MAX_WINDOW + ts, D_POOL), F32),
                        pltpu.VMEM(w_in.shape, BF16), pltpu.VMEM(w_pool.shape, BF16)],
        compiler_params=pltpu.CompilerParams(dimension_semantics=("arbitrary", "arbitrary"),
                                             vmem_limit_bytes=VMEM_LIMIT),
        name="inproj",
    )(x, mod3, g1, w_in, qg_t, kg_t, head_mean, w_pool, pool_scale)


def _attn_kernel(q_ref, k_ref, v_ref, u_ref, o_ref, acc_ref):
    qi = pl.program_id(2)
    q = q_ref[0]
    lane = lax.broadcasted_iota(jnp.int32, (TQ, LANES), 1)
    first = lane < HEAD_DIM
    zero = jnp.zeros_like(q)
    qh = (jnp.where(first, q, zero), jnp.where(first, zero, q))
    u = u_ref[...]
    row = lax.broadcasted_iota(jnp.int32, (TQ, TK), 0)
    col = lax.broadcasted_iota(jnp.int32, (TQ, TK), 1)
    acc_ref[...] = jnp.zeros(acc_ref.shape, F32)

    def block(kb, survs, diag):
        start = pl.multiple_of(kb * TK, TK)
        k = k_ref[0, pl.ds(start, TK), :]
        v = v_ref[0, pl.ds(start, TK), :]
        r0 = 0 if diag is None else diag * TK
        if diag is not None:
            valid = (col + r0 < row)[r0:]
        out = []
        for h in range(2):
            z = _dot_nt(qh[h][r0:], k)
            zb = z.astype(BF16)
            sp = jnp.maximum(zb, 0) + jnp.log(1 + jnp.exp(-jnp.abs(zb)))
            if diag is not None:
                sp = jnp.where(valid, sp, jnp.zeros_like(sp))
            r = _dot(sp, u)
            arg = z + r + survs[h][r0:]
            if diag is not None:
                arg = jnp.where(valid, arg, -jnp.inf)
            acc_ref[h, r0:, :] += _dot(jnp.exp(arg).astype(BF16), v)
            surv = survs[h][r0:] + r[:, 0:1]
            out.append(surv if r0 == 0 else jnp.concatenate([survs[h][:r0], surv], axis=0))
        return tuple(out)

    survs = (jnp.zeros((TQ, 1), F32), jnp.zeros((TQ, 1), F32))
    n_diag = TQ // TK
    for d in reversed(range(n_diag)):
        survs = block(qi * n_diag + d, survs, d)
    def full_blocks(i, c):
        for d in range(n_diag):
            c = block((qi - i) * n_diag - 1 - d, c, None)
        return c

    lax.fori_loop(0, qi, full_blocks, survs)
    o_ref[0] = jnp.where(first, acc_ref[0], acc_ref[1]).astype(BF16)


def _attention(q, k, v, umat):
    B, S, _ = q.shape
    n_pairs = D_ATTN // LANES
    kv_spec = pl.BlockSpec((1, S, LANES), lambda b, p, i: (b, 0, p))
    q_spec = pl.BlockSpec((1, TQ, LANES), lambda b, p, i: (b, i, p))
    return pl.pallas_call(
        _attn_kernel,
        grid=(B, n_pairs, S // TQ),
        in_specs=[q_spec, kv_spec, kv_spec, pl.BlockSpec((TK, TK), lambda b, p, i: (0, 0))],
        out_specs=q_spec,
        out_shape=jax.ShapeDtypeStruct((B, S, D_ATTN), BF16),
        scratch_shapes=[pltpu.VMEM((2, TQ, LANES), F32)],
        compiler_params=pltpu.CompilerParams(
            dimension_semantics=("arbitrary", "arbitrary", "arbitrary"),
            vmem_limit_bytes=VMEM_LIMIT),
        name="stickbreak_attn",
    )(q, k, v, umat)


def _route_t(scores, biased):
    ts = scores.shape[-1]
    neg = -jnp.inf
    b3 = biased.reshape(N_GROUPS, GROUP_SIZE, ts)
    e_in_g = lax.broadcasted_iota(jnp.int32, b3.shape, 1)
    m1 = jnp.max(b3, axis=1, keepdims=True)
    i1 = jnp.min(jnp.where(b3 == m1, e_in_g, GROUP_SIZE), axis=1, keepdims=True)
    m2 = jnp.max(jnp.where(e_in_g == i1, neg, b3), axis=1, keepdims=True)
    gs = (m1 + m2)[:, 0, :]
    g_iota = lax.broadcasted_iota(jnp.int32, gs.shape, 0)
    g_sel = jnp.zeros(gs.shape, jnp.bool_)
    for _ in range(TOPK_GROUPS):
        gm = jnp.max(gs, axis=0, keepdims=True)
        gi = jnp.min(jnp.where(gs == gm, g_iota, N_GROUPS), axis=0, keepdims=True)
        pick = g_iota == gi
        g_sel = jnp.logical_or(g_sel, pick)
        gs = jnp.where(pick, neg, gs)
    masked = jnp.where(g_sel[:, None, :], b3, neg)
    flat = lax.broadcasted_iota(jnp.int32, b3.shape, 0) * GROUP_SIZE + e_in_g
    sel = jnp.zeros(b3.shape, jnp.bool_)
    for _ in range(TOP_K):
        m = jnp.max(jnp.max(masked, axis=1, keepdims=True), axis=0, keepdims=True)
        cand = jnp.where(masked == m, flat, N_EXPERTS)
        idx = jnp.min(jnp.min(cand, axis=1, keepdims=True), axis=0, keepdims=True)
        pick = flat == idx
        sel = jnp.logical_or(sel, pick)
        masked = jnp.where(pick, neg, masked)
    s3 = scores.reshape(N_GROUPS, GROUP_SIZE, ts)
    w = jnp.where(sel, s3, 0.0)
    tot = jnp.sum(jnp.sum(w, axis=1, keepdims=True), axis=0, keepdims=True)
    return (w / tot * ROUTED_SCALE).reshape(N_EXPERTS, ts), sel.reshape(N_EXPERTS, ts)


def _pack_bf16_pair(lo, hi):
    lo_bits = pltpu.bitcast(lo.astype(BF16).astype(F32), jnp.uint32) >> 16
    hi_bits = pltpu.bitcast(hi.astype(BF16).astype(F32), jnp.uint32) & jnp.uint32(0xFFFF0000)
    return lo_bits | hi_bits


def _unpack_bf16_pair(p):
    lo = pltpu.bitcast(p << 16, F32).astype(BF16)
    hi = pltpu.bitcast(p & jnp.uint32(0xFFFF0000), F32).astype(BF16)
    return lo, hi


def _outproj_kernel(x_ref, a_ref, p_ref, mod_ref, wo32_ref, g2_ref, wrt_ref, rb_ref, ut_ref,
                    x1_ref, hp_ref, rank_ref, wd_ref, cnt_ref, run_ref, wo_ref):
    @pl.when(_first_grid_step())
    def _():
        run_ref[...] = jnp.zeros(run_ref.shape, jnp.int32)
        wo_ref[...] = wo32_ref[...].astype(BF16)

    gate1 = mod_ref[0, 2:3, :]
    shift2 = mod_ref[0, 3:4, :]
    scale2 = mod_ref[0, 4:5, :]
    mixp = _dot(a_ref[0], wo_ref[0:D_ATTN, :]) + _dot(p_ref[0], wo_ref[D_ATTN:, :])
    x1 = x_ref[0] + gate1 * mixp
    x1_ref[0] = x1
    h2 = _rms_mod(x1, g2_ref[...], scale2, shift2)
    half = h2.shape[-1] // 2
    hp_ref[0] = _pack_bf16_pair(h2[:, :half], h2[:, half:])
    logits_t = _dot3_nt(wrt_ref[...], h2)
    scores = 1.0 / (1.0 + jnp.exp(-logits_t))
    wd_t, sel = _route_t(scores, scores + rb_ref[...])
    wd_ref[...] = wd_t
    self_f = jnp.where(sel, 1.0, 0.0)
    before = _dot(self_f.astype(BF16), ut_ref[...]).astype(jnp.int32)
    run = run_ref[:, 0:1]
    rank_ref[...] = jnp.where(sel, run + before, -1)
    run_ref[...] = run_ref[...] + jnp.sum(self_f, axis=1, keepdims=True).astype(jnp.int32)
    cnt_ref[...] = run_ref[...]


def _outproj(x, attn, pool, mod3, w_out, g2, wr_t, rbias, ut, b0, B):
    _, S, D = x.shape
    ts = TS_PROJ
    n_s = S // ts
    const2 = lambda b, s: (0, 0)
    tok_in = lambda w: pl.BlockSpec((1, ts, w), lambda b, s: (b + b0, s, 0))
    tok = lambda w: pl.BlockSpec((1, ts, w), lambda b, s: (b, s, 0))
    tok_t = pl.BlockSpec((N_EXPERTS, ts), lambda b, s: (0, b * n_s + s))
    return pl.pallas_call(
        _outproj_kernel,
        grid=(B, n_s),
        in_specs=[tok_in(D), tok_in(D_ATTN), tok_in(D_POOL),
                  pl.BlockSpec((1, N_MOD, D), lambda b, s: (b + b0, 0, 0)),
                  pl.BlockSpec(w_out.shape, const2),
                  pl.BlockSpec((1, D), const2),
                  pl.BlockSpec(wr_t.shape, const2),
                  pl.BlockSpec((N_EXPERTS, 1), const2),
                  pl.BlockSpec((ts, ts), const2)],
        out_specs=[tok(D), tok(D // 2), tok_t, tok_t, pl.BlockSpec((N_EXPERTS, LANES), const2)],
        out_shape=[jax.ShapeDtypeStruct((B, S, D), F32),
                   jax.ShapeDtypeStruct((B, S, D // 2), jnp.uint32),
                   jax.ShapeDtypeStruct((N_EXPERTS, B * S), jnp.int32),
                   jax.ShapeDtypeStruct((N_EXPERTS, B * S), F32),
                   jax.ShapeDtypeStruct((N_EXPERTS, LANES), jnp.int32)],
        scratch_shapes=[pltpu.VMEM((N_EXPERTS, LANES), jnp.int32), pltpu.VMEM(w_out.shape, BF16)],
        compiler_params=pltpu.CompilerParams(dimension_semantics=("arbitrary", "arbitrary"),
                                             vmem_limit_bytes=VMEM_LIMIT),
        name="outproj_router",
    )(x, attn, pool, mod3, w_out, g2, wr_t, rbias, ut)


def _n_row_blocks(n_tokens):
    n_blocks = -(-(n_tokens * TOP_K + N_EXPERTS * (ROW_BLOCK - 1)) // ROW_BLOCK)
    return -(-n_blocks // BLOCKS_PER_STEP) * BLOCKS_PER_STEP


def _slots_kernel(rank_ref, wd_ref, cnt_ref, lt_ref, slot_ref, wk_ref, be_ref, nu_ref):
    lt = lt_ref[...]
    nblk = lax.shift_right_logical(cnt_ref[...] + (ROW_BLOCK - 1), ROW_BLOCK_LOG2).astype(F32)
    nb_hi, nb_lo = _split_bf16(nblk)
    blk_start = _dot(lt, nb_hi) + _dot(lt, nb_lo)

    @pl.when(pl.program_id(0) == 0)
    def _():
        blk_end = (blk_start + nblk)[:, 0:1]
        b_iota = lax.broadcasted_iota(jnp.int32, (N_EXPERTS, be_ref.shape[-1]), 1).astype(F32)
        owner = jnp.sum(jnp.where(blk_end <= b_iota, 1, 0), axis=0, keepdims=True)
        be_ref[...] = jnp.minimum(owner, N_EXPERTS - 1)
        nu_ref[...] = jnp.broadcast_to(blk_end[N_EXPERTS - 1:, :].astype(jnp.int32), nu_ref.shape)

    rank = rank_ref[...]
    sel = rank >= 0
    row_start = (blk_start[:, 0:1] * ROW_BLOCK).astype(jnp.int32)
    slot_d = row_start + rank
    wd = wd_ref[...]
    choice = _dot(lt, jnp.where(sel, 1.0, 0.0).astype(BF16)).astype(jnp.int32)
    ts = rank.shape[-1]
    slots, wks = [], []
    for k in range(TOP_K):
        m = jnp.logical_and(sel, choice == k)
        slots.append(jnp.sum(jnp.where(m, slot_d, 0), axis=0, keepdims=True))
        wks.append(jnp.sum(jnp.where(m, wd, 0.0), axis=0, keepdims=True))
    slot_ref[...] = jnp.concatenate(slots + [jnp.zeros((SUBLANES - TOP_K, ts), jnp.int32)], axis=0)
    wk_pad = jnp.concatenate(wks + [jnp.zeros((LANES - TOP_K, ts), F32)], axis=0)
    wk_ref[...] = wk_pad.T


def _slots(rank_t, wd_t, cnt, lt):
    T = rank_t.shape[1]
    ts = TS_SLOT
    nb_pad = -(-_n_row_blocks(T) // LANES) * LANES
    const = lambda i: (0, 0)
    tok_t = pl.BlockSpec((N_EXPERTS, ts), lambda i: (0, i))
    return pl.pallas_call(
        _slots_kernel,
        grid=(T // ts,),
        in_specs=[tok_t, tok_t, pl.BlockSpec((N_EXPERTS, LANES), const), pl.BlockSpec((N_EXPERTS, N_EXPERTS), const)],
        out_specs=[pl.BlockSpec((SUBLANES, ts), lambda i: (0, i)),
                   pl.BlockSpec((ts, LANES), lambda i: (i, 0)),
                   pl.BlockSpec((1, nb_pad), const),
                   pl.BlockSpec((1, LANES), const)],
        out_shape=[jax.ShapeDtypeStruct((SUBLANES, T), jnp.int32),
                   jax.ShapeDtypeStruct((T, LANES), F32),
                   jax.ShapeDtypeStruct((1, nb_pad), jnp.int32),
                   jax.ShapeDtypeStruct((1, LANES), jnp.int32)],
        compiler_params=pltpu.CompilerParams(dimension_semantics=("arbitrary",), vmem_limit_bytes=VMEM_LIMIT),
        name="slots",
    )(rank_t, wd_t, cnt, lt)


def _sc_mesh():
    return plsc.VectorSubcoreMesh(core_axis_name="c", subcore_axis_name="s",
                                  num_cores=SC_CORES, num_subcores=SC_SUBCORES)


def _sc_dispatch(rows, slot_flat, n_out):
    T, width = rows.shape
    n_workers = SC_CORES * SC_SUBCORES
    per_worker = T // n_workers
    steps = per_worker // SC_CHUNK

    @functools.partial(
        pl.kernel, mesh=_sc_mesh(),
        out_type=jax.ShapeDtypeStruct((n_out, width), rows.dtype),
        scratch_types=[pltpu.VMEM((SC_CHUNK, width), rows.dtype)]
        + [pltpu.VMEM((SC_CHUNK,), jnp.int32)] * TOP_K + [pltpu.SemaphoreType.DMA],
        name="dispatch",
    )
    def run(rows_hbm, slot_hbm, out_hbm, rows_v, *rest):
        idx_v, sem = rest[:TOP_K], rest[TOP_K]
        base = (lax.axis_index("s") * SC_CORES + lax.axis_index("c")) * per_worker

        @pl.loop(0, steps)
        def _(i):
            off = base + i * SC_CHUNK
            pltpu.sync_copy(rows_hbm.at[pl.ds(off, SC_CHUNK)], rows_v)
            for k in range(TOP_K):
                pltpu.sync_copy(slot_hbm.at[pl.ds(k * T + off, SC_CHUNK)], idx_v[k])
            copies = [pltpu.async_copy(rows_v, out_hbm.at[idx_v[k]], sem) for k in range(TOP_K)]
            for cp in copies:
                cp.wait()

    return run(rows, slot_flat)


def _sc_gather(rows, idx):
    n = idx.shape[0]
    width = rows.shape[1]
    n_workers = SC_CORES * SC_SUBCORES
    per_worker = n // n_workers
    steps = per_worker // SC_CHUNK

    assert steps % 2 == 0
    slot_types = [pltpu.VMEM((SC_CHUNK,), jnp.int32), pltpu.VMEM((SC_CHUNK, width), rows.dtype),
                  pltpu.SemaphoreType.DMA]

    @functools.partial(
        pl.kernel, mesh=_sc_mesh(),
        out_type=jax.ShapeDtypeStruct((n, width), rows.dtype),
        scratch_types=slot_types * 2,
        name="combine",
    )
    def run(rows_hbm, idx_hbm, out_hbm, *scratch):
        base = (lax.axis_index("s") * SC_CORES + lax.axis_index("c")) * per_worker
        slots = (scratch[0:3], scratch[3:6])

        def gather(slot):
            idx_v, rows_v, sem = slots[slot]
            return pltpu.make_async_copy(rows_hbm.at[idx_v], rows_v, sem)

        def start(chunk, slot):
            pltpu.sync_copy(idx_hbm.at[pl.ds(base + chunk * SC_CHUNK, SC_CHUNK)], slots[slot][0])
            gather(slot).start()

        def finish(chunk, slot):
            gather(slot).wait()
            pltpu.sync_copy(slots[slot][1], out_hbm.at[pl.ds(base + chunk * SC_CHUNK, SC_CHUNK)])

        start(0, 0)

        @pl.loop(0, steps, step=2)
        def _(chunk):
            start(chunk + 1, 1)
            finish(chunk, 0)

            @pl.when(chunk + 2 < steps)
            def _():
                start(chunk + 2, 0)

            finish(chunk + 1, 1)

    return run(rows, idx)


def _swiglu_packed(xp, wgu, wd):
    lo, hi = _unpack_bf16_pair(xp)
    half = xp.shape[-1]
    gu = _dot(lo, wgu[:half, :]) + _dot(hi, wgu[half:, :])
    dh = wgu.shape[-1] // 2
    act = _silu(gu[:, :dh]) * gu[:, dh:]
    return _dot(act.astype(BF16), wd)


def _cast_swiglu_weights(wg32_ref, wu32_ref, wd32_ref, wgu_ref, wd_ref):
    dh = wg32_ref.shape[-1]
    wgu_ref[:, :dh] = wg32_ref[...].reshape(wg32_ref.shape[-2:]).astype(BF16)
    wgu_ref[:, dh:] = wu32_ref[...].reshape(wu32_ref.shape[-2:]).astype(BF16)
    wd_ref[...] = wd32_ref[...].reshape(wd32_ref.shape[-2:]).astype(BF16)


def _experts_kernel(be_ref, nu_ref, x_ref, *refs):
    n = BLOCKS_PER_STEP
    w32 = [refs[3 * j:3 * j + 3] for j in range(n)]
    y_ref = refs[3 * n]
    wbf = [refs[3 * n + 1 + 2 * j:3 * n + 3 + 2 * j] for j in range(n)]
    step = pl.program_id(0)
    for j in range(n):
        blk = step * n + j
        new_expert = jnp.logical_or(step == 0, be_ref[blk] != be_ref[jnp.maximum(blk - n, 0)])

        @pl.when(new_expert)
        def _():
            _cast_swiglu_weights(*w32[j], *wbf[j])

    @pl.when(step * n < nu_ref[0])
    def _():
        for j in range(n):
            rows = pl.ds(j * ROW_BLOCK, ROW_BLOCK)
            y = _swiglu_packed(x_ref[rows, :], wbf[j][0][...], wbf[j][1][...])
            half = y.shape[-1] // 2
            y_ref[rows, :] = _pack_bf16_pair(y[:, :half], y[:, half:])


def _experts(xs, block_expert, n_used, w_gate, w_up, w_down):
    P, half = xs.shape
    _, D, dh = w_gate.shape
    n = BLOCKS_PER_STEP
    rows = n * ROW_BLOCK
    assert P % rows == 0

    def by_expert(shape, j):
        return pl.BlockSpec((1,) + shape, lambda s, be, nu: (be[s * n + j], 0, 0))

    w_specs, w_args = [], []
    for j in range(n):
        w_specs += [by_expert((D, dh), j), by_expert((D, dh), j), by_expert((dh, D), j)]
        w_args += [w_gate, w_up, w_down]
    def row_step(s, be, nu):
        return jnp.minimum(s, lax.div(jnp.maximum(nu[0], 1) - 1, n)), 0

    grid_spec = pltpu.PrefetchScalarGridSpec(
        num_scalar_prefetch=2,
        grid=(P // rows,),
        in_specs=[pl.BlockSpec((rows, half), row_step)] + w_specs,
        out_specs=pl.BlockSpec((rows, half), row_step),
        scratch_shapes=[pltpu.VMEM((D, 2 * dh), BF16), pltpu.VMEM((dh, D), BF16)] * n,
    )
    return pl.pallas_call(
        _experts_kernel,
        grid_spec=grid_spec,
        out_shape=jax.ShapeDtypeStruct((P, half), jnp.uint32),
        compiler_params=pltpu.CompilerParams(dimension_semantics=("arbitrary",), vmem_limit_bytes=VMEM_LIMIT),
        name="experts",
    )(block_expert, n_used, xs, *w_args)


def _final_kernel(x1_ref, hp_ref, g_ref, wk_ref, mod_ref, sg32_ref, su32_ref, sd32_ref, *rest):
    o_ref, sgu_ref, sd_ref = rest[-3:]

    @pl.when(_first_grid_step())
    def _():
        _cast_swiglu_weights(sg32_ref, su32_ref, sd32_ref, sgu_ref, sd_ref)

    acc = _swiglu_packed(hp_ref[0], sgu_ref[...], sd_ref[...])
    wk = wk_ref[0]
    for k in range(TOP_K):
        lo, hi = _unpack_bf16_pair(g_ref[k, 0])
        y = jnp.concatenate([lo.astype(F32), hi.astype(F32)], axis=-1)
        acc = acc + wk[:, k:k + 1] * y
    gate2 = mod_ref[0, 5:6, :]
    o_ref[0] = x1_ref[0] + gate2 * acc


def _final(x1, hp, g, wk_tok, mod3, ws_gate, ws_up, ws_down, prev_out, b0, b_total):
    B, S, D = x1.shape
    dh = ws_gate.shape[-1]
    ts = TS_PROJ
    tok = lambda w: pl.BlockSpec((1, ts, w), lambda b, s: (b, s, 0))
    const2 = lambda b, s: (0, 0)
    in_specs = [tok(D), tok(D // 2),
                pl.BlockSpec((TOP_K, 1, ts, D // 2), lambda b, s: (0, b, s, 0)),
                tok(LANES),
                pl.BlockSpec((1, N_MOD, D), lambda b, s: (b + b0, 0, 0)),
                pl.BlockSpec(ws_gate.shape, const2),
                pl.BlockSpec(ws_up.shape, const2),
                pl.BlockSpec(ws_down.shape, const2)]
    args = [x1, hp, g, wk_tok, mod3, ws_gate, ws_up, ws_down]
    aliases = {}
    if prev_out is not None:
        in_specs.append(pl.BlockSpec(memory_space=pl.ANY))
        args.append(prev_out)
        aliases = {len(args) - 1: 0}
    return pl.pallas_call(
        _final_kernel,
        grid=(B, S // ts),
        in_specs=in_specs,
        out_specs=pl.BlockSpec((1, ts, D), lambda b, s: (b + b0, s, 0)),
        out_shape=jax.ShapeDtypeStruct((b_total, S, D), F32),
        scratch_shapes=[pltpu.VMEM((D, 2 * dh), BF16), pltpu.VMEM((dh, D), BF16)],
        input_output_aliases=aliases,
        compiler_params=pltpu.CompilerParams(dimension_semantics=("arbitrary", "arbitrary"),
                                             vmem_limit_bytes=VMEM_LIMIT),
        name="final",
    )(*args)


def _layer(x, c_act_mod, norm1_g, norm2_g, w_in, q_norm_g, k_norm_g, w_pool, pool_scale, w_out,
           w_router, router_bias, w_gate, w_up, w_down, ws_gate, ws_up, ws_down):
    B, S, D = x.shape
    mod3 = c_act_mod.reshape(B, N_MOD, D)
    head_of = jnp.arange(D_ATTN, dtype=jnp.int32) // HEAD_DIM
    head_mean = jnp.where(head_of[:, None] == head_of[None, :], 1.0 / HEAD_DIM, 0.0).astype(BF16)
    j = jnp.arange(TK, dtype=jnp.int32)
    umat = jnp.where(j[:, None] >= j[None, :], -1.0, 0.0).astype(BF16)

    q, k, v, pool = _inproj(
        x, mod3, norm1_g.reshape(1, D), w_in,
        jnp.tile(q_norm_g, N_HEADS).reshape(1, D_ATTN), jnp.tile(k_norm_g, N_HEADS).reshape(1, D_ATTN),
        head_mean, w_pool, pool_scale.reshape(1, D_POOL))
    attn = _attention(q, k, v, umat)
    t = jnp.arange(TS_PROJ, dtype=jnp.int32)
    ut = (t[:, None] < t[None, :]).astype(BF16)
    e = jnp.arange(N_EXPERTS, dtype=jnp.int32)
    lt = (e[None, :] < e[:, None]).astype(BF16)
    bp = B // MOE_PARTS
    T = bp * S
    n_blocks = _n_row_blocks(T)
    out = None
    for part in range(MOE_PARTS):
        b0 = part * bp
        x1, hp, rank_t, wd_t, cnt = _outproj(x, attn, pool, mod3, w_out, norm2_g.reshape(1, D),
                                             w_router.T, router_bias.reshape(N_EXPERTS, 1), ut, b0, bp)
        slots, wk_tok, block_expert, n_used = _slots(rank_t, wd_t, cnt, lt)
        slot_flat = slots[:TOP_K].reshape(TOP_K * T)
        xs = _sc_dispatch(hp.reshape(T, D // 2), slot_flat, n_blocks * ROW_BLOCK)
        ys = _experts(xs, block_expert[0, :n_blocks], n_used[0, :1], w_gate, w_up, w_down)
        g = _sc_gather(ys, slot_flat).reshape(TOP_K, bp, S, D // 2)
        out = _final(x1, hp, g, wk_tok.reshape(bp, S, LANES), mod3, ws_gate, ws_up, ws_down, out, b0, B)
    return out


def kernel(x, c, w_ada, b_ada, norm1_g, norm2_g, w_in, q_norm_g, k_norm_g, w_pool, pool_scale, w_out,
           w_router, router_bias, w_gate, w_up, w_down, ws_gate, ws_up, ws_down):
    depth = w_ada.shape[0]
    for l in range(depth):
        mod = _adaln(c, w_ada[l], b_ada[l])
        x = _layer(x, mod, norm1_g[l], norm2_g[l], w_in[l], q_norm_g[l], k_norm_g[l], w_pool[l],
                   pool_scale[l], w_out[l], w_router[l], router_bias[l], w_gate[l], w_up[l], w_down[l],
                   ws_gate[l], ws_up[l], ws_down[l])
    return x
```

```python
import functools

import jax
import jax.numpy as jnp
from jax import lax
from jax.experimental import pallas as pl
from jax.experimental.pallas import tpu as pltpu
from jax.experimental.pallas import tpu_sc as plsc

F32 = jnp.float32
BF16 = jnp.bfloat16

HEAD_DIM = 64
N_HEADS = 8
D_ATTN = N_HEADS * HEAD_DIM
POOL_WINDOWS = (2, 4, 8, 16)
POOL_GROUP_DIM = 128
D_POOL = len(POOL_WINDOWS) * POOL_GROUP_DIM
MAX_WINDOW = max(POOL_WINDOWS)
N_EXPERTS = 64
TOP_K = 6
N_GROUPS = 8
GROUP_SIZE = N_EXPERTS // N_GROUPS
TOPK_GROUPS = 4
ROUTED_SCALE = 2.5
RMS_EPS = 1e-6
N_MOD = 6

LANES = 128
SUBLANES = 8
VMEM_LIMIT = 56 * 1024 * 1024

TS_PROJ = 1024
TQ = 1024
TK = 256
PAIRS_PER_STEP = 2
TS_SLOT = 2048
ROW_BLOCK_LOG2 = 9
ROW_BLOCK = 1 << ROW_BLOCK_LOG2
BLOCKS_PER_STEP = 2
MOE_PARTS = 2
SC_CORES = 2
SC_SUBCORES = 16
SC_CHUNK = 64


def _split_bf16(a):
    hi = a.astype(BF16)
    lo = (a - hi.astype(F32)).astype(BF16)
    return hi, lo


def _dot(a, b):
    return jnp.dot(a, b, preferred_element_type=F32)


def _dot_nt(a, b):
    return lax.dot_general(a, b, (((1,), (1,)), ((), ())), preferred_element_type=F32)


def _dot3(a, b):
    ah, al = _split_bf16(a)
    bh, bl = _split_bf16(b)
    return _dot(ah, bh) + _dot(ah, bl) + _dot(al, bh)


def _dot3_nt(a, b):
    ah, al = _split_bf16(a)
    bh, bl = _split_bf16(b)
    return _dot_nt(ah, bh) + _dot_nt(ah, bl) + _dot_nt(al, bh)


def _silu(x):
    return x * (1.0 / (1.0 + jnp.exp(-x)))


def _rms_mod(x, gain, scale, shift):
    ms = jnp.mean(x * x, axis=-1, keepdims=True)
    y = x * lax.rsqrt(ms + RMS_EPS) * gain
    return y * (1.0 + scale) + shift


def _adaln_kernel(c_ref, w_ref, b_ref, o_ref):
    c = c_ref[...]
    o_ref[...] = _dot3(_silu(c), w_ref[...]) + b_ref[...]


def _adaln(c, w_ada, b_ada):
    nb, D = c.shape
    B = -(-nb // SUBLANES) * SUBLANES
    c = jnp.pad(c, ((0, B - nb), (0, 0)))
    N = w_ada.shape[1]
    tn = 1024
    out = pl.pallas_call(
        _adaln_kernel,
        grid=(N // tn,),
        in_specs=[pl.BlockSpec((B, D), lambda j: (0, 0)),
                  pl.BlockSpec((D, tn), lambda j: (0, j)),
                  pl.BlockSpec((1, tn), lambda j: (0, j))],
        out_specs=pl.BlockSpec((B, tn), lambda j: (0, j)),
        out_shape=jax.ShapeDtypeStruct((B, N), F32),
        compiler_params=pltpu.CompilerParams(dimension_semantics=("arbitrary",),
                                             vmem_limit_bytes=VMEM_LIMIT),
        name="adaln",
    )(c, w_ada, b_ada.reshape(1, N))
    return out[:nb]


def _first_grid_step():
    return jnp.logical_and(pl.program_id(0) == 0, pl.program_id(1) == 0)


def _inproj_kernel(x_ref, mod_ref, g1_ref, win32_ref, qg_ref, kg_ref, hm_ref, wp32_ref, ps_ref,
                   q_ref, k_ref, v_ref, p_ref, ext_ref, win_ref, wp_ref):
    @pl.when(_first_grid_step())
    def _():
        win_ref[...] = win32_ref[...].astype(BF16)
        wp_ref[...] = wp32_ref[...].astype(BF16)

    si = pl.program_id(1)
    ts = x_ref.shape[1]
    x = x_ref[0]
    shift1 = mod_ref[0, 0:1, :]
    scale1 = mod_ref[0, 1:2, :]
    h = _rms_mod(x, g1_ref[...], scale1, shift1)
    proj = _dot(h.astype(BF16), win_ref[...])

    hm = hm_ref[...]

    def head_norm(t, gain):
        ms = _dot((t * t).astype(BF16), hm)
        return t * lax.rsqrt(ms + RMS_EPS) * gain

    hq = proj[:, 0:D_ATTN]
    hk = proj[:, D_ATTN:2 * D_ATTN]
    q_ref[0] = (head_norm(hq, qg_ref[...]) * (HEAD_DIM ** -0.5)).astype(BF16)
    k_ref[0] = head_norm(hk, kg_ref[...]).astype(BF16)
    v_ref[0] = proj[:, 2 * D_ATTN:3 * D_ATTN].astype(BF16)

    hp = proj[:, 3 * D_ATTN:]

    @pl.when(si == 0)
    def _():
        ext_ref[0:MAX_WINDOW, :] = jnp.zeros((MAX_WINDOW, D_POOL), F32)

    ext_ref[MAX_WINDOW:, :] = hp
    pos = si * ts + lax.broadcasted_iota(jnp.int32, (ts, 1), 0)
    for g, w in enumerate(POOL_WINDOWS):
        lo_l, hi_l = g * POOL_GROUP_DIM, (g + 1) * POOL_GROUP_DIM
        u = hp[:, lo_l:hi_l]
        acc = u
        for i in range(1, w):
            acc = acc + ext_ref[pl.ds(MAX_WINDOW - i, ts), lo_l:hi_l]
        count = jnp.minimum(pos + 1, w).astype(F32)
        d = acc / count - u
        mixed = _dot(d.astype(BF16), wp_ref[g])
        p_ref[0, :, lo_l:hi_l] = (mixed * ps_ref[:, lo_l:hi_l]).astype(BF16)
    ext_ref[0:MAX_WINDOW, :] = hp[ts - MAX_WINDOW:, :]


def _inproj(x, mod3, g1, w_in, qg_t, kg_t, head_mean, w_pool, pool_scale):
    B, S, D = x.shape
    ts = TS_PROJ
    out_sd = jax.ShapeDtypeStruct((B, S, D_ATTN), BF16)
    blk = pl.BlockSpec((1, ts, D_ATTN), lambda b, s: (b, s, 0))
    const2 = lambda b, s: (0, 0)
    return pl.pallas_call(
        _inproj_kernel,
        grid=(B, S // ts),
        in_specs=[pl.BlockSpec((1, ts, D), lambda b, s: (b, s, 0)),
                  pl.BlockSpec((1, N_MOD, D), lambda b, s: (b, 0, 0)),
                  pl.BlockSpec((1, D), const2),
                  pl.BlockSpec(w_in.shape, const2),
                  pl.BlockSpec((1, D_ATTN), const2),
                  pl.BlockSpec((1, D_ATTN), const2),
                  pl.BlockSpec((D_ATTN, D_ATTN), const2),
                  pl.BlockSpec(w_pool.shape, lambda b, s: (0, 0, 0)),
                  pl.BlockSpec((1, D_POOL), const2)],
        out_specs=[blk, blk, blk, blk],
        out_shape=[out_sd, out_sd, out_sd, out_sd],
        scratch_shapes=[pltpu.VMEM((MAX_WINDOW + ts, D_POOL), F32),
                        pltpu.VMEM(w_in.shape, BF16), pltpu.VMEM(w_pool.shape, BF16)],
        compiler_params=pltpu.CompilerParams(dimension_semantics=("arbitrary", "arbitrary"),
                                             vmem_limit_bytes=VMEM_LIMIT),
        name="inproj",
    )(x, mod3, g1, w_in, qg_t, kg_t, head_mean, w_pool, pool_scale)


def _attn_kernel(q_ref, k_ref, v_ref, u_ref, o_ref, acc_ref):
    qi = pl.program_id(2)
    lane = lax.broadcasted_iota(jnp.int32, (TQ, LANES), 1)
    first = lane < HEAD_DIM
    n_heads = 2 * PAIRS_PER_STEP
    pair_lanes = lambda h: pl.ds((h // 2) * LANES, LANES)
    qh = []
    for h in range(n_heads):
        q = q_ref[0, :, pair_lanes(h)]
        keep = first if h % 2 == 0 else jnp.logical_not(first)
        qh.append(jnp.where(keep, q, jnp.zeros_like(q)))
    u = u_ref[...]
    row = lax.broadcasted_iota(jnp.int32, (TQ, TK), 0)
    col = lax.broadcasted_iota(jnp.int32, (TQ, TK), 1)
    acc_ref[...] = jnp.zeros(acc_ref.shape, F32)

    def block(kb, survs, diag):
        start = pl.multiple_of(kb * TK, TK)
        r0 = 0 if diag is None else diag * TK
        if diag is not None:
            valid = (col + r0 < row)[r0:]
        out = []
        for h in range(n_heads):
            k = k_ref[0, pl.ds(start, TK), pair_lanes(h)]
            v = v_ref[0, pl.ds(start, TK), pair_lanes(h)]
            z = _dot_nt(qh[h][r0:], k)
            zb = z.astype(BF16)
            sp = jnp.maximum(zb, 0) + jnp.log(1 + jnp.exp(-jnp.abs(zb)))
            if diag is not None:
                sp = jnp.where(valid, sp, jnp.zeros_like(sp))
            r = _dot(sp, u)
            arg = z + r + survs[h][r0:]
            if diag is not None:
                arg = jnp.where(valid, arg, -jnp.inf)
            acc_ref[h, r0:, :] += _dot(jnp.exp(arg).astype(BF16), v)
            surv = survs[h][r0:] + r[:, 0:1]
            out.append(surv if r0 == 0 else jnp.concatenate([survs[h][:r0], surv], axis=0))
        return tuple(out)

    survs = tuple(jnp.zeros((TQ, 1), F32) for _ in range(n_heads))
    n_diag = TQ // TK
    for d in reversed(range(n_diag)):
        survs = block(qi * n_diag + d, survs, d)
    def full_blocks(i, c):
        for d in range(n_diag):
            c = block((qi - i) * n_diag - 1 - d, c, None)
        return c

    lax.fori_loop(0, qi, full_blocks, survs)
    for p in range(PAIRS_PER_STEP):
        o_ref[0, :, pl.ds(p * LANES, LANES)] = jnp.where(first, acc_ref[2 * p], acc_ref[2 * p + 1]).astype(BF16)


def _attention(q, k, v, umat):
    B, S, _ = q.shape
    width = PAIRS_PER_STEP * LANES
    kv_spec = pl.BlockSpec((1, S, width), lambda b, p, i: (b, 0, p))
    q_spec = pl.BlockSpec((1, TQ, width), lambda b, p, i: (b, i, p))
    return pl.pallas_call(
        _attn_kernel,
        grid=(B, D_ATTN // width, S // TQ),
        in_specs=[q_spec, kv_spec, kv_spec, pl.BlockSpec((TK, TK), lambda b, p, i: (0, 0))],
        out_specs=q_spec,
        out_shape=jax.ShapeDtypeStruct((B, S, D_ATTN), BF16),
        scratch_shapes=[pltpu.VMEM((2 * PAIRS_PER_STEP, TQ, LANES), F32)],
        compiler_params=pltpu.CompilerParams(
            dimension_semantics=("arbitrary", "arbitrary", "arbitrary"),
            vmem_limit_bytes=VMEM_LIMIT),
        name="stickbreak_attn",
    )(q, k, v, umat)


def _route_t(scores, biased):
    ts = scores.shape[-1]
    neg = -jnp.inf
    b3 = biased.reshape(N_GROUPS, GROUP_SIZE, ts)
    e_in_g = lax.broadcasted_iota(jnp.int32, b3.shape, 1)
    m1 = jnp.max(b3, axis=1, keepdims=True)
    i1 = jnp.min(jnp.where(b3 == m1, e_in_g, GROUP_SIZE), axis=1, keepdims=True)
    m2 = jnp.max(jnp.where(e_in_g == i1, neg, b3), axis=1, keepdims=True)
    gs = (m1 + m2)[:, 0, :]
    g_iota = lax.broadcasted_iota(jnp.int32, gs.shape, 0)
    g_sel = jnp.zeros(gs.shape, jnp.bool_)
    for _ in range(TOPK_GROUPS):
        gm = jnp.max(gs, axis=0, keepdims=True)
        gi = jnp.min(jnp.where(gs == gm, g_iota, N_GROUPS), axis=0, keepdims=True)
        pick = g_iota == gi
        g_sel = jnp.logical_or(g_sel, pick)
        gs = jnp.where(pick, neg, gs)
    masked = jnp.where(g_sel[:, None, :], b3, neg)
    flat = lax.broadcasted_iota(jnp.int32, b3.shape, 0) * GROUP_SIZE + e_in_g
    sel = jnp.zeros(b3.shape, jnp.bool_)
    for _ in range(TOP_K):
        m = jnp.max(jnp.max(masked, axis=1, keepdims=True), axis=0, keepdims=True)
        cand = jnp.where(masked == m, flat, N_EXPERTS)
        idx = jnp.min(jnp.min(cand, axis=1, keepdims=True), axis=0, keepdims=True)
        pick = flat == idx
        sel = jnp.logical_or(sel, pick)
        masked = jnp.where(pick, neg, masked)
    s3 = scores.reshape(N_GROUPS, GROUP_SIZE, ts)
    w = jnp.where(sel, s3, 0.0)
    tot = jnp.sum(jnp.sum(w, axis=1, keepdims=True), axis=0, keepdims=True)
    return (w / tot * ROUTED_SCALE).reshape(N_EXPERTS, ts), sel.reshape(N_EXPERTS, ts)


def _pack_bf16_pair(lo, hi):
    lo_bits = pltpu.bitcast(lo.astype(BF16).astype(F32), jnp.uint32) >> 16
    hi_bits = pltpu.bitcast(hi.astype(BF16).astype(F32), jnp.uint32) & jnp.uint32(0xFFFF0000)
    return lo_bits | hi_bits


def _unpack_bf16_pair(p):
    lo = pltpu.bitcast(p << 16, F32).astype(BF16)
    hi = pltpu.bitcast(p & jnp.uint32(0xFFFF0000), F32).astype(BF16)
    return lo, hi


def _outproj_kernel(x_ref, a_ref, p_ref, mod_ref, wo32_ref, g2_ref, wrt_ref, rb_ref, ut_ref,
                    x1_ref, hp_ref, rank_ref, wd_ref, cnt_ref, run_ref, wo_ref):
    @pl.when(_first_grid_step())
    def _():
        run_ref[...] = jnp.zeros(run_ref.shape, jnp.int32)
        wo_ref[...] = wo32_ref[...].astype(BF16)

    gate1 = mod_ref[0, 2:3, :]
    shift2 = mod_ref[0, 3:4, :]
    scale2 = mod_ref[0, 4:5, :]
    mixp = _dot(a_ref[0], wo_ref[0:D_ATTN, :]) + _dot(p_ref[0], wo_ref[D_ATTN:, :])
    x1 = x_ref[0] + gate1 * mixp
    x1_ref[0] = x1
    h2 = _rms_mod(x1, g2_ref[...], scale2, shift2)
    half = h2.shape[-1] // 2
    hp_ref[0] = _pack_bf16_pair(h2[:, :half], h2[:, half:])
    logits_t = _dot3_nt(wrt_ref[...], h2)
    scores = 1.0 / (1.0 + jnp.exp(-logits_t))
    wd_t, sel = _route_t(scores, scores + rb_ref[...])
    wd_ref[...] = wd_t
    self_f = jnp.where(sel, 1.0, 0.0)
    before = _dot(self_f.astype(BF16), ut_ref[...]).astype(jnp.int32)
    run = run_ref[:, 0:1]
    rank_ref[...] = jnp.where(sel, run + before, -1)
    run_ref[...] = run_ref[...] + jnp.sum(self_f, axis=1, keepdims=True).astype(jnp.int32)
    cnt_ref[...] = run_ref[...]


def _outproj(x, attn, pool, mod3, w_out, g2, wr_t, rbias, ut, b0, B):
    _, S, D = x.shape
    ts = TS_PROJ
    n_s = S // ts
    const2 = lambda b, s: (0, 0)
    tok_in = lambda w: pl.BlockSpec((1, ts, w), lambda b, s: (b + b0, s, 0))
    tok = lambda w: pl.BlockSpec((1, ts, w), lambda b, s: (b, s, 0))
    tok_t = pl.BlockSpec((N_EXPERTS, ts), lambda b, s: (0, b * n_s + s))
    return pl.pallas_call(
        _outproj_kernel,
        grid=(B, n_s),
        in_specs=[tok_in(D), tok_in(D_ATTN), tok_in(D_POOL),
                  pl.BlockSpec((1, N_MOD, D), lambda b, s: (b + b0, 0, 0)),
                  pl.BlockSpec(w_out.shape, const2),
                  pl.BlockSpec((1, D), const2),
                  pl.BlockSpec(wr_t.shape, const2),
                  pl.BlockSpec((N_EXPERTS, 1), const2),
                  pl.BlockSpec((ts, ts), const2)],
        out_specs=[tok(D), tok(D // 2), tok_t, tok_t, pl.BlockSpec((N_EXPERTS, LANES), const2)],
        out_shape=[jax.ShapeDtypeStruct((B, S, D), F32),
                   jax.ShapeDtypeStruct((B, S, D // 2), jnp.uint32),
                   jax.ShapeDtypeStruct((N_EXPERTS, B * S), jnp.int32),
                   jax.ShapeDtypeStruct((N_EXPERTS, B * S), F32),
                   jax.ShapeDtypeStruct((N_EXPERTS, LANES), jnp.int32)],
        scratch_shapes=[pltpu.VMEM((N_EXPERTS, LANES), jnp.int32), pltpu.VMEM(w_out.shape, BF16)],
        compiler_params=pltpu.CompilerParams(dimension_semantics=("arbitrary", "arbitrary"),
                                             vmem_limit_bytes=VMEM_LIMIT),
        name="outproj_router",
    )(x, attn, pool, mod3, w_out, g2, wr_t, rbias, ut)


def _n_row_blocks(n_tokens):
    n_blocks = -(-(n_tokens * TOP_K + N_EXPERTS * (ROW_BLOCK - 1)) // ROW_BLOCK)
    return -(-n_blocks // BLOCKS_PER_STEP) * BLOCKS_PER_STEP


def _slots_kernel(rank_ref, wd_ref, cnt_ref, lt_ref, slot_ref, wk_ref, be_ref, nu_ref):
    lt = lt_ref[...]
    nblk = lax.shift_right_logical(cnt_ref[...] + (ROW_BLOCK - 1), ROW_BLOCK_LOG2).astype(F32)
    nb_hi, nb_lo = _split_bf16(nblk)
    blk_start = _dot(lt, nb_hi) + _dot(lt, nb_lo)

    @pl.when(pl.program_id(0) == 0)
    def _():
        blk_end = (blk_start + nblk)[:, 0:1]
        b_iota = lax.broadcasted_iota(jnp.int32, (N_EXPERTS, be_ref.shape[-1]), 1).astype(F32)
        owner = jnp.sum(jnp.where(blk_end <= b_iota, 1, 0), axis=0, keepdims=True)
        be_ref[...] = jnp.minimum(owner, N_EXPERTS - 1)
        nu_ref[...] = jnp.broadcast_to(blk_end[N_EXPERTS - 1:, :].astype(jnp.int32), nu_ref.shape)

    rank = rank_ref[...]
    sel = rank >= 0
    row_start = (blk_start[:, 0:1] * ROW_BLOCK).astype(jnp.int32)
    slot_d = row_start + rank
    wd = wd_ref[...]
    choice = _dot(lt, jnp.where(sel, 1.0, 0.0).astype(BF16)).astype(jnp.int32)
    ts = rank.shape[-1]
    slots, wks = [], []
    for k in range(TOP_K):
        m = jnp.logical_and(sel, choice == k)
        slots.append(jnp.sum(jnp.where(m, slot_d, 0), axis=0, keepdims=True))
        wks.append(jnp.sum(jnp.where(m, wd, 0.0), axis=0, keepdims=True))
    slot_ref[...] = jnp.concatenate(slots + [jnp.zeros((SUBLANES - TOP_K, ts), jnp.int32)], axis=0)
    wk_pad = jnp.concatenate(wks + [jnp.zeros((LANES - TOP_K, ts), F32)], axis=0)
    wk_ref[...] = wk_pad.T


def _slots(rank_t, wd_t, cnt, lt):
    T = rank_t.shape[1]
    ts = TS_SLOT
    nb_pad = -(-_n_row_blocks(T) // LANES) * LANES
    const = lambda i: (0, 0)
    tok_t = pl.BlockSpec((N_EXPERTS, ts), lambda i: (0, i))
    return pl.pallas_call(
        _slots_kernel,
        grid=(T // ts,),
        in_specs=[tok_t, tok_t, pl.BlockSpec((N_EXPERTS, LANES), const), pl.BlockSpec((N_EXPERTS, N_EXPERTS), const)],
        out_specs=[pl.BlockSpec((SUBLANES, ts), lambda i: (0, i)),
                   pl.BlockSpec((ts, LANES), lambda i: (i, 0)),
                   pl.BlockSpec((1, nb_pad), const),
                   pl.BlockSpec((1, LANES), const)],
        out_shape=[jax.ShapeDtypeStruct((SUBLANES, T), jnp.int32),
                   jax.ShapeDtypeStruct((T, LANES), F32),
                   jax.ShapeDtypeStruct((1, nb_pad), jnp.int32),
                   jax.ShapeDtypeStruct((1, LANES), jnp.int32)],
        compiler_params=pltpu.CompilerParams(dimension_semantics=("arbitrary",), vmem_limit_bytes=VMEM_LIMIT),
        name="slots",
    )(rank_t, wd_t, cnt, lt)


def _sc_mesh():
    return plsc.VectorSubcoreMesh(core_axis_name="c", subcore_axis_name="s",
                                  num_cores=SC_CORES, num_subcores=SC_SUBCORES)


def _sc_dispatch(rows, slot_flat, n_out):
    T, width = rows.shape
    n_workers = SC_CORES * SC_SUBCORES
    per_worker = T // n_workers
    steps = per_worker // SC_CHUNK

    @functools.partial(
        pl.kernel, mesh=_sc_mesh(),
        out_type=jax.ShapeDtypeStruct((n_out, width), rows.dtype),
        scratch_types=[pltpu.VMEM((SC_CHUNK, width), rows.dtype)]
        + [pltpu.VMEM((SC_CHUNK,), jnp.int32)] * TOP_K + [pltpu.SemaphoreType.DMA],
        name="dispatch",
    )
    def run(rows_hbm, slot_hbm, out_hbm, rows_v, *rest):
        idx_v, sem = rest[:TOP_K], rest[TOP_K]
        base = (lax.axis_index("s") * SC_CORES + lax.axis_index("c")) * per_worker

        @pl.loop(0, steps)
        def _(i):
            off = base + i * SC_CHUNK
            pltpu.sync_copy(rows_hbm.at[pl.ds(off, SC_CHUNK)], rows_v)
            for k in range(TOP_K):
                pltpu.sync_copy(slot_hbm.at[pl.ds(k * T + off, SC_CHUNK)], idx_v[k])
            copies = [pltpu.async_copy(rows_v, out_hbm.at[idx_v[k]], sem) for k in range(TOP_K)]
            for cp in copies:
                cp.wait()

    return run(rows, slot_flat)


def _sc_gather(rows, idx):
    n = idx.shape[0]
    width = rows.shape[1]
    n_workers = SC_CORES * SC_SUBCORES
    per_worker = n // n_workers
    steps = per_worker // SC_CHUNK

    assert steps % 2 == 0
    slot_types = [pltpu.VMEM((SC_CHUNK,), jnp.int32), pltpu.VMEM((SC_CHUNK, width), rows.dtype),
                  pltpu.SemaphoreType.DMA]

    @functools.partial(
        pl.kernel, mesh=_sc_mesh(),
        out_type=jax.ShapeDtypeStruct((n, width), rows.dtype),
        scratch_types=slot_types * 2,
        name="combine",
    )
    def run(rows_hbm, idx_hbm, out_hbm, *scratch):
        base = (lax.axis_index("s") * SC_CORES + lax.axis_index("c")) * per_worker
        slots = (scratch[0:3], scratch[3:6])

        def gather(slot):
            idx_v, rows_v, sem = slots[slot]
            return pltpu.make_async_copy(rows_hbm.at[idx_v], rows_v, sem)

        def start(chunk, slot):
            pltpu.sync_copy(idx_hbm.at[pl.ds(base + chunk * SC_CHUNK, SC_CHUNK)], slots[slot][0])
            gather(slot).start()

        def finish(chunk, slot):
            gather(slot).wait()
            pltpu.sync_copy(slots[slot][1], out_hbm.at[pl.ds(base + chunk * SC_CHUNK, SC_CHUNK)])

        start(0, 0)

        @pl.loop(0, steps, step=2)
        def _(chunk):
            start(chunk + 1, 1)
            finish(chunk, 0)

            @pl.when(chunk + 2 < steps)
            def _():
                start(chunk + 2, 0)

            finish(chunk + 1, 1)

    return run(rows, idx)


def _swiglu_packed(xp, wgu, wd):
    lo, hi = _unpack_bf16_pair(xp)
    half = xp.shape[-1]
    gu = _dot(lo, wgu[:half, :]) + _dot(hi, wgu[half:, :])
    dh = wgu.shape[-1] // 2
    act = _silu(gu[:, :dh]) * gu[:, dh:]
    return _dot(act.astype(BF16), wd)


def _cast_swiglu_weights(wg32_ref, wu32_ref, wd32_ref, wgu_ref, wd_ref):
    dh = wg32_ref.shape[-1]
    wgu_ref[:, :dh] = wg32_ref[...].reshape(wg32_ref.shape[-2:]).astype(BF16)
    wgu_ref[:, dh:] = wu32_ref[...].reshape(wu32_ref.shape[-2:]).astype(BF16)
    wd_ref[...] = wd32_ref[...].reshape(wd32_ref.shape[-2:]).astype(BF16)


def _experts_kernel(be_ref, nu_ref, x_ref, *refs):
    n = BLOCKS_PER_STEP
    w32 = [refs[3 * j:3 * j + 3] for j in range(n)]
    y_ref = refs[3 * n]
    wbf = [refs[3 * n + 1 + 2 * j:3 * n + 3 + 2 * j] for j in range(n)]
    step = pl.program_id(0)
    for j in range(n):
        blk = step * n + j
        new_expert = jnp.logical_or(step == 0, be_ref[blk] != be_ref[jnp.maximum(blk - n, 0)])

        @pl.when(new_expert)
        def _():
            _cast_swiglu_weights(*w32[j], *wbf[j])

    @pl.when(step * n < nu_ref[0])
    def _():
        for j in range(n):
            rows = pl.ds(j * ROW_BLOCK, ROW_BLOCK)
            y = _swiglu_packed(x_ref[rows, :], wbf[j][0][...], wbf[j][1][...])
            half = y.shape[-1] // 2
            y_ref[rows, :] = _pack_bf16_pair(y[:, :half], y[:, half:])


def _experts(xs, block_expert, n_used, w_gate, w_up, w_down):
    P, half = xs.shape
    _, D, dh = w_gate.shape
    n = BLOCKS_PER_STEP
    rows = n * ROW_BLOCK
    assert P % rows == 0

    def by_expert(shape, j):
        return pl.BlockSpec((1,) + shape, lambda s, be, nu: (be[s * n + j], 0, 0))

    w_specs, w_args = [], []
    for j in range(n):
        w_specs += [by_expert((D, dh), j), by_expert((D, dh), j), by_expert((dh, D), j)]
        w_args += [w_gate, w_up, w_down]
    def row_step(s, be, nu):
        return jnp.minimum(s, lax.div(jnp.maximum(nu[0], 1) - 1, n)), 0

    grid_spec = pltpu.PrefetchScalarGridSpec(
        num_scalar_prefetch=2,
        grid=(P // rows,),
        in_specs=[pl.BlockSpec((rows, half), row_step)] + w_specs,
        out_specs=pl.BlockSpec((rows, half), row_step),
        scratch_shapes=[pltpu.VMEM((D, 2 * dh), BF16), pltpu.VMEM((dh, D), BF16)] * n,
    )
    return pl.pallas_call(
        _experts_kernel,
        grid_spec=grid_spec,
        out_shape=jax.ShapeDtypeStruct((P, half), jnp.uint32),
        compiler_params=pltpu.CompilerParams(dimension_semantics=("arbitrary",), vmem_limit_bytes=VMEM_LIMIT),
        name="experts",
    )(block_expert, n_used, xs, *w_args)


def _final_kernel(x1_ref, hp_ref, g_ref, wk_ref, mod_ref, sg32_ref, su32_ref, sd32_ref, *rest):
    o_ref, sgu_ref, sd_ref = rest[-3:]

    @pl.when(_first_grid_step())
    def _():
        _cast_swiglu_weights(sg32_ref, su32_ref, sd32_ref, sgu_ref, sd_ref)

    acc = _swiglu_packed(hp_ref[0], sgu_ref[...], sd_ref[...])
    wk = wk_ref[0]
    for k in range(TOP_K):
        lo, hi = _unpack_bf16_pair(g_ref[k, 0])
        y = jnp.concatenate([lo.astype(F32), hi.astype(F32)], axis=-1)
        acc = acc + wk[:, k:k + 1] * y
    gate2 = mod_ref[0, 5:6, :]
    o_ref[0] = x1_ref[0] + gate2 * acc


def _final(x1, hp, g, wk_tok, mod3, ws_gate, ws_up, ws_down, prev_out, b0, b_total):
    B, S, D = x1.shape
    dh = ws_gate.shape[-1]
    ts = TS_PROJ
    tok = lambda w: pl.BlockSpec((1, ts, w), lambda b, s: (b, s, 0))
    const2 = lambda b, s: (0, 0)
    in_specs = [tok(D), tok(D // 2),
                pl.BlockSpec((TOP_K, 1, ts, D // 2), lambda b, s: (0, b, s, 0)),
                tok(LANES),
                pl.BlockSpec((1, N_MOD, D), lambda b, s: (b + b0, 0, 0)),
                pl.BlockSpec(ws_gate.shape, const2),
                pl.BlockSpec(ws_up.shape, const2),
                pl.BlockSpec(ws_down.shape, const2)]
    args = [x1, hp, g, wk_tok, mod3, ws_gate, ws_up, ws_down]
    aliases = {}
    if prev_out is not None:
        in_specs.append(pl.BlockSpec(memory_space=pl.ANY))
        args.append(prev_out)
        aliases = {len(args) - 1: 0}
    return pl.pallas_call(
        _final_kernel,
        grid=(B, S // ts),
        in_specs=in_specs,
        out_specs=pl.BlockSpec((1, ts, D), lambda b, s: (b + b0, s, 0)),
        out_shape=jax.ShapeDtypeStruct((b_total, S, D), F32),
        scratch_shapes=[pltpu.VMEM((D, 2 * dh), BF16), pltpu.VMEM((dh, D), BF16)],
        input_output_aliases=aliases,
        compiler_params=pltpu.CompilerParams(dimension_semantics=("arbitrary", "arbitrary"),
                                             vmem_limit_bytes=VMEM_LIMIT),
        name="final",
    )(*args)


def _layer(x, c_act_mod, norm1_g, norm2_g, w_in, q_norm_g, k_norm_g, w_pool, pool_scale, w_out,
           w_router, router_bias, w_gate, w_up, w_down, ws_gate, ws_up, ws_down):
    B, S, D = x.shape
    mod3 = c_act_mod.reshape(B, N_MOD, D)
    head_of = jnp.arange(D_ATTN, dtype=jnp.int32) // HEAD_DIM
    head_mean = jnp.where(head_of[:, None] == head_of[None, :], 1.0 / HEAD_DIM, 0.0).astype(BF16)
    j = jnp.arange(TK, dtype=jnp.int32)
    umat = jnp.where(j[:, None] >= j[None, :], -1.0, 0.0).astype(BF16)

    q, k, v, pool = _inproj(
        x, mod3, norm1_g.reshape(1, D), w_in,
        jnp.tile(q_norm_g, N_HEADS).reshape(1, D_ATTN), jnp.tile(k_norm_g, N_HEADS).reshape(1, D_ATTN),
        head_mean, w_pool, pool_scale.reshape(1, D_POOL))
    attn = _attention(q, k, v, umat)
    t = jnp.arange(TS_PROJ, dtype=jnp.int32)
    ut = (t[:, None] < t[None, :]).astype(BF16)
    e = jnp.arange(N_EXPERTS, dtype=jnp.int32)
    lt = (e[None, :] < e[:, None]).astype(BF16)
    bp = B // MOE_PARTS
    T = bp * S
    n_blocks = _n_row_blocks(T)
    out = None
    for part in range(MOE_PARTS):
        b0 = part * bp
        x1, hp, rank_t, wd_t, cnt = _outproj(x, attn, pool, mod3, w_out, norm2_g.reshape(1, D),
                                             w_router.T, router_bias.reshape(N_EXPERTS, 1), ut, b0, bp)
        slots, wk_tok, block_expert, n_used = _slots(rank_t, wd_t, cnt, lt)
        slot_flat = slots[:TOP_K].reshape(TOP_K * T)
        xs = _sc_dispatch(hp.reshape(T, D // 2), slot_flat, n_blocks * ROW_BLOCK)
        ys = _experts(xs, block_expert[0, :n_blocks], n_used[0, :1], w_gate, w_up, w_down)
        g = _sc_gather(ys, slot_flat).reshape(TOP_K, bp, S, D // 2)
        out = _final(x1, hp, g, wk_tok.reshape(bp, S, LANES), mod3, ws_gate, ws_up, ws_down, out, b0, B)
    return out


def kernel(x, c, w_ada, b_ada, norm1_g, norm2_g, w_in, q_norm_g, k_norm_g, w_pool, pool_scale, w_out,
           w_router, router_bias, w_gate, w_up, w_down, ws_gate, ws_up, ws_down):
    depth = w_ada.shape[0]
    for l in range(depth):
        mod = _adaln(c, w_ada[l], b_ada[l])
        x = _layer(x, mod, norm1_g[l], norm2_g[l], w_in[l], q_norm_g[l], k_norm_g[l], w_pool[l],
                   pool_scale[l], w_out[l], w_router[l], router_bias[l], w_gate[l], w_up[l], w_down[l],
                   ws_gate[l], ws_up[l], ws_down[l])
    return x
```

```python
import functools

import jax
import jax.numpy as jnp
from jax import lax
from jax.experimental import pallas as pl
from jax.experimental.pallas import tpu as pltpu
from jax.experimental.pallas import tpu_sc as plsc

F32 = jnp.float32
BF16 = jnp.bfloat16

HEAD_DIM = 64
N_HEADS = 8
D_ATTN = N_HEADS * HEAD_DIM
POOL_WINDOWS = (2, 4, 8, 16)
POOL_GROUP_DIM = 128
D_POOL = len(POOL_WINDOWS) * POOL_GROUP_DIM
MAX_WINDOW = max(POOL_WINDOWS)
assert all(w & (w - 1) == 0 for w in POOL_WINDOWS)
N_EXPERTS = 64
TOP_K = 6
N_GROUPS = 8
GROUP_SIZE = N_EXPERTS // N_GROUPS
TOPK_GROUPS = 4
ROUTED_SCALE = 2.5
RMS_EPS = 1e-6
N_MOD = 6

LANES = 128
SUBLANES = 8
VMEM_LIMIT = 56 * 1024 * 1024

TS_PROJ = 1024
TQ = 1024
TK = 256
PAIRS_PER_STEP = 2
TS_SLOT = 2048
ROW_BLOCK_LOG2 = 9
ROW_BLOCK = 1 << ROW_BLOCK_LOG2
BLOCKS_PER_STEP = 2
MOE_PARTS = 2
SC_CORES = 2
SC_SUBCORES = 16
SC_CHUNK = 64


def _split_bf16(a):
    hi = a.astype(BF16)
    lo = (a - hi.astype(F32)).astype(BF16)
    return hi, lo


def _dot(a, b):
    return jnp.dot(a, b, preferred_element_type=F32)


def _dot_nt(a, b):
    return lax.dot_general(a, b, (((1,), (1,)), ((), ())), preferred_element_type=F32)


def _dot3(a, b):
    ah, al = _split_bf16(a)
    bh, bl = _split_bf16(b)
    return _dot(ah, bh) + _dot(ah, bl) + _dot(al, bh)


def _dot3_nt(a, b):
    ah, al = _split_bf16(a)
    bh, bl = _split_bf16(b)
    return _dot_nt(ah, bh) + _dot_nt(ah, bl) + _dot_nt(al, bh)


def _silu(x):
    return x * (1.0 / (1.0 + jnp.exp(-x)))


def _rms_mod(x, gain, scale, shift):
    ms = jnp.mean(x * x, axis=-1, keepdims=True)
    y = x * lax.rsqrt(ms + RMS_EPS) * gain
    return y * (1.0 + scale) + shift


def _adaln_kernel(c_ref, w_ref, b_ref, o_ref):
    c = c_ref[...]
    o_ref[...] = _dot3(_silu(c), w_ref[...]) + b_ref[...]


def _adaln(c, w_ada, b_ada):
    nb, D = c.shape
    B = -(-nb // SUBLANES) * SUBLANES
    c = jnp.pad(c, ((0, B - nb), (0, 0)))
    N = w_ada.shape[1]
    tn = 1024
    out = pl.pallas_call(
        _adaln_kernel,
        grid=(N // tn,),
        in_specs=[pl.BlockSpec((B, D), lambda j: (0, 0)),
                  pl.BlockSpec((D, tn), lambda j: (0, j)),
                  pl.BlockSpec((1, tn), lambda j: (0, j))],
        out_specs=pl.BlockSpec((B, tn), lambda j: (0, j)),
        out_shape=jax.ShapeDtypeStruct((B, N), F32),
        compiler_params=pltpu.CompilerParams(dimension_semantics=("arbitrary",),
                                             vmem_limit_bytes=VMEM_LIMIT),
        name="adaln",
    )(c, w_ada, b_ada.reshape(1, N))
    return out[:nb]


def _first_grid_step():
    return jnp.logical_and(pl.program_id(0) == 0, pl.program_id(1) == 0)


def _inproj_kernel(x_ref, mod_ref, g1_ref, win32_ref, qg_ref, kg_ref, hm_ref, wp32_ref, ps_ref,
                   q_ref, k_ref, v_ref, p_ref, ext_ref, win_ref, wp_ref):
    @pl.when(_first_grid_step())
    def _():
        win_ref[...] = win32_ref[...].astype(BF16)
        wp_ref[...] = wp32_ref[...].astype(BF16)

    si = pl.program_id(1)
    ts = x_ref.shape[1]
    x = x_ref[0]
    shift1 = mod_ref[0, 0:1, :]
    scale1 = mod_ref[0, 1:2, :]
    h = _rms_mod(x, g1_ref[...], scale1, shift1)
    proj = _dot(h.astype(BF16), win_ref[...])

    hm = hm_ref[...]

    def head_norm(t, gain):
        ms = _dot((t * t).astype(BF16), hm)
        return t * lax.rsqrt(ms + RMS_EPS) * gain

    hq = proj[:, 0:D_ATTN]
    hk = proj[:, D_ATTN:2 * D_ATTN]
    q_ref[0] = (head_norm(hq, qg_ref[...]) * (HEAD_DIM ** -0.5)).astype(BF16)
    k_ref[0] = head_norm(hk, kg_ref[...]).astype(BF16)
    v_ref[0] = proj[:, 2 * D_ATTN:3 * D_ATTN].astype(BF16)

    hp = proj[:, 3 * D_ATTN:]

    @pl.when(si == 0)
    def _():
        ext_ref[0:MAX_WINDOW, :] = jnp.zeros((MAX_WINDOW, D_POOL), F32)

    ext_ref[MAX_WINDOW:, :] = hp
    pos = si * ts + lax.broadcasted_iota(jnp.int32, (ts, 1), 0)
    for g, w in enumerate(POOL_WINDOWS):
        lo_l, hi_l = g * POOL_GROUP_DIM, (g + 1) * POOL_GROUP_DIM
        u = hp[:, lo_l:hi_l]
        run = ext_ref[:, lo_l:hi_l]
        span = 1
        while span < w:
            run = run + pltpu.roll(run, span, axis=0)
            span *= 2
        acc = run[MAX_WINDOW:]
        count = jnp.minimum(pos + 1, w).astype(F32)
        d = acc / count - u
        mixed = _dot(d.astype(BF16), wp_ref[g])
        p_ref[0, :, lo_l:hi_l] = (mixed * ps_ref[:, lo_l:hi_l]).astype(BF16)
    ext_ref[0:MAX_WINDOW, :] = hp[ts - MAX_WINDOW:, :]


def _inproj(x, mod3, g1, w_in, qg_t, kg_t, head_mean, w_pool, pool_scale):
    B, S, D = x.shape
    ts = TS_PROJ
    out_sd = jax.ShapeDtypeStruct((B, S, D_ATTN), BF16)
    blk = pl.BlockSpec((1, ts, D_ATTN), lambda b, s: (b, s, 0))
    const2 = lambda b, s: (0, 0)
    return pl.pallas_call(
        _inproj_kernel,
        grid=(B, S // ts),
        in_specs=[pl.BlockSpec((1, ts, D), lambda b, s: (b, s, 0)),
                  pl.BlockSpec((1, N_MOD, D), lambda b, s: (b, 0, 0)),
                  pl.BlockSpec((1, D), const2),
                  pl.BlockSpec(w_in.shape, const2),
                  pl.BlockSpec((1, D_ATTN), const2),
                  pl.BlockSpec((1, D_ATTN), const2),
                  pl.BlockSpec((D_ATTN, D_ATTN), const2),
                  pl.BlockSpec(w_pool.shape, lambda b, s: (0, 0, 0)),
                  pl.BlockSpec((1, D_POOL), const2)],
        out_specs=[blk, blk, blk, blk],
        out_shape=[out_sd, out_sd, out_sd, out_sd],
        scratch_shapes=[pltpu.VMEM((MAX_WINDOW + ts, D_POOL), F32),
                        pltpu.VMEM(w_in.shape, BF16), pltpu.VMEM(w_pool.shape, BF16)],
        compiler_params=pltpu.CompilerParams(dimension_semantics=("arbitrary", "arbitrary"),
                                             vmem_limit_bytes=VMEM_LIMIT),
        name="inproj",
    )(x, mod3, g1, w_in, qg_t, kg_t, head_mean, w_pool, pool_scale)


def _attn_kernel(q_ref, k_ref, v_ref, u_ref, o_ref, acc_ref):
    qi = pl.program_id(2)
    lane = lax.broadcasted_iota(jnp.int32, (TQ, LANES), 1)
    first = lane < HEAD_DIM
    n_heads = 2 * PAIRS_PER_STEP
    pair_lanes = lambda h: pl.ds((h // 2) * LANES, LANES)
    qh = []
    for h in range(n_heads):
        q = q_ref[0, :, pair_lanes(h)]
        keep = first if h % 2 == 0 else jnp.logical_not(first)
        qh.append(jnp.where(keep, q, jnp.zeros_like(q)))
    u = u_ref[...]
    row = lax.broadcasted_iota(jnp.int32, (TQ, TK), 0)
    col = lax.broadcasted_iota(jnp.int32, (TQ, TK), 1)
    acc_ref[...] = jnp.zeros(acc_ref.shape, F32)

    def block(kb, survs, diag):
        start = pl.multiple_of(kb * TK, TK)
        r0 = 0 if diag is None else diag * TK
        if diag is not None:
            valid = (col + r0 < row)[r0:]
        out = []
        for h in range(n_heads):
            k = k_ref[0, pl.ds(start, TK), pair_lanes(h)]
            v = v_ref[0, pl.ds(start, TK), pair_lanes(h)]
            z = _dot_nt(qh[h][r0:], k)
            zb = z.astype(BF16)
            sp = jnp.maximum(zb, 0) + jnp.log(1 + jnp.exp(-jnp.abs(zb)))
            if diag is not None:
                sp = jnp.where(valid, sp, jnp.zeros_like(sp))
            r = _dot(sp, u)
            arg = z + r + survs[h][r0:]
            if diag is not None:
                arg = jnp.where(valid, arg, -jnp.inf)
            acc_ref[h, r0:, :] += _dot(jnp.exp(arg).astype(BF16), v)
            surv = survs[h][r0:] + r[:, 0:1]
            out.append(surv if r0 == 0 else jnp.concatenate([survs[h][:r0], surv], axis=0))
        return tuple(out)

    survs = tuple(jnp.zeros((TQ, 1), F32) for _ in range(n_heads))
    n_diag = TQ // TK
    for d in reversed(range(n_diag)):
        survs = block(qi * n_diag + d, survs, d)
    def full_blocks(i, c):
        for d in range(n_diag):
            c = block((qi - i) * n_diag - 1 - d, c, None)
        return c

    lax.fori_loop(0, qi, full_blocks, survs)
    for p in range(PAIRS_PER_STEP):
        o_ref[0, :, pl.ds(p * LANES, LANES)] = jnp.where(first, acc_ref[2 * p], acc_ref[2 * p + 1]).astype(BF16)


def _attention(q, k, v, umat):
    B, S, _ = q.shape
    width = PAIRS_PER_STEP * LANES
    kv_spec = pl.BlockSpec((1, S, width), lambda b, p, i: (b, 0, p))
    q_spec = pl.BlockSpec((1, TQ, width), lambda b, p, i: (b, i, p))
    return pl.pallas_call(
        _attn_kernel,
        grid=(B, D_ATTN // width, S // TQ),
        in_specs=[q_spec, kv_spec, kv_spec, pl.BlockSpec((TK, TK), lambda b, p, i: (0, 0))],
        out_specs=q_spec,
        out_shape=jax.ShapeDtypeStruct((B, S, D_ATTN), BF16),
        scratch_shapes=[pltpu.VMEM((2 * PAIRS_PER_STEP, TQ, LANES), F32)],
        compiler_params=pltpu.CompilerParams(
            dimension_semantics=("arbitrary", "arbitrary", "arbitrary"),
            vmem_limit_bytes=VMEM_LIMIT),
        name="stickbreak_attn",
    )(q, k, v, umat)


def _route_t(scores, biased):
    ts = scores.shape[-1]
    neg = -jnp.inf
    b3 = biased.reshape(N_GROUPS, GROUP_SIZE, ts)
    e_in_g = lax.broadcasted_iota(jnp.int32, b3.shape, 1)
    m1 = jnp.max(b3, axis=1, keepdims=True)
    i1 = jnp.min(jnp.where(b3 == m1, e_in_g, GROUP_SIZE), axis=1, keepdims=True)
    m2 = jnp.max(jnp.where(e_in_g == i1, neg, b3), axis=1, keepdims=True)
    gs = (m1 + m2)[:, 0, :]
    g_iota = lax.broadcasted_iota(jnp.int32, gs.shape, 0)
    g_sel = jnp.zeros(gs.shape, jnp.bool_)
    for _ in range(TOPK_GROUPS):
        gm = jnp.max(gs, axis=0, keepdims=True)
        gi = jnp.min(jnp.where(gs == gm, g_iota, N_GROUPS), axis=0, keepdims=True)
        pick = g_iota == gi
        g_sel = jnp.logical_or(g_sel, pick)
        gs = jnp.where(pick, neg, gs)
    masked = jnp.where(g_sel[:, None, :], b3, neg)
    flat = lax.broadcasted_iota(jnp.int32, b3.shape, 0) * GROUP_SIZE + e_in_g
    sel = jnp.zeros(b3.shape, jnp.bool_)
    for _ in range(TOP_K):
        m = jnp.max(jnp.max(masked, axis=1, keepdims=True), axis=0, keepdims=True)
        cand = jnp.where(masked == m, flat, N_EXPERTS)
        idx = jnp.min(jnp.min(cand, axis=1, keepdims=True), axis=0, keepdims=True)
        pick = flat == idx
        sel = jnp.logical_or(sel, pick)
        masked = jnp.where(pick, neg, masked)
    s3 = scores.reshape(N_GROUPS, GROUP_SIZE, ts)
    w = jnp.where(sel, s3, 0.0)
    tot = jnp.sum(jnp.sum(w, axis=1, keepdims=True), axis=0, keepdims=True)
    return (w / tot * ROUTED_SCALE).reshape(N_EXPERTS, ts), sel.reshape(N_EXPERTS, ts)


def _pack_bf16_pair(lo, hi):
    lo_bits = pltpu.bitcast(lo.astype(BF16).astype(F32), jnp.uint32) >> 16
    hi_bits = pltpu.bitcast(hi.astype(BF16).astype(F32), jnp.uint32) & jnp.uint32(0xFFFF0000)
    return lo_bits | hi_bits


def _unpack_bf16_pair(p):
    lo = pltpu.bitcast(p << 16, F32).astype(BF16)
    hi = pltpu.bitcast(p & jnp.uint32(0xFFFF0000), F32).astype(BF16)
    return lo, hi


def _outproj_kernel(x_ref, a_ref, p_ref, mod_ref, wo32_ref, g2_ref, wrt_ref, rb_ref, ut_ref,
                    x1_ref, hp_ref, rank_ref, wd_ref, cnt_ref, run_ref, wo_ref):
    @pl.when(_first_grid_step())
    def _():
        run_ref[...] = jnp.zeros(run_ref.shape, jnp.int32)
        wo_ref[...] = wo32_ref[...].astype(BF16)

    gate1 = mod_ref[0, 2:3, :]
    shift2 = mod_ref[0, 3:4, :]
    scale2 = mod_ref[0, 4:5, :]
    mixp = _dot(a_ref[0], wo_ref[0:D_ATTN, :]) + _dot(p_ref[0], wo_ref[D_ATTN:, :])
    x1 = x_ref[0] + gate1 * mixp
    x1_ref[0] = x1
    h2 = _rms_mod(x1, g2_ref[...], scale2, shift2)
    half = h2.shape[-1] // 2
    hp_ref[0] = _pack_bf16_pair(h2[:, :half], h2[:, half:])
    logits_t = _dot3_nt(wrt_ref[...], h2)
    scores = 1.0 / (1.0 + jnp.exp(-logits_t))
    wd_t, sel = _route_t(scores, scores + rb_ref[...])
    wd_ref[...] = wd_t
    self_f = jnp.where(sel, 1.0, 0.0)
    before = _dot(self_f.astype(BF16), ut_ref[...]).astype(jnp.int32)
    run = run_ref[:, 0:1]
    rank_ref[...] = jnp.where(sel, run + before, -1)
    run_ref[...] = run_ref[...] + jnp.sum(self_f, axis=1, keepdims=True).astype(jnp.int32)
    cnt_ref[...] = run_ref[...]


def _outproj(x, attn, pool, mod3, w_out, g2, wr_t, rbias, ut, b0, B):
    _, S, D = x.shape
    ts = TS_PROJ
    n_s = S // ts
    const2 = lambda b, s: (0, 0)
    tok_in = lambda w: pl.BlockSpec((1, ts, w), lambda b, s: (b + b0, s, 0))
    tok = lambda w: pl.BlockSpec((1, ts, w), lambda b, s: (b, s, 0))
    tok_t = pl.BlockSpec((N_EXPERTS, ts), lambda b, s: (0, b * n_s + s))
    return pl.pallas_call(
        _outproj_kernel,
        grid=(B, n_s),
        in_specs=[tok_in(D), tok_in(D_ATTN), tok_in(D_POOL),
                  pl.BlockSpec((1, N_MOD, D), lambda b, s: (b + b0, 0, 0)),
                  pl.BlockSpec(w_out.shape, const2),
                  pl.BlockSpec((1, D), const2),
                  pl.BlockSpec(wr_t.shape, const2),
                  pl.BlockSpec((N_EXPERTS, 1), const2),
                  pl.BlockSpec((ts, ts), const2)],
        out_specs=[tok(D), tok(D // 2), tok_t, tok_t, pl.BlockSpec((N_EXPERTS, LANES), const2)],
        out_shape=[jax.ShapeDtypeStruct((B, S, D), F32),
                   jax.ShapeDtypeStruct((B, S, D // 2), jnp.uint32),
                   jax.ShapeDtypeStruct((N_EXPERTS, B * S), jnp.int32),
                   jax.ShapeDtypeStruct((N_EXPERTS, B * S), F32),
                   jax.ShapeDtypeStruct((N_EXPERTS, LANES), jnp.int32)],
        scratch_shapes=[pltpu.VMEM((N_EXPERTS, LANES), jnp.int32), pltpu.VMEM(w_out.shape, BF16)],
        compiler_params=pltpu.CompilerParams(dimension_semantics=("arbitrary", "arbitrary"),
                                             vmem_limit_bytes=VMEM_LIMIT),
        name="outproj_router",
    )(x, attn, pool, mod3, w_out, g2, wr_t, rbias, ut)


def _n_row_blocks(n_tokens):
    n_blocks = -(-(n_tokens * TOP_K + N_EXPERTS * (ROW_BLOCK - 1)) // ROW_BLOCK)
    return -(-n_blocks // BLOCKS_PER_STEP) * BLOCKS_PER_STEP


def _slots_kernel(rank_ref, wd_ref, cnt_ref, lt_ref, slot_ref, wk_ref, be_ref, nu_ref):
    lt = lt_ref[...]
    nblk = lax.shift_right_logical(cnt_ref[...] + (ROW_BLOCK - 1), ROW_BLOCK_LOG2).astype(F32)
    nb_hi, nb_lo = _split_bf16(nblk)
    blk_start = _dot(lt, nb_hi) + _dot(lt, nb_lo)

    @pl.when(pl.program_id(0) == 0)
    def _():
        blk_end = (blk_start + nblk)[:, 0:1]
        b_iota = lax.broadcasted_iota(jnp.int32, (N_EXPERTS, be_ref.shape[-1]), 1).astype(F32)
        owner = jnp.sum(jnp.where(blk_end <= b_iota, 1, 0), axis=0, keepdims=True)
        be_ref[...] = jnp.minimum(owner, N_EXPERTS - 1)
        nu_ref[...] = jnp.broadcast_to(blk_end[N_EXPERTS - 1:, :].astype(jnp.int32), nu_ref.shape)

    rank = rank_ref[...]
    sel = rank >= 0
    row_start = (blk_start[:, 0:1] * ROW_BLOCK).astype(jnp.int32)
    slot_d = row_start + rank
    wd = wd_ref[...]
    choice = _dot(lt, jnp.where(sel, 1.0, 0.0).astype(BF16)).astype(jnp.int32)
    ts = rank.shape[-1]
    slots, wks = [], []
    for k in range(TOP_K):
        m = jnp.logical_and(sel, choice == k)
        slots.append(jnp.sum(jnp.where(m, slot_d, 0), axis=0, keepdims=True))
        wks.append(jnp.sum(jnp.where(m, wd, 0.0), axis=0, keepdims=True))
    slot_ref[...] = jnp.concatenate(slots + [jnp.zeros((SUBLANES - TOP_K, ts), jnp.int32)], axis=0)
    wk_pad = jnp.concatenate(wks + [jnp.zeros((LANES - TOP_K, ts), F32)], axis=0)
    wk_ref[...] = wk_pad.T


def _slots(rank_t, wd_t, cnt, lt):
    T = rank_t.shape[1]
    ts = TS_SLOT
    nb_pad = -(-_n_row_blocks(T) // LANES) * LANES
    const = lambda i: (0, 0)
    tok_t = pl.BlockSpec((N_EXPERTS, ts), lambda i: (0, i))
    return pl.pallas_call(
        _slots_kernel,
        grid=(T // ts,),
        in_specs=[tok_t, tok_t, pl.BlockSpec((N_EXPERTS, LANES), const), pl.BlockSpec((N_EXPERTS, N_EXPERTS), const)],
        out_specs=[pl.BlockSpec((SUBLANES, ts), lambda i: (0, i)),
                   pl.BlockSpec((ts, LANES), lambda i: (i, 0)),
                   pl.BlockSpec((1, nb_pad), const),
                   pl.BlockSpec((1, LANES), const)],
        out_shape=[jax.ShapeDtypeStruct((SUBLANES, T), jnp.int32),
                   jax.ShapeDtypeStruct((T, LANES), F32),
                   jax.ShapeDtypeStruct((1, nb_pad), jnp.int32),
                   jax.ShapeDtypeStruct((1, LANES), jnp.int32)],
        compiler_params=pltpu.CompilerParams(dimension_semantics=("arbitrary",), vmem_limit_bytes=VMEM_LIMIT),
        name="slots",
    )(rank_t, wd_t, cnt, lt)


def _sc_mesh():
    return plsc.VectorSubcoreMesh(core_axis_name="c", subcore_axis_name="s",
                                  num_cores=SC_CORES, num_subcores=SC_SUBCORES)


def _sc_dispatch(rows, slot_flat, n_out):
    T, width = rows.shape
    n_workers = SC_CORES * SC_SUBCORES
    per_worker = T // n_workers
    steps = per_worker // SC_CHUNK

    @functools.partial(
        pl.kernel, mesh=_sc_mesh(),
        out_type=jax.ShapeDtypeStruct((n_out, width), rows.dtype),
        scratch_types=[pltpu.VMEM((SC_CHUNK, width), rows.dtype)]
        + [pltpu.VMEM((SC_CHUNK,), jnp.int32)] * TOP_K + [pltpu.SemaphoreType.DMA],
        name="dispatch",
    )
    def run(rows_hbm, slot_hbm, out_hbm, rows_v, *rest):
        idx_v, sem = rest[:TOP_K], rest[TOP_K]
        base = (lax.axis_index("s") * SC_CORES + lax.axis_index("c")) * per_worker

        @pl.loop(0, steps)
        def _(i):
            off = base + i * SC_CHUNK
            pltpu.sync_copy(rows_hbm.at[pl.ds(off, SC_CHUNK)], rows_v)
            for k in range(TOP_K):
                pltpu.sync_copy(slot_hbm.at[pl.ds(k * T + off, SC_CHUNK)], idx_v[k])
            copies = [pltpu.async_copy(rows_v, out_hbm.at[idx_v[k]], sem) for k in range(TOP_K)]
            for cp in copies:
                cp.wait()

    return run(rows, slot_flat)


def _sc_gather(rows, idx):
    n = idx.shape[0]
    width = rows.shape[1]
    n_workers = SC_CORES * SC_SUBCORES
    per_worker = n // n_workers
    steps = per_worker // SC_CHUNK

    assert steps % 2 == 0
    slot_types = [pltpu.VMEM((SC_CHUNK,), jnp.int32), pltpu.VMEM((SC_CHUNK, width), rows.dtype),
                  pltpu.SemaphoreType.DMA]

    @functools.partial(
        pl.kernel, mesh=_sc_mesh(),
        out_type=jax.ShapeDtypeStruct((n, width), rows.dtype),
        scratch_types=slot_types * 2,
        name="combine",
    )
    def run(rows_hbm, idx_hbm, out_hbm, *scratch):
        base = (lax.axis_index("s") * SC_CORES + lax.axis_index("c")) * per_worker
        slots = (scratch[0:3], scratch[3:6])

        def gather(slot):
            idx_v, rows_v, sem = slots[slot]
            return pltpu.make_async_copy(rows_hbm.at[idx_v], rows_v, sem)

        def start(chunk, slot):
            pltpu.sync_copy(idx_hbm.at[pl.ds(base + chunk * SC_CHUNK, SC_CHUNK)], slots[slot][0])
            gather(slot).start()

        def finish(chunk, slot):
            gather(slot).wait()
            pltpu.sync_copy(slots[slot][1], out_hbm.at[pl.ds(base + chunk * SC_CHUNK, SC_CHUNK)])

        start(0, 0)

        @pl.loop(0, steps, step=2)
        def _(chunk):
            start(chunk + 1, 1)
            finish(chunk, 0)

            @pl.when(chunk + 2 < steps)
            def _():
                start(chunk + 2, 0)

            finish(chunk + 1, 1)

    return run(rows, idx)


def _swiglu_packed(xp, wgu, wd):
    lo, hi = _unpack_bf16_pair(xp)
    half = xp.shape[-1]
    gu = _dot(lo, wgu[:half, :]) + _dot(hi, wgu[half:, :])
    dh = wgu.shape[-1] // 2
    act = _silu(gu[:, :dh]) * gu[:, dh:]
    return _dot(act.astype(BF16), wd)


def _cast_swiglu_weights(wg32_ref, wu32_ref, wd32_ref, wgu_ref, wd_ref):
    dh = wg32_ref.shape[-1]
    wgu_ref[:, :dh] = wg32_ref[...].reshape(wg32_ref.shape[-2:]).astype(BF16)
    wgu_ref[:, dh:] = wu32_ref[...].reshape(wu32_ref.shape[-2:]).astype(BF16)
    wd_ref[...] = wd32_ref[...].reshape(wd32_ref.shape[-2:]).astype(BF16)


def _experts_kernel(be_ref, nu_ref, x_ref, *refs):
    n = BLOCKS_PER_STEP
    w32 = [refs[3 * j:3 * j + 3] for j in range(n)]
    y_ref = refs[3 * n]
    wbf = [refs[3 * n + 1 + 2 * j:3 * n + 3 + 2 * j] for j in range(n)]
    step = pl.program_id(0)
    for j in range(n):
        blk = step * n + j
        new_expert = jnp.logical_or(step == 0, be_ref[blk] != be_ref[jnp.maximum(blk - n, 0)])

        @pl.when(new_expert)
        def _():
            _cast_swiglu_weights(*w32[j], *wbf[j])

    @pl.when(step * n < nu_ref[0])
    def _():
        for j in range(n):
            rows = pl.ds(j * ROW_BLOCK, ROW_BLOCK)
            y = _swiglu_packed(x_ref[rows, :], wbf[j][0][...], wbf[j][1][...])
            half = y.shape[-1] // 2
            y_ref[rows, :] = _pack_bf16_pair(y[:, :half], y[:, half:])


def _experts(xs, block_expert, n_used, w_gate, w_up, w_down):
    P, half = xs.shape
    _, D, dh = w_gate.shape
    n = BLOCKS_PER_STEP
    rows = n * ROW_BLOCK
    assert P % rows == 0

    def by_expert(shape, j):
        return pl.BlockSpec((1,) + shape, lambda s, be, nu: (be[s * n + j], 0, 0))

    w_specs, w_args = [], []
    for j in range(n):
        w_specs += [by_expert((D, dh), j), by_expert((D, dh), j), by_expert((dh, D), j)]
        w_args += [w_gate, w_up, w_down]
    def row_step(s, be, nu):
        return jnp.minimum(s, lax.div(jnp.maximum(nu[0], 1) - 1, n)), 0

    grid_spec = pltpu.PrefetchScalarGridSpec(
        num_scalar_prefetch=2,
        grid=(P // rows,),
        in_specs=[pl.BlockSpec((rows, half), row_step)] + w_specs,
        out_specs=pl.BlockSpec((rows, half), row_step),
        scratch_shapes=[pltpu.VMEM((D, 2 * dh), BF16), pltpu.VMEM((dh, D), BF16)] * n,
    )
    return pl.pallas_call(
        _experts_kernel,
        grid_spec=grid_spec,
        out_shape=jax.ShapeDtypeStruct((P, half), jnp.uint32),
        compiler_params=pltpu.CompilerParams(dimension_semantics=("arbitrary",), vmem_limit_bytes=VMEM_LIMIT),
        name="experts",
    )(block_expert, n_used, xs, *w_args)


def _final_kernel(x1_ref, hp_ref, g_ref, wk_ref, mod_ref, sg32_ref, su32_ref, sd32_ref, *rest):
    o_ref, sgu_ref, sd_ref = rest[-3:]

    @pl.when(_first_grid_step())
    def _():
        _cast_swiglu_weights(sg32_ref, su32_ref, sd32_ref, sgu_ref, sd_ref)

    acc = _swiglu_packed(hp_ref[0], sgu_ref[...], sd_ref[...])
    wk = wk_ref[0]
    for k in range(TOP_K):
        lo, hi = _unpack_bf16_pair(g_ref[k, 0])
        y = jnp.concatenate([lo.astype(F32), hi.astype(F32)], axis=-1)
        acc = acc + wk[:, k:k + 1] * y
    gate2 = mod_ref[0, 5:6, :]
    o_ref[0] = x1_ref[0] + gate2 * acc


def _final(x1, hp, g, wk_tok, mod3, ws_gate, ws_up, ws_down, prev_out, b0, b_total):
    B, S, D = x1.shape
    dh = ws_gate.shape[-1]
    ts = TS_PROJ
    tok = lambda w: pl.BlockSpec((1, ts, w), lambda b, s: (b, s, 0))
    const2 = lambda b, s: (0, 0)
    in_specs = [tok(D), tok(D // 2),
                pl.BlockSpec((TOP_K, 1, ts, D // 2), lambda b, s: (0, b, s, 0)),
                tok(LANES),
                pl.BlockSpec((1, N_MOD, D), lambda b, s: (b + b0, 0, 0)),
                pl.BlockSpec(ws_gate.shape, const2),
                pl.BlockSpec(ws_up.shape, const2),
                pl.BlockSpec(ws_down.shape, const2)]
    args = [x1, hp, g, wk_tok, mod3, ws_gate, ws_up, ws_down]
    aliases = {}
    if prev_out is not None:
        in_specs.append(pl.BlockSpec(memory_space=pl.ANY))
        args.append(prev_out)
        aliases = {len(args) - 1: 0}
    return pl.pallas_call(
        _final_kernel,
        grid=(B, S // ts),
        in_specs=in_specs,
        out_specs=pl.BlockSpec((1, ts, D), lambda b, s: (b + b0, s, 0)),
        out_shape=jax.ShapeDtypeStruct((b_total, S, D), F32),
        scratch_shapes=[pltpu.VMEM((D, 2 * dh), BF16), pltpu.VMEM((dh, D), BF16)],
        input_output_aliases=aliases,
        compiler_params=pltpu.CompilerParams(dimension_semantics=("arbitrary", "arbitrary"),
                                             vmem_limit_bytes=VMEM_LIMIT),
        name="final",
    )(*args)


def _layer(x, c_act_mod, norm1_g, norm2_g, w_in, q_norm_g, k_norm_g, w_pool, pool_scale, w_out,
           w_router, router_bias, w_gate, w_up, w_down, ws_gate, ws_up, ws_down):
    B, S, D = x.shape
    mod3 = c_act_mod.reshape(B, N_MOD, D)
    head_of = jnp.arange(D_ATTN, dtype=jnp.int32) // HEAD_DIM
    head_mean = jnp.where(head_of[:, None] == head_of[None, :], 1.0 / HEAD_DIM, 0.0).astype(BF16)
    j = jnp.arange(TK, dtype=jnp.int32)
    umat = jnp.where(j[:, None] >= j[None, :], -1.0, 0.0).astype(BF16)

    q, k, v, pool = _inproj(
        x, mod3, norm1_g.reshape(1, D), w_in,
        jnp.tile(q_norm_g, N_HEADS).reshape(1, D_ATTN), jnp.tile(k_norm_g, N_HEADS).reshape(1, D_ATTN),
        head_mean, w_pool, pool_scale.reshape(1, D_POOL))
    attn = _attention(q, k, v, umat)
    t = jnp.arange(TS_PROJ, dtype=jnp.int32)
    ut = (t[:, None] < t[None, :]).astype(BF16)
    e = jnp.arange(N_EXPERTS, dtype=jnp.int32)
    lt = (e[None, :] < e[:, None]).astype(BF16)
    bp = B // MOE_PARTS
    T = bp * S
    n_blocks = _n_row_blocks(T)
    out = None
    for part in range(MOE_PARTS):
        b0 = part * bp
        x1, hp, rank_t, wd_t, cnt = _outproj(x, attn, pool, mod3, w_out, norm2_g.reshape(1, D),
                                             w_router.T, router_bias.reshape(N_EXPERTS, 1), ut, b0, bp)
        slots, wk_tok, block_expert, n_used = _slots(rank_t, wd_t, cnt, lt)
        slot_flat = slots[:TOP_K].reshape(TOP_K * T)
        xs = _sc_dispatch(hp.reshape(T, D // 2), slot_flat, n_blocks * ROW_BLOCK)
        ys = _experts(xs, block_expert[0, :n_blocks], n_used[0, :1], w_gate, w_up, w_down)
        g = _sc_gather(ys, slot_flat).reshape(TOP_K, bp, S, D // 2)
        out = _final(x1, hp, g, wk_tok.reshape(bp, S, LANES), mod3, ws_gate, ws_up, ws_down, out, b0, B)
    return out


def kernel(x, c, w_ada, b_ada, norm1_g, norm2_g, w_in, q_norm_g, k_norm_g, w_pool, pool_scale, w_out,
           w_router, router_bias, w_gate, w_up, w_down, ws_gate, ws_up, ws_down):
    depth = w_ada.shape[0]
    for l in range(depth):
        mod = _adaln(c, w_ada[l], b_ada[l])
        x = _layer(x, mod, norm1_g[l], norm2_g[l], w_in[l], q_norm_g[l], k_norm_g[l], w_pool[l],
                   pool_scale[l], w_out[l], w_router[l], router_bias[l], w_gate[l], w_up[l], w_down[l],
                   ws_gate[l], ws_up[l], ws_down[l])
    return x
```

```python
import functools

import jax
import jax.numpy as jnp
from jax import lax
from jax.experimental import pallas as pl
from jax.experimental.pallas import tpu as pltpu
from jax.experimental.pallas import tpu_sc as plsc

F32 = jnp.float32
BF16 = jnp.bfloat16

HEAD_DIM = 64
N_HEADS = 8
D_ATTN = N_HEADS * HEAD_DIM
POOL_WINDOWS = (2, 4, 8, 16)
POOL_GROUP_DIM = 128
D_POOL = len(POOL_WINDOWS) * POOL_GROUP_DIM
MAX_WINDOW = max(POOL_WINDOWS)
assert all(w & (w - 1) == 0 for w in POOL_WINDOWS)
N_EXPERTS = 64
TOP_K = 6
N_GROUPS = 8
GROUP_SIZE = N_EXPERTS // N_GROUPS
TOPK_GROUPS = 4
ROUTED_SCALE = 2.5
RMS_EPS = 1e-6
N_MOD = 6

LANES = 128
SUBLANES = 8
VMEM_LIMIT = 56 * 1024 * 1024

TS_PROJ = 1024
TQ = 1024
TK = 256
PAIRS_PER_STEP = 4
TS_SLOT = 2048
ROW_BLOCK_LOG2 = 9
ROW_BLOCK = 1 << ROW_BLOCK_LOG2
BLOCKS_PER_STEP = 2
MOE_PARTS = 2
SC_CORES = 2
SC_SUBCORES = 16
SC_CHUNK = 64


def _split_bf16(a):
    hi = a.astype(BF16)
    lo = (a - hi.astype(F32)).astype(BF16)
    return hi, lo


def _dot(a, b):
    return jnp.dot(a, b, preferred_element_type=F32)


def _dot_nt(a, b):
    return lax.dot_general(a, b, (((1,), (1,)), ((), ())), preferred_element_type=F32)


def _dot3(a, b):
    ah, al = _split_bf16(a)
    bh, bl = _split_bf16(b)
    return _dot(ah, bh) + _dot(ah, bl) + _dot(al, bh)


def _dot3_nt(a, b):
    ah, al = _split_bf16(a)
    bh, bl = _split_bf16(b)
    return _dot_nt(ah, bh) + _dot_nt(ah, bl) + _dot_nt(al, bh)


def _silu(x):
    return x * (1.0 / (1.0 + jnp.exp(-x)))


def _rms_mod(x, gain, scale, shift):
    ms = jnp.mean(x * x, axis=-1, keepdims=True)
    y = x * lax.rsqrt(ms + RMS_EPS) * gain
    return y * (1.0 + scale) + shift


def _adaln_kernel(c_ref, w_ref, b_ref, o_ref):
    c = c_ref[...]
    o_ref[...] = _dot3(_silu(c), w_ref[...]) + b_ref[...]


def _adaln(c, w_ada, b_ada):
    nb, D = c.shape
    B = -(-nb // SUBLANES) * SUBLANES
    c = jnp.pad(c, ((0, B - nb), (0, 0)))
    N = w_ada.shape[1]
    tn = 1024
    out = pl.pallas_call(
        _adaln_kernel,
        grid=(N // tn,),
        in_specs=[pl.BlockSpec((B, D), lambda j: (0, 0)),
                  pl.BlockSpec((D, tn), lambda j: (0, j)),
                  pl.BlockSpec((1, tn), lambda j: (0, j))],
        out_specs=pl.BlockSpec((B, tn), lambda j: (0, j)),
        out_shape=jax.ShapeDtypeStruct((B, N), F32),
        compiler_params=pltpu.CompilerParams(dimension_semantics=("arbitrary",),
                                             vmem_limit_bytes=VMEM_LIMIT),
        name="adaln",
    )(c, w_ada, b_ada.reshape(1, N))
    return out[:nb]


def _first_grid_step():
    return jnp.logical_and(pl.program_id(0) == 0, pl.program_id(1) == 0)


def _inproj_kernel(x_ref, mod_ref, g1_ref, win32_ref, qg_ref, kg_ref, hm_ref, wp32_ref, ps_ref,
                   q_ref, k_ref, v_ref, p_ref, ext_ref, win_ref, wp_ref):
    @pl.when(_first_grid_step())
    def _():
        win_ref[...] = win32_ref[...].astype(BF16)
        wp_ref[...] = wp32_ref[...].astype(BF16)

    si = pl.program_id(1)
    ts = x_ref.shape[1]
    x = x_ref[0]
    shift1 = mod_ref[0, 0:1, :]
    scale1 = mod_ref[0, 1:2, :]
    h = _rms_mod(x, g1_ref[...], scale1, shift1)
    proj = _dot(h.astype(BF16), win_ref[...])

    hm = hm_ref[...]

    def head_norm(t, gain):
        ms = _dot((t * t).astype(BF16), hm)
        return t * lax.rsqrt(ms + RMS_EPS) * gain

    hq = proj[:, 0:D_ATTN]
    hk = proj[:, D_ATTN:2 * D_ATTN]
    q_ref[0] = (head_norm(hq, qg_ref[...]) * (HEAD_DIM ** -0.5)).astype(BF16)
    k_ref[0] = head_norm(hk, kg_ref[...]).astype(BF16)
    v_ref[0] = proj[:, 2 * D_ATTN:3 * D_ATTN].astype(BF16)

    hp = proj[:, 3 * D_ATTN:]

    @pl.when(si == 0)
    def _():
        ext_ref[0:MAX_WINDOW, :] = jnp.zeros((MAX_WINDOW, D_POOL), F32)

    ext_ref[MAX_WINDOW:, :] = hp
    pos = si * ts + lax.broadcasted_iota(jnp.int32, (ts, 1), 0)
    for g, w in enumerate(POOL_WINDOWS):
        lo_l, hi_l = g * POOL_GROUP_DIM, (g + 1) * POOL_GROUP_DIM
        u = hp[:, lo_l:hi_l]
        run = ext_ref[:, lo_l:hi_l]
        span = 1
        while span < w:
            run = run + pltpu.roll(run, span, axis=0)
            span *= 2
        acc = run[MAX_WINDOW:]
        count = jnp.minimum(pos + 1, w).astype(F32)
        d = acc / count - u
        mixed = _dot(d.astype(BF16), wp_ref[g])
        p_ref[0, :, lo_l:hi_l] = (mixed * ps_ref[:, lo_l:hi_l]).astype(BF16)
    ext_ref[0:MAX_WINDOW, :] = hp[ts - MAX_WINDOW:, :]


def _inproj(x, mod3, g1, w_in, qg_t, kg_t, head_mean, w_pool, pool_scale):
    B, S, D = x.shape
    ts = TS_PROJ
    out_sd = jax.ShapeDtypeStruct((B, S, D_ATTN), BF16)
    blk = pl.BlockSpec((1, ts, D_ATTN), lambda b, s: (b, s, 0))
    const2 = lambda b, s: (0, 0)
    return pl.pallas_call(
        _inproj_kernel,
        grid=(B, S // ts),
        in_specs=[pl.BlockSpec((1, ts, D), lambda b, s: (b, s, 0)),
                  pl.BlockSpec((1, N_MOD, D), lambda b, s: (b, 0, 0)),
                  pl.BlockSpec((1, D), const2),
                  pl.BlockSpec(w_in.shape, const2),
                  pl.BlockSpec((1, D_ATTN), const2),
                  pl.BlockSpec((1, D_ATTN), const2),
                  pl.BlockSpec((D_ATTN, D_ATTN), const2),
                  pl.BlockSpec(w_pool.shape, lambda b, s: (0, 0, 0)),
                  pl.BlockSpec((1, D_POOL), const2)],
        out_specs=[blk, blk, blk, blk],
        out_shape=[out_sd, out_sd, out_sd, out_sd],
        scratch_shapes=[pltpu.VMEM((MAX_WINDOW + ts, D_POOL), F32),
                        pltpu.VMEM(w_in.shape, BF16), pltpu.VMEM(w_pool.shape, BF16)],
        compiler_params=pltpu.CompilerParams(dimension_semantics=("arbitrary", "arbitrary"),
                                             vmem_limit_bytes=VMEM_LIMIT),
        name="inproj",
    )(x, mod3, g1, w_in, qg_t, kg_t, head_mean, w_pool, pool_scale)


def _attn_kernel(q_ref, k_ref, v_ref, u_ref, o_ref, acc_ref):
    qi = pl.program_id(2)
    lane = lax.broadcasted_iota(jnp.int32, (TQ, LANES), 1)
    first = lane < HEAD_DIM
    n_heads = 2 * PAIRS_PER_STEP
    pair_lanes = lambda h: pl.ds((h // 2) * LANES, LANES)
    qh = []
    for h in range(n_heads):
        q = q_ref[0, :, pair_lanes(h)]
        keep = first if h % 2 == 0 else jnp.logical_not(first)
        qh.append(jnp.where(keep, q, jnp.zeros_like(q)))
    u = u_ref[...]
    row = lax.broadcasted_iota(jnp.int32, (TQ, TK), 0)
    col = lax.broadcasted_iota(jnp.int32, (TQ, TK), 1)
    acc_ref[...] = jnp.zeros(acc_ref.shape, F32)

    def block(kb, survs, diag):
        start = pl.multiple_of(kb * TK, TK)
        r0 = 0 if diag is None else diag * TK
        if diag is not None:
            valid = (col + r0 < row)[r0:]
        out = []
        for h in range(n_heads):
            k = k_ref[0, pl.ds(start, TK), pair_lanes(h)]
            v = v_ref[0, pl.ds(start, TK), pair_lanes(h)]
            z = _dot_nt(qh[h][r0:], k)
            zb = z.astype(BF16)
            sp = jnp.maximum(zb, 0) + jnp.log(1 + jnp.exp(-jnp.abs(zb)))
            if diag is not None:
                sp = jnp.where(valid, sp, jnp.zeros_like(sp))
            r = _dot(sp, u)
            arg = z + r + survs[h][r0:]
            if diag is not None:
                arg = jnp.where(valid, arg, -jnp.inf)
            acc_ref[h, r0:, :] += _dot(jnp.exp(arg).astype(BF16), v)
            surv = survs[h][r0:] + r[:, 0:1]
            out.append(surv if r0 == 0 else jnp.concatenate([survs[h][:r0], surv], axis=0))
        return tuple(out)

    survs = tuple(jnp.zeros((TQ, 1), F32) for _ in range(n_heads))
    n_diag = TQ // TK
    for d in reversed(range(n_diag)):
        survs = block(qi * n_diag + d, survs, d)
    def full_blocks(i, c):
        for d in range(n_diag):
            c = block((qi - i) * n_diag - 1 - d, c, None)
        return c

    lax.fori_loop(0, qi, full_blocks, survs)
    for p in range(PAIRS_PER_STEP):
        o_ref[0, :, pl.ds(p * LANES, LANES)] = jnp.where(first, acc_ref[2 * p], acc_ref[2 * p + 1]).astype(BF16)


def _attention(q, k, v, umat):
    B, S, _ = q.shape
    width = PAIRS_PER_STEP * LANES
    kv_spec = pl.BlockSpec((1, S, width), lambda b, p, i: (b, 0, p))
    q_spec = pl.BlockSpec((1, TQ, width), lambda b, p, i: (b, i, p))
    return pl.pallas_call(
        _attn_kernel,
        grid=(B, D_ATTN // width, S // TQ),
        in_specs=[q_spec, kv_spec, kv_spec, pl.BlockSpec((TK, TK), lambda b, p, i: (0, 0))],
        out_specs=q_spec,
        out_shape=jax.ShapeDtypeStruct((B, S, D_ATTN), BF16),
        scratch_shapes=[pltpu.VMEM((2 * PAIRS_PER_STEP, TQ, LANES), F32)],
        compiler_params=pltpu.CompilerParams(
            dimension_semantics=("arbitrary", "arbitrary", "arbitrary"),
            vmem_limit_bytes=VMEM_LIMIT),
        name="stickbreak_attn",
    )(q, k, v, umat)


def _route_t(scores, biased):
    ts = scores.shape[-1]
    neg = -jnp.inf
    b3 = biased.reshape(N_GROUPS, GROUP_SIZE, ts)
    e_in_g = lax.broadcasted_iota(jnp.int32, b3.shape, 1)
    m1 = jnp.max(b3, axis=1, keepdims=True)
    i1 = jnp.min(jnp.where(b3 == m1, e_in_g, GROUP_SIZE), axis=1, keepdims=True)
    m2 = jnp.max(jnp.where(e_in_g == i1, neg, b3), axis=1, keepdims=True)
    gs = (m1 + m2)[:, 0, :]
    g_iota = lax.broadcasted_iota(jnp.int32, gs.shape, 0)
    g_sel = jnp.zeros(gs.shape, jnp.bool_)
    for _ in range(TOPK_GROUPS):
        gm = jnp.max(gs, axis=0, keepdims=True)
        gi = jnp.min(jnp.where(gs == gm, g_iota, N_GROUPS), axis=0, keepdims=True)
        pick = g_iota == gi
        g_sel = jnp.logical_or(g_sel, pick)
        gs = jnp.where(pick, neg, gs)
    masked = jnp.where(g_sel[:, None, :], b3, neg)
    flat = lax.broadcasted_iota(jnp.int32, b3.shape, 0) * GROUP_SIZE + e_in_g
    sel = jnp.zeros(b3.shape, jnp.bool_)
    for _ in range(TOP_K):
        m = jnp.max(jnp.max(masked, axis=1, keepdims=True), axis=0, keepdims=True)
        cand = jnp.where(masked == m, flat, N_EXPERTS)
        idx = jnp.min(jnp.min(cand, axis=1, keepdims=True), axis=0, keepdims=True)
        pick = flat == idx
        sel = jnp.logical_or(sel, pick)
        masked = jnp.where(pick, neg, masked)
    s3 = scores.reshape(N_GROUPS, GROUP_SIZE, ts)
    w = jnp.where(sel, s3, 0.0)
    tot = jnp.sum(jnp.sum(w, axis=1, keepdims=True), axis=0, keepdims=True)
    return (w / tot * ROUTED_SCALE).reshape(N_EXPERTS, ts), sel.reshape(N_EXPERTS, ts)


def _pack_bf16_pair(lo, hi):
    lo_bits = pltpu.bitcast(lo.astype(BF16).astype(F32), jnp.uint32) >> 16
    hi_bits = pltpu.bitcast(hi.astype(BF16).astype(F32), jnp.uint32) & jnp.uint32(0xFFFF0000)
    return lo_bits | hi_bits


def _unpack_bf16_pair(p):
    lo = pltpu.bitcast(p << 16, F32).astype(BF16)
    hi = pltpu.bitcast(p & jnp.uint32(0xFFFF0000), F32).astype(BF16)
    return lo, hi


def _outproj_kernel(x_ref, a_ref, p_ref, mod_ref, wo32_ref, g2_ref, wrt_ref, rb_ref, ut_ref,
                    x1_ref, hp_ref, rank_ref, wd_ref, cnt_ref, run_ref, wo_ref):
    @pl.when(_first_grid_step())
    def _():
        run_ref[...] = jnp.zeros(run_ref.shape, jnp.int32)
        wo_ref[...] = wo32_ref[...].astype(BF16)

    gate1 = mod_ref[0, 2:3, :]
    shift2 = mod_ref[0, 3:4, :]
    scale2 = mod_ref[0, 4:5, :]
    mixp = _dot(a_ref[0], wo_ref[0:D_ATTN, :]) + _dot(p_ref[0], wo_ref[D_ATTN:, :])
    x1 = x_ref[0] + gate1 * mixp
    x1_ref[0] = x1
    h2 = _rms_mod(x1, g2_ref[...], scale2, shift2)
    half = h2.shape[-1] // 2
    hp_ref[0] = _pack_bf16_pair(h2[:, :half], h2[:, half:])
    logits_t = _dot3_nt(wrt_ref[...], h2)
    scores = 1.0 / (1.0 + jnp.exp(-logits_t))
    wd_t, sel = _route_t(scores, scores + rb_ref[...])
    wd_ref[...] = wd_t
    self_f = jnp.where(sel, 1.0, 0.0)
    before = _dot(self_f.astype(BF16), ut_ref[...]).astype(jnp.int32)
    run = run_ref[:, 0:1]
    rank_ref[...] = jnp.where(sel, run + before, -1)
    run_ref[...] = run_ref[...] + jnp.sum(self_f, axis=1, keepdims=True).astype(jnp.int32)
    cnt_ref[...] = run_ref[...]


def _outproj(x, attn, pool, mod3, w_out, g2, wr_t, rbias, ut, b0, B):
    _, S, D = x.shape
    ts = TS_PROJ
    n_s = S // ts
    const2 = lambda b, s: (0, 0)
    tok_in = lambda w: pl.BlockSpec((1, ts, w), lambda b, s: (b + b0, s, 0))
    tok = lambda w: pl.BlockSpec((1, ts, w), lambda b, s: (b, s, 0))
    tok_t = pl.BlockSpec((N_EXPERTS, ts), lambda b, s: (0, b * n_s + s))
    return pl.pallas_call(
        _outproj_kernel,
        grid=(B, n_s),
        in_specs=[tok_in(D), tok_in(D_ATTN), tok_in(D_POOL),
                  pl.BlockSpec((1, N_MOD, D), lambda b, s: (b + b0, 0, 0)),
                  pl.BlockSpec(w_out.shape, const2),
                  pl.BlockSpec((1, D), const2),
                  pl.BlockSpec(wr_t.shape, const2),
                  pl.BlockSpec((N_EXPERTS, 1), const2),
                  pl.BlockSpec((ts, ts), const2)],
        out_specs=[tok(D), tok(D // 2), tok_t, tok_t, pl.BlockSpec((N_EXPERTS, LANES), const2)],
        out_shape=[jax.ShapeDtypeStruct((B, S, D), F32),
                   jax.ShapeDtypeStruct((B, S, D // 2), jnp.uint32),
                   jax.ShapeDtypeStruct((N_EXPERTS, B * S), jnp.int32),
                   jax.ShapeDtypeStruct((N_EXPERTS, B * S), F32),
                   jax.ShapeDtypeStruct((N_EXPERTS, LANES), jnp.int32)],
        scratch_shapes=[pltpu.VMEM((N_EXPERTS, LANES), jnp.int32), pltpu.VMEM(w_out.shape, BF16)],
        compiler_params=pltpu.CompilerParams(dimension_semantics=("arbitrary", "arbitrary"),
                                             vmem_limit_bytes=VMEM_LIMIT),
        name="outproj_router",
    )(x, attn, pool, mod3, w_out, g2, wr_t, rbias, ut)


def _n_row_blocks(n_tokens):
    n_blocks = -(-(n_tokens * TOP_K + N_EXPERTS * (ROW_BLOCK - 1)) // ROW_BLOCK)
    return -(-n_blocks // BLOCKS_PER_STEP) * BLOCKS_PER_STEP


def _slots_kernel(rank_ref, wd_ref, cnt_ref, lt_ref, slot_ref, wk_ref, be_ref, nu_ref):
    lt = lt_ref[...]
    nblk = lax.shift_right_logical(cnt_ref[...] + (ROW_BLOCK - 1), ROW_BLOCK_LOG2).astype(F32)
    nb_hi, nb_lo = _split_bf16(nblk)
    blk_start = _dot(lt, nb_hi) + _dot(lt, nb_lo)

    @pl.when(pl.program_id(0) == 0)
    def _():
        blk_end = (blk_start + nblk)[:, 0:1]
        b_iota = lax.broadcasted_iota(jnp.int32, (N_EXPERTS, be_ref.shape[-1]), 1).astype(F32)
        owner = jnp.sum(jnp.where(blk_end <= b_iota, 1, 0), axis=0, keepdims=True)
        be_ref[...] = jnp.minimum(owner, N_EXPERTS - 1)
        nu_ref[...] = jnp.broadcast_to(blk_end[N_EXPERTS - 1:, :].astype(jnp.int32), nu_ref.shape)

    rank = rank_ref[...]
    sel = rank >= 0
    row_start = (blk_start[:, 0:1] * ROW_BLOCK).astype(jnp.int32)
    slot_d = row_start + rank
    wd = wd_ref[...]
    choice = _dot(lt, jnp.where(sel, 1.0, 0.0).astype(BF16)).astype(jnp.int32)
    ts = rank.shape[-1]
    slots, wks = [], []
    for k in range(TOP_K):
        m = jnp.logical_and(sel, choice == k)
        slots.append(jnp.sum(jnp.where(m, slot_d, 0), axis=0, keepdims=True))
        wks.append(jnp.sum(jnp.where(m, wd, 0.0), axis=0, keepdims=True))
    slot_ref[...] = jnp.concatenate(slots + [jnp.zeros((SUBLANES - TOP_K, ts), jnp.int32)], axis=0)
    wk_pad = jnp.concatenate(wks + [jnp.zeros((LANES - TOP_K, ts), F32)], axis=0)
    wk_ref[...] = wk_pad.T


def _slots(rank_t, wd_t, cnt, lt):
    T = rank_t.shape[1]
    ts = TS_SLOT
    nb_pad = -(-_n_row_blocks(T) // LANES) * LANES
    const = lambda i: (0, 0)
    tok_t = pl.BlockSpec((N_EXPERTS, ts), lambda i: (0, i))
    return pl.pallas_call(
        _slots_kernel,
        grid=(T // ts,),
        in_specs=[tok_t, tok_t, pl.BlockSpec((N_EXPERTS, LANES), const), pl.BlockSpec((N_EXPERTS, N_EXPERTS), const)],
        out_specs=[pl.BlockSpec((SUBLANES, ts), lambda i: (0, i)),
                   pl.BlockSpec((ts, LANES), lambda i: (i, 0)),
                   pl.BlockSpec((1, nb_pad), const),
                   pl.BlockSpec((1, LANES), const)],
        out_shape=[jax.ShapeDtypeStruct((SUBLANES, T), jnp.int32),
                   jax.ShapeDtypeStruct((T, LANES), F32),
                   jax.ShapeDtypeStruct((1, nb_pad), jnp.int32),
                   jax.ShapeDtypeStruct((1, LANES), jnp.int32)],
        compiler_params=pltpu.CompilerParams(dimension_semantics=("arbitrary",), vmem_limit_bytes=VMEM_LIMIT),
        name="slots",
    )(rank_t, wd_t, cnt, lt)


def _sc_mesh():
    return plsc.VectorSubcoreMesh(core_axis_name="c", subcore_axis_name="s",
                                  num_cores=SC_CORES, num_subcores=SC_SUBCORES)


def _sc_dispatch(rows, slot_flat, n_out):
    T, width = rows.shape
    n_workers = SC_CORES * SC_SUBCORES
    per_worker = T // n_workers
    steps = per_worker // SC_CHUNK

    @functools.partial(
        pl.kernel, mesh=_sc_mesh(),
        out_type=jax.ShapeDtypeStruct((n_out, width), rows.dtype),
        scratch_types=[pltpu.VMEM((SC_CHUNK, width), rows.dtype)]
        + [pltpu.VMEM((SC_CHUNK,), jnp.int32)] * TOP_K + [pltpu.SemaphoreType.DMA],
        name="dispatch",
    )
    def run(rows_hbm, slot_hbm, out_hbm, rows_v, *rest):
        idx_v, sem = rest[:TOP_K], rest[TOP_K]
        base = (lax.axis_index("s") * SC_CORES + lax.axis_index("c")) * per_worker

        @pl.loop(0, steps)
        def _(i):
            off = base + i * SC_CHUNK
            pltpu.sync_copy(rows_hbm.at[pl.ds(off, SC_CHUNK)], rows_v)
            for k in range(TOP_K):
                pltpu.sync_copy(slot_hbm.at[pl.ds(k * T + off, SC_CHUNK)], idx_v[k])
            copies = [pltpu.async_copy(rows_v, out_hbm.at[idx_v[k]], sem) for k in range(TOP_K)]
            for cp in copies:
                cp.wait()

    return run(rows, slot_flat)


def _sc_gather(rows, idx):
    n = idx.shape[0]
    width = rows.shape[1]
    n_workers = SC_CORES * SC_SUBCORES
    per_worker = n // n_workers
    steps = per_worker // SC_CHUNK

    assert steps % 2 == 0
    slot_types = [pltpu.VMEM((SC_CHUNK,), jnp.int32), pltpu.VMEM((SC_CHUNK, width), rows.dtype),
                  pltpu.SemaphoreType.DMA]

    @functools.partial(
        pl.kernel, mesh=_sc_mesh(),
        out_type=jax.ShapeDtypeStruct((n, width), rows.dtype),
        scratch_types=slot_types * 2,
        name="combine",
    )
    def run(rows_hbm, idx_hbm, out_hbm, *scratch):
        base = (lax.axis_index("s") * SC_CORES + lax.axis_index("c")) * per_worker
        slots = (scratch[0:3], scratch[3:6])

        def gather(slot):
            idx_v, rows_v, sem = slots[slot]
            return pltpu.make_async_copy(rows_hbm.at[idx_v], rows_v, sem)

        def start(chunk, slot):
            pltpu.sync_copy(idx_hbm.at[pl.ds(base + chunk * SC_CHUNK, SC_CHUNK)], slots[slot][0])
            gather(slot).start()

        def finish(chunk, slot):
            gather(slot).wait()
            pltpu.sync_copy(slots[slot][1], out_hbm.at[pl.ds(base + chunk * SC_CHUNK, SC_CHUNK)])

        start(0, 0)

        @pl.loop(0, steps, step=2)
        def _(chunk):
            start(chunk + 1, 1)
            finish(chunk, 0)

            @pl.when(chunk + 2 < steps)
            def _():
                start(chunk + 2, 0)

            finish(chunk + 1, 1)

    return run(rows, idx)


def _swiglu_packed(xp, wgu, wd):
    lo, hi = _unpack_bf16_pair(xp)
    half = xp.shape[-1]
    gu = _dot(lo, wgu[:half, :]) + _dot(hi, wgu[half:, :])
    dh = wgu.shape[-1] // 2
    act = _silu(gu[:, :dh]) * gu[:, dh:]
    return _dot(act.astype(BF16), wd)


def _cast_swiglu_weights(wg32_ref, wu32_ref, wd32_ref, wgu_ref, wd_ref):
    dh = wg32_ref.shape[-1]
    wgu_ref[:, :dh] = wg32_ref[...].reshape(wg32_ref.shape[-2:]).astype(BF16)
    wgu_ref[:, dh:] = wu32_ref[...].reshape(wu32_ref.shape[-2:]).astype(BF16)
    wd_ref[...] = wd32_ref[...].reshape(wd32_ref.shape[-2:]).astype(BF16)


def _experts_kernel(be_ref, nu_ref, x_ref, *refs):
    n = BLOCKS_PER_STEP
    w32 = [refs[3 * j:3 * j + 3] for j in range(n)]
    y_ref = refs[3 * n]
    wbf = [refs[3 * n + 1 + 2 * j:3 * n + 3 + 2 * j] for j in range(n)]
    step = pl.program_id(0)
    for j in range(n):
        blk = step * n + j
        new_expert = jnp.logical_or(step == 0, be_ref[blk] != be_ref[jnp.maximum(blk - n, 0)])

        @pl.when(new_expert)
        def _():
            _cast_swiglu_weights(*w32[j], *wbf[j])

    @pl.when(step * n < nu_ref[0])
    def _():
        for j in range(n):
            rows = pl.ds(j * ROW_BLOCK, ROW_BLOCK)
            y = _swiglu_packed(x_ref[rows, :], wbf[j][0][...], wbf[j][1][...])
            half = y.shape[-1] // 2
            y_ref[rows, :] = _pack_bf16_pair(y[:, :half], y[:, half:])


def _experts(xs, block_expert, n_used, w_gate, w_up, w_down):
    P, half = xs.shape
    _, D, dh = w_gate.shape
    n = BLOCKS_PER_STEP
    rows = n * ROW_BLOCK
    assert P % rows == 0

    def by_expert(shape, j):
        return pl.BlockSpec((1,) + shape, lambda s, be, nu: (be[s * n + j], 0, 0))

    w_specs, w_args = [], []
    for j in range(n):
        w_specs += [by_expert((D, dh), j), by_expert((D, dh), j), by_expert((dh, D), j)]
        w_args += [w_gate, w_up, w_down]
    def row_step(s, be, nu):
        return jnp.minimum(s, lax.div(jnp.maximum(nu[0], 1) - 1, n)), 0

    grid_spec = pltpu.PrefetchScalarGridSpec(
        num_scalar_prefetch=2,
        grid=(P // rows,),
        in_specs=[pl.BlockSpec((rows, half), row_step)] + w_specs,
        out_specs=pl.BlockSpec((rows, half), row_step),
        scratch_shapes=[pltpu.VMEM((D, 2 * dh), BF16), pltpu.VMEM((dh, D), BF16)] * n,
    )
    return pl.pallas_call(
        _experts_kernel,
        grid_spec=grid_spec,
        out_shape=jax.ShapeDtypeStruct((P, half), jnp.uint32),
        compiler_params=pltpu.CompilerParams(dimension_semantics=("arbitrary",), vmem_limit_bytes=VMEM_LIMIT),
        name="experts",
    )(block_expert, n_used, xs, *w_args)


def _final_kernel(x1_ref, hp_ref, g_ref, wk_ref, mod_ref, sg32_ref, su32_ref, sd32_ref, *rest):
    o_ref, sgu_ref, sd_ref = rest[-3:]

    @pl.when(_first_grid_step())
    def _():
        _cast_swiglu_weights(sg32_ref, su32_ref, sd32_ref, sgu_ref, sd_ref)

    acc = _swiglu_packed(hp_ref[0], sgu_ref[...], sd_ref[...])
    wk = wk_ref[0]
    for k in range(TOP_K):
        lo, hi = _unpack_bf16_pair(g_ref[k, 0])
        y = jnp.concatenate([lo.astype(F32), hi.astype(F32)], axis=-1)
        acc = acc + wk[:, k:k + 1] * y
    gate2 = mod_ref[0, 5:6, :]
    o_ref[0] = x1_ref[0] + gate2 * acc


def _final(x1, hp, g, wk_tok, mod3, ws_gate, ws_up, ws_down, prev_out, b0, b_total):
    B, S, D = x1.shape
    dh = ws_gate.shape[-1]
    ts = TS_PROJ
    tok = lambda w: pl.BlockSpec((1, ts, w), lambda b, s: (b, s, 0))
    const2 = lambda b, s: (0, 0)
    in_specs = [tok(D), tok(D // 2),
                pl.BlockSpec((TOP_K, 1, ts, D // 2), lambda b, s: (0, b, s, 0)),
                tok(LANES),
                pl.BlockSpec((1, N_MOD, D), lambda b, s: (b + b0, 0, 0)),
                pl.BlockSpec(ws_gate.shape, const2),
                pl.BlockSpec(ws_up.shape, const2),
                pl.BlockSpec(ws_down.shape, const2)]
    args = [x1, hp, g, wk_tok, mod3, ws_gate, ws_up, ws_down]
    aliases = {}
    if prev_out is not None:
        in_specs.append(pl.BlockSpec(memory_space=pl.ANY))
        args.append(prev_out)
        aliases = {len(args) - 1: 0}
    return pl.pallas_call(
        _final_kernel,
        grid=(B, S // ts),
        in_specs=in_specs,
        out_specs=pl.BlockSpec((1, ts, D), lambda b, s: (b + b0, s, 0)),
        out_shape=jax.ShapeDtypeStruct((b_total, S, D), F32),
        scratch_shapes=[pltpu.VMEM((D, 2 * dh), BF16), pltpu.VMEM((dh, D), BF16)],
        input_output_aliases=aliases,
        compiler_params=pltpu.CompilerParams(dimension_semantics=("arbitrary", "arbitrary"),
                                             vmem_limit_bytes=VMEM_LIMIT),
        name="final",
    )(*args)


def _layer(x, c_act_mod, norm1_g, norm2_g, w_in, q_norm_g, k_norm_g, w_pool, pool_scale, w_out,
           w_router, router_bias, w_gate, w_up, w_down, ws_gate, ws_up, ws_down):
    B, S, D = x.shape
    mod3 = c_act_mod.reshape(B, N_MOD, D)
    head_of = jnp.arange(D_ATTN, dtype=jnp.int32) // HEAD_DIM
    head_mean = jnp.where(head_of[:, None] == head_of[None, :], 1.0 / HEAD_DIM, 0.0).astype(BF16)
    j = jnp.arange(TK, dtype=jnp.int32)
    umat = jnp.where(j[:, None] >= j[None, :], -1.0, 0.0).astype(BF16)

    q, k, v, pool = _inproj(
        x, mod3, norm1_g.reshape(1, D), w_in,
        jnp.tile(q_norm_g, N_HEADS).reshape(1, D_ATTN), jnp.tile(k_norm_g, N_HEADS).reshape(1, D_ATTN),
        head_mean, w_pool, pool_scale.reshape(1, D_POOL))
    attn = _attention(q, k, v, umat)
    t = jnp.arange(TS_PROJ, dtype=jnp.int32)
    ut = (t[:, None] < t[None, :]).astype(BF16)
    e = jnp.arange(N_EXPERTS, dtype=jnp.int32)
    lt = (e[None, :] < e[:, None]).astype(BF16)
    bp = B // MOE_PARTS
    T = bp * S
    n_blocks = _n_row_blocks(T)
    out = None
    for part in range(MOE_PARTS):
        b0 = part * bp
        x1, hp, rank_t, wd_t, cnt = _outproj(x, attn, pool, mod3, w_out, norm2_g.reshape(1, D),
                                             w_router.T, router_bias.reshape(N_EXPERTS, 1), ut, b0, bp)
        slots, wk_tok, block_expert, n_used = _slots(rank_t, wd_t, cnt, lt)
        slot_flat = slots[:TOP_K].reshape(TOP_K * T)
        xs = _sc_dispatch(hp.reshape(T, D // 2), slot_flat, n_blocks * ROW_BLOCK)
        ys = _experts(xs, block_expert[0, :n_blocks], n_used[0, :1], w_gate, w_up, w_down)
        g = _sc_gather(ys, slot_flat).reshape(TOP_K, bp, S, D // 2)
        out = _final(x1, hp, g, wk_tok.reshape(bp, S, LANES), mod3, ws_gate, ws_up, ws_down, out, b0, B)
    return out


def kernel(x, c, w_ada, b_ada, norm1_g, norm2_g, w_in, q_norm_g, k_norm_g, w_pool, pool_scale, w_out,
           w_router, router_bias, w_gate, w_up, w_down, ws_gate, ws_up, ws_down):
    depth = w_ada.shape[0]
    for l in range(depth):
        mod = _adaln(c, w_ada[l], b_ada[l])
        x = _layer(x, mod, norm1_g[l], norm2_g[l], w_in[l], q_norm_g[l], k_norm_g[l], w_pool[l],
                   pool_scale[l], w_out[l], w_router[l], router_bias[l], w_gate[l], w_up[l], w_down[l],
                   ws_gate[l], ws_up[l], ws_down[l])
    return x
```

```python
import functools

import jax
import jax.numpy as jnp
from jax import lax
from jax.experimental import pallas as pl
from jax.experimental.pallas import tpu as pltpu
from jax.experimental.pallas import tpu_sc as plsc

F32 = jnp.float32
BF16 = jnp.bfloat16

HEAD_DIM = 64
N_HEADS = 8
D_ATTN = N_HEADS * HEAD_DIM
POOL_WINDOWS = (2, 4, 8, 16)
POOL_GROUP_DIM = 128
D_POOL = len(POOL_WINDOWS) * POOL_GROUP_DIM
MAX_WINDOW = max(POOL_WINDOWS)
assert all(w & (w - 1) == 0 for w in POOL_WINDOWS)
N_EXPERTS = 64
TOP_K = 6
N_GROUPS = 8
GROUP_SIZE = N_EXPERTS // N_GROUPS
TOPK_GROUPS = 4
ROUTED_SCALE = 2.5
RMS_EPS = 1e-6
N_MOD = 6

LANES = 128
SUBLANES = 8
VMEM_LIMIT = 56 * 1024 * 1024

TS_PROJ = 1024
TQ = 2048
TK = 256
PAIRS_PER_STEP = 2
TS_SLOT = 2048
ROW_BLOCK_LOG2 = 9
ROW_BLOCK = 1 << ROW_BLOCK_LOG2
BLOCKS_PER_STEP = 2
MOE_PARTS = 2
SC_CORES = 2
SC_SUBCORES = 16
SC_CHUNK = 64


def _split_bf16(a):
    hi = a.astype(BF16)
    lo = (a - hi.astype(F32)).astype(BF16)
    return hi, lo


def _dot(a, b):
    return jnp.dot(a, b, preferred_element_type=F32)


def _dot_nt(a, b):
    return lax.dot_general(a, b, (((1,), (1,)), ((), ())), preferred_element_type=F32)


def _dot3(a, b):
    ah, al = _split_bf16(a)
    bh, bl = _split_bf16(b)
    return _dot(ah, bh) + _dot(ah, bl) + _dot(al, bh)


def _dot3_nt(a, b):
    ah, al = _split_bf16(a)
    bh, bl = _split_bf16(b)
    return _dot_nt(ah, bh) + _dot_nt(ah, bl) + _dot_nt(al, bh)


def _silu(x):
    return x * (1.0 / (1.0 + jnp.exp(-x)))


def _rms_mod(x, gain, scale, shift):
    ms = jnp.mean(x * x, axis=-1, keepdims=True)
    y = x * lax.rsqrt(ms + RMS_EPS) * gain
    return y * (1.0 + scale) + shift


def _adaln_kernel(c_ref, w_ref, b_ref, o_ref):
    c = c_ref[...]
    o_ref[...] = _dot3(_silu(c), w_ref[...]) + b_ref[...]


def _adaln(c, w_ada, b_ada):
    nb, D = c.shape
    B = -(-nb // SUBLANES) * SUBLANES
    c = jnp.pad(c, ((0, B - nb), (0, 0)))
    N = w_ada.shape[1]
    tn = 1024
    out = pl.pallas_call(
        _adaln_kernel,
        grid=(N // tn,),
        in_specs=[pl.BlockSpec((B, D), lambda j: (0, 0)),
                  pl.BlockSpec((D, tn), lambda j: (0, j)),
                  pl.BlockSpec((1, tn), lambda j: (0, j))],
        out_specs=pl.BlockSpec((B, tn), lambda j: (0, j)),
        out_shape=jax.ShapeDtypeStruct((B, N), F32),
        compiler_params=pltpu.CompilerParams(dimension_semantics=("arbitrary",),
                                             vmem_limit_bytes=VMEM_LIMIT),
        name="adaln",
    )(c, w_ada, b_ada.reshape(1, N))
    return out[:nb]


def _first_grid_step():
    return jnp.logical_and(pl.program_id(0) == 0, pl.program_id(1) == 0)


def _inproj_kernel(x_ref, mod_ref, g1_ref, win32_ref, qg_ref, kg_ref, hm_ref, wp32_ref, ps_ref,
                   q_ref, k_ref, v_ref, p_ref, ext_ref, win_ref, wp_ref):
    @pl.when(_first_grid_step())
    def _():
        win_ref[...] = win32_ref[...].astype(BF16)
        wp_ref[...] = wp32_ref[...].astype(BF16)

    si = pl.program_id(1)
    ts = x_ref.shape[1]
    x = x_ref[0]
    shift1 = mod_ref[0, 0:1, :]
    scale1 = mod_ref[0, 1:2, :]
    h = _rms_mod(x, g1_ref[...], scale1, shift1)
    proj = _dot(h.astype(BF16), win_ref[...])

    hm = hm_ref[...]

    def head_norm(t, gain):
        ms = _dot((t * t).astype(BF16), hm)
        return t * lax.rsqrt(ms + RMS_EPS) * gain

    hq = proj[:, 0:D_ATTN]
    hk = proj[:, D_ATTN:2 * D_ATTN]
    q_ref[0] = (head_norm(hq, qg_ref[...]) * (HEAD_DIM ** -0.5)).astype(BF16)
    k_ref[0] = head_norm(hk, kg_ref[...]).astype(BF16)
    v_ref[0] = proj[:, 2 * D_ATTN:3 * D_ATTN].astype(BF16)

    hp = proj[:, 3 * D_ATTN:]

    @pl.when(si == 0)
    def _():
        ext_ref[0:MAX_WINDOW, :] = jnp.zeros((MAX_WINDOW, D_POOL), F32)

    ext_ref[MAX_WINDOW:, :] = hp
    pos = si * ts + lax.broadcasted_iota(jnp.int32, (ts, 1), 0)
    for g, w in enumerate(POOL_WINDOWS):
        lo_l, hi_l = g * POOL_GROUP_DIM, (g + 1) * POOL_GROUP_DIM
        u = hp[:, lo_l:hi_l]
        run = ext_ref[:, lo_l:hi_l]
        span = 1
        while span < w:
            run = run + pltpu.roll(run, span, axis=0)
            span *= 2
        acc = run[MAX_WINDOW:]
        count = jnp.minimum(pos + 1, w).astype(F32)
        d = acc / count - u
        mixed = _dot(d.astype(BF16), wp_ref[g])
        p_ref[0, :, lo_l:hi_l] = (mixed * ps_ref[:, lo_l:hi_l]).astype(BF16)
    ext_ref[0:MAX_WINDOW, :] = hp[ts - MAX_WINDOW:, :]


def _inproj(x, mod3, g1, w_in, qg_t, kg_t, head_mean, w_pool, pool_scale):
    B, S, D = x.shape
    ts = TS_PROJ
    out_sd = jax.ShapeDtypeStruct((B, S, D_ATTN), BF16)
    blk = pl.BlockSpec((1, ts, D_ATTN), lambda b, s: (b, s, 0))
    const2 = lambda b, s: (0, 0)
    return pl.pallas_call(
        _inproj_kernel,
        grid=(B, S // ts),
        in_specs=[pl.BlockSpec((1, ts, D), lambda b, s: (b, s, 0)),
                  pl.BlockSpec((1, N_MOD, D), lambda b, s: (b, 0, 0)),
                  pl.BlockSpec((1, D), const2),
                  pl.BlockSpec(w_in.shape, const2),
                  pl.BlockSpec((1, D_ATTN), const2),
                  pl.BlockSpec((1, D_ATTN), const2),
                  pl.BlockSpec((D_ATTN, D_ATTN), const2),
                  pl.BlockSpec(w_pool.shape, lambda b, s: (0, 0, 0)),
                  pl.BlockSpec((1, D_POOL), const2)],
        out_specs=[blk, blk, blk, blk],
        out_shape=[out_sd, out_sd, out_sd, out_sd],
        scratch_shapes=[pltpu.VMEM((MAX_WINDOW + ts, D_POOL), F32),
                        pltpu.VMEM(w_in.shape, BF16), pltpu.VMEM(w_pool.shape, BF16)],
        compiler_params=pltpu.CompilerParams(dimension_semantics=("arbitrary", "arbitrary"),
                                             vmem_limit_bytes=VMEM_LIMIT),
        name="inproj",
    )(x, mod3, g1, w_in, qg_t, kg_t, head_mean, w_pool, pool_scale)


def _attn_kernel(q_ref, k_ref, v_ref, u_ref, o_ref, acc_ref):
    qi = pl.program_id(2)
    lane = lax.broadcasted_iota(jnp.int32, (TQ, LANES), 1)
    first = lane < HEAD_DIM
    n_heads = 2 * PAIRS_PER_STEP
    pair_lanes = lambda h: pl.ds((h // 2) * LANES, LANES)
    qh = []
    for h in range(n_heads):
        q = q_ref[0, :, pair_lanes(h)]
        keep = first if h % 2 == 0 else jnp.logical_not(first)
        qh.append(jnp.where(keep, q, jnp.zeros_like(q)))
    u = u_ref[...]
    row = lax.broadcasted_iota(jnp.int32, (TQ, TK), 0)
    col = lax.broadcasted_iota(jnp.int32, (TQ, TK), 1)
    acc_ref[...] = jnp.zeros(acc_ref.shape, F32)

    def block(kb, survs, diag):
        start = pl.multiple_of(kb * TK, TK)
        r0 = 0 if diag is None else diag * TK
        if diag is not None:
            valid = (col + r0 < row)[r0:]
        out = []
        for h in range(n_heads):
            k = k_ref[0, pl.ds(start, TK), pair_lanes(h)]
            v = v_ref[0, pl.ds(start, TK), pair_lanes(h)]
            z = _dot_nt(qh[h][r0:], k)
            zb = z.astype(BF16)
            sp = jnp.maximum(zb, 0) + jnp.log(1 + jnp.exp(-jnp.abs(zb)))
            if diag is not None:
                sp = jnp.where(valid, sp, jnp.zeros_like(sp))
            r = _dot(sp, u)
            arg = z + r + survs[h][r0:]
            if diag is not None:
                arg = jnp.where(valid, arg, -jnp.inf)
            acc_ref[h, r0:, :] += _dot(jnp.exp(arg).astype(BF16), v)
            surv = survs[h][r0:] + r[:, 0:1]
            out.append(surv if r0 == 0 else jnp.concatenate([survs[h][:r0], surv], axis=0))
        return tuple(out)

    survs = tuple(jnp.zeros((TQ, 1), F32) for _ in range(n_heads))
    n_diag = TQ // TK
    for d in reversed(range(n_diag)):
        survs = block(qi * n_diag + d, survs, d)
    def full_blocks(i, c):
        for d in range(n_diag):
            c = block((qi - i) * n_diag - 1 - d, c, None)
        return c

    lax.fori_loop(0, qi, full_blocks, survs)
    for p in range(PAIRS_PER_STEP):
        o_ref[0, :, pl.ds(p * LANES, LANES)] = jnp.where(first, acc_ref[2 * p], acc_ref[2 * p + 1]).astype(BF16)


def _attention(q, k, v, umat):
    B, S, _ = q.shape
    width = PAIRS_PER_STEP * LANES
    kv_spec = pl.BlockSpec((1, S, width), lambda b, p, i: (b, 0, p))
    q_spec = pl.BlockSpec((1, TQ, width), lambda b, p, i: (b, i, p))
    return pl.pallas_call(
        _attn_kernel,
        grid=(B, D_ATTN // width, S // TQ),
        in_specs=[q_spec, kv_spec, kv_spec, pl.BlockSpec((TK, TK), lambda b, p, i: (0, 0))],
        out_specs=q_spec,
        out_shape=jax.ShapeDtypeStruct((B, S, D_ATTN), BF16),
        scratch_shapes=[pltpu.VMEM((2 * PAIRS_PER_STEP, TQ, LANES), F32)],
        compiler_params=pltpu.CompilerParams(
            dimension_semantics=("arbitrary", "arbitrary", "arbitrary"),
            vmem_limit_bytes=VMEM_LIMIT),
        name="stickbreak_attn",
    )(q, k, v, umat)


def _route_t(scores, biased):
    ts = scores.shape[-1]
    neg = -jnp.inf
    b3 = biased.reshape(N_GROUPS, GROUP_SIZE, ts)
    e_in_g = lax.broadcasted_iota(jnp.int32, b3.shape, 1)
    m1 = jnp.max(b3, axis=1, keepdims=True)
    i1 = jnp.min(jnp.where(b3 == m1, e_in_g, GROUP_SIZE), axis=1, keepdims=True)
    m2 = jnp.max(jnp.where(e_in_g == i1, neg, b3), axis=1, keepdims=True)
    gs = (m1 + m2)[:, 0, :]
    g_iota = lax.broadcasted_iota(jnp.int32, gs.shape, 0)
    g_sel = jnp.zeros(gs.shape, jnp.bool_)
    for _ in range(TOPK_GROUPS):
        gm = jnp.max(gs, axis=0, keepdims=True)
        gi = jnp.min(jnp.where(gs == gm, g_iota, N_GROUPS), axis=0, keepdims=True)
        pick = g_iota == gi
        g_sel = jnp.logical_or(g_sel, pick)
        gs = jnp.where(pick, neg, gs)
    masked = jnp.where(g_sel[:, None, :], b3, neg)
    flat = lax.broadcasted_iota(jnp.int32, b3.shape, 0) * GROUP_SIZE + e_in_g
    sel = jnp.zeros(b3.shape, jnp.bool_)
    for _ in range(TOP_K):
        m = jnp.max(jnp.max(masked, axis=1, keepdims=True), axis=0, keepdims=True)
        cand = jnp.where(masked == m, flat, N_EXPERTS)
        idx = jnp.min(jnp.min(cand, axis=1, keepdims=True), axis=0, keepdims=True)
        pick = flat == idx
        sel = jnp.logical_or(sel, pick)
        masked = jnp.where(pick, neg, masked)
    s3 = scores.reshape(N_GROUPS, GROUP_SIZE, ts)
    w = jnp.where(sel, s3, 0.0)
    tot = jnp.sum(jnp.sum(w, axis=1, keepdims=True), axis=0, keepdims=True)
    return (w / tot * ROUTED_SCALE).reshape(N_EXPERTS, ts), sel.reshape(N_EXPERTS, ts)


def _pack_bf16_pair(lo, hi):
    lo_bits = pltpu.bitcast(lo.astype(BF16).astype(F32), jnp.uint32) >> 16
    hi_bits = pltpu.bitcast(hi.astype(BF16).astype(F32), jnp.uint32) & jnp.uint32(0xFFFF0000)
    return lo_bits | hi_bits


def _unpack_bf16_pair(p):
    lo = pltpu.bitcast(p << 16, F32).astype(BF16)
    hi = pltpu.bitcast(p & jnp.uint32(0xFFFF0000), F32).astype(BF16)
    return lo, hi


def _outproj_kernel(x_ref, a_ref, p_ref, mod_ref, wo32_ref, g2_ref, wrt_ref, rb_ref, ut_ref,
                    x1_ref, hp_ref, rank_ref, wd_ref, cnt_ref, run_ref, wo_ref):
    @pl.when(_first_grid_step())
    def _():
        run_ref[...] = jnp.zeros(run_ref.shape, jnp.int32)
        wo_ref[...] = wo32_ref[...].astype(BF16)

    gate1 = mod_ref[0, 2:3, :]
    shift2 = mod_ref[0, 3:4, :]
    scale2 = mod_ref[0, 4:5, :]
    mixp = _dot(a_ref[0], wo_ref[0:D_ATTN, :]) + _dot(p_ref[0], wo_ref[D_ATTN:, :])
    x1 = x_ref[0] + gate1 * mixp
    x1_ref[0] = x1
    h2 = _rms_mod(x1, g2_ref[...], scale2, shift2)
    half = h2.shape[-1] // 2
    hp_ref[0] = _pack_bf16_pair(h2[:, :half], h2[:, half:])
    logits_t = _dot3_nt(wrt_ref[...], h2)
    scores = 1.0 / (1.0 + jnp.exp(-logits_t))
    wd_t, sel = _route_t(scores, scores + rb_ref[...])
    wd_ref[...] = wd_t
    self_f = jnp.where(sel, 1.0, 0.0)
    before = _dot(self_f.astype(BF16), ut_ref[...]).astype(jnp.int32)
    run = run_ref[:, 0:1]
    rank_ref[...] = jnp.where(sel, run + before, -1)
    run_ref[...] = run_ref[...] + jnp.sum(self_f, axis=1, keepdims=True).astype(jnp.int32)
    cnt_ref[...] = run_ref[...]


def _outproj(x, attn, pool, mod3, w_out, g2, wr_t, rbias, ut, b0, B):
    _, S, D = x.shape
    ts = TS_PROJ
    n_s = S // ts
    const2 = lambda b, s: (0, 0)
    tok_in = lambda w: pl.BlockSpec((1, ts, w), lambda b, s: (b + b0, s, 0))
    tok = lambda w: pl.BlockSpec((1, ts, w), lambda b, s: (b, s, 0))
    tok_t = pl.BlockSpec((N_EXPERTS, ts), lambda b, s: (0, b * n_s + s))
    return pl.pallas_call(
        _outproj_kernel,
        grid=(B, n_s),
        in_specs=[tok_in(D), tok_in(D_ATTN), tok_in(D_POOL),
                  pl.BlockSpec((1, N_MOD, D), lambda b, s: (b + b0, 0, 0)),
                  pl.BlockSpec(w_out.shape, const2),
                  pl.BlockSpec((1, D), const2),
                  pl.BlockSpec(wr_t.shape, const2),
                  pl.BlockSpec((N_EXPERTS, 1), const2),
                  pl.BlockSpec((ts, ts), const2)],
        out_specs=[tok(D), tok(D // 2), tok_t, tok_t, pl.BlockSpec((N_EXPERTS, LANES), const2)],
        out_shape=[jax.ShapeDtypeStruct((B, S, D), F32),
                   jax.ShapeDtypeStruct((B, S, D // 2), jnp.uint32),
                   jax.ShapeDtypeStruct((N_EXPERTS, B * S), jnp.int32),
                   jax.ShapeDtypeStruct((N_EXPERTS, B * S), F32),
                   jax.ShapeDtypeStruct((N_EXPERTS, LANES), jnp.int32)],
        scratch_shapes=[pltpu.VMEM((N_EXPERTS, LANES), jnp.int32), pltpu.VMEM(w_out.shape, BF16)],
        compiler_params=pltpu.CompilerParams(dimension_semantics=("arbitrary", "arbitrary"),
                                             vmem_limit_bytes=VMEM_LIMIT),
        name="outproj_router",
    )(x, attn, pool, mod3, w_out, g2, wr_t, rbias, ut)


def _n_row_blocks(n_tokens):
    n_blocks = -(-(n_tokens * TOP_K + N_EXPERTS * (ROW_BLOCK - 1)) // ROW_BLOCK)
    return -(-n_blocks // BLOCKS_PER_STEP) * BLOCKS_PER_STEP


def _slots_kernel(rank_ref, wd_ref, cnt_ref, lt_ref, slot_ref, wk_ref, be_ref, nu_ref):
    lt = lt_ref[...]
    nblk = lax.shift_right_logical(cnt_ref[...] + (ROW_BLOCK - 1), ROW_BLOCK_LOG2).astype(F32)
    nb_hi, nb_lo = _split_bf16(nblk)
    blk_start = _dot(lt, nb_hi) + _dot(lt, nb_lo)

    @pl.when(pl.program_id(0) == 0)
    def _():
        blk_end = (blk_start + nblk)[:, 0:1]
        b_iota = lax.broadcasted_iota(jnp.int32, (N_EXPERTS, be_ref.shape[-1]), 1).astype(F32)
        owner = jnp.sum(jnp.where(blk_end <= b_iota, 1, 0), axis=0, keepdims=True)
        be_ref[...] = jnp.minimum(owner, N_EXPERTS - 1)
        nu_ref[...] = jnp.broadcast_to(blk_end[N_EXPERTS - 1:, :].astype(jnp.int32), nu_ref.shape)

    rank = rank_ref[...]
    sel = rank >= 0
    row_start = (blk_start[:, 0:1] * ROW_BLOCK).astype(jnp.int32)
    slot_d = row_start + rank
    wd = wd_ref[...]
    choice = _dot(lt, jnp.where(sel, 1.0, 0.0).astype(BF16)).astype(jnp.int32)
    ts = rank.shape[-1]
    slots, wks = [], []
    for k in range(TOP_K):
        m = jnp.logical_and(sel, choice == k)
        slots.append(jnp.sum(jnp.where(m, slot_d, 0), axis=0, keepdims=True))
        wks.append(jnp.sum(jnp.where(m, wd, 0.0), axis=0, keepdims=True))
    slot_ref[...] = jnp.concatenate(slots + [jnp.zeros((SUBLANES - TOP_K, ts), jnp.int32)], axis=0)
    wk_pad = jnp.concatenate(wks + [jnp.zeros((LANES - TOP_K, ts), F32)], axis=0)
    wk_ref[...] = wk_pad.T


def _slots(rank_t, wd_t, cnt, lt):
    T = rank_t.shape[1]
    ts = TS_SLOT
    nb_pad = -(-_n_row_blocks(T) // LANES) * LANES
    const = lambda i: (0, 0)
    tok_t = pl.BlockSpec((N_EXPERTS, ts), lambda i: (0, i))
    return pl.pallas_call(
        _slots_kernel,
        grid=(T // ts,),
        in_specs=[tok_t, tok_t, pl.BlockSpec((N_EXPERTS, LANES), const), pl.BlockSpec((N_EXPERTS, N_EXPERTS), const)],
        out_specs=[pl.BlockSpec((SUBLANES, ts), lambda i: (0, i)),
                   pl.BlockSpec((ts, LANES), lambda i: (i, 0)),
                   pl.BlockSpec((1, nb_pad), const),
                   pl.BlockSpec((1, LANES), const)],
        out_shape=[jax.ShapeDtypeStruct((SUBLANES, T), jnp.int32),
                   jax.ShapeDtypeStruct((T, LANES), F32),
                   jax.ShapeDtypeStruct((1, nb_pad), jnp.int32),
                   jax.ShapeDtypeStruct((1, LANES), jnp.int32)],
        compiler_params=pltpu.CompilerParams(dimension_semantics=("arbitrary",), vmem_limit_bytes=VMEM_LIMIT),
        name="slots",
    )(rank_t, wd_t, cnt, lt)


def _sc_mesh():
    return plsc.VectorSubcoreMesh(core_axis_name="c", subcore_axis_name="s",
                                  num_cores=SC_CORES, num_subcores=SC_SUBCORES)


def _sc_dispatch(rows, slot_flat, n_out):
    T, width = rows.shape
    n_workers = SC_CORES * SC_SUBCORES
    per_worker = T // n_workers
    steps = per_worker // SC_CHUNK

    @functools.partial(
        pl.kernel, mesh=_sc_mesh(),
        out_type=jax.ShapeDtypeStruct((n_out, width), rows.dtype),
        scratch_types=[pltpu.VMEM((SC_CHUNK, width), rows.dtype)]
        + [pltpu.VMEM((SC_CHUNK,), jnp.int32)] * TOP_K + [pltpu.SemaphoreType.DMA],
        name="dispatch",
    )
    def run(rows_hbm, slot_hbm, out_hbm, rows_v, *rest):
        idx_v, sem = rest[:TOP_K], rest[TOP_K]
        base = (lax.axis_index("s") * SC_CORES + lax.axis_index("c")) * per_worker

        @pl.loop(0, steps)
        def _(i):
            off = base + i * SC_CHUNK
            pltpu.sync_copy(rows_hbm.at[pl.ds(off, SC_CHUNK)], rows_v)
            for k in range(TOP_K):
                pltpu.sync_copy(slot_hbm.at[pl.ds(k * T + off, SC_CHUNK)], idx_v[k])
            copies = [pltpu.async_copy(rows_v, out_hbm.at[idx_v[k]], sem) for k in range(TOP_K)]
            for cp in copies:
                cp.wait()

    return run(rows, slot_flat)


def _sc_gather(rows, idx):
    n = idx.shape[0]
    width = rows.shape[1]
    n_workers = SC_CORES * SC_SUBCORES
    per_worker = n // n_workers
    steps = per_worker // SC_CHUNK

    assert steps % 2 == 0
    slot_types = [pltpu.VMEM((SC_CHUNK,), jnp.int32), pltpu.VMEM((SC_CHUNK, width), rows.dtype),
                  pltpu.SemaphoreType.DMA]

    @functools.partial(
        pl.kernel, mesh=_sc_mesh(),
        out_type=jax.ShapeDtypeStruct((n, width), rows.dtype),
        scratch_types=slot_types * 2,
        name="combine",
    )
    def run(rows_hbm, idx_hbm, out_hbm, *scratch):
        base = (lax.axis_index("s") * SC_CORES + lax.axis_index("c")) * per_worker
        slots = (scratch[0:3], scratch[3:6])

        def gather(slot):
            idx_v, rows_v, sem = slots[slot]
            return pltpu.make_async_copy(rows_hbm.at[idx_v], rows_v, sem)

        def start(chunk, slot):
            pltpu.sync_copy(idx_hbm.at[pl.ds(base + chunk * SC_CHUNK, SC_CHUNK)], slots[slot][0])
            gather(slot).start()

        def finish(chunk, slot):
            gather(slot).wait()
            pltpu.sync_copy(slots[slot][1], out_hbm.at[pl.ds(base + chunk * SC_CHUNK, SC_CHUNK)])

        start(0, 0)

        @pl.loop(0, steps, step=2)
        def _(chunk):
            start(chunk + 1, 1)
            finish(chunk, 0)

            @pl.when(chunk + 2 < steps)
            def _():
                start(chunk + 2, 0)

            finish(chunk + 1, 1)

    return run(rows, idx)


def _swiglu_packed(xp, wgu, wd):
    lo, hi = _unpack_bf16_pair(xp)
    half = xp.shape[-1]
    gu = _dot(lo, wgu[:half, :]) + _dot(hi, wgu[half:, :])
    dh = wgu.shape[-1] // 2
    act = _silu(gu[:, :dh]) * gu[:, dh:]
    return _dot(act.astype(BF16), wd)


def _cast_swiglu_weights(wg32_ref, wu32_ref, wd32_ref, wgu_ref, wd_ref):
    dh = wg32_ref.shape[-1]
    wgu_ref[:, :dh] = wg32_ref[...].reshape(wg32_ref.shape[-2:]).astype(BF16)
    wgu_ref[:, dh:] = wu32_ref[...].reshape(wu32_ref.shape[-2:]).astype(BF16)
    wd_ref[...] = wd32_ref[...].reshape(wd32_ref.shape[-2:]).astype(BF16)


def _experts_kernel(be_ref, nu_ref, x_ref, *refs):
    n = BLOCKS_PER_STEP
    w32 = [refs[3 * j:3 * j + 3] for j in range(n)]
    y_ref = refs[3 * n]
    wbf = [refs[3 * n + 1 + 2 * j:3 * n + 3 + 2 * j] for j in range(n)]
    step = pl.program_id(0)
    for j in range(n):
        blk = step * n + j
        new_expert = jnp.logical_or(step == 0, be_ref[blk] != be_ref[jnp.maximum(blk - n, 0)])

        @pl.when(new_expert)
        def _():
            _cast_swiglu_weights(*w32[j], *wbf[j])

    @pl.when(step * n < nu_ref[0])
    def _():
        for j in range(n):
            rows = pl.ds(j * ROW_BLOCK, ROW_BLOCK)
            y = _swiglu_packed(x_ref[rows, :], wbf[j][0][...], wbf[j][1][...])
            half = y.shape[-1] // 2
            y_ref[rows, :] = _pack_bf16_pair(y[:, :half], y[:, half:])


def _experts(xs, block_expert, n_used, w_gate, w_up, w_down):
    P, half = xs.shape
    _, D, dh = w_gate.shape
    n = BLOCKS_PER_STEP
    rows = n * ROW_BLOCK
    assert P % rows == 0

    def by_expert(shape, j):
        return pl.BlockSpec((1,) + shape, lambda s, be, nu: (be[s * n + j], 0, 0))

    w_specs, w_args = [], []
    for j in range(n):
        w_specs += [by_expert((D, dh), j), by_expert((D, dh), j), by_expert((dh, D), j)]
        w_args += [w_gate, w_up, w_down]
    def row_step(s, be, nu):
        return jnp.minimum(s, lax.div(jnp.maximum(nu[0], 1) - 1, n)), 0

    grid_spec = pltpu.PrefetchScalarGridSpec(
        num_scalar_prefetch=2,
        grid=(P // rows,),
        in_specs=[pl.BlockSpec((rows, half), row_step)] + w_specs,
        out_specs=pl.BlockSpec((rows, half), row_step),
        scratch_shapes=[pltpu.VMEM((D, 2 * dh), BF16), pltpu.VMEM((dh, D), BF16)] * n,
    )
    return pl.pallas_call(
        _experts_kernel,
        grid_spec=grid_spec,
        out_shape=jax.ShapeDtypeStruct((P, half), jnp.uint32),
        compiler_params=pltpu.CompilerParams(dimension_semantics=("arbitrary",), vmem_limit_bytes=VMEM_LIMIT),
        name="experts",
    )(block_expert, n_used, xs, *w_args)


def _final_kernel(x1_ref, hp_ref, g_ref, wk_ref, mod_ref, sg32_ref, su32_ref, sd32_ref, *rest):
    o_ref, sgu_ref, sd_ref = rest[-3:]

    @pl.when(_first_grid_step())
    def _():
        _cast_swiglu_weights(sg32_ref, su32_ref, sd32_ref, sgu_ref, sd_ref)

    acc = _swiglu_packed(hp_ref[0], sgu_ref[...], sd_ref[...])
    wk = wk_ref[0]
    for k in range(TOP_K):
        lo, hi = _unpack_bf16_pair(g_ref[k, 0])
        y = jnp.concatenate([lo.astype(F32), hi.astype(F32)], axis=-1)
        acc = acc + wk[:, k:k + 1] * y
    gate2 = mod_ref[0, 5:6, :]
    o_ref[0] = x1_ref[0] + gate2 * acc


def _final(x1, hp, g, wk_tok, mod3, ws_gate, ws_up, ws_down, prev_out, b0, b_total):
    B, S, D = x1.shape
    dh = ws_gate.shape[-1]
    ts = TS_PROJ
    tok = lambda w: pl.BlockSpec((1, ts, w), lambda b, s: (b, s, 0))
    const2 = lambda b, s: (0, 0)
    in_specs = [tok(D), tok(D // 2),
                pl.BlockSpec((TOP_K, 1, ts, D // 2), lambda b, s: (0, b, s, 0)),
                tok(LANES),
                pl.BlockSpec((1, N_MOD, D), lambda b, s: (b + b0, 0, 0)),
                pl.BlockSpec(ws_gate.shape, const2),
                pl.BlockSpec(ws_up.shape, const2),
                pl.BlockSpec(ws_down.shape, const2)]
    args = [x1, hp, g, wk_tok, mod3, ws_gate, ws_up, ws_down]
    aliases = {}
    if prev_out is not None:
        in_specs.append(pl.BlockSpec(memory_space=pl.ANY))
        args.append(prev_out)
        aliases = {len(args) - 1: 0}
    return pl.pallas_call(
        _final_kernel,
        grid=(B, S // ts),
        in_specs=in_specs,
        out_specs=pl.BlockSpec((1, ts, D), lambda b, s: (b + b0, s, 0)),
        out_shape=jax.ShapeDtypeStruct((b_total, S, D), F32),
        scratch_shapes=[pltpu.VMEM((D, 2 * dh), BF16), pltpu.VMEM((dh, D), BF16)],
        input_output_aliases=aliases,
        compiler_params=pltpu.CompilerParams(dimension_semantics=("arbitrary", "arbitrary"),
                                             vmem_limit_bytes=VMEM_LIMIT),
        name="final",
    )(*args)


def _layer(x, c_act_mod, norm1_g, norm2_g, w_in, q_norm_g, k_norm_g, w_pool, pool_scale, w_out,
           w_router, router_bias, w_gate, w_up, w_down, ws_gate, ws_up, ws_down):
    B, S, D = x.shape
    mod3 = c_act_mod.reshape(B, N_MOD, D)
    head_of = jnp.arange(D_ATTN, dtype=jnp.int32) // HEAD_DIM
    head_mean = jnp.where(head_of[:, None] == head_of[None, :], 1.0 / HEAD_DIM, 0.0).astype(BF16)
    j = jnp.arange(TK, dtype=jnp.int32)
    umat = jnp.where(j[:, None] >= j[None, :], -1.0, 0.0).astype(BF16)

    q, k, v, pool = _inproj(
        x, mod3, norm1_g.reshape(1, D), w_in,
        jnp.tile(q_norm_g, N_HEADS).reshape(1, D_ATTN), jnp.tile(k_norm_g, N_HEADS).reshape(1, D_ATTN),
        head_mean, w_pool, pool_scale.reshape(1, D_POOL))
    attn = _attention(q, k, v, umat)
    t = jnp.arange(TS_PROJ, dtype=jnp.int32)
    ut = (t[:, None] < t[None, :]).astype(BF16)
    e = jnp.arange(N_EXPERTS, dtype=jnp.int32)
    lt = (e[None, :] < e[:, None]).astype(BF16)
    bp = B // MOE_PARTS
    T = bp * S
    n_blocks = _n_row_blocks(T)
    out = None
    for part in range(MOE_PARTS):
        b0 = part * bp
        x1, hp, rank_t, wd_t, cnt = _outproj(x, attn, pool, mod3, w_out, norm2_g.reshape(1, D),
                                             w_router.T, router_bias.reshape(N_EXPERTS, 1), ut, b0, bp)
        slots, wk_tok, block_expert, n_used = _slots(rank_t, wd_t, cnt, lt)
        slot_flat = slots[:TOP_K].reshape(TOP_K * T)
        xs = _sc_dispatch(hp.reshape(T, D // 2), slot_flat, n_blocks * ROW_BLOCK)
        ys = _experts(xs, block_expert[0, :n_blocks], n_used[0, :1], w_gate, w_up, w_down)
        g = _sc_gather(ys, slot_flat).reshape(TOP_K, bp, S, D // 2)
        out = _final(x1, hp, g, wk_tok.reshape(bp, S, LANES), mod3, ws_gate, ws_up, ws_down, out, b0, B)
    return out


def kernel(x, c, w_ada, b_ada, norm1_g, norm2_g, w_in, q_norm_g, k_norm_g, w_pool, pool_scale, w_out,
           w_router, router_bias, w_gate, w_up, w_down, ws_gate, ws_up, ws_down):
    depth = w_ada.shape[0]
    for l in range(depth):
        mod = _adaln(c, w_ada[l], b_ada[l])
        x = _layer(x, mod, norm1_g[l], norm2_g[l], w_in[l], q_norm_g[l], k_norm_g[l], w_pool[l],
                   pool_scale[l], w_out[l], w_router[l], router_bias[l], w_gate[l], w_up[l], w_down[l],
                   ws_gate[l], ws_up[l], ws_down[l])
    return x
```

```python
import functools

import jax
import jax.numpy as jnp
from jax import lax
from jax.experimental import pallas as pl
from jax.experimental.pallas import tpu as pltpu
from jax.experimental.pallas import tpu_sc as plsc

F32 = jnp.float32
BF16 = jnp.bfloat16

HEAD_DIM = 64
N_HEADS = 8
D_ATTN = N_HEADS * HEAD_DIM
POOL_WINDOWS = (2, 4, 8, 16)
POOL_GROUP_DIM = 128
D_POOL = len(POOL_WINDOWS) * POOL_GROUP_DIM
MAX_WINDOW = max(POOL_WINDOWS)
assert all(w & (w - 1) == 0 for w in POOL_WINDOWS)
N_EXPERTS = 64
TOP_K = 6
N_GROUPS = 8
GROUP_SIZE = N_EXPERTS // N_GROUPS
TOPK_GROUPS = 4
ROUTED_SCALE = 2.5
RMS_EPS = 1e-6
N_MOD = 6

LANES = 128
SUBLANES = 8
VMEM_LIMIT = 56 * 1024 * 1024

TS_PROJ = 1024
TQ = 1024
TK = 256
PAIRS_PER_STEP = 4
TS_SLOT = 2048
ROW_BLOCK_LOG2 = 9
ROW_BLOCK = 1 << ROW_BLOCK_LOG2
BLOCKS_PER_STEP = 2
MOE_PARTS = 2
SC_CORES = 2
SC_SUBCORES = 16
SC_CHUNK = 64


def _split_bf16(a):
    hi = a.astype(BF16)
    lo = (a - hi.astype(F32)).astype(BF16)
    return hi, lo


def _dot(a, b):
    return jnp.dot(a, b, preferred_element_type=F32)


def _dot_nt(a, b):
    return lax.dot_general(a, b, (((1,), (1,)), ((), ())), preferred_element_type=F32)


def _dot3(a, b):
    ah, al = _split_bf16(a)
    bh, bl = _split_bf16(b)
    return _dot(ah, bh) + _dot(ah, bl) + _dot(al, bh)


def _dot3_nt(a, b):
    ah, al = _split_bf16(a)
    bh, bl = _split_bf16(b)
    return _dot_nt(ah, bh) + _dot_nt(ah, bl) + _dot_nt(al, bh)


def _silu(x):
    return x * (1.0 / (1.0 + jnp.exp(-x)))


def _rms_mod(x, gain, scale, shift):
    ms = jnp.mean(x * x, axis=-1, keepdims=True)
    y = x * lax.rsqrt(ms + RMS_EPS) * gain
    return y * (1.0 + scale) + shift


def _adaln_kernel(c_ref, w_ref, b_ref, o_ref):
    c = c_ref[...]
    o_ref[...] = _dot3(_silu(c), w_ref[...]) + b_ref[...]


def _adaln(c, w_ada, b_ada):
    nb, D = c.shape
    B = -(-nb // SUBLANES) * SUBLANES
    c = jnp.pad(c, ((0, B - nb), (0, 0)))
    N = w_ada.shape[1]
    tn = 1024
    out = pl.pallas_call(
        _adaln_kernel,
        grid=(N // tn,),
        in_specs=[pl.BlockSpec((B, D), lambda j: (0, 0)),
                  pl.BlockSpec((D, tn), lambda j: (0, j)),
                  pl.BlockSpec((1, tn), lambda j: (0, j))],
        out_specs=pl.BlockSpec((B, tn), lambda j: (0, j)),
        out_shape=jax.ShapeDtypeStruct((B, N), F32),
        compiler_params=pltpu.CompilerParams(dimension_semantics=("arbitrary",),
                                             vmem_limit_bytes=VMEM_LIMIT),
        name="adaln",
    )(c, w_ada, b_ada.reshape(1, N))
    return out[:nb]


def _first_grid_step():
    return jnp.logical_and(pl.program_id(0) == 0, pl.program_id(1) == 0)


def _inproj_kernel(x_ref, mod_ref, g1_ref, win32_ref, qg_ref, kg_ref, hm_ref, wp32_ref, ps_ref,
                   q_ref, k_ref, v_ref, p_ref, ext_ref, win_ref, wp_ref):
    @pl.when(_first_grid_step())
    def _():
        win_ref[...] = win32_ref[...].astype(BF16)
        wp_ref[...] = wp32_ref[...].astype(BF16)

    si = pl.program_id(1)
    ts = x_ref.shape[1]
    x = x_ref[0]
    shift1 = mod_ref[0, 0:1, :]
    scale1 = mod_ref[0, 1:2, :]
    h = _rms_mod(x, g1_ref[...], scale1, shift1)
    proj = _dot(h.astype(BF16), win_ref[...])

    hm = hm_ref[...]

    def head_norm(t, gain):
        ms = _dot((t * t).astype(BF16), hm)
        return t * lax.rsqrt(ms + RMS_EPS) * gain

    hq = proj[:, 0:D_ATTN]
    hk = proj[:, D_ATTN:2 * D_ATTN]
    q_ref[0] = (head_norm(hq, qg_ref[...]) * (HEAD_DIM ** -0.5)).astype(BF16)
    k_ref[0] = head_norm(hk, kg_ref[...]).astype(BF16)
    v_ref[0] = proj[:, 2 * D_ATTN:3 * D_ATTN].astype(BF16)

    hp = proj[:, 3 * D_ATTN:]

    @pl.when(si == 0)
    def _():
        ext_ref[0:MAX_WINDOW, :] = jnp.zeros((MAX_WINDOW, D_POOL), F32)

    ext_ref[MAX_WINDOW:, :] = hp
    pos = si * ts + lax.broadcasted_iota(jnp.int32, (ts, 1), 0)
    for g, w in enumerate(POOL_WINDOWS):
        lo_l, hi_l = g * POOL_GROUP_DIM, (g + 1) * POOL_GROUP_DIM
        u = hp[:, lo_l:hi_l]
        run = ext_ref[:, lo_l:hi_l]
        span = 1
        while span < w:
            run = run + pltpu.roll(run, span, axis=0)
            span *= 2
        acc = run[MAX_WINDOW:]
        count = jnp.minimum(pos + 1, w).astype(F32)
        d = acc / count - u
        mixed = _dot(d.astype(BF16), wp_ref[g])
        p_ref[0, :, lo_l:hi_l] = (mixed * ps_ref[:, lo_l:hi_l]).astype(BF16)
    ext_ref[0:MAX_WINDOW, :] = hp[ts - MAX_WINDOW:, :]


def _inproj(x, mod3, g1, w_in, qg_t, kg_t, head_mean, w_pool, pool_scale):
    B, S, D = x.shape
    ts = TS_PROJ
    out_sd = jax.ShapeDtypeStruct((B, S, D_ATTN), BF16)
    blk = pl.BlockSpec((1, ts, D_ATTN), lambda b, s: (b, s, 0))
    const2 = lambda b, s: (0, 0)
    return pl.pallas_call(
        _inproj_kernel,
        grid=(B, S // ts),
        in_specs=[pl.BlockSpec((1, ts, D), lambda b, s: (b, s, 0)),
                  pl.BlockSpec((1, N_MOD, D), lambda b, s: (b, 0, 0)),
                  pl.BlockSpec((1, D), const2),
                  pl.BlockSpec(w_in.shape, const2),
                  pl.BlockSpec((1, D_ATTN), const2),
                  pl.BlockSpec((1, D_ATTN), const2),
                  pl.BlockSpec((D_ATTN, D_ATTN), const2),
                  pl.BlockSpec(w_pool.shape, lambda b, s: (0, 0, 0)),
                  pl.BlockSpec((1, D_POOL), const2)],
        out_specs=[blk, blk, blk, blk],
        out_shape=[out_sd, out_sd, out_sd, out_sd],
        scratch_shapes=[pltpu.VMEM((MAX_WINDOW + ts, D_POOL), F32),
                        pltpu.VMEM(w_in.shape, BF16), pltpu.VMEM(w_pool.shape, BF16)],
        compiler_params=pltpu.CompilerParams(dimension_semantics=("arbitrary", "arbitrary"),
                                             vmem_limit_bytes=VMEM_LIMIT),
        name="inproj",
    )(x, mod3, g1, w_in, qg_t, kg_t, head_mean, w_pool, pool_scale)


def _attn_kernel(q_ref, k_ref, v_ref, u_ref, o_ref, acc_ref):
    qi = pl.program_id(2)
    lane = lax.broadcasted_iota(jnp.int32, (TQ, LANES), 1)
    first = lane < HEAD_DIM
    n_heads = 2 * PAIRS_PER_STEP
    pair_lanes = lambda h: pl.ds((h // 2) * LANES, LANES)
    qh = []
    for h in range(n_heads):
        q = q_ref[0, :, pair_lanes(h)]
        keep = first if h % 2 == 0 else jnp.logical_not(first)
        qh.append(jnp.where(keep, q, jnp.zeros_like(q)))
    u = u_ref[...]
    row = lax.broadcasted_iota(jnp.int32, (TQ, TK), 0)
    col = lax.broadcasted_iota(jnp.int32, (TQ, TK), 1)
    acc_ref[...] = jnp.zeros(acc_ref.shape, F32)

    def block(kb, survs, diag):
        start = pl.multiple_of(kb * TK, TK)
        r0 = 0 if diag is None else diag * TK
        if diag is not None:
            valid = (col + r0 < row)[r0:]
        out = []
        for h in range(n_heads):
            k = k_ref[0, pl.ds(start, TK), pair_lanes(h)]
            v = v_ref[0, pl.ds(start, TK), pair_lanes(h)]
            z = _dot_nt(qh[h][r0:], k)
            zb = z.astype(BF16)
            sp = jnp.maximum(zb, 0) + jnp.log(1 + jnp.exp(-jnp.abs(zb)))
            if diag is not None:
                sp = jnp.where(valid, sp, jnp.zeros_like(sp))
            r = _dot(sp, u)
            arg = z + r + survs[h][r0:]
            if diag is not None:
                arg = jnp.where(valid, arg, -jnp.inf)
            acc_ref[h, r0:, :] += _dot(jnp.exp(arg).astype(BF16), v)
            surv = survs[h][r0:] + r[:, 0:1]
            out.append(surv if r0 == 0 else jnp.concatenate([survs[h][:r0], surv], axis=0))
        return tuple(out)

    survs = tuple(jnp.zeros((TQ, 1), F32) for _ in range(n_heads))
    n_diag = TQ // TK
    for d in reversed(range(n_diag)):
        survs = block(qi * n_diag + d, survs, d)
    def full_blocks(i, c):
        for d in range(n_diag):
            c = block((qi - i) * n_diag - 1 - d, c, None)
        return c

    lax.fori_loop(0, qi, full_blocks, survs)
    for p in range(PAIRS_PER_STEP):
        o_ref[0, :, pl.ds(p * LANES, LANES)] = jnp.where(first, acc_ref[2 * p], acc_ref[2 * p + 1]).astype(BF16)


def _attention(q, k, v, umat):
    B, S, _ = q.shape
    width = PAIRS_PER_STEP * LANES
    kv_spec = pl.BlockSpec((1, S, width), lambda b, p, i: (b, 0, p))
    q_spec = pl.BlockSpec((1, TQ, width), lambda b, p, i: (b, i, p))
    return pl.pallas_call(
        _attn_kernel,
        grid=(B, D_ATTN // width, S // TQ),
        in_specs=[q_spec, kv_spec, kv_spec, pl.BlockSpec((TK, TK), lambda b, p, i: (0, 0))],
        out_specs=q_spec,
        out_shape=jax.ShapeDtypeStruct((B, S, D_ATTN), BF16),
        scratch_shapes=[pltpu.VMEM((2 * PAIRS_PER_STEP, TQ, LANES), F32)],
        compiler_params=pltpu.CompilerParams(
            dimension_semantics=("arbitrary", "arbitrary", "arbitrary"),
            vmem_limit_bytes=VMEM_LIMIT),
        name="stickbreak_attn",
    )(q, k, v, umat)


def _route_t(scores, biased):
    ts = scores.shape[-1]
    neg = -jnp.inf
    b3 = biased.reshape(N_GROUPS, GROUP_SIZE, ts)
    e_in_g = lax.broadcasted_iota(jnp.int32, b3.shape, 1)
    m1 = jnp.max(b3, axis=1, keepdims=True)
    i1 = jnp.min(jnp.where(b3 == m1, e_in_g, GROUP_SIZE), axis=1, keepdims=True)
    m2 = jnp.max(jnp.where(e_in_g == i1, neg, b3), axis=1, keepdims=True)
    gs = (m1 + m2)[:, 0, :]
    g_iota = lax.broadcasted_iota(jnp.int32, gs.shape, 0)
    g_sel = jnp.zeros(gs.shape, jnp.bool_)
    for _ in range(TOPK_GROUPS):
        gm = jnp.max(gs, axis=0, keepdims=True)
        gi = jnp.min(jnp.where(gs == gm, g_iota, N_GROUPS), axis=0, keepdims=True)
        pick = g_iota == gi
        g_sel = jnp.logical_or(g_sel, pick)
        gs = jnp.where(pick, neg, gs)
    masked = jnp.where(g_sel[:, None, :], b3, neg)
    flat = lax.broadcasted_iota(jnp.int32, b3.shape, 0) * GROUP_SIZE + e_in_g
    sel = jnp.zeros(b3.shape, jnp.bool_)
    for _ in range(TOP_K):
        m = jnp.max(jnp.max(masked, axis=1, keepdims=True), axis=0, keepdims=True)
        cand = jnp.where(masked == m, flat, N_EXPERTS)
        idx = jnp.min(jnp.min(cand, axis=1, keepdims=True), axis=0, keepdims=True)
        pick = flat == idx
        sel = jnp.logical_or(sel, pick)
        masked = jnp.where(pick, neg, masked)
    s3 = scores.reshape(N_GROUPS, GROUP_SIZE, ts)
    w = jnp.where(sel, s3, 0.0)
    tot = jnp.sum(jnp.sum(w, axis=1, keepdims=True), axis=0, keepdims=True)
    return (w / tot * ROUTED_SCALE).reshape(N_EXPERTS, ts), sel.reshape(N_EXPERTS, ts)


def _pack_bf16_pair(lo, hi):
    lo_bits = pltpu.bitcast(lo.astype(BF16).astype(F32), jnp.uint32) >> 16
    hi_bits = pltpu.bitcast(hi.astype(BF16).astype(F32), jnp.uint32) & jnp.uint32(0xFFFF0000)
    return lo_bits | hi_bits


def _unpack_bf16_pair(p):
    lo = pltpu.bitcast(p << 16, F32).astype(BF16)
    hi = pltpu.bitcast(p & jnp.uint32(0xFFFF0000), F32).astype(BF16)
    return lo, hi


def _outproj_kernel(x_ref, a_ref, p_ref, mod_ref, wo32_ref, g2_ref, wrt_ref, rb_ref, ut_ref,
                    x1_ref, hp_ref, rank_ref, wd_ref, cnt_ref, run_ref, wo_ref):
    @pl.when(_first_grid_step())
    def _():
        run_ref[...] = jnp.zeros(run_ref.shape, jnp.int32)
        wo_ref[...] = wo32_ref[...].astype(BF16)

    gate1 = mod_ref[0, 2:3, :]
    shift2 = mod_ref[0, 3:4, :]
    scale2 = mod_ref[0, 4:5, :]
    mixp = _dot(a_ref[0], wo_ref[0:D_ATTN, :]) + _dot(p_ref[0], wo_ref[D_ATTN:, :])
    x1 = x_ref[0] + gate1 * mixp
    x1_ref[0] = x1
    h2 = _rms_mod(x1, g2_ref[...], scale2, shift2)
    half = h2.shape[-1] // 2
    hp_ref[0] = _pack_bf16_pair(h2[:, :half], h2[:, half:])
    logits_t = _dot3_nt(wrt_ref[...], h2)
    scores = 1.0 / (1.0 + jnp.exp(-logits_t))
    wd_t, sel = _route_t(scores, scores + rb_ref[...])
    wd_ref[...] = wd_t
    self_f = jnp.where(sel, 1.0, 0.0)
    before = _dot(self_f.astype(BF16), ut_ref[...]).astype(jnp.int32)
    run = run_ref[:, 0:1]
    rank_ref[...] = jnp.where(sel, run + before, -1)
    run_ref[...] = run_ref[...] + jnp.sum(self_f, axis=1, keepdims=True).astype(jnp.int32)
    cnt_ref[...] = run_ref[...]


def _outproj(x, attn, pool, mod3, w_out, g2, wr_t, rbias, ut, b0, B):
    _, S, D = x.shape
    ts = TS_PROJ
    n_s = S // ts
    const2 = lambda b, s: (0, 0)
    tok_in = lambda w: pl.BlockSpec((1, ts, w), lambda b, s: (b + b0, s, 0))
    tok = lambda w: pl.BlockSpec((1, ts, w), lambda b, s: (b, s, 0))
    tok_t = pl.BlockSpec((N_EXPERTS, ts), lambda b, s: (0, b * n_s + s))
    return pl.pallas_call(
        _outproj_kernel,
        grid=(B, n_s),
        in_specs=[tok_in(D), tok_in(D_ATTN), tok_in(D_POOL),
                  pl.BlockSpec((1, N_MOD, D), lambda b, s: (b + b0, 0, 0)),
                  pl.BlockSpec(w_out.shape, const2),
                  pl.BlockSpec((1, D), const2),
                  pl.BlockSpec(wr_t.shape, const2),
                  pl.BlockSpec((N_EXPERTS, 1), const2),
                  pl.BlockSpec((ts, ts), const2)],
        out_specs=[tok(D), tok(D // 2), tok_t, tok_t, pl.BlockSpec((N_EXPERTS, LANES), const2)],
        out_shape=[jax.ShapeDtypeStruct((B, S, D), F32),
                   jax.ShapeDtypeStruct((B, S, D // 2), jnp.uint32),
                   jax.ShapeDtypeStruct((N_EXPERTS, B * S), jnp.int32),
                   jax.ShapeDtypeStruct((N_EXPERTS, B * S), F32),
                   jax.ShapeDtypeStruct((N_EXPERTS, LANES), jnp.int32)],
        scratch_shapes=[pltpu.VMEM((N_EXPERTS, LANES), jnp.int32), pltpu.VMEM(w_out.shape, BF16)],
        compiler_params=pltpu.CompilerParams(dimension_semantics=("arbitrary", "arbitrary"),
                                             vmem_limit_bytes=VMEM_LIMIT),
        name="outproj_router",
    )(x, attn, pool, mod3, w_out, g2, wr_t, rbias, ut)


def _n_row_blocks(n_tokens):
    n_blocks = -(-(n_tokens * TOP_K + N_EXPERTS * (ROW_BLOCK - 1)) // ROW_BLOCK)
    return -(-n_blocks // BLOCKS_PER_STEP) * BLOCKS_PER_STEP


def _slots_kernel(rank_ref, wd_ref, cnt_ref, lt_ref, slot_ref, wk_ref, be_ref, nu_ref):
    lt = lt_ref[...]
    nblk = lax.shift_right_logical(cnt_ref[...] + (ROW_BLOCK - 1), ROW_BLOCK_LOG2).astype(F32)
    nb_hi, nb_lo = _split_bf16(nblk)
    blk_start = _dot(lt, nb_hi) + _dot(lt, nb_lo)

    @pl.when(pl.program_id(0) == 0)
    def _():
        blk_end = (blk_start + nblk)[:, 0:1]
        b_iota = lax.broadcasted_iota(jnp.int32, (N_EXPERTS, be_ref.shape[-1]), 1).astype(F32)
        owner = jnp.sum(jnp.where(blk_end <= b_iota, 1, 0), axis=0, keepdims=True)
        be_ref[...] = jnp.minimum(owner, N_EXPERTS - 1)
        nu_ref[...] = jnp.broadcast_to(blk_end[N_EXPERTS - 1:, :].astype(jnp.int32), nu_ref.shape)

    rank = rank_ref[...]
    sel = rank >= 0
    row_start = (blk_start[:, 0:1] * ROW_BLOCK).astype(jnp.int32)
    slot_d = row_start + rank
    wd = wd_ref[...]
    choice = _dot(lt, jnp.where(sel, 1.0, 0.0).astype(BF16)).astype(jnp.int32)
    ts = rank.shape[-1]
    slots, wks = [], []
    for k in range(TOP_K):
        m = jnp.logical_and(sel, choice == k)
        slots.append(jnp.sum(jnp.where(m, slot_d, 0), axis=0, keepdims=True))
        wks.append(jnp.sum(jnp.where(m, wd, 0.0), axis=0, keepdims=True))
    slot_ref[...] = jnp.concatenate(slots + [jnp.zeros((SUBLANES - TOP_K, ts), jnp.int32)], axis=0)
    wk_pad = jnp.concatenate(wks + [jnp.zeros((LANES - TOP_K, ts), F32)], axis=0)
    wk_ref[...] = wk_pad.T


def _slots(rank_t, wd_t, cnt, lt):
    T = rank_t.shape[1]
    ts = TS_SLOT
    nb_pad = -(-_n_row_blocks(T) // LANES) * LANES
    const = lambda i: (0, 0)
    tok_t = pl.BlockSpec((N_EXPERTS, ts), lambda i: (0, i))
    return pl.pallas_call(
        _slots_kernel,
        grid=(T // ts,),
        in_specs=[tok_t, tok_t, pl.BlockSpec((N_EXPERTS, LANES), const), pl.BlockSpec((N_EXPERTS, N_EXPERTS), const)],
        out_specs=[pl.BlockSpec((SUBLANES, ts), lambda i: (0, i)),
                   pl.BlockSpec((ts, LANES), lambda i: (i, 0)),
                   pl.BlockSpec((1, nb_pad), const),
                   pl.BlockSpec((1, LANES), const)],
        out_shape=[jax.ShapeDtypeStruct((SUBLANES, T), jnp.int32),
                   jax.ShapeDtypeStruct((T, LANES), F32),
                   jax.ShapeDtypeStruct((1, nb_pad), jnp.int32),
                   jax.ShapeDtypeStruct((1, LANES), jnp.int32)],
        compiler_params=pltpu.CompilerParams(dimension_semantics=("arbitrary",), vmem_limit_bytes=VMEM_LIMIT),
        name="slots",
    )(rank_t, wd_t, cnt, lt)


def _sc_mesh():
    return plsc.VectorSubcoreMesh(core_axis_name="c", subcore_axis_name="s",
                                  num_cores=SC_CORES, num_subcores=SC_SUBCORES)


def _sc_dispatch(rows, slot_flat, n_out):
    T, width = rows.shape
    n_workers = SC_CORES * SC_SUBCORES
    per_worker = T // n_workers
    steps = per_worker // SC_CHUNK

    @functools.partial(
        pl.kernel, mesh=_sc_mesh(),
        out_type=jax.ShapeDtypeStruct((n_out, width), rows.dtype),
        scratch_types=[pltpu.VMEM((SC_CHUNK, width), rows.dtype)]
        + [pltpu.VMEM((SC_CHUNK,), jnp.int32)] * TOP_K + [pltpu.SemaphoreType.DMA],
        name="dispatch",
    )
    def run(rows_hbm, slot_hbm, out_hbm, rows_v, *rest):
        idx_v, sem = rest[:TOP_K], rest[TOP_K]
        base = (lax.axis_index("s") * SC_CORES + lax.axis_index("c")) * per_worker

        @pl.loop(0, steps)
        def _(i):
            off = base + i * SC_CHUNK
            pltpu.sync_copy(rows_hbm.at[pl.ds(off, SC_CHUNK)], rows_v)
            for k in range(TOP_K):
                pltpu.sync_copy(slot_hbm.at[pl.ds(k * T + off, SC_CHUNK)], idx_v[k])
            copies = [pltpu.async_copy(rows_v, out_hbm.at[idx_v[k]], sem) for k in range(TOP_K)]
            for cp in copies:
                cp.wait()

    return run(rows, slot_flat)


def _sc_gather(rows, idx):
    n = idx.shape[0]
    width = rows.shape[1]
    n_workers = SC_CORES * SC_SUBCORES
    per_worker = n // n_workers
    steps = per_worker // SC_CHUNK

    assert steps % 2 == 0
    slot_types = [pltpu.VMEM((SC_CHUNK,), jnp.int32), pltpu.VMEM((SC_CHUNK, width), rows.dtype),
                  pltpu.SemaphoreType.DMA]

    @functools.partial(
        pl.kernel, mesh=_sc_mesh(),
        out_type=jax.ShapeDtypeStruct((n, width), rows.dtype),
        scratch_types=slot_types * 2,
        name="combine",
    )
    def run(rows_hbm, idx_hbm, out_hbm, *scratch):
        base = (lax.axis_index("s") * SC_CORES + lax.axis_index("c")) * per_worker
        slots = (scratch[0:3], scratch[3:6])

        def gather(slot):
            idx_v, rows_v, sem = slots[slot]
            return pltpu.make_async_copy(rows_hbm.at[idx_v], rows_v, sem)

        def start(chunk, slot):
            pltpu.sync_copy(idx_hbm.at[pl.ds(base + chunk * SC_CHUNK, SC_CHUNK)], slots[slot][0])
            gather(slot).start()

        def finish(chunk, slot):
            gather(slot).wait()
            pltpu.sync_copy(slots[slot][1], out_hbm.at[pl.ds(base + chunk * SC_CHUNK, SC_CHUNK)])

        start(0, 0)

        @pl.loop(0, steps, step=2)
        def _(chunk):
            start(chunk + 1, 1)
            finish(chunk, 0)

            @pl.when(chunk + 2 < steps)
            def _():
                start(chunk + 2, 0)

            finish(chunk + 1, 1)

    return run(rows, idx)


def _swiglu_packed(xp, wgu, wd):
    lo, hi = _unpack_bf16_pair(xp)
    half = xp.shape[-1]
    gu = _dot(lo, wgu[:half, :]) + _dot(hi, wgu[half:, :])
    dh = wgu.shape[-1] // 2
    act = _silu(gu[:, :dh]) * gu[:, dh:]
    return _dot(act.astype(BF16), wd)


def _cast_swiglu_weights(wg32_ref, wu32_ref, wd32_ref, wgu_ref, wd_ref):
    dh = wg32_ref.shape[-1]
    wgu_ref[:, :dh] = wg32_ref[...].reshape(wg32_ref.shape[-2:]).astype(BF16)
    wgu_ref[:, dh:] = wu32_ref[...].reshape(wu32_ref.shape[-2:]).astype(BF16)
    wd_ref[...] = wd32_ref[...].reshape(wd32_ref.shape[-2:]).astype(BF16)


def _experts_kernel(be_ref, nu_ref, x_ref, *refs):
    n = BLOCKS_PER_STEP
    w32 = [refs[3 * j:3 * j + 3] for j in range(n)]
    y_ref = refs[3 * n]
    wbf = [refs[3 * n + 1 + 2 * j:3 * n + 3 + 2 * j] for j in range(n)]
    step = pl.program_id(0)
    for j in range(n):
        blk = step * n + j
        new_expert = jnp.logical_or(step == 0, be_ref[blk] != be_ref[jnp.maximum(blk - n, 0)])

        @pl.when(new_expert)
        def _():
            _cast_swiglu_weights(*w32[j], *wbf[j])

    @pl.when(step * n < nu_ref[0])
    def _():
        for j in range(n):
            rows = pl.ds(j * ROW_BLOCK, ROW_BLOCK)
            y = _swiglu_packed(x_ref[rows, :], wbf[j][0][...], wbf[j][1][...])
            half = y.shape[-1] // 2
            y_ref[rows, :] = _pack_bf16_pair(y[:, :half], y[:, half:])


def _experts(xs, block_expert, n_used, w_gate, w_up, w_down):
    P, half = xs.shape
    _, D, dh = w_gate.shape
    n = BLOCKS_PER_STEP
    rows = n * ROW_BLOCK
    assert P % rows == 0

    def by_expert(shape, j):
        return pl.BlockSpec((1,) + shape, lambda s, be, nu: (be[s * n + j], 0, 0))

    w_specs, w_args = [], []
    for j in range(n):
        w_specs += [by_expert((D, dh), j), by_expert((D, dh), j), by_expert((dh, D), j)]
        w_args += [w_gate, w_up, w_down]
    def row_step(s, be, nu):
        return jnp.minimum(s, lax.div(jnp.maximum(nu[0], 1) - 1, n)), 0

    grid_spec = pltpu.PrefetchScalarGridSpec(
        num_scalar_prefetch=2,
        grid=(P // rows,),
        in_specs=[pl.BlockSpec((rows, half), row_step)] + w_specs,
        out_specs=pl.BlockSpec((rows, half), row_step),
        scratch_shapes=[pltpu.VMEM((D, 2 * dh), BF16), pltpu.VMEM((dh, D), BF16)] * n,
    )
    return pl.pallas_call(
        _experts_kernel,
        grid_spec=grid_spec,
        out_shape=jax.ShapeDtypeStruct((P, half), jnp.uint32),
        compiler_params=pltpu.CompilerParams(dimension_semantics=("arbitrary",), vmem_limit_bytes=VMEM_LIMIT),
        name="experts",
    )(block_expert, n_used, xs, *w_args)


def _final_kernel(x1_ref, hp_ref, g_ref, wk_ref, mod_ref, sg32_ref, su32_ref, sd32_ref, *rest):
    o_ref, sgu_ref, sd_ref = rest[-3:]

    @pl.when(_first_grid_step())
    def _():
        _cast_swiglu_weights(sg32_ref, su32_ref, sd32_ref, sgu_ref, sd_ref)

    acc = _swiglu_packed(hp_ref[0], sgu_ref[...], sd_ref[...])
    wk = wk_ref[0]
    for k in range(TOP_K):
        lo, hi = _unpack_bf16_pair(g_ref[k, 0])
        y = jnp.concatenate([lo.astype(F32), hi.astype(F32)], axis=-1)
        acc = acc + wk[:, k:k + 1] * y
    gate2 = mod_ref[0, 5:6, :]
    o_ref[0] = x1_ref[0] + gate2 * acc


def _final(x1, hp, g, wk_tok, mod3, ws_gate, ws_up, ws_down, prev_out, b0, b_total):
    B, S, D = x1.shape
    dh = ws_gate.shape[-1]
    ts = TS_PROJ
    tok = lambda w: pl.BlockSpec((1, ts, w), lambda b, s: (b, s, 0))
    const2 = lambda b, s: (0, 0)
    in_specs = [tok(D), tok(D // 2),
                pl.BlockSpec((TOP_K, 1, ts, D // 2), lambda b, s: (0, b, s, 0)),
                tok(LANES),
                pl.BlockSpec((1, N_MOD, D), lambda b, s: (b + b0, 0, 0)),
                pl.BlockSpec(ws_gate.shape, const2),
                pl.BlockSpec(ws_up.shape, const2),
                pl.BlockSpec(ws_down.shape, const2)]
    args = [x1, hp, g, wk_tok, mod3, ws_gate, ws_up, ws_down]
    aliases = {}
    if prev_out is not None:
        in_specs.append(pl.BlockSpec(memory_space=pl.ANY))
        args.append(prev_out)
        aliases = {len(args) - 1: 0}
    return pl.pallas_call(
        _final_kernel,
        grid=(B, S // ts),
        in_specs=in_specs,
        out_specs=pl.BlockSpec((1, ts, D), lambda b, s: (b + b0, s, 0)),
        out_shape=jax.ShapeDtypeStruct((b_total, S, D), F32),
        scratch_shapes=[pltpu.VMEM((D, 2 * dh), BF16), pltpu.VMEM((dh, D), BF16)],
        input_output_aliases=aliases,
        compiler_params=pltpu.CompilerParams(dimension_semantics=("arbitrary", "arbitrary"),
                                             vmem_limit_bytes=VMEM_LIMIT),
        name="final",
    )(*args)


def _layer(x, c_act_mod, norm1_g, norm2_g, w_in, q_norm_g, k_norm_g, w_pool, pool_scale, w_out,
           w_router, router_bias, w_gate, w_up, w_down, ws_gate, ws_up, ws_down):
    B, S, D = x.shape
    mod3 = c_act_mod.reshape(B, N_MOD, D)
    head_of = jnp.arange(D_ATTN, dtype=jnp.int32) // HEAD_DIM
    head_mean = jnp.where(head_of[:, None] == head_of[None, :], 1.0 / HEAD_DIM, 0.0).astype(BF16)
    j = jnp.arange(TK, dtype=jnp.int32)
    umat = jnp.where(j[:, None] >= j[None, :], -1.0, 0.0).astype(BF16)

    q, k, v, pool = _inproj(
        x, mod3, norm1_g.reshape(1, D), w_in,
        jnp.tile(q_norm_g, N_HEADS).reshape(1, D_ATTN), jnp.tile(k_norm_g, N_HEADS).reshape(1, D_ATTN),
        head_mean, w_pool, pool_scale.reshape(1, D_POOL))
    attn = _attention(q, k, v, umat)
    t = jnp.arange(TS_PROJ, dtype=jnp.int32)
    ut = (t[:, None] < t[None, :]).astype(BF16)
    e = jnp.arange(N_EXPERTS, dtype=jnp.int32)
    lt = (e[None, :] < e[:, None]).astype(BF16)
    bp = B // MOE_PARTS
    T = bp * S
    n_blocks = _n_row_blocks(T)
    out = None
    for part in range(MOE_PARTS):
        b0 = part * bp
        x1, hp, rank_t, wd_t, cnt = _outproj(x, attn, pool, mod3, w_out, norm2_g.reshape(1, D),
                                             w_router.T, router_bias.reshape(N_EXPERTS, 1), ut, b0, bp)
        slots, wk_tok, block_expert, n_used = _slots(rank_t, wd_t, cnt, lt)
        slot_flat = slots[:TOP_K].reshape(TOP_K * T)
        xs = _sc_dispatch(hp.reshape(T, D // 2), slot_flat, n_blocks * ROW_BLOCK)
        ys = _experts(xs, block_expert[0, :n_blocks], n_used[0, :1], w_gate, w_up, w_down)
        g = _sc_gather(ys, slot_flat).reshape(TOP_K, bp, S, D // 2)
        out = _final(x1, hp, g, wk_tok.reshape(bp, S, LANES), mod3, ws_gate, ws_up, ws_down, out, b0, B)
    return out


def kernel(x, c, w_ada, b_ada, norm1_g, norm2_g, w_in, q_norm_g, k_norm_g, w_pool, pool_scale, w_out,
           w_router, router_bias, w_gate, w_up, w_down, ws_gate, ws_up, ws_down):
    depth = w_ada.shape[0]
    for l in range(depth):
        mod = _adaln(c, w_ada[l], b_ada[l])
        x = _layer(x, mod, norm1_g[l], norm2_g[l], w_in[l], q_norm_g[l], k_norm_g[l], w_pool[l],
                   pool_scale[l], w_out[l], w_router[l], router_bias[l], w_gate[l], w_up[l], w_down[l],
                   ws_gate[l], ws_up[l], ws_down[l])
    return x
```

```python
import functools

import jax
import jax.numpy as jnp
from jax import lax
from jax.experimental import pallas as pl
from jax.experimental.pallas import tpu as pltpu
from jax.experimental.pallas import tpu_sc as plsc

F32 = jnp.float32
BF16 = jnp.bfloat16

HEAD_DIM = 64
N_HEADS = 8
D_ATTN = N_HEADS * HEAD_DIM
POOL_WINDOWS = (2, 4, 8, 16)
POOL_GROUP_DIM = 128
D_POOL = len(POOL_WINDOWS) * POOL_GROUP_DIM
MAX_WINDOW = max(POOL_WINDOWS)
assert all(w & (w - 1) == 0 for w in POOL_WINDOWS)
N_EXPERTS = 64
TOP_K = 6
N_GROUPS = 8
GROUP_SIZE = N_EXPERTS // N_GROUPS
TOPK_GROUPS = 4
ROUTED_SCALE = 2.5
RMS_EPS = 1e-6
N_MOD = 6

LANES = 128
SUBLANES = 8
VMEM_LIMIT = 56 * 1024 * 1024

TS_PROJ = 1024
TQ = 1024
TK = 256
PAIRS_PER_STEP = 4
TS_SLOT = 2048
ROW_BLOCK_LOG2 = 9
ROW_BLOCK = 1 << ROW_BLOCK_LOG2
BLOCKS_PER_STEP = 2
MOE_PARTS = 2
SC_CORES = 2
SC_SUBCORES = 16
SC_CHUNK = 64


def _split_bf16(a):
    hi = a.astype(BF16)
    lo = (a - hi.astype(F32)).astype(BF16)
    return hi, lo


def _dot(a, b):
    return jnp.dot(a, b, preferred_element_type=F32)


def _dot_nt(a, b):
    return lax.dot_general(a, b, (((1,), (1,)), ((), ())), preferred_element_type=F32)


def _dot3(a, b):
    ah, al = _split_bf16(a)
    bh, bl = _split_bf16(b)
    return _dot(ah, bh) + _dot(ah, bl) + _dot(al, bh)


def _dot3_nt(a, b):
    ah, al = _split_bf16(a)
    bh, bl = _split_bf16(b)
    return _dot_nt(ah, bh) + _dot_nt(ah, bl) + _dot_nt(al, bh)


def _silu(x):
    return x * (1.0 / (1.0 + jnp.exp(-x)))


def _rms_mod(x, gain, scale, shift):
    ms = jnp.mean(x * x, axis=-1, keepdims=True)
    y = x * lax.rsqrt(ms + RMS_EPS) * gain
    return y * (1.0 + scale) + shift


def _adaln_kernel(c_ref, w_ref, b_ref, o_ref):
    c = c_ref[...]
    o_ref[...] = _dot3(_silu(c), w_ref[...]) + b_ref[...]


def _adaln(c, w_ada, b_ada):
    nb, D = c.shape
    B = -(-nb // SUBLANES) * SUBLANES
    c = jnp.pad(c, ((0, B - nb), (0, 0)))
    N = w_ada.shape[1]
    tn = 1024
    out = pl.pallas_call(
        _adaln_kernel,
        grid=(N // tn,),
        in_specs=[pl.BlockSpec((B, D), lambda j: (0, 0)),
                  pl.BlockSpec((D, tn), lambda j: (0, j)),
                  pl.BlockSpec((1, tn), lambda j: (0, j))],
        out_specs=pl.BlockSpec((B, tn), lambda j: (0, j)),
        out_shape=jax.ShapeDtypeStruct((B, N), F32),
        compiler_params=pltpu.CompilerParams(dimension_semantics=("arbitrary",),
                                             vmem_limit_bytes=VMEM_LIMIT),
        name="adaln",
    )(c, w_ada, b_ada.reshape(1, N))
    return out[:nb]


def _first_grid_step():
    return jnp.logical_and(pl.program_id(0) == 0, pl.program_id(1) == 0)


def _inproj_kernel(x_ref, mod_ref, g1_ref, win32_ref, qg_ref, kg_ref, hm_ref, wp32_ref, ps_ref,
                   q_ref, k_ref, v_ref, p_ref, ext_ref, win_ref, wp_ref):
    @pl.when(_first_grid_step())
    def _():
        win_ref[...] = win32_ref[...].astype(BF16)
        wp_ref[...] = wp32_ref[...].astype(BF16)

    si = pl.program_id(1)
    ts = x_ref.shape[1]
    x = x_ref[0]
    shift1 = mod_ref[0, 0:1, :]
    scale1 = mod_ref[0, 1:2, :]
    h = _rms_mod(x, g1_ref[...], scale1, shift1)
    proj = _dot(h.astype(BF16), win_ref[...])

    hm = hm_ref[...]

    def head_norm(t, gain):
        ms = _dot((t * t).astype(BF16), hm)
        return t * lax.rsqrt(ms + RMS_EPS) * gain

    hq = proj[:, 0:D_ATTN]
    hk = proj[:, D_ATTN:2 * D_ATTN]
    q_ref[0] = (head_norm(hq, qg_ref[...]) * (HEAD_DIM ** -0.5)).astype(BF16)
    k_ref[0] = head_norm(hk, kg_ref[...]).astype(BF16)
    v_ref[0] = proj[:, 2 * D_ATTN:3 * D_ATTN].astype(BF16)

    hp = proj[:, 3 * D_ATTN:]

    @pl.when(si == 0)
    def _():
        ext_ref[0:MAX_WINDOW, :] = jnp.zeros((MAX_WINDOW, D_POOL), F32)

    ext_ref[MAX_WINDOW:, :] = hp
    pos = si * ts + lax.broadcasted_iota(jnp.int32, (ts, 1), 0)
    for g, w in enumerate(POOL_WINDOWS):
        lo_l, hi_l = g * POOL_GROUP_DIM, (g + 1) * POOL_GROUP_DIM
        u = hp[:, lo_l:hi_l]
        run = ext_ref[:, lo_l:hi_l]
        span = 1
        while span < w:
            run = run + pltpu.roll(run, span, axis=0)
            span *= 2
        acc = run[MAX_WINDOW:]
        count = jnp.minimum(pos + 1, w).astype(F32)
        d = acc / count - u
        mixed = _dot(d.astype(BF16), wp_ref[g])
        p_ref[0, :, lo_l:hi_l] = (mixed * ps_ref[:, lo_l:hi_l]).astype(BF16)
    ext_ref[0:MAX_WINDOW, :] = hp[ts - MAX_WINDOW:, :]


def _inproj(x, mod3, g1, w_in, qg_t, kg_t, head_mean, w_pool, pool_scale):
    B, S, D = x.shape
    ts = TS_PROJ
    out_sd = jax.ShapeDtypeStruct((B, S, D_ATTN), BF16)
    blk = pl.BlockSpec((1, ts, D_ATTN), lambda b, s: (b, s, 0))
    const2 = lambda b, s: (0, 0)
    return pl.pallas_call(
        _inproj_kernel,
        grid=(B, S // ts),
        in_specs=[pl.BlockSpec((1, ts, D), lambda b, s: (b, s, 0)),
                  pl.BlockSpec((1, N_MOD, D), lambda b, s: (b, 0, 0)),
                  pl.BlockSpec((1, D), const2),
                  pl.BlockSpec(w_in.shape, const2),
                  pl.BlockSpec((1, D_ATTN), const2),
                  pl.BlockSpec((1, D_ATTN), const2),
                  pl.BlockSpec((D_ATTN, D_ATTN), const2),
                  pl.BlockSpec(w_pool.shape, lambda b, s: (0, 0, 0)),
                  pl.BlockSpec((1, D_POOL), const2)],
        out_specs=[blk, blk, blk, blk],
        out_shape=[out_sd, out_sd, out_sd, out_sd],
        scratch_shapes=[pltpu.VMEM((MAX_WINDOW + ts, D_POOL), F32),
                        pltpu.VMEM(w_in.shape, BF16), pltpu.VMEM(w_pool.shape, BF16)],
        compiler_params=pltpu.CompilerParams(dimension_semantics=("arbitrary", "arbitrary"),
                                             vmem_limit_bytes=VMEM_LIMIT),
        name="inproj",
    )(x, mod3, g1, w_in, qg_t, kg_t, head_mean, w_pool, pool_scale)


def _attn_kernel(q_ref, k_ref, v_ref, u_ref, o_ref, acc_ref):
    qi = pl.program_id(2)
    lane = lax.broadcasted_iota(jnp.int32, (TQ, LANES), 1)
    first = lane < HEAD_DIM
    n_heads = 2 * PAIRS_PER_STEP
    pair_lanes = lambda h: pl.ds((h // 2) * LANES, LANES)
    qh = []
    for h in range(n_heads):
        q = q_ref[0, :, pair_lanes(h)]
        keep = first if h % 2 == 0 else jnp.logical_not(first)
        qh.append(jnp.where(keep, q, jnp.zeros_like(q)))
    u = u_ref[...]
    row = lax.broadcasted_iota(jnp.int32, (TQ, TK), 0)
    col = lax.broadcasted_iota(jnp.int32, (TQ, TK), 1)
    acc_ref[...] = jnp.zeros(acc_ref.shape, F32)

    def block(kb, survs, diag):
        start = pl.multiple_of(kb * TK, TK)
        r0 = 0 if diag is None else diag * TK
        if diag is not None:
            valid = (col + r0 < row)[r0:]
        out = []
        for h in range(n_heads):
            k = k_ref[0, pl.ds(start, TK), pair_lanes(h)]
            v = v_ref[0, pl.ds(start, TK), pair_lanes(h)]
            z = _dot_nt(qh[h][r0:], k)
            zb = z.astype(BF16)
            sp = jnp.maximum(zb, 0) + jnp.log(1 + jnp.exp(-jnp.abs(zb)))
            if diag is not None:
                sp = jnp.where(valid, sp, jnp.zeros_like(sp))
            r = _dot(sp, u)
            arg = z + r + survs[h][r0:]
            if diag is not None:
                arg = jnp.where(valid, arg, -jnp.inf)
            acc_ref[h, r0:, :] += _dot(jnp.exp(arg).astype(BF16), v)
            surv = survs[h][r0:] + r[:, 0:1]
            out.append(surv if r0 == 0 else jnp.concatenate([survs[h][:r0], surv], axis=0))
        return tuple(out)

    survs = tuple(jnp.zeros((TQ, 1), F32) for _ in range(n_heads))
    n_diag = TQ // TK
    for d in reversed(range(n_diag)):
        survs = block(qi * n_diag + d, survs, d)
    def full_blocks(i, c):
        for d in range(n_diag):
            c = block((qi - i) * n_diag - 1 - d, c, None)
        return c

    lax.fori_loop(0, qi, full_blocks, survs)
    for p in range(PAIRS_PER_STEP):
        o_ref[0, :, pl.ds(p * LANES, LANES)] = jnp.where(first, acc_ref[2 * p], acc_ref[2 * p + 1]).astype(BF16)


def _attention(q, k, v, umat):
    B, S, _ = q.shape
    width = PAIRS_PER_STEP * LANES
    kv_spec = pl.BlockSpec((1, S, width), lambda b, p, i: (b, 0, p))
    q_spec = pl.BlockSpec((1, TQ, width), lambda b, p, i: (b, i, p))
    return pl.pallas_call(
        _attn_kernel,
        grid=(B, D_ATTN // width, S // TQ),
        in_specs=[q_spec, kv_spec, kv_spec, pl.BlockSpec((TK, TK), lambda b, p, i: (0, 0))],
        out_specs=q_spec,
        out_shape=jax.ShapeDtypeStruct((B, S, D_ATTN), BF16),
        scratch_shapes=[pltpu.VMEM((2 * PAIRS_PER_STEP, TQ, LANES), F32)],
        compiler_params=pltpu.CompilerParams(
            dimension_semantics=("arbitrary", "arbitrary", "arbitrary"),
            vmem_limit_bytes=VMEM_LIMIT),
        name="stickbreak_attn",
    )(q, k, v, umat)


def _all_sublanes(x, op, axis):
    assert x.shape[axis] == SUBLANES
    for shift in (1, 2, 4):
        x = op(x, pltpu.roll(x, shift, axis=axis))
    return x


def _route_t(scores, biased):
    ts = scores.shape[-1]
    neg = -jnp.inf
    b3 = biased.reshape(N_GROUPS, GROUP_SIZE, ts)
    e_in_g = lax.broadcasted_iota(jnp.int32, b3.shape, 1)
    m1 = _all_sublanes(b3, jnp.maximum, 1)
    i1 = _all_sublanes(jnp.where(b3 == m1, e_in_g, GROUP_SIZE), jnp.minimum, 1)
    m2 = _all_sublanes(jnp.where(e_in_g == i1, neg, b3), jnp.maximum, 1)
    gs = (m1 + m2)[:, 0, :]
    g_iota = lax.broadcasted_iota(jnp.int32, gs.shape, 0)
    g_sel = jnp.zeros(gs.shape, jnp.bool_)
    for _ in range(TOPK_GROUPS):
        gm = _all_sublanes(gs, jnp.maximum, 0)
        gi = _all_sublanes(jnp.where(gs == gm, g_iota, N_GROUPS), jnp.minimum, 0)
        pick = g_iota == gi
        g_sel = jnp.logical_or(g_sel, pick)
        gs = jnp.where(pick, neg, gs)
    masked = jnp.where(g_sel[:, None, :], b3, neg)
    flat = lax.broadcasted_iota(jnp.int32, b3.shape, 0) * GROUP_SIZE + e_in_g
    sel = jnp.zeros(b3.shape, jnp.bool_)
    for _ in range(TOP_K):
        m = jnp.max(_all_sublanes(masked, jnp.maximum, 1), axis=0, keepdims=True)
        cand = jnp.where(masked == m, flat, N_EXPERTS)
        idx = jnp.min(_all_sublanes(cand, jnp.minimum, 1), axis=0, keepdims=True)
        pick = flat == idx
        sel = jnp.logical_or(sel, pick)
        masked = jnp.where(pick, neg, masked)
    s3 = scores.reshape(N_GROUPS, GROUP_SIZE, ts)
    w = jnp.where(sel, s3, 0.0)
    tot = jnp.sum(_all_sublanes(w, jnp.add, 1), axis=0, keepdims=True)
    return (w / tot * ROUTED_SCALE).reshape(N_EXPERTS, ts), sel.reshape(N_EXPERTS, ts)


def _pack_bf16_pair(lo, hi):
    lo_bits = pltpu.bitcast(lo.astype(BF16).astype(F32), jnp.uint32) >> 16
    hi_bits = pltpu.bitcast(hi.astype(BF16).astype(F32), jnp.uint32) & jnp.uint32(0xFFFF0000)
    return lo_bits | hi_bits


def _unpack_bf16_pair(p):
    lo = pltpu.bitcast(p << 16, F32).astype(BF16)
    hi = pltpu.bitcast(p & jnp.uint32(0xFFFF0000), F32).astype(BF16)
    return lo, hi


def _outproj_kernel(x_ref, a_ref, p_ref, mod_ref, wo32_ref, g2_ref, wrt_ref, rb_ref, ut_ref,
                    x1_ref, hp_ref, rank_ref, wd_ref, cnt_ref, run_ref, wo_ref):
    @pl.when(_first_grid_step())
    def _():
        run_ref[...] = jnp.zeros(run_ref.shape, jnp.int32)
        wo_ref[...] = wo32_ref[...].astype(BF16)

    gate1 = mod_ref[0, 2:3, :]
    shift2 = mod_ref[0, 3:4, :]
    scale2 = mod_ref[0, 4:5, :]
    mixp = _dot(a_ref[0], wo_ref[0:D_ATTN, :]) + _dot(p_ref[0], wo_ref[D_ATTN:, :])
    x1 = x_ref[0] + gate1 * mixp
    x1_ref[0] = x1
    h2 = _rms_mod(x1, g2_ref[...], scale2, shift2)
    half = h2.shape[-1] // 2
    hp_ref[0] = _pack_bf16_pair(h2[:, :half], h2[:, half:])
    logits_t = _dot3_nt(wrt_ref[...], h2)
    scores = 1.0 / (1.0 + jnp.exp(-logits_t))
    wd_t, sel = _route_t(scores, scores + rb_ref[...])
    wd_ref[...] = wd_t
    self_f = jnp.where(sel, 1.0, 0.0)
    before = _dot(self_f.astype(BF16), ut_ref[...]).astype(jnp.int32)
    run = run_ref[:, 0:1]
    rank_ref[...] = jnp.where(sel, run + before, -1)
    run_ref[...] = run_ref[...] + jnp.sum(self_f, axis=1, keepdims=True).astype(jnp.int32)
    cnt_ref[...] = run_ref[...]


def _outproj(x, attn, pool, mod3, w_out, g2, wr_t, rbias, ut, b0, B):
    _, S, D = x.shape
    ts = TS_PROJ
    n_s = S // ts
    const2 = lambda b, s: (0, 0)
    tok_in = lambda w: pl.BlockSpec((1, ts, w), lambda b, s: (b + b0, s, 0))
    tok = lambda w: pl.BlockSpec((1, ts, w), lambda b, s: (b, s, 0))
    tok_t = pl.BlockSpec((N_EXPERTS, ts), lambda b, s: (0, b * n_s + s))
    return pl.pallas_call(
        _outproj_kernel,
        grid=(B, n_s),
        in_specs=[tok_in(D), tok_in(D_ATTN), tok_in(D_POOL),
                  pl.BlockSpec((1, N_MOD, D), lambda b, s: (b + b0, 0, 0)),
                  pl.BlockSpec(w_out.shape, const2),
                  pl.BlockSpec((1, D), const2),
                  pl.BlockSpec(wr_t.shape, const2),
                  pl.BlockSpec((N_EXPERTS, 1), const2),
                  pl.BlockSpec((ts, ts), const2)],
        out_specs=[tok(D), tok(D // 2), tok_t, tok_t, pl.BlockSpec((N_EXPERTS, LANES), const2)],
        out_shape=[jax.ShapeDtypeStruct((B, S, D), F32),
                   jax.ShapeDtypeStruct((B, S, D // 2), jnp.uint32),
                   jax.ShapeDtypeStruct((N_EXPERTS, B * S), jnp.int32),
                   jax.ShapeDtypeStruct((N_EXPERTS, B * S), F32),
                   jax.ShapeDtypeStruct((N_EXPERTS, LANES), jnp.int32)],
        scratch_shapes=[pltpu.VMEM((N_EXPERTS, LANES), jnp.int32), pltpu.VMEM(w_out.shape, BF16)],
        compiler_params=pltpu.CompilerParams(dimension_semantics=("arbitrary", "arbitrary"),
                                             vmem_limit_bytes=VMEM_LIMIT),
        name="outproj_router",
    )(x, attn, pool, mod3, w_out, g2, wr_t, rbias, ut)


def _n_row_blocks(n_tokens):
    n_blocks = -(-(n_tokens * TOP_K + N_EXPERTS * (ROW_BLOCK - 1)) // ROW_BLOCK)
    return -(-n_blocks // BLOCKS_PER_STEP) * BLOCKS_PER_STEP


def _slots_kernel(rank_ref, wd_ref, cnt_ref, lt_ref, slot_ref, wk_ref, be_ref, nu_ref):
    lt = lt_ref[...]
    nblk = lax.shift_right_logical(cnt_ref[...] + (ROW_BLOCK - 1), ROW_BLOCK_LOG2).astype(F32)
    nb_hi, nb_lo = _split_bf16(nblk)
    blk_start = _dot(lt, nb_hi) + _dot(lt, nb_lo)

    @pl.when(pl.program_id(0) == 0)
    def _():
        blk_end = (blk_start + nblk)[:, 0:1]
        b_iota = lax.broadcasted_iota(jnp.int32, (N_EXPERTS, be_ref.shape[-1]), 1).astype(F32)
        owner = jnp.sum(jnp.where(blk_end <= b_iota, 1, 0), axis=0, keepdims=True)
        be_ref[...] = jnp.minimum(owner, N_EXPERTS - 1)
        nu_ref[...] = jnp.broadcast_to(blk_end[N_EXPERTS - 1:, :].astype(jnp.int32), nu_ref.shape)

    rank = rank_ref[...]
    sel = rank >= 0
    row_start = (blk_start[:, 0:1] * ROW_BLOCK).astype(jnp.int32)
    slot_d = row_start + rank
    wd = wd_ref[...]
    choice = _dot(lt, jnp.where(sel, 1.0, 0.0).astype(BF16)).astype(jnp.int32)
    ts = rank.shape[-1]
    slots, wks = [], []
    for k in range(TOP_K):
        m = jnp.logical_and(sel, choice == k)
        slots.append(jnp.sum(jnp.where(m, slot_d, 0), axis=0, keepdims=True))
        wks.append(jnp.sum(jnp.where(m, wd, 0.0), axis=0, keepdims=True))
    slot_ref[...] = jnp.concatenate(slots + [jnp.zeros((SUBLANES - TOP_K, ts), jnp.int32)], axis=0)
    wk_pad = jnp.concatenate(wks + [jnp.zeros((LANES - TOP_K, ts), F32)], axis=0)
    wk_ref[...] = wk_pad.T


def _slots(rank_t, wd_t, cnt, lt):
    T = rank_t.shape[1]
    ts = TS_SLOT
    nb_pad = -(-_n_row_blocks(T) // LANES) * LANES
    const = lambda i: (0, 0)
    tok_t = pl.BlockSpec((N_EXPERTS, ts), lambda i: (0, i))
    return pl.pallas_call(
        _slots_kernel,
        grid=(T // ts,),
        in_specs=[tok_t, tok_t, pl.BlockSpec((N_EXPERTS, LANES), const), pl.BlockSpec((N_EXPERTS, N_EXPERTS), const)],
        out_specs=[pl.BlockSpec((SUBLANES, ts), lambda i: (0, i)),
                   pl.BlockSpec((ts, LANES), lambda i: (i, 0)),
                   pl.BlockSpec((1, nb_pad), const),
                   pl.BlockSpec((1, LANES), const)],
        out_shape=[jax.ShapeDtypeStruct((SUBLANES, T), jnp.int32),
                   jax.ShapeDtypeStruct((T, LANES), F32),
                   jax.ShapeDtypeStruct((1, nb_pad), jnp.int32),
                   jax.ShapeDtypeStruct((1, LANES), jnp.int32)],
        compiler_params=pltpu.CompilerParams(dimension_semantics=("arbitrary",), vmem_limit_bytes=VMEM_LIMIT),
        name="slots",
    )(rank_t, wd_t, cnt, lt)


def _sc_mesh():
    return plsc.VectorSubcoreMesh(core_axis_name="c", subcore_axis_name="s",
                                  num_cores=SC_CORES, num_subcores=SC_SUBCORES)


def _sc_dispatch(rows, slot_flat, n_out):
    T, width = rows.shape
    n_workers = SC_CORES * SC_SUBCORES
    per_worker = T // n_workers
    steps = per_worker // SC_CHUNK

    @functools.partial(
        pl.kernel, mesh=_sc_mesh(),
        out_type=jax.ShapeDtypeStruct((n_out, width), rows.dtype),
        scratch_types=[pltpu.VMEM((SC_CHUNK, width), rows.dtype)]
        + [pltpu.VMEM((SC_CHUNK,), jnp.int32)] * TOP_K + [pltpu.SemaphoreType.DMA],
        name="dispatch",
    )
    def run(rows_hbm, slot_hbm, out_hbm, rows_v, *rest):
        idx_v, sem = rest[:TOP_K], rest[TOP_K]
        base = (lax.axis_index("s") * SC_CORES + lax.axis_index("c")) * per_worker

        @pl.loop(0, steps)
        def _(i):
            off = base + i * SC_CHUNK
            pltpu.sync_copy(rows_hbm.at[pl.ds(off, SC_CHUNK)], rows_v)
            for k in range(TOP_K):
                pltpu.sync_copy(slot_hbm.at[pl.ds(k * T + off, SC_CHUNK)], idx_v[k])
            copies = [pltpu.async_copy(rows_v, out_hbm.at[idx_v[k]], sem) for k in range(TOP_K)]
            for cp in copies:
                cp.wait()

    return run(rows, slot_flat)


def _sc_gather(rows, idx):
    n = idx.shape[0]
    width = rows.shape[1]
    n_workers = SC_CORES * SC_SUBCORES
    per_worker = n // n_workers
    steps = per_worker // SC_CHUNK

    assert steps % 2 == 0
    slot_types = [pltpu.VMEM((SC_CHUNK,), jnp.int32), pltpu.VMEM((SC_CHUNK, width), rows.dtype),
                  pltpu.SemaphoreType.DMA]

    @functools.partial(
        pl.kernel, mesh=_sc_mesh(),
        out_type=jax.ShapeDtypeStruct((n, width), rows.dtype),
        scratch_types=slot_types * 2,
        name="combine",
    )
    def run(rows_hbm, idx_hbm, out_hbm, *scratch):
        base = (lax.axis_index("s") * SC_CORES + lax.axis_index("c")) * per_worker
        slots = (scratch[0:3], scratch[3:6])

        def gather(slot):
            idx_v, rows_v, sem = slots[slot]
            return pltpu.make_async_copy(rows_hbm.at[idx_v], rows_v, sem)

        def start(chunk, slot):
            pltpu.sync_copy(idx_hbm.at[pl.ds(base + chunk * SC_CHUNK, SC_CHUNK)], slots[slot][0])
            gather(slot).start()

        def finish(chunk, slot):
            gather(slot).wait()
            pltpu.sync_copy(slots[slot][1], out_hbm.at[pl.ds(base + chunk * SC_CHUNK, SC_CHUNK)])

        start(0, 0)

        @pl.loop(0, steps, step=2)
        def _(chunk):
            start(chunk + 1, 1)
            finish(chunk, 0)

            @pl.when(chunk + 2 < steps)
            def _():
                start(chunk + 2, 0)

            finish(chunk + 1, 1)

    return run(rows, idx)


def _swiglu_packed(xp, wgu, wd):
    lo, hi = _unpack_bf16_pair(xp)
    half = xp.shape[-1]
    gu = _dot(lo, wgu[:half, :]) + _dot(hi, wgu[half:, :])
    dh = wgu.shape[-1] // 2
    act = _silu(gu[:, :dh]) * gu[:, dh:]
    return _dot(act.astype(BF16), wd)


def _cast_swiglu_weights(wg32_ref, wu32_ref, wd32_ref, wgu_ref, wd_ref):
    dh = wg32_ref.shape[-1]
    wgu_ref[:, :dh] = wg32_ref[...].reshape(wg32_ref.shape[-2:]).astype(BF16)
    wgu_ref[:, dh:] = wu32_ref[...].reshape(wu32_ref.shape[-2:]).astype(BF16)
    wd_ref[...] = wd32_ref[...].reshape(wd32_ref.shape[-2:]).astype(BF16)


def _experts_kernel(be_ref, nu_ref, x_ref, *refs):
    n = BLOCKS_PER_STEP
    w32 = [refs[3 * j:3 * j + 3] for j in range(n)]
    y_ref = refs[3 * n]
    wbf = [refs[3 * n + 1 + 2 * j:3 * n + 3 + 2 * j] for j in range(n)]
    step = pl.program_id(0)
    for j in range(n):
        blk = step * n + j
        new_expert = jnp.logical_or(step == 0, be_ref[blk] != be_ref[jnp.maximum(blk - n, 0)])

        @pl.when(new_expert)
        def _():
            _cast_swiglu_weights(*w32[j], *wbf[j])

    @pl.when(step * n < nu_ref[0])
    def _():
        for j in range(n):
            rows = pl.ds(j * ROW_BLOCK, ROW_BLOCK)
            y = _swiglu_packed(x_ref[rows, :], wbf[j][0][...], wbf[j][1][...])
            half = y.shape[-1] // 2
            y_ref[rows, :] = _pack_bf16_pair(y[:, :half], y[:, half:])


def _experts(xs, block_expert, n_used, w_gate, w_up, w_down):
    P, half = xs.shape
    _, D, dh = w_gate.shape
    n = BLOCKS_PER_STEP
    rows = n * ROW_BLOCK
    assert P % rows == 0

    def by_expert(shape, j):
        return pl.BlockSpec((1,) + shape, lambda s, be, nu: (be[s * n + j], 0, 0))

    w_specs, w_args = [], []
    for j in range(n):
        w_specs += [by_expert((D, dh), j), by_expert((D, dh), j), by_expert((dh, D), j)]
        w_args += [w_gate, w_up, w_down]
    def row_step(s, be, nu):
        return jnp.minimum(s, lax.div(jnp.maximum(nu[0], 1) - 1, n)), 0

    grid_spec = pltpu.PrefetchScalarGridSpec(
        num_scalar_prefetch=2,
        grid=(P // rows,),
        in_specs=[pl.BlockSpec((rows, half), row_step)] + w_specs,
        out_specs=pl.BlockSpec((rows, half), row_step),
        scratch_shapes=[pltpu.VMEM((D, 2 * dh), BF16), pltpu.VMEM((dh, D), BF16)] * n,
    )
    return pl.pallas_call(
        _experts_kernel,
        grid_spec=grid_spec,
        out_shape=jax.ShapeDtypeStruct((P, half), jnp.uint32),
        compiler_params=pltpu.CompilerParams(dimension_semantics=("arbitrary",), vmem_limit_bytes=VMEM_LIMIT),
        name="experts",
    )(block_expert, n_used, xs, *w_args)


def _final_kernel(x1_ref, hp_ref, g_ref, wk_ref, mod_ref, sg32_ref, su32_ref, sd32_ref, *rest):
    o_ref, sgu_ref, sd_ref = rest[-3:]

    @pl.when(_first_grid_step())
    def _():
        _cast_swiglu_weights(sg32_ref, su32_ref, sd32_ref, sgu_ref, sd_ref)

    acc = _swiglu_packed(hp_ref[0], sgu_ref[...], sd_ref[...])
    wk = wk_ref[0]
    for k in range(TOP_K):
        lo, hi = _unpack_bf16_pair(g_ref[k, 0])
        y = jnp.concatenate([lo.astype(F32), hi.astype(F32)], axis=-1)
        acc = acc + wk[:, k:k + 1] * y
    gate2 = mod_ref[0, 5:6, :]
    o_ref[0] = x1_ref[0] + gate2 * acc


def _final(x1, hp, g, wk_tok, mod3, ws_gate, ws_up, ws_down, prev_out, b0, b_total):
    B, S, D = x1.shape
    dh = ws_gate.shape[-1]
    ts = TS_PROJ
    tok = lambda w: pl.BlockSpec((1, ts, w), lambda b, s: (b, s, 0))
    const2 = lambda b, s: (0, 0)
    in_specs = [tok(D), tok(D // 2),
                pl.BlockSpec((TOP_K, 1, ts, D // 2), lambda b, s: (0, b, s, 0)),
                tok(LANES),
                pl.BlockSpec((1, N_MOD, D), lambda b, s: (b + b0, 0, 0)),
                pl.BlockSpec(ws_gate.shape, const2),
                pl.BlockSpec(ws_up.shape, const2),
                pl.BlockSpec(ws_down.shape, const2)]
    args = [x1, hp, g, wk_tok, mod3, ws_gate, ws_up, ws_down]
    aliases = {}
    if prev_out is not None:
        in_specs.append(pl.BlockSpec(memory_space=pl.ANY))
        args.append(prev_out)
        aliases = {len(args) - 1: 0}
    return pl.pallas_call(
        _final_kernel,
        grid=(B, S // ts),
        in_specs=in_specs,
        out_specs=pl.BlockSpec((1, ts, D), lambda b, s: (b + b0, s, 0)),
        out_shape=jax.ShapeDtypeStruct((b_total, S, D), F32),
        scratch_shapes=[pltpu.VMEM((D, 2 * dh), BF16), pltpu.VMEM((dh, D), BF16)],
        input_output_aliases=aliases,
        compiler_params=pltpu.CompilerParams(dimension_semantics=("arbitrary", "arbitrary"),
                                             vmem_limit_bytes=VMEM_LIMIT),
        name="final",
    )(*args)


def _layer(x, c_act_mod, norm1_g, norm2_g, w_in, q_norm_g, k_norm_g, w_pool, pool_scale, w_out,
           w_router, router_bias, w_gate, w_up, w_down, ws_gate, ws_up, ws_down):
    B, S, D = x.shape
    mod3 = c_act_mod.reshape(B, N_MOD, D)
    head_of = jnp.arange(D_ATTN, dtype=jnp.int32) // HEAD_DIM
    head_mean = jnp.where(head_of[:, None] == head_of[None, :], 1.0 / HEAD_DIM, 0.0).astype(BF16)
    j = jnp.arange(TK, dtype=jnp.int32)
    umat = jnp.where(j[:, None] >= j[None, :], -1.0, 0.0).astype(BF16)

    q, k, v, pool = _inproj(
        x, mod3, norm1_g.reshape(1, D), w_in,
        jnp.tile(q_norm_g, N_HEADS).reshape(1, D_ATTN), jnp.tile(k_norm_g, N_HEADS).reshape(1, D_ATTN),
        head_mean, w_pool, pool_scale.reshape(1, D_POOL))
    attn = _attention(q, k, v, umat)
    t = jnp.arange(TS_PROJ, dtype=jnp.int32)
    ut = (t[:, None] < t[None, :]).astype(BF16)
    e = jnp.arange(N_EXPERTS, dtype=jnp.int32)
    lt = (e[None, :] < e[:, None]).astype(BF16)
    bp = B // MOE_PARTS
    T = bp * S
    n_blocks = _n_row_blocks(T)
    out = None
    for part in range(MOE_PARTS):
        b0 = part * bp
        x1, hp, rank_t, wd_t, cnt = _outproj(x, attn, pool, mod3, w_out, norm2_g.reshape(1, D),
                                             w_router.T, router_bias.reshape(N_EXPERTS, 1), ut, b0, bp)
        slots, wk_tok, block_expert, n_used = _slots(rank_t, wd_t, cnt, lt)
        slot_flat = slots[:TOP_K].reshape(TOP_K * T)
        xs = _sc_dispatch(hp.reshape(T, D // 2), slot_flat, n_blocks * ROW_BLOCK)
        ys = _experts(xs, block_expert[0, :n_blocks], n_used[0, :1], w_gate, w_up, w_down)
        g = _sc_gather(ys, slot_flat).reshape(TOP_K, bp, S, D // 2)
        out = _final(x1, hp, g, wk_tok.reshape(bp, S, LANES), mod3, ws_gate, ws_up, ws_down, out, b0, B)
    return out


def kernel(x, c, w_ada, b_ada, norm1_g, norm2_g, w_in, q_norm_g, k_norm_g, w_pool, pool_scale, w_out,
           w_router, router_bias, w_gate, w_up, w_down, ws_gate, ws_up, ws_down):
    depth = w_ada.shape[0]
    for l in range(depth):
        mod = _adaln(c, w_ada[l], b_ada[l])
        x = _layer(x, mod, norm1_g[l], norm2_g[l], w_in[l], q_norm_g[l], k_norm_g[l], w_pool[l],
                   pool_scale[l], w_out[l], w_router[l], router_bias[l], w_gate[l], w_up[l], w_down[l],
                   ws_gate[l], ws_up[l], ws_down[l])
    return x
```

```python
import functools

import jax
import jax.numpy as jnp
from jax import lax
from jax.experimental import pallas as pl
from jax.experimental.pallas import tpu as pltpu
from jax.experimental.pallas import tpu_sc as plsc

F32 = jnp.float32
BF16 = jnp.bfloat16

HEAD_DIM = 64
N_HEADS = 8
D_ATTN = N_HEADS * HEAD_DIM
POOL_WINDOWS = (2, 4, 8, 16)
POOL_GROUP_DIM = 128
D_POOL = len(POOL_WINDOWS) * POOL_GROUP_DIM
MAX_WINDOW = max(POOL_WINDOWS)
assert all(w & (w - 1) == 0 for w in POOL_WINDOWS)
N_EXPERTS = 64
TOP_K = 6
N_GROUPS = 8
GROUP_SIZE = N_EXPERTS // N_GROUPS
TOPK_GROUPS = 4
ROUTED_SCALE = 2.5
RMS_EPS = 1e-6
N_MOD = 6

LANES = 128
SUBLANES = 8
VMEM_LIMIT = 56 * 1024 * 1024

TS_PROJ = 1024
TQ = 1024
TK = 256
PAIRS_PER_STEP = 2
TS_SLOT = 2048
ROW_BLOCK_LOG2 = 9
ROW_BLOCK = 1 << ROW_BLOCK_LOG2
BLOCKS_PER_STEP = 2
MOE_PARTS = 2
SC_CORES = 2
SC_SUBCORES = 16
SC_CHUNK = 64


def _split_bf16(a):
    hi = a.astype(BF16)
    lo = (a - hi.astype(F32)).astype(BF16)
    return hi, lo


def _dot(a, b):
    return jnp.dot(a, b, preferred_element_type=F32)


def _dot_nt(a, b):
    return lax.dot_general(a, b, (((1,), (1,)), ((), ())), preferred_element_type=F32)


def _dot3(a, b):
    ah, al = _split_bf16(a)
    bh, bl = _split_bf16(b)
    return _dot(ah, bh) + _dot(ah, bl) + _dot(al, bh)


def _dot3_nt(a, b):
    ah, al = _split_bf16(a)
    bh, bl = _split_bf16(b)
    return _dot_nt(ah, bh) + _dot_nt(ah, bl) + _dot_nt(al, bh)


def _silu(x):
    return x * (1.0 / (1.0 + jnp.exp(-x)))


def _rms_mod(x, gain, scale, shift):
    ms = jnp.mean(x * x, axis=-1, keepdims=True)
    y = x * lax.rsqrt(ms + RMS_EPS) * gain
    return y * (1.0 + scale) + shift


def _adaln_kernel(c_ref, w_ref, b_ref, o_ref):
    c = c_ref[...]
    o_ref[...] = _dot3(_silu(c), w_ref[...]) + b_ref[...]


def _adaln(c, w_ada, b_ada):
    nb, D = c.shape
    B = -(-nb // SUBLANES) * SUBLANES
    c = jnp.pad(c, ((0, B - nb), (0, 0)))
    N = w_ada.shape[1]
    tn = 1024
    out = pl.pallas_call(
        _adaln_kernel,
        grid=(N // tn,),
        in_specs=[pl.BlockSpec((B, D), lambda j: (0, 0)),
                  pl.BlockSpec((D, tn), lambda j: (0, j)),
                  pl.BlockSpec((1, tn), lambda j: (0, j))],
        out_specs=pl.BlockSpec((B, tn), lambda j: (0, j)),
        out_shape=jax.ShapeDtypeStruct((B, N), F32),
        compiler_params=pltpu.CompilerParams(dimension_semantics=("arbitrary",),
                                             vmem_limit_bytes=VMEM_LIMIT),
        name="adaln",
    )(c, w_ada, b_ada.reshape(1, N))
    return out[:nb]


def _first_grid_step():
    return jnp.logical_and(pl.program_id(0) == 0, pl.program_id(1) == 0)


def _inproj_kernel(x_ref, mod_ref, g1_ref, win32_ref, qg_ref, kg_ref, hm_ref, wp32_ref, ps_ref,
                   q_ref, k_ref, v_ref, p_ref, ext_ref, win_ref, wp_ref):
    @pl.when(_first_grid_step())
    def _():
        win_ref[...] = win32_ref[...].astype(BF16)
        wp_ref[...] = wp32_ref[...].astype(BF16)

    si = pl.program_id(1)
    ts = x_ref.shape[1]
    x = x_ref[0]
    shift1 = mod_ref[0, 0:1, :]
    scale1 = mod_ref[0, 1:2, :]
    h = _rms_mod(x, g1_ref[...], scale1, shift1)
    proj = _dot(h.astype(BF16), win_ref[...])

    hm = hm_ref[...]

    def head_norm(t, gain):
        ms = _dot((t * t).astype(BF16), hm)
        return t * lax.rsqrt(ms + RMS_EPS) * gain

    hq = proj[:, 0:D_ATTN]
    hk = proj[:, D_ATTN:2 * D_ATTN]
    q_ref[0] = (head_norm(hq, qg_ref[...]) * (HEAD_DIM ** -0.5)).astype(BF16)
    k_ref[0] = head_norm(hk, kg_ref[...]).astype(BF16)
    v_ref[0] = proj[:, 2 * D_ATTN:3 * D_ATTN].astype(BF16)

    hp = proj[:, 3 * D_ATTN:]

    @pl.when(si == 0)
    def _():
        ext_ref[0:MAX_WINDOW, :] = jnp.zeros((MAX_WINDOW, D_POOL), F32)

    ext_ref[MAX_WINDOW:, :] = hp
    pos = si * ts + lax.broadcasted_iota(jnp.int32, (ts, 1), 0)
    for g, w in enumerate(POOL_WINDOWS):
        lo_l, hi_l = g * POOL_GROUP_DIM, (g + 1) * POOL_GROUP_DIM
        u = hp[:, lo_l:hi_l]
        run = ext_ref[:, lo_l:hi_l]
        span = 1
        while span < w:
            run = run + pltpu.roll(run, span, axis=0)
            span *= 2
        acc = run[MAX_WINDOW:]
        count = jnp.minimum(pos + 1, w).astype(F32)
        d = acc / count - u
        mixed = _dot(d.astype(BF16), wp_ref[g])
        p_ref[0, :, lo_l:hi_l] = (mixed * ps_ref[:, lo_l:hi_l]).astype(BF16)
    ext_ref[0:MAX_WINDOW, :] = hp[ts - MAX_WINDOW:, :]


def _inproj(x, mod3, g1, w_in, qg_t, kg_t, head_mean, w_pool, pool_scale):
    B, S, D = x.shape
    ts = TS_PROJ
    out_sd = jax.ShapeDtypeStruct((B, S, D_ATTN), BF16)
    blk = pl.BlockSpec((1, ts, D_ATTN), lambda b, s: (b, s, 0))
    const2 = lambda b, s: (0, 0)
    return pl.pallas_call(
        _inproj_kernel,
        grid=(B, S // ts),
        in_specs=[pl.BlockSpec((1, ts, D), lambda b, s: (b, s, 0)),
                  pl.BlockSpec((1, N_MOD, D), lambda b, s: (b, 0, 0)),
                  pl.BlockSpec((1, D), const2),
                  pl.BlockSpec(w_in.shape, const2),
                  pl.BlockSpec((1, D_ATTN), const2),
                  pl.BlockSpec((1, D_ATTN), const2),
                  pl.BlockSpec((D_ATTN, D_ATTN), const2),
                  pl.BlockSpec(w_pool.shape, lambda b, s: (0, 0, 0)),
                  pl.BlockSpec((1, D_POOL), const2)],
        out_specs=[blk, blk, blk, blk],
        out_shape=[out_sd, out_sd, out_sd, out_sd],
        scratch_shapes=[pltpu.VMEM((MAX_WINDOW + ts, D_POOL), F32),
                        pltpu.VMEM(w_in.shape, BF16), pltpu.VMEM(w_pool.shape, BF16)],
        compiler_params=pltpu.CompilerParams(dimension_semantics=("arbitrary", "arbitrary"),
                                             vmem_limit_bytes=VMEM_LIMIT),
        name="inproj",
    )(x, mod3, g1, w_in, qg_t, kg_t, head_mean, w_pool, pool_scale)


def _attn_kernel(q_ref, k_ref, v_ref, u_ref, o_ref, acc_ref):
    qi = pl.program_id(2)
    lane = lax.broadcasted_iota(jnp.int32, (TQ, LANES), 1)
    first = lane < HEAD_DIM
    n_heads = 2 * PAIRS_PER_STEP
    pair_lanes = lambda h: pl.ds((h // 2) * LANES, LANES)
    qh = []
    for h in range(n_heads):
        q = q_ref[0, :, pair_lanes(h)]
        keep = first if h % 2 == 0 else jnp.logical_not(first)
        qh.append(jnp.where(keep, q, jnp.zeros_like(q)))
    u = u_ref[...]
    row = lax.broadcasted_iota(jnp.int32, (TQ, TK), 0)
    col = lax.broadcasted_iota(jnp.int32, (TQ, TK), 1)
    acc_ref[...] = jnp.zeros(acc_ref.shape, F32)

    def block(kb, survs, diag):
        start = pl.multiple_of(kb * TK, TK)
        r0 = 0 if diag is None else diag * TK
        if diag is not None:
            valid = (col + r0 < row)[r0:]
        out, pvs = [], []
        for h in range(n_heads):
            k = k_ref[0, pl.ds(start, TK), pair_lanes(h)]
            v = v_ref[0, pl.ds(start, TK), pair_lanes(h)]
            z = _dot_nt(qh[h][r0:], k)
            zb = z.astype(BF16)
            sp = jnp.maximum(zb, 0) + jnp.log(1 + jnp.exp(-jnp.abs(zb)))
            if diag is not None:
                sp = jnp.where(valid, sp, jnp.zeros_like(sp))
            r = _dot(sp, u)
            arg = z + r + survs[h][r0:]
            if diag is not None:
                arg = jnp.where(valid, arg, -jnp.inf)
            pvs.append(_dot(jnp.exp(arg).astype(BF16), v))
            surv = survs[h][r0:] + r[:, 0:1]
            out.append(surv if r0 == 0 else jnp.concatenate([survs[h][:r0], surv], axis=0))
        return tuple(out), pvs, r0

    survs = tuple(jnp.zeros((TQ, 1), F32) for _ in range(n_heads))
    n_diag = TQ // TK
    for d in reversed(range(n_diag)):
        survs, pvs, r0 = block(qi * n_diag + d, survs, d)
        for h in range(n_heads):
            acc_ref[h, r0:, :] += pvs[h]

    def full_blocks(i, c):
        total = None
        for d in range(n_diag):
            c, pvs, _ = block((qi - i) * n_diag - 1 - d, c, None)
            total = pvs if total is None else [t + p for t, p in zip(total, pvs)]
        for h in range(n_heads):
            acc_ref[h] += total[h]
        return c

    lax.fori_loop(0, qi, full_blocks, survs)
    for p in range(PAIRS_PER_STEP):
        o_ref[0, :, pl.ds(p * LANES, LANES)] = jnp.where(first, acc_ref[2 * p], acc_ref[2 * p + 1]).astype(BF16)


def _attention(q, k, v, umat):
    B, S, _ = q.shape
    width = PAIRS_PER_STEP * LANES
    kv_spec = pl.BlockSpec((1, S, width), lambda b, p, i: (b, 0, p))
    q_spec = pl.BlockSpec((1, TQ, width), lambda b, p, i: (b, i, p))
    return pl.pallas_call(
        _attn_kernel,
        grid=(B, D_ATTN // width, S // TQ),
        in_specs=[q_spec, kv_spec, kv_spec, pl.BlockSpec((TK, TK), lambda b, p, i: (0, 0))],
        out_specs=q_spec,
        out_shape=jax.ShapeDtypeStruct((B, S, D_ATTN), BF16),
        scratch_shapes=[pltpu.VMEM((2 * PAIRS_PER_STEP, TQ, LANES), F32)],
        compiler_params=pltpu.CompilerParams(
            dimension_semantics=("arbitrary", "arbitrary", "arbitrary"),
            vmem_limit_bytes=VMEM_LIMIT),
        name="stickbreak_attn",
    )(q, k, v, umat)


def _route_t(scores, biased):
    ts = scores.shape[-1]
    neg = -jnp.inf
    b3 = biased.reshape(N_GROUPS, GROUP_SIZE, ts)
    e_in_g = lax.broadcasted_iota(jnp.int32, b3.shape, 1)
    m1 = jnp.max(b3, axis=1, keepdims=True)
    i1 = jnp.min(jnp.where(b3 == m1, e_in_g, GROUP_SIZE), axis=1, keepdims=True)
    m2 = jnp.max(jnp.where(e_in_g == i1, neg, b3), axis=1, keepdims=True)
    gs = (m1 + m2)[:, 0, :]
    g_iota = lax.broadcasted_iota(jnp.int32, gs.shape, 0)
    g_sel = jnp.zeros(gs.shape, jnp.bool_)
    for _ in range(TOPK_GROUPS):
        gm = jnp.max(gs, axis=0, keepdims=True)
        gi = jnp.min(jnp.where(gs == gm, g_iota, N_GROUPS), axis=0, keepdims=True)
        pick = g_iota == gi
        g_sel = jnp.logical_or(g_sel, pick)
        gs = jnp.where(pick, neg, gs)
    masked = jnp.where(g_sel[:, None, :], b3, neg)
    flat = lax.broadcasted_iota(jnp.int32, b3.shape, 0) * GROUP_SIZE + e_in_g
    sel = jnp.zeros(b3.shape, jnp.bool_)
    for _ in range(TOP_K):
        m = jnp.max(jnp.max(masked, axis=1, keepdims=True), axis=0, keepdims=True)
        cand = jnp.where(masked == m, flat, N_EXPERTS)
        idx = jnp.min(jnp.min(cand, axis=1, keepdims=True), axis=0, keepdims=True)
        pick = flat == idx
        sel = jnp.logical_or(sel, pick)
        masked = jnp.where(pick, neg, masked)
    s3 = scores.reshape(N_GROUPS, GROUP_SIZE, ts)
    w = jnp.where(sel, s3, 0.0)
    tot = jnp.sum(jnp.sum(w, axis=1, keepdims=True), axis=0, keepdims=True)
    return (w / tot * ROUTED_SCALE).reshape(N_EXPERTS, ts), sel.reshape(N_EXPERTS, ts)


def _pack_bf16_pair(lo, hi):
    lo_bits = pltpu.bitcast(lo.astype(BF16).astype(F32), jnp.uint32) >> 16
    hi_bits = pltpu.bitcast(hi.astype(BF16).astype(F32), jnp.uint32) & jnp.uint32(0xFFFF0000)
    return lo_bits | hi_bits


def _unpack_bf16_pair(p):
    lo = pltpu.bitcast(p << 16, F32).astype(BF16)
    hi = pltpu.bitcast(p & jnp.uint32(0xFFFF0000), F32).astype(BF16)
    return lo, hi


def _outproj_kernel(x_ref, a_ref, p_ref, mod_ref, wo32_ref, g2_ref, wrt_ref, rb_ref, ut_ref,
                    x1_ref, hp_ref, rank_ref, wd_ref, cnt_ref, run_ref, wo_ref):
    @pl.when(_first_grid_step())
    def _():
        run_ref[...] = jnp.zeros(run_ref.shape, jnp.int32)
        wo_ref[...] = wo32_ref[...].astype(BF16)

    gate1 = mod_ref[0, 2:3, :]
    shift2 = mod_ref[0, 3:4, :]
    scale2 = mod_ref[0, 4:5, :]
    mixp = _dot(a_ref[0], wo_ref[0:D_ATTN, :]) + _dot(p_ref[0], wo_ref[D_ATTN:, :])
    x1 = x_ref[0] + gate1 * mixp
    x1_ref[0] = x1
    h2 = _rms_mod(x1, g2_ref[...], scale2, shift2)
    half = h2.shape[-1] // 2
    hp_ref[0] = _pack_bf16_pair(h2[:, :half], h2[:, half:])
    logits_t = _dot3_nt(wrt_ref[...], h2)
    scores = 1.0 / (1.0 + jnp.exp(-logits_t))
    wd_t, sel = _route_t(scores, scores + rb_ref[...])
    wd_ref[...] = wd_t
    self_f = jnp.where(sel, 1.0, 0.0)
    before = _dot(self_f.astype(BF16), ut_ref[...]).astype(jnp.int32)
    run = run_ref[:, 0:1]
    rank_ref[...] = jnp.where(sel, run + before, -1)
    run_ref[...] = run_ref[...] + jnp.sum(self_f, axis=1, keepdims=True).astype(jnp.int32)
    cnt_ref[...] = run_ref[...]


def _outproj(x, attn, pool, mod3, w_out, g2, wr_t, rbias, ut, b0, B):
    _, S, D = x.shape
    ts = TS_PROJ
    n_s = S // ts
    const2 = lambda b, s: (0, 0)
    tok_in = lambda w: pl.BlockSpec((1, ts, w), lambda b, s: (b + b0, s, 0))
    tok = lambda w: pl.BlockSpec((1, ts, w), lambda b, s: (b, s, 0))
    tok_t = pl.BlockSpec((N_EXPERTS, ts), lambda b, s: (0, b * n_s + s))
    return pl.pallas_call(
        _outproj_kernel,
        grid=(B, n_s),
        in_specs=[tok_in(D), tok_in(D_ATTN), tok_in(D_POOL),
                  pl.BlockSpec((1, N_MOD, D), lambda b, s: (b + b0, 0, 0)),
                  pl.BlockSpec(w_out.shape, const2),
                  pl.BlockSpec((1, D), const2),
                  pl.BlockSpec(wr_t.shape, const2),
                  pl.BlockSpec((N_EXPERTS, 1), const2),
                  pl.BlockSpec((ts, ts), const2)],
        out_specs=[tok(D), tok(D // 2), tok_t, tok_t, pl.BlockSpec((N_EXPERTS, LANES), const2)],
        out_shape=[jax.ShapeDtypeStruct((B, S, D), F32),
                   jax.ShapeDtypeStruct((B, S, D // 2), jnp.uint32),
                   jax.ShapeDtypeStruct((N_EXPERTS, B * S), jnp.int32),
                   jax.ShapeDtypeStruct((N_EXPERTS, B * S), F32),
                   jax.ShapeDtypeStruct((N_EXPERTS, LANES), jnp.int32)],
        scratch_shapes=[pltpu.VMEM((N_EXPERTS, LANES), jnp.int32), pltpu.VMEM(w_out.shape, BF16)],
        compiler_params=pltpu.CompilerParams(dimension_semantics=("arbitrary", "arbitrary"),
                                             vmem_limit_bytes=VMEM_LIMIT),
        name="outproj_router",
    )(x, attn, pool, mod3, w_out, g2, wr_t, rbias, ut)


def _n_row_blocks(n_tokens):
    n_blocks = -(-(n_tokens * TOP_K + N_EXPERTS * (ROW_BLOCK - 1)) // ROW_BLOCK)
    return -(-n_blocks // BLOCKS_PER_STEP) * BLOCKS_PER_STEP


def _slots_kernel(rank_ref, wd_ref, cnt_ref, lt_ref, slot_ref, wk_ref, be_ref, nu_ref):
    lt = lt_ref[...]
    nblk = lax.shift_right_logical(cnt_ref[...] + (ROW_BLOCK - 1), ROW_BLOCK_LOG2).astype(F32)
    nb_hi, nb_lo = _split_bf16(nblk)
    blk_start = _dot(lt, nb_hi) + _dot(lt, nb_lo)

    @pl.when(pl.program_id(0) == 0)
    def _():
        blk_end = (blk_start + nblk)[:, 0:1]
        b_iota = lax.broadcasted_iota(jnp.int32, (N_EXPERTS, be_ref.shape[-1]), 1).astype(F32)
        owner = jnp.sum(jnp.where(blk_end <= b_iota, 1, 0), axis=0, keepdims=True)
        be_ref[...] = jnp.minimum(owner, N_EXPERTS - 1)
        nu_ref[...] = jnp.broadcast_to(blk_end[N_EXPERTS - 1:, :].astype(jnp.int32), nu_ref.shape)

    rank = rank_ref[...]
    sel = rank >= 0
    row_start = (blk_start[:, 0:1] * ROW_BLOCK).astype(jnp.int32)
    slot_d = row_start + rank
    wd = wd_ref[...]
    choice = _dot(lt, jnp.where(sel, 1.0, 0.0).astype(BF16)).astype(jnp.int32)
    ts = rank.shape[-1]
    slots, wks = [], []
    for k in range(TOP_K):
        m = jnp.logical_and(sel, choice == k)
        slots.append(jnp.sum(jnp.where(m, slot_d, 0), axis=0, keepdims=True))
        wks.append(jnp.sum(jnp.where(m, wd, 0.0), axis=0, keepdims=True))
    slot_ref[...] = jnp.concatenate(slots + [jnp.zeros((SUBLANES - TOP_K, ts), jnp.int32)], axis=0)
    wk_pad = jnp.concatenate(wks + [jnp.zeros((LANES - TOP_K, ts), F32)], axis=0)
    wk_ref[...] = wk_pad.T


def _slots(rank_t, wd_t, cnt, lt):
    T = rank_t.shape[1]
    ts = TS_SLOT
    nb_pad = -(-_n_row_blocks(T) // LANES) * LANES
    const = lambda i: (0, 0)
    tok_t = pl.BlockSpec((N_EXPERTS, ts), lambda i: (0, i))
    return pl.pallas_call(
        _slots_kernel,
        grid=(T // ts,),
        in_specs=[tok_t, tok_t, pl.BlockSpec((N_EXPERTS, LANES), const), pl.BlockSpec((N_EXPERTS, N_EXPERTS), const)],
        out_specs=[pl.BlockSpec((SUBLANES, ts), lambda i: (0, i)),
                   pl.BlockSpec((ts, LANES), lambda i: (i, 0)),
                   pl.BlockSpec((1, nb_pad), const),
                   pl.BlockSpec((1, LANES), const)],
        out_shape=[jax.ShapeDtypeStruct((SUBLANES, T), jnp.int32),
                   jax.ShapeDtypeStruct((T, LANES), F32),
                   jax.ShapeDtypeStruct((1, nb_pad), jnp.int32),
                   jax.ShapeDtypeStruct((1, LANES), jnp.int32)],
        compiler_params=pltpu.CompilerParams(dimension_semantics=("arbitrary",), vmem_limit_bytes=VMEM_LIMIT),
        name="slots",
    )(rank_t, wd_t, cnt, lt)


def _sc_mesh():
    return plsc.VectorSubcoreMesh(core_axis_name="c", subcore_axis_name="s",
                                  num_cores=SC_CORES, num_subcores=SC_SUBCORES)


def _sc_dispatch(rows, slot_flat, n_out):
    T, width = rows.shape
    n_workers = SC_CORES * SC_SUBCORES
    per_worker = T // n_workers
    steps = per_worker // SC_CHUNK

    @functools.partial(
        pl.kernel, mesh=_sc_mesh(),
        out_type=jax.ShapeDtypeStruct((n_out, width), rows.dtype),
        scratch_types=[pltpu.VMEM((SC_CHUNK, width), rows.dtype)]
        + [pltpu.VMEM((SC_CHUNK,), jnp.int32)] * TOP_K + [pltpu.SemaphoreType.DMA],
        name="dispatch",
    )
    def run(rows_hbm, slot_hbm, out_hbm, rows_v, *rest):
        idx_v, sem = rest[:TOP_K], rest[TOP_K]
        base = (lax.axis_index("s") * SC_CORES + lax.axis_index("c")) * per_worker

        @pl.loop(0, steps)
        def _(i):
            off = base + i * SC_CHUNK
            pltpu.sync_copy(rows_hbm.at[pl.ds(off, SC_CHUNK)], rows_v)
            for k in range(TOP_K):
                pltpu.sync_copy(slot_hbm.at[pl.ds(k * T + off, SC_CHUNK)], idx_v[k])
            copies = [pltpu.async_copy(rows_v, out_hbm.at[idx_v[k]], sem) for k in range(TOP_K)]
            for cp in copies:
                cp.wait()

    return run(rows, slot_flat)


def _sc_gather(rows, idx):
    n = idx.shape[0]
    width = rows.shape[1]
    n_workers = SC_CORES * SC_SUBCORES
    per_worker = n // n_workers
    steps = per_worker // SC_CHUNK

    assert steps % 2 == 0
    slot_types = [pltpu.VMEM((SC_CHUNK,), jnp.int32), pltpu.VMEM((SC_CHUNK, width), rows.dtype),
                  pltpu.SemaphoreType.DMA]

    @functools.partial(
        pl.kernel, mesh=_sc_mesh(),
        out_type=jax.ShapeDtypeStruct((n, width), rows.dtype),
        scratch_types=slot_types * 2,
        name="combine",
    )
    def run(rows_hbm, idx_hbm, out_hbm, *scratch):
        base = (lax.axis_index("s") * SC_CORES + lax.axis_index("c")) * per_worker
        slots = (scratch[0:3], scratch[3:6])

        def gather(slot):
            idx_v, rows_v, sem = slots[slot]
            return pltpu.make_async_copy(rows_hbm.at[idx_v], rows_v, sem)

        def start(chunk, slot):
            pltpu.sync_copy(idx_hbm.at[pl.ds(base + chunk * SC_CHUNK, SC_CHUNK)], slots[slot][0])
            gather(slot).start()

        def finish(chunk, slot):
            gather(slot).wait()
            pltpu.sync_copy(slots[slot][1], out_hbm.at[pl.ds(base + chunk * SC_CHUNK, SC_CHUNK)])

        start(0, 0)

        @pl.loop(0, steps, step=2)
        def _(chunk):
            start(chunk + 1, 1)
            finish(chunk, 0)

            @pl.when(chunk + 2 < steps)
            def _():
                start(chunk + 2, 0)

            finish(chunk + 1, 1)

    return run(rows, idx)


def _swiglu_packed(xp, wgu, wd):
    lo, hi = _unpack_bf16_pair(xp)
    half = xp.shape[-1]
    gu = _dot(lo, wgu[:half, :]) + _dot(hi, wgu[half:, :])
    dh = wgu.shape[-1] // 2
    act = _silu(gu[:, :dh]) * gu[:, dh:]
    return _dot(act.astype(BF16), wd)


def _cast_swiglu_weights(wg32_ref, wu32_ref, wd32_ref, wgu_ref, wd_ref):
    dh = wg32_ref.shape[-1]
    wgu_ref[:, :dh] = wg32_ref[...].reshape(wg32_ref.shape[-2:]).astype(BF16)
    wgu_ref[:, dh:] = wu32_ref[...].reshape(wu32_ref.shape[-2:]).astype(BF16)
    wd_ref[...] = wd32_ref[...].reshape(wd32_ref.shape[-2:]).astype(BF16)


def _experts_kernel(be_ref, nu_ref, x_ref, *refs):
    n = BLOCKS_PER_STEP
    w32 = [refs[3 * j:3 * j + 3] for j in range(n)]
    y_ref = refs[3 * n]
    wbf = [refs[3 * n + 1 + 2 * j:3 * n + 3 + 2 * j] for j in range(n)]
    step = pl.program_id(0)
    for j in range(n):
        blk = step * n + j
        new_expert = jnp.logical_or(step == 0, be_ref[blk] != be_ref[jnp.maximum(blk - n, 0)])

        @pl.when(new_expert)
        def _():
            _cast_swiglu_weights(*w32[j], *wbf[j])

    @pl.when(step * n < nu_ref[0])
    def _():
        for j in range(n):
            rows = pl.ds(j * ROW_BLOCK, ROW_BLOCK)
            y = _swiglu_packed(x_ref[rows, :], wbf[j][0][...], wbf[j][1][...])
            half = y.shape[-1] // 2
            y_ref[rows, :] = _pack_bf16_pair(y[:, :half], y[:, half:])


def _experts(xs, block_expert, n_used, w_gate, w_up, w_down):
    P, half = xs.shape
    _, D, dh = w_gate.shape
    n = BLOCKS_PER_STEP
    rows = n * ROW_BLOCK
    assert P % rows == 0

    def by_expert(shape, j):
        return pl.BlockSpec((1,) + shape, lambda s, be, nu: (be[s * n + j], 0, 0))

    w_specs, w_args = [], []
    for j in range(n):
        w_specs += [by_expert((D, dh), j), by_expert((D, dh), j), by_expert((dh, D), j)]
        w_args += [w_gate, w_up, w_down]
    def row_step(s, be, nu):
        return jnp.minimum(s, lax.div(jnp.maximum(nu[0], 1) - 1, n)), 0

    grid_spec = pltpu.PrefetchScalarGridSpec(
        num_scalar_prefetch=2,
        grid=(P // rows,),
        in_specs=[pl.BlockSpec((rows, half), row_step)] + w_specs,
        out_specs=pl.BlockSpec((rows, half), row_step),
        scratch_shapes=[pltpu.VMEM((D, 2 * dh), BF16), pltpu.VMEM((dh, D), BF16)] * n,
    )
    return pl.pallas_call(
        _experts_kernel,
        grid_spec=grid_spec,
        out_shape=jax.ShapeDtypeStruct((P, half), jnp.uint32),
        compiler_params=pltpu.CompilerParams(dimension_semantics=("arbitrary",), vmem_limit_bytes=VMEM_LIMIT),
        name="experts",
    )(block_expert, n_used, xs, *w_args)


def _final_kernel(x1_ref, hp_ref, g_ref, wk_ref, mod_ref, sg32_ref, su32_ref, sd32_ref, *rest):
    o_ref, sgu_ref, sd_ref = rest[-3:]

    @pl.when(_first_grid_step())
    def _():
        _cast_swiglu_weights(sg32_ref, su32_ref, sd32_ref, sgu_ref, sd_ref)

    acc = _swiglu_packed(hp_ref[0], sgu_ref[...], sd_ref[...])
    wk = wk_ref[0]
    for k in range(TOP_K):
        lo, hi = _unpack_bf16_pair(g_ref[k, 0])
        y = jnp.concatenate([lo.astype(F32), hi.astype(F32)], axis=-1)
        acc = acc + wk[:, k:k + 1] * y
    gate2 = mod_ref[0, 5:6, :]
    o_ref[0] = x1_ref[0] + gate2 * acc


def _final(x1, hp, g, wk_tok, mod3, ws_gate, ws_up, ws_down, prev_out, b0, b_total):
    B, S, D = x1.shape
    dh = ws_gate.shape[-1]
    ts = TS_PROJ
    tok = lambda w: pl.BlockSpec((1, ts, w), lambda b, s: (b, s, 0))
    const2 = lambda b, s: (0, 0)
    in_specs = [tok(D), tok(D // 2),
                pl.BlockSpec((TOP_K, 1, ts, D // 2), lambda b, s: (0, b, s, 0)),
                tok(LANES),
                pl.BlockSpec((1, N_MOD, D), lambda b, s: (b + b0, 0, 0)),
                pl.BlockSpec(ws_gate.shape, const2),
                pl.BlockSpec(ws_up.shape, const2),
                pl.BlockSpec(ws_down.shape, const2)]
    args = [x1, hp, g, wk_tok, mod3, ws_gate, ws_up, ws_down]
    aliases = {}
    if prev_out is not None:
        in_specs.append(pl.BlockSpec(memory_space=pl.ANY))
        args.append(prev_out)
        aliases = {len(args) - 1: 0}
    return pl.pallas_call(
        _final_kernel,
        grid=(B, S // ts),
        in_specs=in_specs,
        out_specs=pl.BlockSpec((1, ts, D), lambda b, s: (b + b0, s, 0)),
        out_shape=jax.ShapeDtypeStruct((b_total, S, D), F32),
        scratch_shapes=[pltpu.VMEM((D, 2 * dh), BF16), pltpu.VMEM((dh, D), BF16)],
        input_output_aliases=aliases,
        compiler_params=pltpu.CompilerParams(dimension_semantics=("arbitrary", "arbitrary"),
                                             vmem_limit_bytes=VMEM_LIMIT),
        name="final",
    )(*args)


def _layer(x, c_act_mod, norm1_g, norm2_g, w_in, q_norm_g, k_norm_g, w_pool, pool_scale, w_out,
           w_router, router_bias, w_gate, w_up, w_down, ws_gate, ws_up, ws_down):
    B, S, D = x.shape
    mod3 = c_act_mod.reshape(B, N_MOD, D)
    head_of = jnp.arange(D_ATTN, dtype=jnp.int32) // HEAD_DIM
    head_mean = jnp.where(head_of[:, None] == head_of[None, :], 1.0 / HEAD_DIM, 0.0).astype(BF16)
    j = jnp.arange(TK, dtype=jnp.int32)
    umat = jnp.where(j[:, None] >= j[None, :], -1.0, 0.0).astype(BF16)

    q, k, v, pool = _inproj(
        x, mod3, norm1_g.reshape(1, D), w_in,
        jnp.tile(q_norm_g, N_HEADS).reshape(1, D_ATTN), jnp.tile(k_norm_g, N_HEADS).reshape(1, D_ATTN),
        head_mean, w_pool, pool_scale.reshape(1, D_POOL))
    attn = _attention(q, k, v, umat)
    t = jnp.arange(TS_PROJ, dtype=jnp.int32)
    ut = (t[:, None] < t[None, :]).astype(BF16)
    e = jnp.arange(N_EXPERTS, dtype=jnp.int32)
    lt = (e[None, :] < e[:, None]).astype(BF16)
    bp = B // MOE_PARTS
    T = bp * S
    n_blocks = _n_row_blocks(T)
    out = None
    for part in range(MOE_PARTS):
        b0 = part * bp
        x1, hp, rank_t, wd_t, cnt = _outproj(x, attn, pool, mod3, w_out, norm2_g.reshape(1, D),
                                             w_router.T, router_bias.reshape(N_EXPERTS, 1), ut, b0, bp)
        slots, wk_tok, block_expert, n_used = _slots(rank_t, wd_t, cnt, lt)
        slot_flat = slots[:TOP_K].reshape(TOP_K * T)
        xs = _sc_dispatch(hp.reshape(T, D // 2), slot_flat, n_blocks * ROW_BLOCK)
        ys = _experts(xs, block_expert[0, :n_blocks], n_used[0, :1], w_gate, w_up, w_down)
        g = _sc_gather(ys, slot_flat).reshape(TOP_K, bp, S, D // 2)
        out = _final(x1, hp, g, wk_tok.reshape(bp, S, LANES), mod3, ws_gate, ws_up, ws_down, out, b0, B)
    return out


def kernel(x, c, w_ada, b_ada, norm1_g, norm2_g, w_in, q_norm_g, k_norm_g, w_pool, pool_scale, w_out,
           w_router, router_bias, w_gate, w_up, w_down, ws_gate, ws_up, ws_down):
    depth = w_ada.shape[0]
    for l in range(depth):
        mod = _adaln(c, w_ada[l], b_ada[l])
        x = _layer(x, mod, norm1_g[l], norm2_g[l], w_in[l], q_norm_g[l], k_norm_g[l], w_pool[l],
                   pool_scale[l], w_out[l], w_router[l], router_bias[l], w_gate[l], w_up[l], w_down[l],
                   ws_gate[l], ws_up[l], ws_down[l])
    return x
```

```python
import functools

import jax
import jax.numpy as jnp
from jax import lax
from jax.experimental import pallas as pl
from jax.experimental.pallas import tpu as pltpu
from jax.experimental.pallas import tpu_sc as plsc

F32 = jnp.float32
BF16 = jnp.bfloat16

HEAD_DIM = 64
N_HEADS = 8
D_ATTN = N_HEADS * HEAD_DIM
POOL_WINDOWS = (2, 4, 8, 16)
POOL_GROUP_DIM = 128
D_POOL = len(POOL_WINDOWS) * POOL_GROUP_DIM
MAX_WINDOW = max(POOL_WINDOWS)
assert all(w & (w - 1) == 0 for w in POOL_WINDOWS)
N_EXPERTS = 64
TOP_K = 6
N_GROUPS = 8
GROUP_SIZE = N_EXPERTS // N_GROUPS
TOPK_GROUPS = 4
ROUTED_SCALE = 2.5
RMS_EPS = 1e-6
N_MOD = 6

LANES = 128
SUBLANES = 8
VMEM_LIMIT = 56 * 1024 * 1024

TS_PROJ = 1024
TQ = 1024
TK = 256
PAIRS_PER_STEP = 4
TS_SLOT = 2048
ROW_BLOCK_LOG2 = 9
ROW_BLOCK = 1 << ROW_BLOCK_LOG2
BLOCKS_PER_STEP = 2
MOE_PARTS = 2
SC_CORES = 2
SC_SUBCORES = 16
SC_CHUNK = 64


def _split_bf16(a):
    hi = a.astype(BF16)
    lo = (a - hi.astype(F32)).astype(BF16)
    return hi, lo


def _dot(a, b):
    return jnp.dot(a, b, preferred_element_type=F32)


def _dot_nt(a, b):
    return lax.dot_general(a, b, (((1,), (1,)), ((), ())), preferred_element_type=F32)


def _dot3(a, b):
    ah, al = _split_bf16(a)
    bh, bl = _split_bf16(b)
    return _dot(ah, bh) + _dot(ah, bl) + _dot(al, bh)


def _dot3_nt(a, b):
    ah, al = _split_bf16(a)
    bh, bl = _split_bf16(b)
    return _dot_nt(ah, bh) + _dot_nt(ah, bl) + _dot_nt(al, bh)


def _silu(x):
    return x * (1.0 / (1.0 + jnp.exp(-x)))


def _rms_mod(x, gain, scale, shift):
    ms = jnp.mean(x * x, axis=-1, keepdims=True)
    y = x * lax.rsqrt(ms + RMS_EPS) * gain
    return y * (1.0 + scale) + shift


def _adaln_kernel(c_ref, w_ref, b_ref, o_ref):
    c = c_ref[...]
    o_ref[...] = _dot3(_silu(c), w_ref[...]) + b_ref[...]


def _adaln(c, w_ada, b_ada):
    nb, D = c.shape
    B = -(-nb // SUBLANES) * SUBLANES
    c = jnp.pad(c, ((0, B - nb), (0, 0)))
    N = w_ada.shape[1]
    tn = 1024
    out = pl.pallas_call(
        _adaln_kernel,
        grid=(N // tn,),
        in_specs=[pl.BlockSpec((B, D), lambda j: (0, 0)),
                  pl.BlockSpec((D, tn), lambda j: (0, j)),
                  pl.BlockSpec((1, tn), lambda j: (0, j))],
        out_specs=pl.BlockSpec((B, tn), lambda j: (0, j)),
        out_shape=jax.ShapeDtypeStruct((B, N), F32),
        compiler_params=pltpu.CompilerParams(dimension_semantics=("arbitrary",),
                                             vmem_limit_bytes=VMEM_LIMIT),
        name="adaln",
    )(c, w_ada, b_ada.reshape(1, N))
    return out[:nb]


def _first_grid_step():
    return jnp.logical_and(pl.program_id(0) == 0, pl.program_id(1) == 0)


def _inproj_kernel(x_ref, mod_ref, g1_ref, win32_ref, qg_ref, kg_ref, hm_ref, wp32_ref, ps_ref,
                   q_ref, k_ref, v_ref, p_ref, ext_ref, win_ref, wp_ref):
    @pl.when(_first_grid_step())
    def _():
        win_ref[...] = win32_ref[...].astype(BF16)
        wp_ref[...] = wp32_ref[...].astype(BF16)

    si = pl.program_id(1)
    ts = x_ref.shape[1]
    x = x_ref[0]
    shift1 = mod_ref[0, 0:1, :]
    scale1 = mod_ref[0, 1:2, :]
    h = _rms_mod(x, g1_ref[...], scale1, shift1)
    proj = _dot(h.astype(BF16), win_ref[...])

    hm = hm_ref[...]

    def head_norm(t, gain):
        ms = _dot((t * t).astype(BF16), hm)
        return t * lax.rsqrt(ms + RMS_EPS) * gain

    hq = proj[:, 0:D_ATTN]
    hk = proj[:, D_ATTN:2 * D_ATTN]
    q_ref[0] = (head_norm(hq, qg_ref[...]) * (HEAD_DIM ** -0.5)).astype(BF16)
    k_ref[0] = head_norm(hk, kg_ref[...]).astype(BF16)
    v_ref[0] = proj[:, 2 * D_ATTN:3 * D_ATTN].astype(BF16)

    hp = proj[:, 3 * D_ATTN:]

    @pl.when(si == 0)
    def _():
        ext_ref[0:MAX_WINDOW, :] = jnp.zeros((MAX_WINDOW, D_POOL), F32)

    ext_ref[MAX_WINDOW:, :] = hp
    pos = si * ts + lax.broadcasted_iota(jnp.int32, (ts, 1), 0)
    for g, w in enumerate(POOL_WINDOWS):
        lo_l, hi_l = g * POOL_GROUP_DIM, (g + 1) * POOL_GROUP_DIM
        u = hp[:, lo_l:hi_l]
        run = ext_ref[:, lo_l:hi_l]
        span = 1
        while span < w:
            run = run + pltpu.roll(run, span, axis=0)
            span *= 2
        acc = run[MAX_WINDOW:]
        count = jnp.minimum(pos + 1, w).astype(F32)
        d = acc / count - u
        mixed = _dot(d.astype(BF16), wp_ref[g])
        p_ref[0, :, lo_l:hi_l] = (mixed * ps_ref[:, lo_l:hi_l]).astype(BF16)
    ext_ref[0:MAX_WINDOW, :] = hp[ts - MAX_WINDOW:, :]


def _inproj(x, mod3, g1, w_in, qg_t, kg_t, head_mean, w_pool, pool_scale):
    B, S, D = x.shape
    ts = TS_PROJ
    out_sd = jax.ShapeDtypeStruct((B, S, D_ATTN), BF16)
    blk = pl.BlockSpec((1, ts, D_ATTN), lambda b, s: (b, s, 0))
    const2 = lambda b, s: (0, 0)
    return pl.pallas_call(
        _inproj_kernel,
        grid=(B, S // ts),
        in_specs=[pl.BlockSpec((1, ts, D), lambda b, s: (b, s, 0)),
                  pl.BlockSpec((1, N_MOD, D), lambda b, s: (b, 0, 0)),
                  pl.BlockSpec((1, D), const2),
                  pl.BlockSpec(w_in.shape, const2),
                  pl.BlockSpec((1, D_ATTN), const2),
                  pl.BlockSpec((1, D_ATTN), const2),
                  pl.BlockSpec((D_ATTN, D_ATTN), const2),
                  pl.BlockSpec(w_pool.shape, lambda b, s: (0, 0, 0)),
                  pl.BlockSpec((1, D_POOL), const2)],
        out_specs=[blk, blk, blk, blk],
        out_shape=[out_sd, out_sd, out_sd, out_sd],
        scratch_shapes=[pltpu.VMEM((MAX_WINDOW + ts, D_POOL), F32),
                        pltpu.VMEM(w_in.shape, BF16), pltpu.VMEM(w_pool.shape, BF16)],
        compiler_params=pltpu.CompilerParams(dimension_semantics=("arbitrary", "arbitrary"),
                                             vmem_limit_bytes=VMEM_LIMIT),
        name="inproj",
    )(x, mod3, g1, w_in, qg_t, kg_t, head_mean, w_pool, pool_scale)


def _attn_kernel(q_ref, k_ref, v_ref, u_ref, o_ref, acc_ref):
    qi = pl.program_id(2)
    lane = lax.broadcasted_iota(jnp.int32, (TQ, LANES), 1)
    first = lane < HEAD_DIM
    n_heads = 2 * PAIRS_PER_STEP
    pair_lanes = lambda h: pl.ds((h // 2) * LANES, LANES)
    qh = []
    for h in range(n_heads):
        q = q_ref[0, :, pair_lanes(h)]
        keep = first if h % 2 == 0 else jnp.logical_not(first)
        qh.append(jnp.where(keep, q, jnp.zeros_like(q)))
    u = u_ref[...]
    row = lax.broadcasted_iota(jnp.int32, (TQ, TK), 0)
    col = lax.broadcasted_iota(jnp.int32, (TQ, TK), 1)
    acc_ref[...] = jnp.zeros(acc_ref.shape, F32)

    def block(kb, survs, diag):
        start = pl.multiple_of(kb * TK, TK)
        r0 = 0 if diag is None else diag * TK
        if diag is not None:
            valid = (col + r0 < row)[r0:]
        out = []
        for h in range(n_heads):
            k = k_ref[0, pl.ds(start, TK), pair_lanes(h)]
            v = v_ref[0, pl.ds(start, TK), pair_lanes(h)]
            z = _dot_nt(qh[h][r0:], k)
            zb = z.astype(BF16)
            sp = jnp.maximum(zb, 0) + jnp.log(1 + jnp.exp(-jnp.abs(zb)))
            if diag is not None:
                sp = jnp.where(valid, sp, jnp.zeros_like(sp))
            r = _dot(sp, u)
            arg = z + r + survs[h][r0:]
            if diag is not None:
                arg = jnp.where(valid, arg, -jnp.inf)
            acc_ref[h, r0:, :] += _dot(jnp.exp(arg).astype(BF16), v)
            surv = survs[h][r0:] + r[:, 0:1]
            out.append(surv if r0 == 0 else jnp.concatenate([survs[h][:r0], surv], axis=0))
        return tuple(out)

    survs = tuple(jnp.zeros((TQ, 1), F32) for _ in range(n_heads))
    n_diag = TQ // TK
    for d in reversed(range(n_diag)):
        survs = block(qi * n_diag + d, survs, d)
    def full_blocks(i, c):
        for d in range(n_diag):
            c = block((qi - i) * n_diag - 1 - d, c, None)
        return c

    lax.fori_loop(0, qi, full_blocks, survs)
    for p in range(PAIRS_PER_STEP):
        o_ref[0, :, pl.ds(p * LANES, LANES)] = jnp.where(first, acc_ref[2 * p], acc_ref[2 * p + 1]).astype(BF16)


def _attention(q, k, v, umat):
    B, S, _ = q.shape
    width = PAIRS_PER_STEP * LANES
    kv_spec = pl.BlockSpec((1, S, width), lambda b, p, i: (b, 0, p))
    q_spec = pl.BlockSpec((1, TQ, width), lambda b, p, i: (b, i, p))
    return pl.pallas_call(
        _attn_kernel,
        grid=(B, D_ATTN // width, S // TQ),
        in_specs=[q_spec, kv_spec, kv_spec, pl.BlockSpec((TK, TK), lambda b, p, i: (0, 0))],
        out_specs=q_spec,
        out_shape=jax.ShapeDtypeStruct((B, S, D_ATTN), BF16),
        scratch_shapes=[pltpu.VMEM((2 * PAIRS_PER_STEP, TQ, LANES), F32)],
        compiler_params=pltpu.CompilerParams(
            dimension_semantics=("arbitrary", "arbitrary", "arbitrary"),
            vmem_limit_bytes=VMEM_LIMIT),
        name="stickbreak_attn",
    )(q, k, v, umat)


def _route_t(scores, biased):
    ts = scores.shape[-1]
    neg = -jnp.inf
    b3 = biased.reshape(N_GROUPS, GROUP_SIZE, ts)
    e_in_g = lax.broadcasted_iota(jnp.int32, b3.shape, 1)
    m1 = jnp.max(b3, axis=1, keepdims=True)
    i1 = jnp.min(jnp.where(b3 == m1, e_in_g, GROUP_SIZE), axis=1, keepdims=True)
    m2 = jnp.max(jnp.where(e_in_g == i1, neg, b3), axis=1, keepdims=True)
    gs = (m1 + m2)[:, 0, :]
    g_iota = lax.broadcasted_iota(jnp.int32, gs.shape, 0)
    g_sel = jnp.zeros(gs.shape, jnp.bool_)
    for _ in range(TOPK_GROUPS):
        gm = jnp.max(gs, axis=0, keepdims=True)
        gi = jnp.min(jnp.where(gs == gm, g_iota, N_GROUPS), axis=0, keepdims=True)
        pick = g_iota == gi
        g_sel = jnp.logical_or(g_sel, pick)
        gs = jnp.where(pick, neg, gs)
    masked = jnp.where(g_sel[:, None, :], b3, neg)
    flat = lax.broadcasted_iota(jnp.int32, b3.shape, 0) * GROUP_SIZE + e_in_g
    sel = jnp.zeros(b3.shape, jnp.bool_)
    for _ in range(TOP_K):
        m = jnp.max(jnp.max(masked, axis=1, keepdims=True), axis=0, keepdims=True)
        cand = jnp.where(masked == m, flat, N_EXPERTS)
        idx = jnp.min(jnp.min(cand, axis=1, keepdims=True), axis=0, keepdims=True)
        pick = flat == idx
        sel = jnp.logical_or(sel, pick)
        masked = jnp.where(pick, neg, masked)
    s3 = scores.reshape(N_GROUPS, GROUP_SIZE, ts)
    w = jnp.where(sel, s3, 0.0)
    tot = jnp.sum(jnp.sum(w, axis=1, keepdims=True), axis=0, keepdims=True)
    return (w / tot * ROUTED_SCALE).reshape(N_EXPERTS, ts), sel.reshape(N_EXPERTS, ts)


def _pack_bf16_pair(lo, hi):
    lo_bits = pltpu.bitcast(lo.astype(BF16).astype(F32), jnp.uint32) >> 16
    hi_bits = pltpu.bitcast(hi.astype(BF16).astype(F32), jnp.uint32) & jnp.uint32(0xFFFF0000)
    return lo_bits | hi_bits


def _unpack_bf16_pair(p):
    lo = pltpu.bitcast(p << 16, F32).astype(BF16)
    hi = pltpu.bitcast(p & jnp.uint32(0xFFFF0000), F32).astype(BF16)
    return lo, hi


def _outproj_kernel(x_ref, a_ref, p_ref, mod_ref, wo32_ref, g2_ref, wrt_ref, rb_ref, ut_ref,
                    x1_ref, hp_ref, rank_ref, wd_ref, cnt_ref, run_ref, wo_ref):
    @pl.when(_first_grid_step())
    def _():
        run_ref[...] = jnp.zeros(run_ref.shape, jnp.int32)
        wo_ref[...] = wo32_ref[...].astype(BF16)

    gate1 = mod_ref[0, 2:3, :]
    shift2 = mod_ref[0, 3:4, :]
    scale2 = mod_ref[0, 4:5, :]
    mixp = _dot(a_ref[0], wo_ref[0:D_ATTN, :]) + _dot(p_ref[0], wo_ref[D_ATTN:, :])
    x1 = x_ref[0] + gate1 * mixp
    x1_ref[0] = x1
    h2 = _rms_mod(x1, g2_ref[...], scale2, shift2)
    half = h2.shape[-1] // 2
    hp_ref[0] = _pack_bf16_pair(h2[:, :half], h2[:, half:])
    logits_t = _dot3_nt(wrt_ref[...], h2)
    scores = 1.0 / (1.0 + jnp.exp(-logits_t))
    wd_t, sel = _route_t(scores, scores + rb_ref[...])
    wd_ref[...] = wd_t
    self_f = jnp.where(sel, 1.0, 0.0)
    before = _dot(self_f.astype(BF16), ut_ref[...]).astype(jnp.int32)
    run = run_ref[:, 0:1]
    rank_ref[...] = jnp.where(sel, run + before, -1)
    run_ref[...] = run_ref[...] + jnp.sum(self_f, axis=1, keepdims=True).astype(jnp.int32)
    cnt_ref[...] = run_ref[...]


def _outproj(x, attn, pool, mod3, w_out, g2, wr_t, rbias, ut, b0, B):
    _, S, D = x.shape
    ts = TS_PROJ
    n_s = S // ts
    const2 = lambda b, s: (0, 0)
    tok_in = lambda w: pl.BlockSpec((1, ts, w), lambda b, s: (b + b0, s, 0))
    tok = lambda w: pl.BlockSpec((1, ts, w), lambda b, s: (b, s, 0))
    tok_t = pl.BlockSpec((N_EXPERTS, ts), lambda b, s: (0, b * n_s + s))
    return pl.pallas_call(
        _outproj_kernel,
        grid=(B, n_s),
        in_specs=[tok_in(D), tok_in(D_ATTN), tok_in(D_POOL),
                  pl.BlockSpec((1, N_MOD, D), lambda b, s: (b + b0, 0, 0)),
                  pl.BlockSpec(w_out.shape, const2),
                  pl.BlockSpec((1, D), const2),
                  pl.BlockSpec(wr_t.shape, const2),
                  pl.BlockSpec((N_EXPERTS, 1), const2),
                  pl.BlockSpec((ts, ts), const2)],
        out_specs=[tok(D), tok(D // 2), tok_t, tok_t, pl.BlockSpec((N_EXPERTS, LANES), const2)],
        out_shape=[jax.ShapeDtypeStruct((B, S, D), F32),
                   jax.ShapeDtypeStruct((B, S, D // 2), jnp.uint32),
                   jax.ShapeDtypeStruct((N_EXPERTS, B * S), jnp.int32),
                   jax.ShapeDtypeStruct((N_EXPERTS, B * S), F32),
                   jax.ShapeDtypeStruct((N_EXPERTS, LANES), jnp.int32)],
        scratch_shapes=[pltpu.VMEM((N_EXPERTS, LANES), jnp.int32), pltpu.VMEM(w_out.shape, BF16)],
        compiler_params=pltpu.CompilerParams(dimension_semantics=("arbitrary", "arbitrary"),
                                             vmem_limit_bytes=VMEM_LIMIT),
        name="outproj_router",
    )(x, attn, pool, mod3, w_out, g2, wr_t, rbias, ut)


def _n_row_blocks(n_tokens):
    n_blocks = -(-(n_tokens * TOP_K + N_EXPERTS * (ROW_BLOCK - 1)) // ROW_BLOCK)
    return -(-n_blocks // BLOCKS_PER_STEP) * BLOCKS_PER_STEP


def _slots_kernel(rank_ref, wd_ref, cnt_ref, lt_ref, slot_ref, wk_ref, be_ref, nu_ref):
    lt = lt_ref[...]
    nblk = lax.shift_right_logical(cnt_ref[...] + (ROW_BLOCK - 1), ROW_BLOCK_LOG2).astype(F32)
    nb_hi, nb_lo = _split_bf16(nblk)
    blk_start = _dot(lt, nb_hi) + _dot(lt, nb_lo)

    @pl.when(pl.program_id(0) == 0)
    def _():
        blk_end = (blk_start + nblk)[:, 0:1]
        b_iota = lax.broadcasted_iota(jnp.int32, (N_EXPERTS, be_ref.shape[-1]), 1).astype(F32)
        owner = jnp.sum(jnp.where(blk_end <= b_iota, 1, 0), axis=0, keepdims=True)
        be_ref[...] = jnp.minimum(owner, N_EXPERTS - 1)
        nu_ref[...] = jnp.broadcast_to(blk_end[N_EXPERTS - 1:, :].astype(jnp.int32), nu_ref.shape)

    rank = rank_ref[...]
    sel = rank >= 0
    row_start = (blk_start[:, 0:1] * ROW_BLOCK).astype(jnp.int32)
    slot_d = row_start + rank
    wd = wd_ref[...]
    choice = _dot(lt, jnp.where(sel, 1.0, 0.0).astype(BF16)).astype(jnp.int32)
    ts = rank.shape[-1]
    slots, wks = [], []
    for k in range(TOP_K):
        m = jnp.logical_and(sel, choice == k)
        slots.append(jnp.sum(jnp.where(m, slot_d, 0), axis=0, keepdims=True))
        wks.append(jnp.sum(jnp.where(m, wd, 0.0), axis=0, keepdims=True))
    slot_ref[...] = jnp.concatenate(slots + [jnp.zeros((SUBLANES - TOP_K, ts), jnp.int32)], axis=0)
    wk_pad = jnp.concatenate(wks + [jnp.zeros((LANES - TOP_K, ts), F32)], axis=0)
    wk_ref[...] = wk_pad.T


def _slots(rank_t, wd_t, cnt, lt):
    T = rank_t.shape[1]
    ts = TS_SLOT
    nb_pad = -(-_n_row_blocks(T) // LANES) * LANES
    const = lambda i: (0, 0)
    tok_t = pl.BlockSpec((N_EXPERTS, ts), lambda i: (0, i))
    return pl.pallas_call(
        _slots_kernel,
        grid=(T // ts,),
        in_specs=[tok_t, tok_t, pl.BlockSpec((N_EXPERTS, LANES), const), pl.BlockSpec((N_EXPERTS, N_EXPERTS), const)],
        out_specs=[pl.BlockSpec((SUBLANES, ts), lambda i: (0, i)),
                   pl.BlockSpec((ts, LANES), lambda i: (i, 0)),
                   pl.BlockSpec((1, nb_pad), const),
                   pl.BlockSpec((1, LANES), const)],
        out_shape=[jax.ShapeDtypeStruct((SUBLANES, T), jnp.int32),
                   jax.ShapeDtypeStruct((T, LANES), F32),
                   jax.ShapeDtypeStruct((1, nb_pad), jnp.int32),
                   jax.ShapeDtypeStruct((1, LANES), jnp.int32)],
        compiler_params=pltpu.CompilerParams(dimension_semantics=("arbitrary",), vmem_limit_bytes=VMEM_LIMIT),
        name="slots",
    )(rank_t, wd_t, cnt, lt)


def _sc_mesh():
    return plsc.VectorSubcoreMesh(core_axis_name="c", subcore_axis_name="s",
                                  num_cores=SC_CORES, num_subcores=SC_SUBCORES)


def _sc_dispatch(rows, slot_flat, n_out):
    T, width = rows.shape
    n_workers = SC_CORES * SC_SUBCORES
    per_worker = T // n_workers
    steps = per_worker // SC_CHUNK

    assert steps % 2 == 0
    buf_types = ([pltpu.VMEM((SC_CHUNK, width), rows.dtype)] + [pltpu.VMEM((SC_CHUNK,), jnp.int32)] * TOP_K
                 + [pltpu.SemaphoreType.DMA])
    n_buf = len(buf_types)

    @functools.partial(
        pl.kernel, mesh=_sc_mesh(),
        out_type=jax.ShapeDtypeStruct((n_out, width), rows.dtype),
        scratch_types=buf_types * 2,
        name="dispatch",
    )
    def run(rows_hbm, slot_hbm, out_hbm, *scratch):
        base = (lax.axis_index("s") * SC_CORES + lax.axis_index("c")) * per_worker
        bufs = (scratch[:n_buf], scratch[n_buf:])

        def scatters(b):
            rows_v, idx_v, sem = bufs[b][0], bufs[b][1:1 + TOP_K], bufs[b][1 + TOP_K]
            return [pltpu.make_async_copy(rows_v, out_hbm.at[idx_v[k]], sem) for k in range(TOP_K)]

        def load(chunk, b):
            off = base + chunk * SC_CHUNK
            pltpu.sync_copy(rows_hbm.at[pl.ds(off, SC_CHUNK)], bufs[b][0])
            for k in range(TOP_K):
                pltpu.sync_copy(slot_hbm.at[pl.ds(k * T + off, SC_CHUNK)], bufs[b][1 + k])

        def start(b):
            for cp in scatters(b):
                cp.start()

        def wait(b):
            for cp in scatters(b):
                cp.wait()

        load(0, 0)
        start(0)

        @pl.loop(0, steps, step=2)
        def _(chunk):
            load(chunk + 1, 1)
            wait(0)
            start(1)
            more = chunk + 2 < steps

            @pl.when(more)
            def _():
                load(chunk + 2, 0)

            wait(1)

            @pl.when(more)
            def _():
                start(0)

    return run(rows, slot_flat)


def _sc_gather(rows, idx):
    n = idx.shape[0]
    width = rows.shape[1]
    n_workers = SC_CORES * SC_SUBCORES
    per_worker = n // n_workers
    steps = per_worker // SC_CHUNK

    assert steps % 2 == 0
    slot_types = [pltpu.VMEM((SC_CHUNK,), jnp.int32), pltpu.VMEM((SC_CHUNK, width), rows.dtype),
                  pltpu.SemaphoreType.DMA]

    @functools.partial(
        pl.kernel, mesh=_sc_mesh(),
        out_type=jax.ShapeDtypeStruct((n, width), rows.dtype),
        scratch_types=slot_types * 2,
        name="combine",
    )
    def run(rows_hbm, idx_hbm, out_hbm, *scratch):
        base = (lax.axis_index("s") * SC_CORES + lax.axis_index("c")) * per_worker
        slots = (scratch[0:3], scratch[3:6])

        def gather(slot):
            idx_v, rows_v, sem = slots[slot]
            return pltpu.make_async_copy(rows_hbm.at[idx_v], rows_v, sem)

        def start(chunk, slot):
            pltpu.sync_copy(idx_hbm.at[pl.ds(base + chunk * SC_CHUNK, SC_CHUNK)], slots[slot][0])
            gather(slot).start()

        def finish(chunk, slot):
            gather(slot).wait()
            pltpu.sync_copy(slots[slot][1], out_hbm.at[pl.ds(base + chunk * SC_CHUNK, SC_CHUNK)])

        start(0, 0)

        @pl.loop(0, steps, step=2)
        def _(chunk):
            start(chunk + 1, 1)
            finish(chunk, 0)

            @pl.when(chunk + 2 < steps)
            def _():
                start(chunk + 2, 0)

            finish(chunk + 1, 1)

    return run(rows, idx)


def _swiglu_packed(xp, wgu, wd):
    lo, hi = _unpack_bf16_pair(xp)
    half = xp.shape[-1]
    gu = _dot(lo, wgu[:half, :]) + _dot(hi, wgu[half:, :])
    dh = wgu.shape[-1] // 2
    act = _silu(gu[:, :dh]) * gu[:, dh:]
    return _dot(act.astype(BF16), wd)


def _cast_swiglu_weights(wg32_ref, wu32_ref, wd32_ref, wgu_ref, wd_ref):
    dh = wg32_ref.shape[-1]
    wgu_ref[:, :dh] = wg32_ref[...].reshape(wg32_ref.shape[-2:]).astype(BF16)
    wgu_ref[:, dh:] = wu32_ref[...].reshape(wu32_ref.shape[-2:]).astype(BF16)
    wd_ref[...] = wd32_ref[...].reshape(wd32_ref.shape[-2:]).astype(BF16)


def _experts_kernel(be_ref, nu_ref, x_ref, *refs):
    n = BLOCKS_PER_STEP
    w32 = [refs[3 * j:3 * j + 3] for j in range(n)]
    y_ref = refs[3 * n]
    wbf = [refs[3 * n + 1 + 2 * j:3 * n + 3 + 2 * j] for j in range(n)]
    step = pl.program_id(0)
    for j in range(n):
        blk = step * n + j
        new_expert = jnp.logical_or(step == 0, be_ref[blk] != be_ref[jnp.maximum(blk - n, 0)])

        @pl.when(new_expert)
        def _():
            _cast_swiglu_weights(*w32[j], *wbf[j])

    @pl.when(step * n < nu_ref[0])
    def _():
        for j in range(n):
            rows = pl.ds(j * ROW_BLOCK, ROW_BLOCK)
            y = _swiglu_packed(x_ref[rows, :], wbf[j][0][...], wbf[j][1][...])
            half = y.shape[-1] // 2
            y_ref[rows, :] = _pack_bf16_pair(y[:, :half], y[:, half:])


def _experts(xs, block_expert, n_used, w_gate, w_up, w_down):
    P, half = xs.shape
    _, D, dh = w_gate.shape
    n = BLOCKS_PER_STEP
    rows = n * ROW_BLOCK
    assert P % rows == 0

    def by_expert(shape, j):
        return pl.BlockSpec((1,) + shape, lambda s, be, nu: (be[s * n + j], 0, 0))

    w_specs, w_args = [], []
    for j in range(n):
        w_specs += [by_expert((D, dh), j), by_expert((D, dh), j), by_expert((dh, D), j)]
        w_args += [w_gate, w_up, w_down]
    def row_step(s, be, nu):
        return jnp.minimum(s, lax.div(jnp.maximum(nu[0], 1) - 1, n)), 0

    grid_spec = pltpu.PrefetchScalarGridSpec(
        num_scalar_prefetch=2,
        grid=(P // rows,),
        in_specs=[pl.BlockSpec((rows, half), row_step)] + w_specs,
        out_specs=pl.BlockSpec((rows, half), row_step),
        scratch_shapes=[pltpu.VMEM((D, 2 * dh), BF16), pltpu.VMEM((dh, D), BF16)] * n,
    )
    return pl.pallas_call(
        _experts_kernel,
        grid_spec=grid_spec,
        out_shape=jax.ShapeDtypeStruct((P, half), jnp.uint32),
        compiler_params=pltpu.CompilerParams(dimension_semantics=("arbitrary",), vmem_limit_bytes=VMEM_LIMIT),
        name="experts",
    )(block_expert, n_used, xs, *w_args)


def _final_kernel(x1_ref, hp_ref, g_ref, wk_ref, mod_ref, sg32_ref, su32_ref, sd32_ref, *rest):
    o_ref, sgu_ref, sd_ref = rest[-3:]

    @pl.when(_first_grid_step())
    def _():
        _cast_swiglu_weights(sg32_ref, su32_ref, sd32_ref, sgu_ref, sd_ref)

    acc = _swiglu_packed(hp_ref[0], sgu_ref[...], sd_ref[...])
    wk = wk_ref[0]
    for k in range(TOP_K):
        lo, hi = _unpack_bf16_pair(g_ref[k, 0])
        y = jnp.concatenate([lo.astype(F32), hi.astype(F32)], axis=-1)
        acc = acc + wk[:, k:k + 1] * y
    gate2 = mod_ref[0, 5:6, :]
    o_ref[0] = x1_ref[0] + gate2 * acc


def _final(x1, hp, g, wk_tok, mod3, ws_gate, ws_up, ws_down, prev_out, b0, b_total):
    B, S, D = x1.shape
    dh = ws_gate.shape[-1]
    ts = TS_PROJ
    tok = lambda w: pl.BlockSpec((1, ts, w), lambda b, s: (b, s, 0))
    const2 = lambda b, s: (0, 0)
    in_specs = [tok(D), tok(D // 2),
                pl.BlockSpec((TOP_K, 1, ts, D // 2), lambda b, s: (0, b, s, 0)),
                tok(LANES),
                pl.BlockSpec((1, N_MOD, D), lambda b, s: (b + b0, 0, 0)),
                pl.BlockSpec(ws_gate.shape, const2),
                pl.BlockSpec(ws_up.shape, const2),
                pl.BlockSpec(ws_down.shape, const2)]
    args = [x1, hp, g, wk_tok, mod3, ws_gate, ws_up, ws_down]
    aliases = {}
    if prev_out is not None:
        in_specs.append(pl.BlockSpec(memory_space=pl.ANY))
        args.append(prev_out)
        aliases = {len(args) - 1: 0}
    return pl.pallas_call(
        _final_kernel,
        grid=(B, S // ts),
        in_specs=in_specs,
        out_specs=pl.BlockSpec((1, ts, D), lambda b, s: (b + b0, s, 0)),
        out_shape=jax.ShapeDtypeStruct((b_total, S, D), F32),
        scratch_shapes=[pltpu.VMEM((D, 2 * dh), BF16), pltpu.VMEM((dh, D), BF16)],
        input_output_aliases=aliases,
        compiler_params=pltpu.CompilerParams(dimension_semantics=("arbitrary", "arbitrary"),
                                             vmem_limit_bytes=VMEM_LIMIT),
        name="final",
    )(*args)


def _layer(x, c_act_mod, norm1_g, norm2_g, w_in, q_norm_g, k_norm_g, w_pool, pool_scale, w_out,
           w_router, router_bias, w_gate, w_up, w_down, ws_gate, ws_up, ws_down):
    B, S, D = x.shape
    mod3 = c_act_mod.reshape(B, N_MOD, D)
    head_of = jnp.arange(D_ATTN, dtype=jnp.int32) // HEAD_DIM
    head_mean = jnp.where(head_of[:, None] == head_of[None, :], 1.0 / HEAD_DIM, 0.0).astype(BF16)
    j = jnp.arange(TK, dtype=jnp.int32)
    umat = jnp.where(j[:, None] >= j[None, :], -1.0, 0.0).astype(BF16)

    q, k, v, pool = _inproj(
        x, mod3, norm1_g.reshape(1, D), w_in,
        jnp.tile(q_norm_g, N_HEADS).reshape(1, D_ATTN), jnp.tile(k_norm_g, N_HEADS).reshape(1, D_ATTN),
        head_mean, w_pool, pool_scale.reshape(1, D_POOL))
    attn = _attention(q, k, v, umat)
    t = jnp.arange(TS_PROJ, dtype=jnp.int32)
    ut = (t[:, None] < t[None, :]).astype(BF16)
    e = jnp.arange(N_EXPERTS, dtype=jnp.int32)
    lt = (e[None, :] < e[:, None]).astype(BF16)
    bp = B // MOE_PARTS
    T = bp * S
    n_blocks = _n_row_blocks(T)
    out = None
    for part in range(MOE_PARTS):
        b0 = part * bp
        x1, hp, rank_t, wd_t, cnt = _outproj(x, attn, pool, mod3, w_out, norm2_g.reshape(1, D),
                                             w_router.T, router_bias.reshape(N_EXPERTS, 1), ut, b0, bp)
        slots, wk_tok, block_expert, n_used = _slots(rank_t, wd_t, cnt, lt)
        slot_flat = slots[:TOP_K].reshape(TOP_K * T)
        xs = _sc_dispatch(hp.reshape(T, D // 2), slot_flat, n_blocks * ROW_BLOCK)
        ys = _experts(xs, block_expert[0, :n_blocks], n_used[0, :1], w_gate, w_up, w_down)
        g = _sc_gather(ys, slot_flat).reshape(TOP_K, bp, S, D // 2)
        out = _final(x1, hp, g, wk_tok.reshape(bp, S, LANES), mod3, ws_gate, ws_up, ws_down, out, b0, B)
    return out


def kernel(x, c, w_ada, b_ada, norm1_g, norm2_g, w_in, q_norm_g, k_norm_g, w_pool, pool_scale, w_out,
           w_router, router_bias, w_gate, w_up, w_down, ws_gate, ws_up, ws_down):
    depth = w_ada.shape[0]
    for l in range(depth):
        mod = _adaln(c, w_ada[l], b_ada[l])
        x = _layer(x, mod, norm1_g[l], norm2_g[l], w_in[l], q_norm_g[l], k_norm_g[l], w_pool[l],
                   pool_scale[l], w_out[l], w_router[l], router_bias[l], w_gate[l], w_up[l], w_down[l],
                   ws_gate[l], ws_up[l], ws_down[l])
    return x
```

```python
import functools

import jax
import jax.numpy as jnp
from jax import lax
from jax.experimental import pallas as pl
from jax.experimental.pallas import tpu as pltpu
from jax.experimental.pallas import tpu_sc as plsc

F32 = jnp.float32
BF16 = jnp.bfloat16

HEAD_DIM = 64
N_HEADS = 8
D_ATTN = N_HEADS * HEAD_DIM
POOL_WINDOWS = (2, 4, 8, 16)
POOL_GROUP_DIM = 128
D_POOL = len(POOL_WINDOWS) * POOL_GROUP_DIM
MAX_WINDOW = max(POOL_WINDOWS)
assert all(w & (w - 1) == 0 for w in POOL_WINDOWS)
N_EXPERTS = 64
TOP_K = 6
N_GROUPS = 8
GROUP_SIZE = N_EXPERTS // N_GROUPS
TOPK_GROUPS = 4
ROUTED_SCALE = 2.5
RMS_EPS = 1e-6
N_MOD = 6

LANES = 128
SUBLANES = 8
VMEM_LIMIT = 56 * 1024 * 1024

TS_PROJ = 1024
TQ = 1024
TK = 256
PAIRS_PER_STEP = 4
TS_SLOT = 2048
ROW_BLOCK_LOG2 = 9
ROW_BLOCK = 1 << ROW_BLOCK_LOG2
BLOCKS_PER_STEP = 2
MOE_PARTS = 2
SC_CORES = 2
SC_SUBCORES = 16
SC_CHUNK = 64


def _split_bf16(a):
    hi = a.astype(BF16)
    lo = (a - hi.astype(F32)).astype(BF16)
    return hi, lo


def _dot(a, b):
    return jnp.dot(a, b, preferred_element_type=F32)


def _dot_nt(a, b):
    return lax.dot_general(a, b, (((1,), (1,)), ((), ())), preferred_element_type=F32)


def _dot3(a, b):
    ah, al = _split_bf16(a)
    bh, bl = _split_bf16(b)
    return _dot(ah, bh) + _dot(ah, bl) + _dot(al, bh)


def _dot3_nt(a, b):
    ah, al = _split_bf16(a)
    bh, bl = _split_bf16(b)
    return _dot_nt(ah, bh) + _dot_nt(ah, bl) + _dot_nt(al, bh)


def _silu(x):
    return x * (1.0 / (1.0 + jnp.exp(-x)))


def _rms_mod(x, gain, scale, shift):
    ms = jnp.mean(x * x, axis=-1, keepdims=True)
    y = x * lax.rsqrt(ms + RMS_EPS) * gain
    return y * (1.0 + scale) + shift


def _adaln_kernel(c_ref, w_ref, b_ref, o_ref):
    c = c_ref[...]
    o_ref[...] = _dot3(_silu(c), w_ref[...]) + b_ref[...]


def _adaln(c, w_ada, b_ada):
    nb, D = c.shape
    B = -(-nb // SUBLANES) * SUBLANES
    c = jnp.pad(c, ((0, B - nb), (0, 0)))
    N = w_ada.shape[1]
    tn = 1024
    out = pl.pallas_call(
        _adaln_kernel,
        grid=(N // tn,),
        in_specs=[pl.BlockSpec((B, D), lambda j: (0, 0)),
                  pl.BlockSpec((D, tn), lambda j: (0, j)),
                  pl.BlockSpec((1, tn), lambda j: (0, j))],
        out_specs=pl.BlockSpec((B, tn), lambda j: (0, j)),
        out_shape=jax.ShapeDtypeStruct((B, N), F32),
        compiler_params=pltpu.CompilerParams(dimension_semantics=("arbitrary",),
                                             vmem_limit_bytes=VMEM_LIMIT),
        name="adaln",
    )(c, w_ada, b_ada.reshape(1, N))
    return out[:nb]


def _first_grid_step():
    return jnp.logical_and(pl.program_id(0) == 0, pl.program_id(1) == 0)


def _inproj_kernel(x_ref, mod_ref, g1_ref, win32_ref, qg_ref, kg_ref, hm_ref, wp32_ref, ps_ref,
                   q_ref, k_ref, v_ref, p_ref, ext_ref, win_ref, wp_ref):
    @pl.when(_first_grid_step())
    def _():
        win_ref[...] = win32_ref[...].astype(BF16)
        wp_ref[...] = wp32_ref[...].astype(BF16)

    si = pl.program_id(1)
    ts = x_ref.shape[1]
    x = x_ref[0]
    shift1 = mod_ref[0, 0:1, :]
    scale1 = mod_ref[0, 1:2, :]
    h = _rms_mod(x, g1_ref[...], scale1, shift1)
    proj = _dot(h.astype(BF16), win_ref[...])

    hm = hm_ref[...]

    def head_norm(t, gain):
        ms = _dot((t * t).astype(BF16), hm)
        return t * lax.rsqrt(ms + RMS_EPS) * gain

    hq = proj[:, 0:D_ATTN]
    hk = proj[:, D_ATTN:2 * D_ATTN]
    q_ref[0] = (head_norm(hq, qg_ref[...]) * (HEAD_DIM ** -0.5)).astype(BF16)
    k_ref[0] = head_norm(hk, kg_ref[...]).astype(BF16)
    v_ref[0] = proj[:, 2 * D_ATTN:3 * D_ATTN].astype(BF16)

    hp = proj[:, 3 * D_ATTN:]

    @pl.when(si == 0)
    def _():
        ext_ref[0:MAX_WINDOW, :] = jnp.zeros((MAX_WINDOW, D_POOL), F32)

    ext_ref[MAX_WINDOW:, :] = hp
    pos = si * ts + lax.broadcasted_iota(jnp.int32, (ts, 1), 0)
    for g, w in enumerate(POOL_WINDOWS):
        lo_l, hi_l = g * POOL_GROUP_DIM, (g + 1) * POOL_GROUP_DIM
        u = hp[:, lo_l:hi_l]
        run = ext_ref[:, lo_l:hi_l]
        span = 1
        while span < w:
            run = run + pltpu.roll(run, span, axis=0)
            span *= 2
        acc = run[MAX_WINDOW:]
        count = jnp.minimum(pos + 1, w).astype(F32)
        d = acc / count - u
        mixed = _dot(d.astype(BF16), wp_ref[g])
        p_ref[0, :, lo_l:hi_l] = (mixed * ps_ref[:, lo_l:hi_l]).astype(BF16)
    ext_ref[0:MAX_WINDOW, :] = hp[ts - MAX_WINDOW:, :]


def _inproj(x, mod3, g1, w_in, qg_t, kg_t, head_mean, w_pool, pool_scale):
    B, S, D = x.shape
    ts = TS_PROJ
    out_sd = jax.ShapeDtypeStruct((B, S, D_ATTN), BF16)
    blk = pl.BlockSpec((1, ts, D_ATTN), lambda b, s: (b, s, 0))
    const2 = lambda b, s: (0, 0)
    return pl.pallas_call(
        _inproj_kernel,
        grid=(B, S // ts),
        in_specs=[pl.BlockSpec((1, ts, D), lambda b, s: (b, s, 0)),
                  pl.BlockSpec((1, N_MOD, D), lambda b, s: (b, 0, 0)),
                  pl.BlockSpec((1, D), const2),
                  pl.BlockSpec(w_in.shape, const2),
                  pl.BlockSpec((1, D_ATTN), const2),
                  pl.BlockSpec((1, D_ATTN), const2),
                  pl.BlockSpec((D_ATTN, D_ATTN), const2),
                  pl.BlockSpec(w_pool.shape, lambda b, s: (0, 0, 0)),
                  pl.BlockSpec((1, D_POOL), const2)],
        out_specs=[blk, blk, blk, blk],
        out_shape=[out_sd, out_sd, out_sd, out_sd],
        scratch_shapes=[pltpu.VMEM((MAX_WINDOW + ts, D_POOL), F32),
                        pltpu.VMEM(w_in.shape, BF16), pltpu.VMEM(w_pool.shape, BF16)],
        compiler_params=pltpu.CompilerParams(dimension_semantics=("arbitrary", "arbitrary"),
                                             vmem_limit_bytes=VMEM_LIMIT),
        name="inproj",
    )(x, mod3, g1, w_in, qg_t, kg_t, head_mean, w_pool, pool_scale)


def _attn_kernel(q_ref, k_ref, v_ref, u_ref, o_ref, acc_ref):
    qi = pl.program_id(2)
    lane = lax.broadcasted_iota(jnp.int32, (TQ, LANES), 1)
    first = lane < HEAD_DIM
    n_heads = 2 * PAIRS_PER_STEP
    pair_lanes = lambda h: pl.ds((h // 2) * LANES, LANES)
    qh = []
    for h in range(n_heads):
        q = q_ref[0, :, pair_lanes(h)]
        keep = first if h % 2 == 0 else jnp.logical_not(first)
        qh.append(jnp.where(keep, q, jnp.zeros_like(q)))
    u = u_ref[...]
    row = lax.broadcasted_iota(jnp.int32, (TQ, TK), 0)
    col = lax.broadcasted_iota(jnp.int32, (TQ, TK), 1)
    acc_ref[...] = jnp.zeros(acc_ref.shape, F32)

    def block(kb, survs, diag):
        start = pl.multiple_of(kb * TK, TK)
        r0 = 0 if diag is None else diag * TK
        if diag is not None:
            valid = (col + r0 < row)[r0:]
        out = []
        for h in range(n_heads):
            k = k_ref[0, pl.ds(start, TK), pair_lanes(h)]
            v = v_ref[0, pl.ds(start, TK), pair_lanes(h)]
            z = _dot_nt(qh[h][r0:], k)
            neg_abs = pltpu.bitcast(pltpu.bitcast(z, jnp.uint32) | jnp.uint32(0x80000000), F32)
            sp = jnp.maximum(z, 0.0) + jnp.log(1.0 + jnp.exp(neg_abs))
            if diag is not None:
                sp = jnp.where(valid, sp, 0.0)
            r = _dot(sp.astype(BF16), u)
            arg = z + r + survs[h][r0:]
            if diag is not None:
                arg = jnp.where(valid, arg, -jnp.inf)
            acc_ref[h, r0:, :] += _dot(jnp.exp(arg).astype(BF16), v)
            surv = survs[h][r0:] + r[:, 0:1]
            out.append(surv if r0 == 0 else jnp.concatenate([survs[h][:r0], surv], axis=0))
        return tuple(out)

    survs = tuple(jnp.zeros((TQ, 1), F32) for _ in range(n_heads))
    n_diag = TQ // TK
    for d in reversed(range(n_diag)):
        survs = block(qi * n_diag + d, survs, d)
    def full_blocks(i, c):
        for d in range(n_diag):
            c = block((qi - i) * n_diag - 1 - d, c, None)
        return c

    lax.fori_loop(0, qi, full_blocks, survs)
    for p in range(PAIRS_PER_STEP):
        o_ref[0, :, pl.ds(p * LANES, LANES)] = jnp.where(first, acc_ref[2 * p], acc_ref[2 * p + 1]).astype(BF16)


def _attention(q, k, v, umat):
    B, S, _ = q.shape
    width = PAIRS_PER_STEP * LANES
    kv_spec = pl.BlockSpec((1, S, width), lambda b, p, i: (b, 0, p))
    q_spec = pl.BlockSpec((1, TQ, width), lambda b, p, i: (b, i, p))
    return pl.pallas_call(
        _attn_kernel,
        grid=(B, D_ATTN // width, S // TQ),
        in_specs=[q_spec, kv_spec, kv_spec, pl.BlockSpec((TK, TK), lambda b, p, i: (0, 0))],
        out_specs=q_spec,
        out_shape=jax.ShapeDtypeStruct((B, S, D_ATTN), BF16),
        scratch_shapes=[pltpu.VMEM((2 * PAIRS_PER_STEP, TQ, LANES), F32)],
        compiler_params=pltpu.CompilerParams(
            dimension_semantics=("arbitrary", "arbitrary", "arbitrary"),
            vmem_limit_bytes=VMEM_LIMIT),
        name="stickbreak_attn",
    )(q, k, v, umat)


def _route_t(scores, biased):
    ts = scores.shape[-1]
    neg = -jnp.inf
    b3 = biased.reshape(N_GROUPS, GROUP_SIZE, ts)
    e_in_g = lax.broadcasted_iota(jnp.int32, b3.shape, 1)
    m1 = jnp.max(b3, axis=1, keepdims=True)
    i1 = jnp.min(jnp.where(b3 == m1, e_in_g, GROUP_SIZE), axis=1, keepdims=True)
    m2 = jnp.max(jnp.where(e_in_g == i1, neg, b3), axis=1, keepdims=True)
    gs = (m1 + m2)[:, 0, :]
    g_iota = lax.broadcasted_iota(jnp.int32, gs.shape, 0)
    g_sel = jnp.zeros(gs.shape, jnp.bool_)
    for _ in range(TOPK_GROUPS):
        gm = jnp.max(gs, axis=0, keepdims=True)
        gi = jnp.min(jnp.where(gs == gm, g_iota, N_GROUPS), axis=0, keepdims=True)
        pick = g_iota == gi
        g_sel = jnp.logical_or(g_sel, pick)
        gs = jnp.where(pick, neg, gs)
    masked = jnp.where(g_sel[:, None, :], b3, neg)
    flat = lax.broadcasted_iota(jnp.int32, b3.shape, 0) * GROUP_SIZE + e_in_g
    sel = jnp.zeros(b3.shape, jnp.bool_)
    for _ in range(TOP_K):
        m = jnp.max(jnp.max(masked, axis=1, keepdims=True), axis=0, keepdims=True)
        cand = jnp.where(masked == m, flat, N_EXPERTS)
        idx = jnp.min(jnp.min(cand, axis=1, keepdims=True), axis=0, keepdims=True)
        pick = flat == idx
        sel = jnp.logical_or(sel, pick)
        masked = jnp.where(pick, neg, masked)
    s3 = scores.reshape(N_GROUPS, GROUP_SIZE, ts)
    w = jnp.where(sel, s3, 0.0)
    tot = jnp.sum(jnp.sum(w, axis=1, keepdims=True), axis=0, keepdims=True)
    return (w / tot * ROUTED_SCALE).reshape(N_EXPERTS, ts), sel.reshape(N_EXPERTS, ts)


def _pack_bf16_pair(lo, hi):
    lo_bits = pltpu.bitcast(lo.astype(BF16).astype(F32), jnp.uint32) >> 16
    hi_bits = pltpu.bitcast(hi.astype(BF16).astype(F32), jnp.uint32) & jnp.uint32(0xFFFF0000)
    return lo_bits | hi_bits


def _unpack_bf16_pair(p):
    lo = pltpu.bitcast(p << 16, F32).astype(BF16)
    hi = pltpu.bitcast(p & jnp.uint32(0xFFFF0000), F32).astype(BF16)
    return lo, hi


def _outproj_kernel(x_ref, a_ref, p_ref, mod_ref, wo32_ref, g2_ref, wrt_ref, rb_ref, ut_ref,
                    x1_ref, hp_ref, rank_ref, wd_ref, cnt_ref, run_ref, wo_ref):
    @pl.when(_first_grid_step())
    def _():
        run_ref[...] = jnp.zeros(run_ref.shape, jnp.int32)
        wo_ref[...] = wo32_ref[...].astype(BF16)

    gate1 = mod_ref[0, 2:3, :]
    shift2 = mod_ref[0, 3:4, :]
    scale2 = mod_ref[0, 4:5, :]
    mixp = _dot(a_ref[0], wo_ref[0:D_ATTN, :]) + _dot(p_ref[0], wo_ref[D_ATTN:, :])
    x1 = x_ref[0] + gate1 * mixp
    x1_ref[0] = x1
    h2 = _rms_mod(x1, g2_ref[...], scale2, shift2)
    half = h2.shape[-1] // 2
    hp_ref[0] = _pack_bf16_pair(h2[:, :half], h2[:, half:])
    logits_t = _dot3_nt(wrt_ref[...], h2)
    scores = 1.0 / (1.0 + jnp.exp(-logits_t))
    wd_t, sel = _route_t(scores, scores + rb_ref[...])
    wd_ref[...] = wd_t
    self_f = jnp.where(sel, 1.0, 0.0)
    before = _dot(self_f.astype(BF16), ut_ref[...]).astype(jnp.int32)
    run = run_ref[:, 0:1]
    rank_ref[...] = jnp.where(sel, run + before, -1)
    run_ref[...] = run_ref[...] + jnp.sum(self_f, axis=1, keepdims=True).astype(jnp.int32)
    cnt_ref[...] = run_ref[...]


def _outproj(x, attn, pool, mod3, w_out, g2, wr_t, rbias, ut, b0, B):
    _, S, D = x.shape
    ts = TS_PROJ
    n_s = S // ts
    const2 = lambda b, s: (0, 0)
    tok_in = lambda w: pl.BlockSpec((1, ts, w), lambda b, s: (b + b0, s, 0))
    tok = lambda w: pl.BlockSpec((1, ts, w), lambda b, s: (b, s, 0))
    tok_t = pl.BlockSpec((N_EXPERTS, ts), lambda b, s: (0, b * n_s + s))
    return pl.pallas_call(
        _outproj_kernel,
        grid=(B, n_s),
        in_specs=[tok_in(D), tok_in(D_ATTN), tok_in(D_POOL),
                  pl.BlockSpec((1, N_MOD, D), lambda b, s: (b + b0, 0, 0)),
                  pl.BlockSpec(w_out.shape, const2),
                  pl.BlockSpec((1, D), const2),
                  pl.BlockSpec(wr_t.shape, const2),
                  pl.BlockSpec((N_EXPERTS, 1), const2),
                  pl.BlockSpec((ts, ts), const2)],
        out_specs=[tok(D), tok(D // 2), tok_t, tok_t, pl.BlockSpec((N_EXPERTS, LANES), const2)],
        out_shape=[jax.ShapeDtypeStruct((B, S, D), F32),
                   jax.ShapeDtypeStruct((B, S, D // 2), jnp.uint32),
                   jax.ShapeDtypeStruct((N_EXPERTS, B * S), jnp.int32),
                   jax.ShapeDtypeStruct((N_EXPERTS, B * S), F32),
                   jax.ShapeDtypeStruct((N_EXPERTS, LANES), jnp.int32)],
        scratch_shapes=[pltpu.VMEM((N_EXPERTS, LANES), jnp.int32), pltpu.VMEM(w_out.shape, BF16)],
        compiler_params=pltpu.CompilerParams(dimension_semantics=("arbitrary", "arbitrary"),
                                             vmem_limit_bytes=VMEM_LIMIT),
        name="outproj_router",
    )(x, attn, pool, mod3, w_out, g2, wr_t, rbias, ut)


def _n_row_blocks(n_tokens):
    n_blocks = -(-(n_tokens * TOP_K + N_EXPERTS * (ROW_BLOCK - 1)) // ROW_BLOCK)
    return -(-n_blocks // BLOCKS_PER_STEP) * BLOCKS_PER_STEP


def _slots_kernel(rank_ref, wd_ref, cnt_ref, lt_ref, slot_ref, wk_ref, be_ref, nu_ref):
    lt = lt_ref[...]
    nblk = lax.shift_right_logical(cnt_ref[...] + (ROW_BLOCK - 1), ROW_BLOCK_LOG2).astype(F32)
    nb_hi, nb_lo = _split_bf16(nblk)
    blk_start = _dot(lt, nb_hi) + _dot(lt, nb_lo)

    @pl.when(pl.program_id(0) == 0)
    def _():
        blk_end = (blk_start + nblk)[:, 0:1]
        b_iota = lax.broadcasted_iota(jnp.int32, (N_EXPERTS, be_ref.shape[-1]), 1).astype(F32)
        owner = jnp.sum(jnp.where(blk_end <= b_iota, 1, 0), axis=0, keepdims=True)
        be_ref[...] = jnp.minimum(owner, N_EXPERTS - 1)
        nu_ref[...] = jnp.broadcast_to(blk_end[N_EXPERTS - 1:, :].astype(jnp.int32), nu_ref.shape)

    rank = rank_ref[...]
    sel = rank >= 0
    row_start = (blk_start[:, 0:1] * ROW_BLOCK).astype(jnp.int32)
    slot_d = row_start + rank
    wd = wd_ref[...]
    choice = _dot(lt, jnp.where(sel, 1.0, 0.0).astype(BF16)).astype(jnp.int32)
    ts = rank.shape[-1]
    slots, wks = [], []
    for k in range(TOP_K):
        m = jnp.logical_and(sel, choice == k)
        slots.append(jnp.sum(jnp.where(m, slot_d, 0), axis=0, keepdims=True))
        wks.append(jnp.sum(jnp.where(m, wd, 0.0), axis=0, keepdims=True))
    slot_ref[...] = jnp.concatenate(slots + [jnp.zeros((SUBLANES - TOP_K, ts), jnp.int32)], axis=0)
    wk_pad = jnp.concatenate(wks + [jnp.zeros((LANES - TOP_K, ts), F32)], axis=0)
    wk_ref[...] = wk_pad.T


def _slots(rank_t, wd_t, cnt, lt):
    T = rank_t.shape[1]
    ts = TS_SLOT
    nb_pad = -(-_n_row_blocks(T) // LANES) * LANES
    const = lambda i: (0, 0)
    tok_t = pl.BlockSpec((N_EXPERTS, ts), lambda i: (0, i))
    return pl.pallas_call(
        _slots_kernel,
        grid=(T // ts,),
        in_specs=[tok_t, tok_t, pl.BlockSpec((N_EXPERTS, LANES), const), pl.BlockSpec((N_EXPERTS, N_EXPERTS), const)],
        out_specs=[pl.BlockSpec((SUBLANES, ts), lambda i: (0, i)),
                   pl.BlockSpec((ts, LANES), lambda i: (i, 0)),
                   pl.BlockSpec((1, nb_pad), const),
                   pl.BlockSpec((1, LANES), const)],
        out_shape=[jax.ShapeDtypeStruct((SUBLANES, T), jnp.int32),
                   jax.ShapeDtypeStruct((T, LANES), F32),
                   jax.ShapeDtypeStruct((1, nb_pad), jnp.int32),
                   jax.ShapeDtypeStruct((1, LANES), jnp.int32)],
        compiler_params=pltpu.CompilerParams(dimension_semantics=("arbitrary",), vmem_limit_bytes=VMEM_LIMIT),
        name="slots",
    )(rank_t, wd_t, cnt, lt)


def _sc_mesh():
    return plsc.VectorSubcoreMesh(core_axis_name="c", subcore_axis_name="s",
                                  num_cores=SC_CORES, num_subcores=SC_SUBCORES)


def _sc_dispatch(rows, slot_flat, n_out):
    T, width = rows.shape
    n_workers = SC_CORES * SC_SUBCORES
    per_worker = T // n_workers
    steps = per_worker // SC_CHUNK

    @functools.partial(
        pl.kernel, mesh=_sc_mesh(),
        out_type=jax.ShapeDtypeStruct((n_out, width), rows.dtype),
        scratch_types=[pltpu.VMEM((SC_CHUNK, width), rows.dtype)]
        + [pltpu.VMEM((SC_CHUNK,), jnp.int32)] * TOP_K + [pltpu.SemaphoreType.DMA],
        name="dispatch",
    )
    def run(rows_hbm, slot_hbm, out_hbm, rows_v, *rest):
        idx_v, sem = rest[:TOP_K], rest[TOP_K]
        base = (lax.axis_index("s") * SC_CORES + lax.axis_index("c")) * per_worker

        @pl.loop(0, steps)
        def _(i):
            off = base + i * SC_CHUNK
            pltpu.sync_copy(rows_hbm.at[pl.ds(off, SC_CHUNK)], rows_v)
            for k in range(TOP_K):
                pltpu.sync_copy(slot_hbm.at[pl.ds(k * T + off, SC_CHUNK)], idx_v[k])
            copies = [pltpu.async_copy(rows_v, out_hbm.at[idx_v[k]], sem) for k in range(TOP_K)]
            for cp in copies:
                cp.wait()

    return run(rows, slot_flat)


def _sc_gather(rows, idx):
    n = idx.shape[0]
    width = rows.shape[1]
    n_workers = SC_CORES * SC_SUBCORES
    per_worker = n // n_workers
    steps = per_worker // SC_CHUNK

    assert steps % 2 == 0
    slot_types = [pltpu.VMEM((SC_CHUNK,), jnp.int32), pltpu.VMEM((SC_CHUNK, width), rows.dtype),
                  pltpu.SemaphoreType.DMA]

    @functools.partial(
        pl.kernel, mesh=_sc_mesh(),
        out_type=jax.ShapeDtypeStruct((n, width), rows.dtype),
        scratch_types=slot_types * 2,
        name="combine",
    )
    def run(rows_hbm, idx_hbm, out_hbm, *scratch):
        base = (lax.axis_index("s") * SC_CORES + lax.axis_index("c")) * per_worker
        slots = (scratch[0:3], scratch[3:6])

        def gather(slot):
            idx_v, rows_v, sem = slots[slot]
            return pltpu.make_async_copy(rows_hbm.at[idx_v], rows_v, sem)

        def start(chunk, slot):
            pltpu.sync_copy(idx_hbm.at[pl.ds(base + chunk * SC_CHUNK, SC_CHUNK)], slots[slot][0])
            gather(slot).start()

        def finish(chunk, slot):
            gather(slot).wait()
            pltpu.sync_copy(slots[slot][1], out_hbm.at[pl.ds(base + chunk * SC_CHUNK, SC_CHUNK)])

        start(0, 0)

        @pl.loop(0, steps, step=2)
        def _(chunk):
            start(chunk + 1, 1)
            finish(chunk, 0)

            @pl.when(chunk + 2 < steps)
            def _():
                start(chunk + 2, 0)

            finish(chunk + 1, 1)

    return run(rows, idx)


def _swiglu_packed(xp, wgu, wd):
    lo, hi = _unpack_bf16_pair(xp)
    half = xp.shape[-1]
    gu = _dot(lo, wgu[:half, :]) + _dot(hi, wgu[half:, :])
    dh = wgu.shape[-1] // 2
    act = _silu(gu[:, :dh]) * gu[:, dh:]
    return _dot(act.astype(BF16), wd)


def _cast_swiglu_weights(wg32_ref, wu32_ref, wd32_ref, wgu_ref, wd_ref):
    dh = wg32_ref.shape[-1]
    wgu_ref[:, :dh] = wg32_ref[...].reshape(wg32_ref.shape[-2:]).astype(BF16)
    wgu_ref[:, dh:] = wu32_ref[...].reshape(wu32_ref.shape[-2:]).astype(BF16)
    wd_ref[...] = wd32_ref[...].reshape(wd32_ref.shape[-2:]).astype(BF16)


def _experts_kernel(be_ref, nu_ref, x_ref, *refs):
    n = BLOCKS_PER_STEP
    w32 = [refs[3 * j:3 * j + 3] for j in range(n)]
    y_ref = refs[3 * n]
    wbf = [refs[3 * n + 1 + 2 * j:3 * n + 3 + 2 * j] for j in range(n)]
    step = pl.program_id(0)
    for j in range(n):
        blk = step * n + j
        new_expert = jnp.logical_or(step == 0, be_ref[blk] != be_ref[jnp.maximum(blk - n, 0)])

        @pl.when(new_expert)
        def _():
            _cast_swiglu_weights(*w32[j], *wbf[j])

    @pl.when(step * n < nu_ref[0])
    def _():
        for j in range(n):
            rows = pl.ds(j * ROW_BLOCK, ROW_BLOCK)
            y = _swiglu_packed(x_ref[rows, :], wbf[j][0][...], wbf[j][1][...])
            half = y.shape[-1] // 2
            y_ref[rows, :] = _pack_bf16_pair(y[:, :half], y[:, half:])


def _experts(xs, block_expert, n_used, w_gate, w_up, w_down):
    P, half = xs.shape
    _, D, dh = w_gate.shape
    n = BLOCKS_PER_STEP
    rows = n * ROW_BLOCK
    assert P % rows == 0

    def by_expert(shape, j):
        return pl.BlockSpec((1,) + shape, lambda s, be, nu: (be[s * n + j], 0, 0))

    w_specs, w_args = [], []
    for j in range(n):
        w_specs += [by_expert((D, dh), j), by_expert((D, dh), j), by_expert((dh, D), j)]
        w_args += [w_gate, w_up, w_down]
    def row_step(s, be, nu):
        return jnp.minimum(s, lax.div(jnp.maximum(nu[0], 1) - 1, n)), 0

    grid_spec = pltpu.PrefetchScalarGridSpec(
        num_scalar_prefetch=2,
        grid=(P // rows,),
        in_specs=[pl.BlockSpec((rows, half), row_step)] + w_specs,
        out_specs=pl.BlockSpec((rows, half), row_step),
        scratch_shapes=[pltpu.VMEM((D, 2 * dh), BF16), pltpu.VMEM((dh, D), BF16)] * n,
    )
    return pl.pallas_call(
        _experts_kernel,
        grid_spec=grid_spec,
        out_shape=jax.ShapeDtypeStruct((P, half), jnp.uint32),
        compiler_params=pltpu.CompilerParams(dimension_semantics=("arbitrary",), vmem_limit_bytes=VMEM_LIMIT),
        name="experts",
    )(block_expert, n_used, xs, *w_args)


def _final_kernel(x1_ref, hp_ref, g_ref, wk_ref, mod_ref, sg32_ref, su32_ref, sd32_ref, *rest):
    o_ref, sgu_ref, sd_ref = rest[-3:]

    @pl.when(_first_grid_step())
    def _():
        _cast_swiglu_weights(sg32_ref, su32_ref, sd32_ref, sgu_ref, sd_ref)

    acc = _swiglu_packed(hp_ref[0], sgu_ref[...], sd_ref[...])
    wk = wk_ref[0]
    for k in range(TOP_K):
        lo, hi = _unpack_bf16_pair(g_ref[k, 0])
        y = jnp.concatenate([lo.astype(F32), hi.astype(F32)], axis=-1)
        acc = acc + wk[:, k:k + 1] * y
    gate2 = mod_ref[0, 5:6, :]
    o_ref[0] = x1_ref[0] + gate2 * acc


def _final(x1, hp, g, wk_tok, mod3, ws_gate, ws_up, ws_down, prev_out, b0, b_total):
    B, S, D = x1.shape
    dh = ws_gate.shape[-1]
    ts = TS_PROJ
    tok = lambda w: pl.BlockSpec((1, ts, w), lambda b, s: (b, s, 0))
    const2 = lambda b, s: (0, 0)
    in_specs = [tok(D), tok(D // 2),
                pl.BlockSpec((TOP_K, 1, ts, D // 2), lambda b, s: (0, b, s, 0)),
                tok(LANES),
                pl.BlockSpec((1, N_MOD, D), lambda b, s: (b + b0, 0, 0)),
                pl.BlockSpec(ws_gate.shape, const2),
                pl.BlockSpec(ws_up.shape, const2),
                pl.BlockSpec(ws_down.shape, const2)]
    args = [x1, hp, g, wk_tok, mod3, ws_gate, ws_up, ws_down]
    aliases = {}
    if prev_out is not None:
        in_specs.append(pl.BlockSpec(memory_space=pl.ANY))
        args.append(prev_out)
        aliases = {len(args) - 1: 0}
    return pl.pallas_call(
        _final_kernel,
        grid=(B, S // ts),
        in_specs=in_specs,
        out_specs=pl.BlockSpec((1, ts, D), lambda b, s: (b + b0, s, 0)),
        out_shape=jax.ShapeDtypeStruct((b_total, S, D), F32),
        scratch_shapes=[pltpu.VMEM((D, 2 * dh), BF16), pltpu.VMEM((dh, D), BF16)],
        input_output_aliases=aliases,
        compiler_params=pltpu.CompilerParams(dimension_semantics=("arbitrary", "arbitrary"),
                                             vmem_limit_bytes=VMEM_LIMIT),
        name="final",
    )(*args)


def _layer(x, c_act_mod, norm1_g, norm2_g, w_in, q_norm_g, k_norm_g, w_pool, pool_scale, w_out,
           w_router, router_bias, w_gate, w_up, w_down, ws_gate, ws_up, ws_down):
    B, S, D = x.shape
    mod3 = c_act_mod.reshape(B, N_MOD, D)
    head_of = jnp.arange(D_ATTN, dtype=jnp.int32) // HEAD_DIM
    head_mean = jnp.where(head_of[:, None] == head_of[None, :], 1.0 / HEAD_DIM, 0.0).astype(BF16)
    j = jnp.arange(TK, dtype=jnp.int32)
    umat = jnp.where(j[:, None] >= j[None, :], -1.0, 0.0).astype(BF16)

    q, k, v, pool = _inproj(
        x, mod3, norm1_g.reshape(1, D), w_in,
        jnp.tile(q_norm_g, N_HEADS).reshape(1, D_ATTN), jnp.tile(k_norm_g, N_HEADS).reshape(1, D_ATTN),
        head_mean, w_pool, pool_scale.reshape(1, D_POOL))
    attn = _attention(q, k, v, umat)
    t = jnp.arange(TS_PROJ, dtype=jnp.int32)
    ut = (t[:, None] < t[None, :]).astype(BF16)
    e = jnp.arange(N_EXPERTS, dtype=jnp.int32)
    lt = (e[None, :] < e[:, None]).astype(BF16)
    bp = B // MOE_PARTS
    T = bp * S
    n_blocks = _n_row_blocks(T)
    out = None
    for part in range(MOE_PARTS):
        b0 = part * bp
        x1, hp, rank_t, wd_t, cnt = _outproj(x, attn, pool, mod3, w_out, norm2_g.reshape(1, D),
                                             w_router.T, router_bias.reshape(N_EXPERTS, 1), ut, b0, bp)
        slots, wk_tok, block_expert, n_used = _slots(rank_t, wd_t, cnt, lt)
        slot_flat = slots[:TOP_K].reshape(TOP_K * T)
        xs = _sc_dispatch(hp.reshape(T, D // 2), slot_flat, n_blocks * ROW_BLOCK)
        ys = _experts(xs, block_expert[0, :n_blocks], n_used[0, :1], w_gate, w_up, w_down)
        g = _sc_gather(ys, slot_flat).reshape(TOP_K, bp, S, D // 2)
        out = _final(x1, hp, g, wk_tok.reshape(bp, S, LANES), mod3, ws_gate, ws_up, ws_down, out, b0, B)
    return out


def kernel(x, c, w_ada, b_ada, norm1_g, norm2_g, w_in, q_norm_g, k_norm_g, w_pool, pool_scale, w_out,
           w_router, router_bias, w_gate, w_up, w_down, ws_gate, ws_up, ws_down):
    depth = w_ada.shape[0]
    for l in range(depth):
        mod = _adaln(c, w_ada[l], b_ada[l])
        x = _layer(x, mod, norm1_g[l], norm2_g[l], w_in[l], q_norm_g[l], k_norm_g[l], w_pool[l],
                   pool_scale[l], w_out[l], w_router[l], router_bias[l], w_gate[l], w_up[l], w_down[l],
                   ws_gate[l], ws_up[l], ws_down[l])
    return x
```

```python
import functools

import jax
import jax.numpy as jnp
from jax import lax
from jax.experimental import pallas as pl
from jax.experimental.pallas import tpu as pltpu
from jax.experimental.pallas import tpu_sc as plsc

F32 = jnp.float32
BF16 = jnp.bfloat16

HEAD_DIM = 64
N_HEADS = 8
D_ATTN = N_HEADS * HEAD_DIM
POOL_WINDOWS = (2, 4, 8, 16)
POOL_GROUP_DIM = 128
D_POOL = len(POOL_WINDOWS) * POOL_GROUP_DIM
MAX_WINDOW = max(POOL_WINDOWS)
assert all(w & (w - 1) == 0 for w in POOL_WINDOWS)
N_EXPERTS = 64
TOP_K = 6
N_GROUPS = 8
GROUP_SIZE = N_EXPERTS // N_GROUPS
TOPK_GROUPS = 4
ROUTED_SCALE = 2.5
RMS_EPS = 1e-6
N_MOD = 6

LANES = 128
SUBLANES = 8
VMEM_LIMIT = 56 * 1024 * 1024

TS_PROJ = 1024
TQ = 1024
TK = 256
PAIRS_PER_STEP = 4
TS_SLOT = 2048
ROW_BLOCK_LOG2 = 9
ROW_BLOCK = 1 << ROW_BLOCK_LOG2
BLOCKS_PER_STEP = 2
MOE_PARTS = 1
SC_CORES = 2
SC_SUBCORES = 16
SC_CHUNK = 64


def _split_bf16(a):
    hi = a.astype(BF16)
    lo = (a - hi.astype(F32)).astype(BF16)
    return hi, lo


def _dot(a, b):
    return jnp.dot(a, b, preferred_element_type=F32)


def _dot_nt(a, b):
    return lax.dot_general(a, b, (((1,), (1,)), ((), ())), preferred_element_type=F32)


def _dot3(a, b):
    ah, al = _split_bf16(a)
    bh, bl = _split_bf16(b)
    return _dot(ah, bh) + _dot(ah, bl) + _dot(al, bh)


def _dot3_nt(a, b):
    ah, al = _split_bf16(a)
    bh, bl = _split_bf16(b)
    return _dot_nt(ah, bh) + _dot_nt(ah, bl) + _dot_nt(al, bh)


def _silu(x):
    return x * (1.0 / (1.0 + jnp.exp(-x)))


def _rms_mod(x, gain, scale, shift):
    ms = jnp.mean(x * x, axis=-1, keepdims=True)
    y = x * lax.rsqrt(ms + RMS_EPS) * gain
    return y * (1.0 + scale) + shift


def _adaln_kernel(c_ref, w_ref, b_ref, o_ref):
    c = c_ref[...]
    o_ref[...] = _dot3(_silu(c), w_ref[...]) + b_ref[...]


def _adaln(c, w_ada, b_ada):
    nb, D = c.shape
    B = -(-nb // SUBLANES) * SUBLANES
    c = jnp.pad(c, ((0, B - nb), (0, 0)))
    N = w_ada.shape[1]
    tn = 1024
    out = pl.pallas_call(
        _adaln_kernel,
        grid=(N // tn,),
        in_specs=[pl.BlockSpec((B, D), lambda j: (0, 0)),
                  pl.BlockSpec((D, tn), lambda j: (0, j)),
                  pl.BlockSpec((1, tn), lambda j: (0, j))],
        out_specs=pl.BlockSpec((B, tn), lambda j: (0, j)),
        out_shape=jax.ShapeDtypeStruct((B, N), F32),
        compiler_params=pltpu.CompilerParams(dimension_semantics=("arbitrary",),
                                             vmem_limit_bytes=VMEM_LIMIT),
        name="adaln",
    )(c, w_ada, b_ada.reshape(1, N))
    return out[:nb]


def _first_grid_step():
    return jnp.logical_and(pl.program_id(0) == 0, pl.program_id(1) == 0)


def _inproj_kernel(x_ref, mod_ref, g1_ref, win32_ref, qg_ref, kg_ref, hm_ref, wp32_ref, ps_ref,
                   q_ref, k_ref, v_ref, p_ref, ext_ref, win_ref, wp_ref):
    @pl.when(_first_grid_step())
    def _():
        win_ref[...] = win32_ref[...].astype(BF16)
        wp_ref[...] = wp32_ref[...].astype(BF16)

    si = pl.program_id(1)
    ts = x_ref.shape[1]
    x = x_ref[0]
    shift1 = mod_ref[0, 0:1, :]
    scale1 = mod_ref[0, 1:2, :]
    h = _rms_mod(x, g1_ref[...], scale1, shift1)
    proj = _dot(h.astype(BF16), win_ref[...])

    hm = hm_ref[...]

    def head_norm(t, gain):
        ms = _dot((t * t).astype(BF16), hm)
        return t * lax.rsqrt(ms + RMS_EPS) * gain

    hq = proj[:, 0:D_ATTN]
    hk = proj[:, D_ATTN:2 * D_ATTN]
    q_ref[0] = (head_norm(hq, qg_ref[...]) * (HEAD_DIM ** -0.5)).astype(BF16)
    k_ref[0] = head_norm(hk, kg_ref[...]).astype(BF16)
    v_ref[0] = proj[:, 2 * D_ATTN:3 * D_ATTN].astype(BF16)

    hp = proj[:, 3 * D_ATTN:]

    @pl.when(si == 0)
    def _():
        ext_ref[0:MAX_WINDOW, :] = jnp.zeros((MAX_WINDOW, D_POOL), F32)

    ext_ref[MAX_WINDOW:, :] = hp
    pos = si * ts + lax.broadcasted_iota(jnp.int32, (ts, 1), 0)
    for g, w in enumerate(POOL_WINDOWS):
        lo_l, hi_l = g * POOL_GROUP_DIM, (g + 1) * POOL_GROUP_DIM
        u = hp[:, lo_l:hi_l]
        run = ext_ref[:, lo_l:hi_l]
        span = 1
        while span < w:
            run = run + pltpu.roll(run, span, axis=0)
            span *= 2
        acc = run[MAX_WINDOW:]
        count = jnp.minimum(pos + 1, w).astype(F32)
        d = acc / count - u
        mixed = _dot(d.astype(BF16), wp_ref[g])
        p_ref[0, :, lo_l:hi_l] = (mixed * ps_ref[:, lo_l:hi_l]).astype(BF16)
    ext_ref[0:MAX_WINDOW, :] = hp[ts - MAX_WINDOW:, :]


def _inproj(x, mod3, g1, w_in, qg_t, kg_t, head_mean, w_pool, pool_scale):
    B, S, D = x.shape
    ts = TS_PROJ
    out_sd = jax.ShapeDtypeStruct((B, S, D_ATTN), BF16)
    blk = pl.BlockSpec((1, ts, D_ATTN), lambda b, s: (b, s, 0))
    const2 = lambda b, s: (0, 0)
    return pl.pallas_call(
        _inproj_kernel,
        grid=(B, S // ts),
        in_specs=[pl.BlockSpec((1, ts, D), lambda b, s: (b, s, 0)),
                  pl.BlockSpec((1, N_MOD, D), lambda b, s: (b, 0, 0)),
                  pl.BlockSpec((1, D), const2),
                  pl.BlockSpec(w_in.shape, const2),
                  pl.BlockSpec((1, D_ATTN), const2),
                  pl.BlockSpec((1, D_ATTN), const2),
                  pl.BlockSpec((D_ATTN, D_ATTN), const2),
                  pl.BlockSpec(w_pool.shape, lambda b, s: (0, 0, 0)),
                  pl.BlockSpec((1, D_POOL), const2)],
        out_specs=[blk, blk, blk, blk],
        out_shape=[out_sd, out_sd, out_sd, out_sd],
        scratch_shapes=[pltpu.VMEM((MAX_WINDOW + ts, D_POOL), F32),
                        pltpu.VMEM(w_in.shape, BF16), pltpu.VMEM(w_pool.shape, BF16)],
        compiler_params=pltpu.CompilerParams(dimension_semantics=("arbitrary", "arbitrary"),
                                             vmem_limit_bytes=VMEM_LIMIT),
        name="inproj",
    )(x, mod3, g1, w_in, qg_t, kg_t, head_mean, w_pool, pool_scale)


def _attn_kernel(q_ref, k_ref, v_ref, u_ref, o_ref, acc_ref):
    qi = pl.program_id(2)
    lane = lax.broadcasted_iota(jnp.int32, (TQ, LANES), 1)
    first = lane < HEAD_DIM
    n_heads = 2 * PAIRS_PER_STEP
    pair_lanes = lambda h: pl.ds((h // 2) * LANES, LANES)
    qh = []
    for h in range(n_heads):
        q = q_ref[0, :, pair_lanes(h)]
        keep = first if h % 2 == 0 else jnp.logical_not(first)
        qh.append(jnp.where(keep, q, jnp.zeros_like(q)))
    u = u_ref[...]
    row = lax.broadcasted_iota(jnp.int32, (TQ, TK), 0)
    col = lax.broadcasted_iota(jnp.int32, (TQ, TK), 1)
    acc_ref[...] = jnp.zeros(acc_ref.shape, F32)

    def block(kb, survs, diag):
        start = pl.multiple_of(kb * TK, TK)
        r0 = 0 if diag is None else diag * TK
        if diag is not None:
            valid = (col + r0 < row)[r0:]
        out = []
        for h in range(n_heads):
            k = k_ref[0, pl.ds(start, TK), pair_lanes(h)]
            v = v_ref[0, pl.ds(start, TK), pair_lanes(h)]
            z = _dot_nt(qh[h][r0:], k)
            zb = z.astype(BF16)
            sp = jnp.maximum(zb, 0) + jnp.log(1 + jnp.exp(-jnp.abs(zb)))
            if diag is not None:
                sp = jnp.where(valid, sp, jnp.zeros_like(sp))
            r = _dot(sp, u)
            arg = z + r + survs[h][r0:]
            if diag is not None:
                arg = jnp.where(valid, arg, -jnp.inf)
            acc_ref[h, r0:, :] += _dot(jnp.exp(arg).astype(BF16), v)
            surv = survs[h][r0:] + r[:, 0:1]
            out.append(surv if r0 == 0 else jnp.concatenate([survs[h][:r0], surv], axis=0))
        return tuple(out)

    survs = tuple(jnp.zeros((TQ, 1), F32) for _ in range(n_heads))
    n_diag = TQ // TK
    for d in reversed(range(n_diag)):
        survs = block(qi * n_diag + d, survs, d)
    def full_blocks(i, c):
        for d in range(n_diag):
            c = block((qi - i) * n_diag - 1 - d, c, None)
        return c

    lax.fori_loop(0, qi, full_blocks, survs)
    for p in range(PAIRS_PER_STEP):
        o_ref[0, :, pl.ds(p * LANES, LANES)] = jnp.where(first, acc_ref[2 * p], acc_ref[2 * p + 1]).astype(BF16)


def _attention(q, k, v, umat):
    B, S, _ = q.shape
    width = PAIRS_PER_STEP * LANES
    kv_spec = pl.BlockSpec((1, S, width), lambda b, p, i: (b, 0, p))
    q_spec = pl.BlockSpec((1, TQ, width), lambda b, p, i: (b, i, p))
    return pl.pallas_call(
        _attn_kernel,
        grid=(B, D_ATTN // width, S // TQ),
        in_specs=[q_spec, kv_spec, kv_spec, pl.BlockSpec((TK, TK), lambda b, p, i: (0, 0))],
        out_specs=q_spec,
        out_shape=jax.ShapeDtypeStruct((B, S, D_ATTN), BF16),
        scratch_shapes=[pltpu.VMEM((2 * PAIRS_PER_STEP, TQ, LANES), F32)],
        compiler_params=pltpu.CompilerParams(
            dimension_semantics=("arbitrary", "arbitrary", "arbitrary"),
            vmem_limit_bytes=VMEM_LIMIT),
        name="stickbreak_attn",
    )(q, k, v, umat)


def _route_t(scores, biased):
    ts = scores.shape[-1]
    neg = -jnp.inf
    b3 = biased.reshape(N_GROUPS, GROUP_SIZE, ts)
    e_in_g = lax.broadcasted_iota(jnp.int32, b3.shape, 1)
    m1 = jnp.max(b3, axis=1, keepdims=True)
    i1 = jnp.min(jnp.where(b3 == m1, e_in_g, GROUP_SIZE), axis=1, keepdims=True)
    m2 = jnp.max(jnp.where(e_in_g == i1, neg, b3), axis=1, keepdims=True)
    gs = (m1 + m2)[:, 0, :]
    g_iota = lax.broadcasted_iota(jnp.int32, gs.shape, 0)
    g_sel = jnp.zeros(gs.shape, jnp.bool_)
    for _ in range(TOPK_GROUPS):
        gm = jnp.max(gs, axis=0, keepdims=True)
        gi = jnp.min(jnp.where(gs == gm, g_iota, N_GROUPS), axis=0, keepdims=True)
        pick = g_iota == gi
        g_sel = jnp.logical_or(g_sel, pick)
        gs = jnp.where(pick, neg, gs)
    masked = jnp.where(g_sel[:, None, :], b3, neg)
    flat = lax.broadcasted_iota(jnp.int32, b3.shape, 0) * GROUP_SIZE + e_in_g
    sel = jnp.zeros(b3.shape, jnp.bool_)
    for _ in range(TOP_K):
        m = jnp.max(jnp.max(masked, axis=1, keepdims=True), axis=0, keepdims=True)
        cand = jnp.where(masked == m, flat, N_EXPERTS)
        idx = jnp.min(jnp.min(cand, axis=1, keepdims=True), axis=0, keepdims=True)
        pick = flat == idx
        sel = jnp.logical_or(sel, pick)
        masked = jnp.where(pick, neg, masked)
    s3 = scores.reshape(N_GROUPS, GROUP_SIZE, ts)
    w = jnp.where(sel, s3, 0.0)
    tot = jnp.sum(jnp.sum(w, axis=1, keepdims=True), axis=0, keepdims=True)
    return (w / tot * ROUTED_SCALE).reshape(N_EXPERTS, ts), sel.reshape(N_EXPERTS, ts)


def _pack_bf16_pair(lo, hi):
    lo_bits = pltpu.bitcast(lo.astype(BF16).astype(F32), jnp.uint32) >> 16
    hi_bits = pltpu.bitcast(hi.astype(BF16).astype(F32), jnp.uint32) & jnp.uint32(0xFFFF0000)
    return lo_bits | hi_bits


def _unpack_bf16_pair(p):
    lo = pltpu.bitcast(p << 16, F32).astype(BF16)
    hi = pltpu.bitcast(p & jnp.uint32(0xFFFF0000), F32).astype(BF16)
    return lo, hi


def _outproj_kernel(x_ref, a_ref, p_ref, mod_ref, wo32_ref, g2_ref, wrt_ref, rb_ref, ut_ref,
                    x1_ref, hp_ref, rank_ref, wd_ref, cnt_ref, run_ref, wo_ref):
    @pl.when(_first_grid_step())
    def _():
        run_ref[...] = jnp.zeros(run_ref.shape, jnp.int32)
        wo_ref[...] = wo32_ref[...].astype(BF16)

    gate1 = mod_ref[0, 2:3, :]
    shift2 = mod_ref[0, 3:4, :]
    scale2 = mod_ref[0, 4:5, :]
    mixp = _dot(a_ref[0], wo_ref[0:D_ATTN, :]) + _dot(p_ref[0], wo_ref[D_ATTN:, :])
    x1 = x_ref[0] + gate1 * mixp
    x1_ref[0] = x1
    h2 = _rms_mod(x1, g2_ref[...], scale2, shift2)
    half = h2.shape[-1] // 2
    hp_ref[0] = _pack_bf16_pair(h2[:, :half], h2[:, half:])
    logits_t = _dot3_nt(wrt_ref[...], h2)
    scores = 1.0 / (1.0 + jnp.exp(-logits_t))
    wd_t, sel = _route_t(scores, scores + rb_ref[...])
    wd_ref[...] = wd_t
    self_f = jnp.where(sel, 1.0, 0.0)
    before = _dot(self_f.astype(BF16), ut_ref[...]).astype(jnp.int32)
    run = run_ref[:, 0:1]
    rank_ref[...] = jnp.where(sel, run + before, -1)
    run_ref[...] = run_ref[...] + jnp.sum(self_f, axis=1, keepdims=True).astype(jnp.int32)
    cnt_ref[...] = run_ref[...]


def _outproj(x, attn, pool, mod3, w_out, g2, wr_t, rbias, ut, b0, B):
    _, S, D = x.shape
    ts = TS_PROJ
    n_s = S // ts
    const2 = lambda b, s: (0, 0)
    tok_in = lambda w: pl.BlockSpec((1, ts, w), lambda b, s: (b + b0, s, 0))
    tok = lambda w: pl.BlockSpec((1, ts, w), lambda b, s: (b, s, 0))
    tok_t = pl.BlockSpec((N_EXPERTS, ts), lambda b, s: (0, b * n_s + s))
    return pl.pallas_call(
        _outproj_kernel,
        grid=(B, n_s),
        in_specs=[tok_in(D), tok_in(D_ATTN), tok_in(D_POOL),
                  pl.BlockSpec((1, N_MOD, D), lambda b, s: (b + b0, 0, 0)),
                  pl.BlockSpec(w_out.shape, const2),
                  pl.BlockSpec((1, D), const2),
                  pl.BlockSpec(wr_t.shape, const2),
                  pl.BlockSpec((N_EXPERTS, 1), const2),
                  pl.BlockSpec((ts, ts), const2)],
        out_specs=[tok(D), tok(D // 2), tok_t, tok_t, pl.BlockSpec((N_EXPERTS, LANES), const2)],
        out_shape=[jax.ShapeDtypeStruct((B, S, D), F32),
                   jax.ShapeDtypeStruct((B, S, D // 2), jnp.uint32),
                   jax.ShapeDtypeStruct((N_EXPERTS, B * S), jnp.int32),
                   jax.ShapeDtypeStruct((N_EXPERTS, B * S), F32),
                   jax.ShapeDtypeStruct((N_EXPERTS, LANES), jnp.int32)],
        scratch_shapes=[pltpu.VMEM((N_EXPERTS, LANES), jnp.int32), pltpu.VMEM(w_out.shape, BF16)],
        compiler_params=pltpu.CompilerParams(dimension_semantics=("arbitrary", "arbitrary"),
                                             vmem_limit_bytes=VMEM_LIMIT),
        name="outproj_router",
    )(x, attn, pool, mod3, w_out, g2, wr_t, rbias, ut)


def _n_row_blocks(n_tokens):
    n_blocks = -(-(n_tokens * TOP_K + N_EXPERTS * (ROW_BLOCK - 1)) // ROW_BLOCK)
    return -(-n_blocks // BLOCKS_PER_STEP) * BLOCKS_PER_STEP


def _slots_kernel(rank_ref, wd_ref, cnt_ref, lt_ref, slot_ref, wk_ref, be_ref, nu_ref):
    lt = lt_ref[...]
    nblk = lax.shift_right_logical(cnt_ref[...] + (ROW_BLOCK - 1), ROW_BLOCK_LOG2).astype(F32)
    nb_hi, nb_lo = _split_bf16(nblk)
    blk_start = _dot(lt, nb_hi) + _dot(lt, nb_lo)

    @pl.when(pl.program_id(0) == 0)
    def _():
        blk_end = (blk_start + nblk)[:, 0:1]
        b_iota = lax.broadcasted_iota(jnp.int32, (N_EXPERTS, be_ref.shape[-1]), 1).astype(F32)
        owner = jnp.sum(jnp.where(blk_end <= b_iota, 1, 0), axis=0, keepdims=True)
        be_ref[...] = jnp.minimum(owner, N_EXPERTS - 1)
        nu_ref[...] = jnp.broadcast_to(blk_end[N_EXPERTS - 1:, :].astype(jnp.int32), nu_ref.shape)

    rank = rank_ref[...]
    sel = rank >= 0
    row_start = (blk_start[:, 0:1] * ROW_BLOCK).astype(jnp.int32)
    slot_d = row_start + rank
    wd = wd_ref[...]
    choice = _dot(lt, jnp.where(sel, 1.0, 0.0).astype(BF16)).astype(jnp.int32)
    ts = rank.shape[-1]
    slots, wks = [], []
    for k in range(TOP_K):
        m = jnp.logical_and(sel, choice == k)
        slots.append(jnp.sum(jnp.where(m, slot_d, 0), axis=0, keepdims=True))
        wks.append(jnp.sum(jnp.where(m, wd, 0.0), axis=0, keepdims=True))
    slot_ref[...] = jnp.concatenate(slots + [jnp.zeros((SUBLANES - TOP_K, ts), jnp.int32)], axis=0)
    wk_pad = jnp.concatenate(wks + [jnp.zeros((LANES - TOP_K, ts), F32)], axis=0)
    wk_ref[...] = wk_pad.T


def _slots(rank_t, wd_t, cnt, lt):
    T = rank_t.shape[1]
    ts = TS_SLOT
    nb_pad = -(-_n_row_blocks(T) // LANES) * LANES
    const = lambda i: (0, 0)
    tok_t = pl.BlockSpec((N_EXPERTS, ts), lambda i: (0, i))
    return pl.pallas_call(
        _slots_kernel,
        grid=(T // ts,),
        in_specs=[tok_t, tok_t, pl.BlockSpec((N_EXPERTS, LANES), const), pl.BlockSpec((N_EXPERTS, N_EXPERTS), const)],
        out_specs=[pl.BlockSpec((SUBLANES, ts), lambda i: (0, i)),
                   pl.BlockSpec((ts, LANES), lambda i: (i, 0)),
                   pl.BlockSpec((1, nb_pad), const),
                   pl.BlockSpec((1, LANES), const)],
        out_shape=[jax.ShapeDtypeStruct((SUBLANES, T), jnp.int32),
                   jax.ShapeDtypeStruct((T, LANES), F32),
                   jax.ShapeDtypeStruct((1, nb_pad), jnp.int32),
                   jax.ShapeDtypeStruct((1, LANES), jnp.int32)],
        compiler_params=pltpu.CompilerParams(dimension_semantics=("arbitrary",), vmem_limit_bytes=VMEM_LIMIT),
        name="slots",
    )(rank_t, wd_t, cnt, lt)


def _sc_mesh():
    return plsc.VectorSubcoreMesh(core_axis_name="c", subcore_axis_name="s",
                                  num_cores=SC_CORES, num_subcores=SC_SUBCORES)


def _sc_dispatch(rows, slot_flat, n_out):
    T, width = rows.shape
    n_workers = SC_CORES * SC_SUBCORES
    per_worker = T // n_workers
    steps = per_worker // SC_CHUNK

    @functools.partial(
        pl.kernel, mesh=_sc_mesh(),
        out_type=jax.ShapeDtypeStruct((n_out, width), rows.dtype),
        scratch_types=[pltpu.VMEM((SC_CHUNK, width), rows.dtype)]
        + [pltpu.VMEM((SC_CHUNK,), jnp.int32)] * TOP_K + [pltpu.SemaphoreType.DMA],
        name="dispatch",
    )
    def run(rows_hbm, slot_hbm, out_hbm, rows_v, *rest):
        idx_v, sem = rest[:TOP_K], rest[TOP_K]
        base = (lax.axis_index("s") * SC_CORES + lax.axis_index("c")) * per_worker

        @pl.loop(0, steps)
        def _(i):
            off = base + i * SC_CHUNK
            pltpu.sync_copy(rows_hbm.at[pl.ds(off, SC_CHUNK)], rows_v)
            for k in range(TOP_K):
                pltpu.sync_copy(slot_hbm.at[pl.ds(k * T + off, SC_CHUNK)], idx_v[k])
            copies = [pltpu.async_copy(rows_v, out_hbm.at[idx_v[k]], sem) for k in range(TOP_K)]
            for cp in copies:
                cp.wait()

    return run(rows, slot_flat)


def _sc_gather(rows, idx):
    n = idx.shape[0]
    width = rows.shape[1]
    n_workers = SC_CORES * SC_SUBCORES
    per_worker = n // n_workers
    steps = per_worker // SC_CHUNK

    assert steps % 2 == 0
    slot_types = [pltpu.VMEM((SC_CHUNK,), jnp.int32), pltpu.VMEM((SC_CHUNK, width), rows.dtype),
                  pltpu.SemaphoreType.DMA]

    @functools.partial(
        pl.kernel, mesh=_sc_mesh(),
        out_type=jax.ShapeDtypeStruct((n, width), rows.dtype),
        scratch_types=slot_types * 2,
        name="combine",
    )
    def run(rows_hbm, idx_hbm, out_hbm, *scratch):
        base = (lax.axis_index("s") * SC_CORES + lax.axis_index("c")) * per_worker
        slots = (scratch[0:3], scratch[3:6])

        def gather(slot):
            idx_v, rows_v, sem = slots[slot]
            return pltpu.make_async_copy(rows_hbm.at[idx_v], rows_v, sem)

        def start(chunk, slot):
            pltpu.sync_copy(idx_hbm.at[pl.ds(base + chunk * SC_CHUNK, SC_CHUNK)], slots[slot][0])
            gather(slot).start()

        def finish(chunk, slot):
            gather(slot).wait()
            pltpu.sync_copy(slots[slot][1], out_hbm.at[pl.ds(base + chunk * SC_CHUNK, SC_CHUNK)])

        start(0, 0)

        @pl.loop(0, steps, step=2)
        def _(chunk):
            start(chunk + 1, 1)
            finish(chunk, 0)

            @pl.when(chunk + 2 < steps)
            def _():
                start(chunk + 2, 0)

            finish(chunk + 1, 1)

    return run(rows, idx)


def _swiglu_packed(xp, wgu, wd):
    lo, hi = _unpack_bf16_pair(xp)
    half = xp.shape[-1]
    gu = _dot(lo, wgu[:half, :]) + _dot(hi, wgu[half:, :])
    dh = wgu.shape[-1] // 2
    act = _silu(gu[:, :dh]) * gu[:, dh:]
    return _dot(act.astype(BF16), wd)


def _cast_swiglu_weights(wg32_ref, wu32_ref, wd32_ref, wgu_ref, wd_ref):
    dh = wg32_ref.shape[-1]
    wgu_ref[:, :dh] = wg32_ref[...].reshape(wg32_ref.shape[-2:]).astype(BF16)
    wgu_ref[:, dh:] = wu32_ref[...].reshape(wu32_ref.shape[-2:]).astype(BF16)
    wd_ref[...] = wd32_ref[...].reshape(wd32_ref.shape[-2:]).astype(BF16)


def _experts_kernel(be_ref, nu_ref, x_ref, *refs):
    n = BLOCKS_PER_STEP
    w32 = [refs[3 * j:3 * j + 3] for j in range(n)]
    y_ref = refs[3 * n]
    wbf = [refs[3 * n + 1 + 2 * j:3 * n + 3 + 2 * j] for j in range(n)]
    step = pl.program_id(0)
    for j in range(n):
        blk = step * n + j
        new_expert = jnp.logical_or(step == 0, be_ref[blk] != be_ref[jnp.maximum(blk - n, 0)])

        @pl.when(new_expert)
        def _():
            _cast_swiglu_weights(*w32[j], *wbf[j])

    @pl.when(step * n < nu_ref[0])
    def _():
        for j in range(n):
            rows = pl.ds(j * ROW_BLOCK, ROW_BLOCK)
            y = _swiglu_packed(x_ref[rows, :], wbf[j][0][...], wbf[j][1][...])
            half = y.shape[-1] // 2
            y_ref[rows, :] = _pack_bf16_pair(y[:, :half], y[:, half:])


def _experts(xs, block_expert, n_used, w_gate, w_up, w_down):
    P, half = xs.shape
    _, D, dh = w_gate.shape
    n = BLOCKS_PER_STEP
    rows = n * ROW_BLOCK
    assert P % rows == 0

    def by_expert(shape, j):
        return pl.BlockSpec((1,) + shape, lambda s, be, nu: (be[s * n + j], 0, 0))

    w_specs, w_args = [], []
    for j in range(n):
        w_specs += [by_expert((D, dh), j), by_expert((D, dh), j), by_expert((dh, D), j)]
        w_args += [w_gate, w_up, w_down]
    def row_step(s, be, nu):
        return jnp.minimum(s, lax.div(jnp.maximum(nu[0], 1) - 1, n)), 0

    grid_spec = pltpu.PrefetchScalarGridSpec(
        num_scalar_prefetch=2,
        grid=(P // rows,),
        in_specs=[pl.BlockSpec((rows, half), row_step)] + w_specs,
        out_specs=pl.BlockSpec((rows, half), row_step),
        scratch_shapes=[pltpu.VMEM((D, 2 * dh), BF16), pltpu.VMEM((dh, D), BF16)] * n,
    )
    return pl.pallas_call(
        _experts_kernel,
        grid_spec=grid_spec,
        out_shape=jax.ShapeDtypeStruct((P, half), jnp.uint32),
        compiler_params=pltpu.CompilerParams(dimension_semantics=("arbitrary",), vmem_limit_bytes=VMEM_LIMIT),
        name="experts",
    )(block_expert, n_used, xs, *w_args)


def _final_kernel(x1_ref, hp_ref, g_ref, wk_ref, mod_ref, sg32_ref, su32_ref, sd32_ref, *rest):
    o_ref, sgu_ref, sd_ref = rest[-3:]

    @pl.when(_first_grid_step())
    def _():
        _cast_swiglu_weights(sg32_ref, su32_ref, sd32_ref, sgu_ref, sd_ref)

    acc = _swiglu_packed(hp_ref[0], sgu_ref[...], sd_ref[...])
    wk = wk_ref[0]
    for k in range(TOP_K):
        lo, hi = _unpack_bf16_pair(g_ref[k, 0])
        y = jnp.concatenate([lo.astype(F32), hi.astype(F32)], axis=-1)
        acc = acc + wk[:, k:k + 1] * y
    gate2 = mod_ref[0, 5:6, :]
    o_ref[0] = x1_ref[0] + gate2 * acc


def _final(x1, hp, g, wk_tok, mod3, ws_gate, ws_up, ws_down, prev_out, b0, b_total):
    B, S, D = x1.shape
    dh = ws_gate.shape[-1]
    ts = TS_PROJ
    tok = lambda w: pl.BlockSpec((1, ts, w), lambda b, s: (b, s, 0))
    const2 = lambda b, s: (0, 0)
    in_specs = [tok(D), tok(D // 2),
                pl.BlockSpec((TOP_K, 1, ts, D // 2), lambda b, s: (0, b, s, 0)),
                tok(LANES),
                pl.BlockSpec((1, N_MOD, D), lambda b, s: (b + b0, 0, 0)),
                pl.BlockSpec(ws_gate.shape, const2),
                pl.BlockSpec(ws_up.shape, const2),
                pl.BlockSpec(ws_down.shape, const2)]
    args = [x1, hp, g, wk_tok, mod3, ws_gate, ws_up, ws_down]
    aliases = {}
    if prev_out is not None:
        in_specs.append(pl.BlockSpec(memory_space=pl.ANY))
        args.append(prev_out)
        aliases = {len(args) - 1: 0}
    return pl.pallas_call(
        _final_kernel,
        grid=(B, S // ts),
        in_specs=in_specs,
        out_specs=pl.BlockSpec((1, ts, D), lambda b, s: (b + b0, s, 0)),
        out_shape=jax.ShapeDtypeStruct((b_total, S, D), F32),
        scratch_shapes=[pltpu.VMEM((D, 2 * dh), BF16), pltpu.VMEM((dh, D), BF16)],
        input_output_aliases=aliases,
        compiler_params=pltpu.CompilerParams(dimension_semantics=("arbitrary", "arbitrary"),
                                             vmem_limit_bytes=VMEM_LIMIT),
        name="final",
    )(*args)


def _layer(x, c_act_mod, norm1_g, norm2_g, w_in, q_norm_g, k_norm_g, w_pool, pool_scale, w_out,
           w_router, router_bias, w_gate, w_up, w_down, ws_gate, ws_up, ws_down):
    B, S, D = x.shape
    mod3 = c_act_mod.reshape(B, N_MOD, D)
    head_of = jnp.arange(D_ATTN, dtype=jnp.int32) // HEAD_DIM
    head_mean = jnp.where(head_of[:, None] == head_of[None, :], 1.0 / HEAD_DIM, 0.0).astype(BF16)
    j = jnp.arange(TK, dtype=jnp.int32)
    umat = jnp.where(j[:, None] >= j[None, :], -1.0, 0.0).astype(BF16)

    q, k, v, pool = _inproj(
        x, mod3, norm1_g.reshape(1, D), w_in,
        jnp.tile(q_norm_g, N_HEADS).reshape(1, D_ATTN), jnp.tile(k_norm_g, N_HEADS).reshape(1, D_ATTN),
        head_mean, w_pool, pool_scale.reshape(1, D_POOL))
    attn = _attention(q, k, v, umat)
    t = jnp.arange(TS_PROJ, dtype=jnp.int32)
    ut = (t[:, None] < t[None, :]).astype(BF16)
    e = jnp.arange(N_EXPERTS, dtype=jnp.int32)
    lt = (e[None, :] < e[:, None]).astype(BF16)
    bp = B // MOE_PARTS
    T = bp * S
    n_blocks = _n_row_blocks(T)
    out = None
    for part in range(MOE_PARTS):
        b0 = part * bp
        x1, hp, rank_t, wd_t, cnt = _outproj(x, attn, pool, mod3, w_out, norm2_g.reshape(1, D),
                                             w_router.T, router_bias.reshape(N_EXPERTS, 1), ut, b0, bp)
        slots, wk_tok, block_expert, n_used = _slots(rank_t, wd_t, cnt, lt)
        slot_flat = slots[:TOP_K].reshape(TOP_K * T)
        xs = _sc_dispatch(hp.reshape(T, D // 2), slot_flat, n_blocks * ROW_BLOCK)
        ys = _experts(xs, block_expert[0, :n_blocks], n_used[0, :1], w_gate, w_up, w_down)
        g = _sc_gather(ys, slot_flat).reshape(TOP_K, bp, S, D // 2)
        out = _final(x1, hp, g, wk_tok.reshape(bp, S, LANES), mod3, ws_gate, ws_up, ws_down, out, b0, B)
    return out


def kernel(x, c, w_ada, b_ada, norm1_g, norm2_g, w_in, q_norm_g, k_norm_g, w_pool, pool_scale, w_out,
           w_router, router_bias, w_gate, w_up, w_down, ws_gate, ws_up, ws_down):
    depth = w_ada.shape[0]
    for l in range(depth):
        mod = _adaln(c, w_ada[l], b_ada[l])
        x = _layer(x, mod, norm1_g[l], norm2_g[l], w_in[l], q_norm_g[l], k_norm_g[l], w_pool[l],
                   pool_scale[l], w_out[l], w_router[l], router_bias[l], w_gate[l], w_up[l], w_down[l],
                   ws_gate[l], ws_up[l], ws_down[l])
    return x
```

```python
import functools

import jax
import jax.numpy as jnp
from jax import lax
from jax.experimental import pallas as pl
from jax.experimental.pallas import tpu as pltpu
from jax.experimental.pallas import tpu_sc as plsc

F32 = jnp.float32
BF16 = jnp.bfloat16

HEAD_DIM = 64
N_HEADS = 8
D_ATTN = N_HEADS * HEAD_DIM
POOL_WINDOWS = (2, 4, 8, 16)
POOL_GROUP_DIM = 128
D_POOL = len(POOL_WINDOWS) * POOL_GROUP_DIM
MAX_WINDOW = max(POOL_WINDOWS)
assert all(w & (w - 1) == 0 for w in POOL_WINDOWS)
N_EXPERTS = 64
TOP_K = 6
N_GROUPS = 8
GROUP_SIZE = N_EXPERTS // N_GROUPS
TOPK_GROUPS = 4
ROUTED_SCALE = 2.5
RMS_EPS = 1e-6
N_MOD = 6

LANES = 128
SUBLANES = 8
VMEM_LIMIT = 56 * 1024 * 1024

TS_PROJ = 1024
TQ = 1024
TK = 256
PAIRS_PER_STEP = 4
TS_SLOT = 2048
ROW_BLOCK_LOG2 = 9
ROW_BLOCK = 1 << ROW_BLOCK_LOG2
BLOCKS_PER_STEP = 2
MOE_PARTS = 2
SC_CORES = 2
SC_SUBCORES = 16
SC_CHUNK = 64


def _split_bf16(a):
    hi = a.astype(BF16)
    lo = (a - hi.astype(F32)).astype(BF16)
    return hi, lo


def _dot(a, b):
    return jnp.dot(a, b, preferred_element_type=F32)


def _dot_nt(a, b):
    return lax.dot_general(a, b, (((1,), (1,)), ((), ())), preferred_element_type=F32)


def _dot3(a, b):
    ah, al = _split_bf16(a)
    bh, bl = _split_bf16(b)
    return _dot(ah, bh) + _dot(ah, bl) + _dot(al, bh)


def _dot3_nt(a, b):
    ah, al = _split_bf16(a)
    bh, bl = _split_bf16(b)
    return _dot_nt(ah, bh) + _dot_nt(ah, bl) + _dot_nt(al, bh)


def _silu(x):
    return x * (1.0 / (1.0 + jnp.exp(-x)))


def _rms_mod(x, gain, scale, shift):
    ms = jnp.mean(x * x, axis=-1, keepdims=True)
    y = x * lax.rsqrt(ms + RMS_EPS) * gain
    return y * (1.0 + scale) + shift


def _adaln_kernel(c_ref, w_ref, b_ref, o_ref):
    c = c_ref[...]
    o_ref[...] = _dot3(_silu(c), w_ref[...]) + b_ref[...]


def _adaln(c, w_ada, b_ada):
    nb, D = c.shape
    B = -(-nb // SUBLANES) * SUBLANES
    c = jnp.pad(c, ((0, B - nb), (0, 0)))
    N = w_ada.shape[1]
    tn = 1024
    out = pl.pallas_call(
        _adaln_kernel,
        grid=(N // tn,),
        in_specs=[pl.BlockSpec((B, D), lambda j: (0, 0)),
                  pl.BlockSpec((D, tn), lambda j: (0, j)),
                  pl.BlockSpec((1, tn), lambda j: (0, j))],
        out_specs=pl.BlockSpec((B, tn), lambda j: (0, j)),
        out_shape=jax.ShapeDtypeStruct((B, N), F32),
        compiler_params=pltpu.CompilerParams(dimension_semantics=("arbitrary",),
                                             vmem_limit_bytes=VMEM_LIMIT),
        name="adaln",
    )(c, w_ada, b_ada.reshape(1, N))
    return out[:nb]


def _first_grid_step():
    return jnp.logical_and(pl.program_id(0) == 0, pl.program_id(1) == 0)


def _inproj_kernel(x_ref, mod_ref, g1_ref, win32_ref, qg_ref, kg_ref, hm_ref, wp32_ref, ps_ref,
                   q_ref, k_ref, v_ref, p_ref, ext_ref, win_ref, wp_ref):
    @pl.when(_first_grid_step())
    def _():
        win_ref[...] = win32_ref[...].astype(BF16)
        wp_ref[...] = wp32_ref[...].astype(BF16)

    si = pl.program_id(1)
    ts = x_ref.shape[1]
    x = x_ref[0]
    shift1 = mod_ref[0, 0:1, :]
    scale1 = mod_ref[0, 1:2, :]
    h = _rms_mod(x, g1_ref[...], scale1, shift1)
    proj = _dot(h.astype(BF16), win_ref[...])

    hm = hm_ref[...]

    def head_norm(t, gain):
        ms = _dot((t * t).astype(BF16), hm)
        return t * lax.rsqrt(ms + RMS_EPS) * gain

    hq = proj[:, 0:D_ATTN]
    hk = proj[:, D_ATTN:2 * D_ATTN]
    q_ref[0] = (head_norm(hq, qg_ref[...]) * (HEAD_DIM ** -0.5)).astype(BF16)
    k_ref[0] = head_norm(hk, kg_ref[...]).astype(BF16)
    v_ref[0] = proj[:, 2 * D_ATTN:3 * D_ATTN].astype(BF16)

    hp = proj[:, 3 * D_ATTN:]

    @pl.when(si == 0)
    def _():
        ext_ref[0:MAX_WINDOW, :] = jnp.zeros((MAX_WINDOW, D_POOL), F32)

    ext_ref[MAX_WINDOW:, :] = hp
    pos = si * ts + lax.broadcasted_iota(jnp.int32, (ts, 1), 0)
    for g, w in enumerate(POOL_WINDOWS):
        lo_l, hi_l = g * POOL_GROUP_DIM, (g + 1) * POOL_GROUP_DIM
        u = hp[:, lo_l:hi_l]
        run = ext_ref[:, lo_l:hi_l]
        span = 1
        while span < w:
            run = run + pltpu.roll(run, span, axis=0)
            span *= 2
        acc = run[MAX_WINDOW:]
        count = jnp.minimum(pos + 1, w).astype(F32)
        d = acc / count - u
        mixed = _dot(d.astype(BF16), wp_ref[g])
        p_ref[0, :, lo_l:hi_l] = (mixed * ps_ref[:, lo_l:hi_l]).astype(BF16)
    ext_ref[0:MAX_WINDOW, :] = hp[ts - MAX_WINDOW:, :]


def _inproj(x, mod3, g1, w_in, qg_t, kg_t, head_mean, w_pool, pool_scale):
    B, S, D = x.shape
    ts = TS_PROJ
    out_sd = jax.ShapeDtypeStruct((B, S, D_ATTN), BF16)
    blk = pl.BlockSpec((1, ts, D_ATTN), lambda b, s: (b, s, 0))
    const2 = lambda b, s: (0, 0)
    return pl.pallas_call(
        _inproj_kernel,
        grid=(B, S // ts),
        in_specs=[pl.BlockSpec((1, ts, D), lambda b, s: (b, s, 0)),
                  pl.BlockSpec((1, N_MOD, D), lambda b, s: (b, 0, 0)),
                  pl.BlockSpec((1, D), const2),
                  pl.BlockSpec(w_in.shape, const2),
                  pl.BlockSpec((1, D_ATTN), const2),
                  pl.BlockSpec((1, D_ATTN), const2),
                  pl.BlockSpec((D_ATTN, D_ATTN), const2),
                  pl.BlockSpec(w_pool.shape, lambda b, s: (0, 0, 0)),
                  pl.BlockSpec((1, D_POOL), const2)],
        out_specs=[blk, blk, blk, blk],
        out_shape=[out_sd, out_sd, out_sd, out_sd],
        scratch_shapes=[pltpu.VMEM((MAX_WINDOW + ts, D_POOL), F32),
                        pltpu.VMEM(w_in.shape, BF16), pltpu.VMEM(w_pool.shape, BF16)],
        compiler_params=pltpu.CompilerParams(dimension_semantics=("arbitrary", "arbitrary"),
                                             vmem_limit_bytes=VMEM_LIMIT),
        name="inproj",
    )(x, mod3, g1, w_in, qg_t, kg_t, head_mean, w_pool, pool_scale)


def _attn_kernel(q_ref, k_ref, v_ref, u_ref, o_ref, acc_ref):
    qi = pl.program_id(2)
    lane = lax.broadcasted_iota(jnp.int32, (TQ, LANES), 1)
    first = lane < HEAD_DIM
    n_heads = 2 * PAIRS_PER_STEP
    pair_lanes = lambda h: pl.ds((h // 2) * LANES, LANES)
    qh = []
    for h in range(n_heads):
        q = q_ref[0, :, pair_lanes(h)]
        keep = first if h % 2 == 0 else jnp.logical_not(first)
        qh.append(jnp.where(keep, q, jnp.zeros_like(q)))
    u = u_ref[...]
    row = lax.broadcasted_iota(jnp.int32, (TQ, TK), 0)
    col = lax.broadcasted_iota(jnp.int32, (TQ, TK), 1)
    acc_ref[...] = jnp.zeros(acc_ref.shape, F32)

    def block(kb, survs, diag):
        start = pl.multiple_of(kb * TK, TK)
        r0 = 0 if diag is None else diag * TK
        if diag is not None:
            valid = (col + r0 < row)[r0:]
        out = []
        for h in range(n_heads):
            k = k_ref[0, pl.ds(start, TK), pair_lanes(h)]
            v = v_ref[0, pl.ds(start, TK), pair_lanes(h)]
            z = _dot_nt(qh[h][r0:], k)
            zb = z.astype(BF16)
            sp = jnp.maximum(zb, 0) + jnp.log(1 + jnp.exp(-jnp.abs(zb)))
            if diag is not None:
                sp = jnp.where(valid, sp, jnp.zeros_like(sp))
            r = _dot(sp, u)
            arg = z + r + survs[h][r0:]
            if diag is not None:
                arg = jnp.where(valid, arg, -jnp.inf)
            acc_ref[h, r0:, :] += _dot(jnp.exp(arg).astype(BF16), v)
            surv = survs[h][r0:] + r[:, 0:1]
            out.append(surv if r0 == 0 else jnp.concatenate([survs[h][:r0], surv], axis=0))
        return tuple(out)

    survs = tuple(jnp.zeros((TQ, 1), F32) for _ in range(n_heads))
    n_diag = TQ // TK
    for d in reversed(range(n_diag)):
        survs = block(qi * n_diag + d, survs, d)
    def full_blocks(i, c):
        for d in range(n_diag):
            c = block((qi - i) * n_diag - 1 - d, c, None)
        return c

    lax.fori_loop(0, qi, full_blocks, survs)
    for p in range(PAIRS_PER_STEP):
        o_ref[0, :, pl.ds(p * LANES, LANES)] = jnp.where(first, acc_ref[2 * p], acc_ref[2 * p + 1]).astype(BF16)


def _attention(q, k, v, umat):
    B, S, _ = q.shape
    width = PAIRS_PER_STEP * LANES
    kv_spec = pl.BlockSpec((1, S, width), lambda b, p, i: (b, 0, p))
    q_spec = pl.BlockSpec((1, TQ, width), lambda b, p, i: (b, i, p))
    return pl.pallas_call(
        _attn_kernel,
        grid=(B, D_ATTN // width, S // TQ),
        in_specs=[q_spec, kv_spec, kv_spec, pl.BlockSpec((TK, TK), lambda b, p, i: (0, 0))],
        out_specs=q_spec,
        out_shape=jax.ShapeDtypeStruct((B, S, D_ATTN), BF16),
        scratch_shapes=[pltpu.VMEM((2 * PAIRS_PER_STEP, TQ, LANES), F32)],
        compiler_params=pltpu.CompilerParams(
            dimension_semantics=("arbitrary", "arbitrary", "arbitrary"),
            vmem_limit_bytes=VMEM_LIMIT),
        name="stickbreak_attn",
    )(q, k, v, umat)


def _route_t(scores, biased):
    ts = scores.shape[-1]
    neg = -jnp.inf
    b3 = biased.reshape(N_GROUPS, GROUP_SIZE, ts)
    e_in_g = lax.broadcasted_iota(jnp.int32, b3.shape, 1)
    m1 = jnp.max(b3, axis=1, keepdims=True)
    i1 = jnp.min(jnp.where(b3 == m1, e_in_g, GROUP_SIZE), axis=1, keepdims=True)
    m2 = jnp.max(jnp.where(e_in_g == i1, neg, b3), axis=1, keepdims=True)
    gs = (m1 + m2)[:, 0, :]
    g_iota = lax.broadcasted_iota(jnp.int32, gs.shape, 0)
    g_sel = jnp.zeros(gs.shape, jnp.bool_)
    for _ in range(TOPK_GROUPS):
        gm = jnp.max(gs, axis=0, keepdims=True)
        gi = jnp.min(jnp.where(gs == gm, g_iota, N_GROUPS), axis=0, keepdims=True)
        pick = g_iota == gi
        g_sel = jnp.logical_or(g_sel, pick)
        gs = jnp.where(pick, neg, gs)
    masked = jnp.where(g_sel[:, None, :], b3, neg)
    flat = lax.broadcasted_iota(jnp.int32, b3.shape, 0) * GROUP_SIZE + e_in_g
    sel = jnp.zeros(b3.shape, jnp.bool_)
    for _ in range(TOP_K):
        m = jnp.max(jnp.max(masked, axis=1, keepdims=True), axis=0, keepdims=True)
        cand = jnp.where(masked == m, flat, N_EXPERTS)
        idx = jnp.min(jnp.min(cand, axis=1, keepdims=True), axis=0, keepdims=True)
        pick = flat == idx
        sel = jnp.logical_or(sel, pick)
        masked = jnp.where(pick, neg, masked)
    s3 = scores.reshape(N_GROUPS, GROUP_SIZE, ts)
    w = jnp.where(sel, s3, 0.0)
    tot = jnp.sum(jnp.sum(w, axis=1, keepdims=True), axis=0, keepdims=True)
    return (w / tot * ROUTED_SCALE).reshape(N_EXPERTS, ts), sel.reshape(N_EXPERTS, ts)


def _pack_bf16_pair(lo, hi):
    lo_bits = pltpu.bitcast(lo.astype(BF16).astype(F32), jnp.uint32) >> 16
    hi_bits = pltpu.bitcast(hi.astype(BF16).astype(F32), jnp.uint32) & jnp.uint32(0xFFFF0000)
    return lo_bits | hi_bits


def _unpack_bf16_pair(p):
    lo = pltpu.bitcast(p << 16, F32).astype(BF16)
    hi = pltpu.bitcast(p & jnp.uint32(0xFFFF0000), F32).astype(BF16)
    return lo, hi


def _outproj_kernel(x_ref, a_ref, p_ref, mod_ref, wo32_ref, g2_ref, wrt_ref, rb_ref, ut_ref,
                    sg32_ref, su32_ref, sd32_ref,
                    x1_ref, hp_ref, rank_ref, wd_ref, cnt_ref, run_ref, wo_ref, sgu_ref, sd_ref):
    @pl.when(_first_grid_step())
    def _():
        run_ref[...] = jnp.zeros(run_ref.shape, jnp.int32)
        wo_ref[...] = wo32_ref[...].astype(BF16)
        _cast_swiglu_weights(sg32_ref, su32_ref, sd32_ref, sgu_ref, sd_ref)

    gate1 = mod_ref[0, 2:3, :]
    shift2 = mod_ref[0, 3:4, :]
    scale2 = mod_ref[0, 4:5, :]
    mixp = _dot(a_ref[0], wo_ref[0:D_ATTN, :]) + _dot(p_ref[0], wo_ref[D_ATTN:, :])
    x1 = x_ref[0] + gate1 * mixp
    h2 = _rms_mod(x1, g2_ref[...], scale2, shift2)
    half = h2.shape[-1] // 2
    hp_ref[0] = _pack_bf16_pair(h2[:, :half], h2[:, half:])
    gu = _dot(h2.astype(BF16), sgu_ref[...])
    dh = gu.shape[-1] // 2
    shared = _dot((_silu(gu[:, :dh]) * gu[:, dh:]).astype(BF16), sd_ref[...])
    x1_ref[0] = x1 + mod_ref[0, 5:6, :] * shared
    logits_t = _dot3_nt(wrt_ref[...], h2)
    scores = 1.0 / (1.0 + jnp.exp(-logits_t))
    wd_t, sel = _route_t(scores, scores + rb_ref[...])
    wd_ref[...] = wd_t
    self_f = jnp.where(sel, 1.0, 0.0)
    before = _dot(self_f.astype(BF16), ut_ref[...]).astype(jnp.int32)
    run = run_ref[:, 0:1]
    rank_ref[...] = jnp.where(sel, run + before, -1)
    run_ref[...] = run_ref[...] + jnp.sum(self_f, axis=1, keepdims=True).astype(jnp.int32)
    cnt_ref[...] = run_ref[...]


def _outproj(x, attn, pool, mod3, w_out, g2, wr_t, rbias, ut, ws_gate, ws_up, ws_down, b0, B):
    _, S, D = x.shape
    dh = ws_gate.shape[-1]
    ts = TS_PROJ
    n_s = S // ts
    const2 = lambda b, s: (0, 0)
    tok_in = lambda w: pl.BlockSpec((1, ts, w), lambda b, s: (b + b0, s, 0))
    tok = lambda w: pl.BlockSpec((1, ts, w), lambda b, s: (b, s, 0))
    tok_t = pl.BlockSpec((N_EXPERTS, ts), lambda b, s: (0, b * n_s + s))
    return pl.pallas_call(
        _outproj_kernel,
        grid=(B, n_s),
        in_specs=[tok_in(D), tok_in(D_ATTN), tok_in(D_POOL),
                  pl.BlockSpec((1, N_MOD, D), lambda b, s: (b + b0, 0, 0)),
                  pl.BlockSpec(w_out.shape, const2),
                  pl.BlockSpec((1, D), const2),
                  pl.BlockSpec(wr_t.shape, const2),
                  pl.BlockSpec((N_EXPERTS, 1), const2),
                  pl.BlockSpec((ts, ts), const2),
                  pl.BlockSpec(ws_gate.shape, const2),
                  pl.BlockSpec(ws_up.shape, const2),
                  pl.BlockSpec(ws_down.shape, const2)],
        out_specs=[tok(D), tok(D // 2), tok_t, tok_t, pl.BlockSpec((N_EXPERTS, LANES), const2)],
        out_shape=[jax.ShapeDtypeStruct((B, S, D), F32),
                   jax.ShapeDtypeStruct((B, S, D // 2), jnp.uint32),
                   jax.ShapeDtypeStruct((N_EXPERTS, B * S), jnp.int32),
                   jax.ShapeDtypeStruct((N_EXPERTS, B * S), F32),
                   jax.ShapeDtypeStruct((N_EXPERTS, LANES), jnp.int32)],
        scratch_shapes=[pltpu.VMEM((N_EXPERTS, LANES), jnp.int32), pltpu.VMEM(w_out.shape, BF16),
                        pltpu.VMEM((D, 2 * dh), BF16), pltpu.VMEM((dh, D), BF16)],
        compiler_params=pltpu.CompilerParams(dimension_semantics=("arbitrary", "arbitrary"),
                                             vmem_limit_bytes=VMEM_LIMIT),
        name="outproj_router",
    )(x, attn, pool, mod3, w_out, g2, wr_t, rbias, ut, ws_gate, ws_up, ws_down)


def _n_row_blocks(n_tokens):
    n_blocks = -(-(n_tokens * TOP_K + N_EXPERTS * (ROW_BLOCK - 1)) // ROW_BLOCK)
    return -(-n_blocks // BLOCKS_PER_STEP) * BLOCKS_PER_STEP


def _slots_kernel(rank_ref, wd_ref, cnt_ref, lt_ref, slot_ref, wk_ref, be_ref, nu_ref):
    lt = lt_ref[...]
    nblk = lax.shift_right_logical(cnt_ref[...] + (ROW_BLOCK - 1), ROW_BLOCK_LOG2).astype(F32)
    nb_hi, nb_lo = _split_bf16(nblk)
    blk_start = _dot(lt, nb_hi) + _dot(lt, nb_lo)

    @pl.when(pl.program_id(0) == 0)
    def _():
        blk_end = (blk_start + nblk)[:, 0:1]
        b_iota = lax.broadcasted_iota(jnp.int32, (N_EXPERTS, be_ref.shape[-1]), 1).astype(F32)
        owner = jnp.sum(jnp.where(blk_end <= b_iota, 1, 0), axis=0, keepdims=True)
        be_ref[...] = jnp.minimum(owner, N_EXPERTS - 1)
        nu_ref[...] = jnp.broadcast_to(blk_end[N_EXPERTS - 1:, :].astype(jnp.int32), nu_ref.shape)

    rank = rank_ref[...]
    sel = rank >= 0
    row_start = (blk_start[:, 0:1] * ROW_BLOCK).astype(jnp.int32)
    slot_d = row_start + rank
    wd = wd_ref[...]
    choice = _dot(lt, jnp.where(sel, 1.0, 0.0).astype(BF16)).astype(jnp.int32)
    ts = rank.shape[-1]
    slots, wks = [], []
    for k in range(TOP_K):
        m = jnp.logical_and(sel, choice == k)
        slots.append(jnp.sum(jnp.where(m, slot_d, 0), axis=0, keepdims=True))
        wks.append(jnp.sum(jnp.where(m, wd, 0.0), axis=0, keepdims=True))
    slot_ref[...] = jnp.concatenate(slots + [jnp.zeros((SUBLANES - TOP_K, ts), jnp.int32)], axis=0)
    wk_pad = jnp.concatenate(wks + [jnp.zeros((LANES - TOP_K, ts), F32)], axis=0)
    wk_ref[...] = wk_pad.T


def _slots(rank_t, wd_t, cnt, lt):
    T = rank_t.shape[1]
    ts = TS_SLOT
    nb_pad = -(-_n_row_blocks(T) // LANES) * LANES
    const = lambda i: (0, 0)
    tok_t = pl.BlockSpec((N_EXPERTS, ts), lambda i: (0, i))
    return pl.pallas_call(
        _slots_kernel,
        grid=(T // ts,),
        in_specs=[tok_t, tok_t, pl.BlockSpec((N_EXPERTS, LANES), const), pl.BlockSpec((N_EXPERTS, N_EXPERTS), const)],
        out_specs=[pl.BlockSpec((SUBLANES, ts), lambda i: (0, i)),
                   pl.BlockSpec((ts, LANES), lambda i: (i, 0)),
                   pl.BlockSpec((1, nb_pad), const),
                   pl.BlockSpec((1, LANES), const)],
        out_shape=[jax.ShapeDtypeStruct((SUBLANES, T), jnp.int32),
                   jax.ShapeDtypeStruct((T, LANES), F32),
                   jax.ShapeDtypeStruct((1, nb_pad), jnp.int32),
                   jax.ShapeDtypeStruct((1, LANES), jnp.int32)],
        compiler_params=pltpu.CompilerParams(dimension_semantics=("arbitrary",), vmem_limit_bytes=VMEM_LIMIT),
        name="slots",
    )(rank_t, wd_t, cnt, lt)


def _sc_mesh():
    return plsc.VectorSubcoreMesh(core_axis_name="c", subcore_axis_name="s",
                                  num_cores=SC_CORES, num_subcores=SC_SUBCORES)


def _sc_dispatch(rows, slot_flat, n_out):
    T, width = rows.shape
    n_workers = SC_CORES * SC_SUBCORES
    per_worker = T // n_workers
    steps = per_worker // SC_CHUNK

    @functools.partial(
        pl.kernel, mesh=_sc_mesh(),
        out_type=jax.ShapeDtypeStruct((n_out, width), rows.dtype),
        scratch_types=[pltpu.VMEM((SC_CHUNK, width), rows.dtype)]
        + [pltpu.VMEM((SC_CHUNK,), jnp.int32)] * TOP_K + [pltpu.SemaphoreType.DMA],
        name="dispatch",
    )
    def run(rows_hbm, slot_hbm, out_hbm, rows_v, *rest):
        idx_v, sem = rest[:TOP_K], rest[TOP_K]
        base = (lax.axis_index("s") * SC_CORES + lax.axis_index("c")) * per_worker

        @pl.loop(0, steps)
        def _(i):
            off = base + i * SC_CHUNK
            pltpu.sync_copy(rows_hbm.at[pl.ds(off, SC_CHUNK)], rows_v)
            for k in range(TOP_K):
                pltpu.sync_copy(slot_hbm.at[pl.ds(k * T + off, SC_CHUNK)], idx_v[k])
            copies = [pltpu.async_copy(rows_v, out_hbm.at[idx_v[k]], sem) for k in range(TOP_K)]
            for cp in copies:
                cp.wait()

    return run(rows, slot_flat)


def _sc_gather(rows, idx):
    n = idx.shape[0]
    width = rows.shape[1]
    n_workers = SC_CORES * SC_SUBCORES
    per_worker = n // n_workers
    steps = per_worker // SC_CHUNK

    assert steps % 2 == 0
    slot_types = [pltpu.VMEM((SC_CHUNK,), jnp.int32), pltpu.VMEM((SC_CHUNK, width), rows.dtype),
                  pltpu.SemaphoreType.DMA]

    @functools.partial(
        pl.kernel, mesh=_sc_mesh(),
        out_type=jax.ShapeDtypeStruct((n, width), rows.dtype),
        scratch_types=slot_types * 2,
        name="combine",
    )
    def run(rows_hbm, idx_hbm, out_hbm, *scratch):
        base = (lax.axis_index("s") * SC_CORES + lax.axis_index("c")) * per_worker
        slots = (scratch[0:3], scratch[3:6])

        def gather(slot):
            idx_v, rows_v, sem = slots[slot]
            return pltpu.make_async_copy(rows_hbm.at[idx_v], rows_v, sem)

        def start(chunk, slot):
            pltpu.sync_copy(idx_hbm.at[pl.ds(base + chunk * SC_CHUNK, SC_CHUNK)], slots[slot][0])
            gather(slot).start()

        def finish(chunk, slot):
            gather(slot).wait()
            pltpu.sync_copy(slots[slot][1], out_hbm.at[pl.ds(base + chunk * SC_CHUNK, SC_CHUNK)])

        start(0, 0)

        @pl.loop(0, steps, step=2)
        def _(chunk):
            start(chunk + 1, 1)
            finish(chunk, 0)

            @pl.when(chunk + 2 < steps)
            def _():
                start(chunk + 2, 0)

            finish(chunk + 1, 1)

    return run(rows, idx)


def _swiglu_packed(xp, wgu, wd):
    lo, hi = _unpack_bf16_pair(xp)
    half = xp.shape[-1]
    gu = _dot(lo, wgu[:half, :]) + _dot(hi, wgu[half:, :])
    dh = wgu.shape[-1] // 2
    act = _silu(gu[:, :dh]) * gu[:, dh:]
    return _dot(act.astype(BF16), wd)


def _cast_swiglu_weights(wg32_ref, wu32_ref, wd32_ref, wgu_ref, wd_ref):
    dh = wg32_ref.shape[-1]
    wgu_ref[:, :dh] = wg32_ref[...].reshape(wg32_ref.shape[-2:]).astype(BF16)
    wgu_ref[:, dh:] = wu32_ref[...].reshape(wu32_ref.shape[-2:]).astype(BF16)
    wd_ref[...] = wd32_ref[...].reshape(wd32_ref.shape[-2:]).astype(BF16)


def _experts_kernel(be_ref, nu_ref, x_ref, *refs):
    n = BLOCKS_PER_STEP
    w32 = [refs[3 * j:3 * j + 3] for j in range(n)]
    y_ref = refs[3 * n]
    wbf = [refs[3 * n + 1 + 2 * j:3 * n + 3 + 2 * j] for j in range(n)]
    step = pl.program_id(0)
    for j in range(n):
        blk = step * n + j
        new_expert = jnp.logical_or(step == 0, be_ref[blk] != be_ref[jnp.maximum(blk - n, 0)])

        @pl.when(new_expert)
        def _():
            _cast_swiglu_weights(*w32[j], *wbf[j])

    @pl.when(step * n < nu_ref[0])
    def _():
        for j in range(n):
            rows = pl.ds(j * ROW_BLOCK, ROW_BLOCK)
            y = _swiglu_packed(x_ref[rows, :], wbf[j][0][...], wbf[j][1][...])
            half = y.shape[-1] // 2
            y_ref[rows, :] = _pack_bf16_pair(y[:, :half], y[:, half:])


def _experts(xs, block_expert, n_used, w_gate, w_up, w_down):
    P, half = xs.shape
    _, D, dh = w_gate.shape
    n = BLOCKS_PER_STEP
    rows = n * ROW_BLOCK
    assert P % rows == 0

    def by_expert(shape, j):
        return pl.BlockSpec((1,) + shape, lambda s, be, nu: (be[s * n + j], 0, 0))

    w_specs, w_args = [], []
    for j in range(n):
        w_specs += [by_expert((D, dh), j), by_expert((D, dh), j), by_expert((dh, D), j)]
        w_args += [w_gate, w_up, w_down]
    def row_step(s, be, nu):
        return jnp.minimum(s, lax.div(jnp.maximum(nu[0], 1) - 1, n)), 0

    grid_spec = pltpu.PrefetchScalarGridSpec(
        num_scalar_prefetch=2,
        grid=(P // rows,),
        in_specs=[pl.BlockSpec((rows, half), row_step)] + w_specs,
        out_specs=pl.BlockSpec((rows, half), row_step),
        scratch_shapes=[pltpu.VMEM((D, 2 * dh), BF16), pltpu.VMEM((dh, D), BF16)] * n,
    )
    return pl.pallas_call(
        _experts_kernel,
        grid_spec=grid_spec,
        out_shape=jax.ShapeDtypeStruct((P, half), jnp.uint32),
        compiler_params=pltpu.CompilerParams(dimension_semantics=("arbitrary",), vmem_limit_bytes=VMEM_LIMIT),
        name="experts",
    )(block_expert, n_used, xs, *w_args)


def _final_kernel(x1_ref, g_ref, wk_ref, mod_ref, *rest):
    o_ref = rest[-1]
    wk = wk_ref[0]
    acc = None
    for k in range(TOP_K):
        lo, hi = _unpack_bf16_pair(g_ref[k, 0])
        y = wk[:, k:k + 1] * jnp.concatenate([lo.astype(F32), hi.astype(F32)], axis=-1)
        acc = y if acc is None else acc + y
    gate2 = mod_ref[0, 5:6, :]
    o_ref[0] = x1_ref[0] + gate2 * acc


def _final(x1, g, wk_tok, mod3, prev_out, b0, b_total):
    B, S, D = x1.shape
    ts = TS_PROJ
    tok = lambda w: pl.BlockSpec((1, ts, w), lambda b, s: (b, s, 0))
    in_specs = [tok(D),
                pl.BlockSpec((TOP_K, 1, ts, D // 2), lambda b, s: (0, b, s, 0)),
                tok(LANES),
                pl.BlockSpec((1, N_MOD, D), lambda b, s: (b + b0, 0, 0))]
    args = [x1, g, wk_tok, mod3]
    aliases = {}
    if prev_out is not None:
        in_specs.append(pl.BlockSpec(memory_space=pl.ANY))
        args.append(prev_out)
        aliases = {len(args) - 1: 0}
    return pl.pallas_call(
        _final_kernel,
        grid=(B, S // ts),
        in_specs=in_specs,
        out_specs=pl.BlockSpec((1, ts, D), lambda b, s: (b + b0, s, 0)),
        out_shape=jax.ShapeDtypeStruct((b_total, S, D), F32),
        input_output_aliases=aliases,
        compiler_params=pltpu.CompilerParams(dimension_semantics=("arbitrary", "arbitrary"),
                                             vmem_limit_bytes=VMEM_LIMIT),
        name="final",
    )(*args)


def _layer(x, c_act_mod, norm1_g, norm2_g, w_in, q_norm_g, k_norm_g, w_pool, pool_scale, w_out,
           w_router, router_bias, w_gate, w_up, w_down, ws_gate, ws_up, ws_down):
    B, S, D = x.shape
    mod3 = c_act_mod.reshape(B, N_MOD, D)
    head_of = jnp.arange(D_ATTN, dtype=jnp.int32) // HEAD_DIM
    head_mean = jnp.where(head_of[:, None] == head_of[None, :], 1.0 / HEAD_DIM, 0.0).astype(BF16)
    j = jnp.arange(TK, dtype=jnp.int32)
    umat = jnp.where(j[:, None] >= j[None, :], -1.0, 0.0).astype(BF16)

    q, k, v, pool = _inproj(
        x, mod3, norm1_g.reshape(1, D), w_in,
        jnp.tile(q_norm_g, N_HEADS).reshape(1, D_ATTN), jnp.tile(k_norm_g, N_HEADS).reshape(1, D_ATTN),
        head_mean, w_pool, pool_scale.reshape(1, D_POOL))
    attn = _attention(q, k, v, umat)
    t = jnp.arange(TS_PROJ, dtype=jnp.int32)
    ut = (t[:, None] < t[None, :]).astype(BF16)
    e = jnp.arange(N_EXPERTS, dtype=jnp.int32)
    lt = (e[None, :] < e[:, None]).astype(BF16)
    bp = B // MOE_PARTS
    T = bp * S
    n_blocks = _n_row_blocks(T)
    out = None
    for part in range(MOE_PARTS):
        b0 = part * bp
        x1, hp, rank_t, wd_t, cnt = _outproj(x, attn, pool, mod3, w_out, norm2_g.reshape(1, D),
                                             w_router.T, router_bias.reshape(N_EXPERTS, 1), ut,
                                             ws_gate, ws_up, ws_down, b0, bp)
        slots, wk_tok, block_expert, n_used = _slots(rank_t, wd_t, cnt, lt)
        slot_flat = slots[:TOP_K].reshape(TOP_K * T)
        xs = _sc_dispatch(hp.reshape(T, D // 2), slot_flat, n_blocks * ROW_BLOCK)
        ys = _experts(xs, block_expert[0, :n_blocks], n_used[0, :1], w_gate, w_up, w_down)
        g = _sc_gather(ys, slot_flat).reshape(TOP_K, bp, S, D // 2)
        out = _final(x1, g, wk_tok.reshape(bp, S, LANES), mod3, out, b0, B)
    return out


def kernel(x, c, w_ada, b_ada, norm1_g, norm2_g, w_in, q_norm_g, k_norm_g, w_pool, pool_scale, w_out,
           w_router, router_bias, w_gate, w_up, w_down, ws_gate, ws_up, ws_down):
    depth = w_ada.shape[0]
    for l in range(depth):
        mod = _adaln(c, w_ada[l], b_ada[l])
        x = _layer(x, mod, norm1_g[l], norm2_g[l], w_in[l], q_norm_g[l], k_norm_g[l], w_pool[l],
                   pool_scale[l], w_out[l], w_router[l], router_bias[l], w_gate[l], w_up[l], w_down[l],
                   ws_gate[l], ws_up[l], ws_down[l])
    return x
```
